```python
import math
import jax, jax.numpy as jnp
from jax import lax
import numpy as np

D_MODEL = 1024
BATCH = 2
SEQ = 8192
DEPTH = 1
DEC_BATCH = 128
DEC_SEQ = 1
PAST_LEN = 8192
PAGE_SIZE = 128

HEAD_DIM = 64
A_HEADS = 8
A_KV_HEADS = 2
A_GROUP = A_HEADS // A_KV_HEADS
A_WINDOW = 128
B_PATTERNS = ((128, 1), (512, 4), (2048, 16))
B_HEADS_PER_PATTERN = 4
B_HEADS = B_HEADS_PER_PATTERN * len(B_PATTERNS)
N_HEADS_TOTAL = A_HEADS + B_HEADS
B_OUT_W = B_HEADS_PER_PATTERN * HEAD_DIM
BLOCK = 128
NUM_BUCKETS = 32
MAX_DISTANCE = 2048
MOE_GROUPS = 4
EXPERTS_PER_GROUP = 8
N_EXPERTS = MOE_GROUPS * EXPERTS_PER_GROUP
TOP_K = 2
D_EXPERT = 512
MOE_BLOCK = 256
EPS = 1e-6

QA_W = A_HEADS * HEAD_DIM
KA_W = A_KV_HEADS * HEAD_DIM
QB_W = B_HEADS * HEAD_DIM
SPLIT_SIZES = (QA_W, KA_W, KA_W, QB_W, QB_W, QB_W, D_MODEL, D_MODEL)
SPLIT_POINTS = tuple(int(v) for v in np.cumsum(SPLIT_SIZES)[:-1])
IN_W = sum(SPLIT_SIZES)

kernel_name = "hybrid_swa_sink_dilated_hmoe_step"


def rms_norm(x, g):
    xf = x.astype(jnp.float32)
    y = xf * lax.rsqrt(jnp.mean(xf * xf, axis=-1, keepdims=True) + EPS)
    return y.astype(x.dtype) * g


def t5_bucket(dist):
    max_exact = NUM_BUCKETS // 2
    df = jnp.maximum(dist, max_exact).astype(jnp.float32)
    large = max_exact + (jnp.log(df / max_exact) / math.log(MAX_DISTANCE / max_exact)
                         * (NUM_BUCKETS - max_exact)).astype(jnp.int32)
    return jnp.where(dist < max_exact, dist, jnp.minimum(large, NUM_BUCKETS - 1))


def softmax_lse(s, sink):
    m = jnp.max(s, axis=-1)
    if sink is not None:
        m = jnp.maximum(m, sink)
    p = jnp.exp(s - m[..., None])
    den = jnp.sum(p, axis=-1)
    if sink is not None:
        den = den + jnp.exp(sink - m)
    return p / den[..., None], m + jnp.log(den)


def band_bias(table_cols, max_dist, dilation, n_kv, n_grp):
    qi = jnp.arange(BLOCK)[:, None]
    c = jnp.arange(2 * BLOCK)[None, :]
    dist = qi + BLOCK - c
    valid = (dist >= 0) & (dist <= max_dist)
    b = table_cols.astype(jnp.float32)[t5_bucket(jnp.clip(dist, 0) * dilation)]
    b = jnp.where(valid[..., None], b, -jnp.inf)
    return jnp.transpose(b, (2, 0, 1)).reshape(n_kv, n_grp, BLOCK, 2 * BLOCK)


def banded_attention(q, k, v, bias, sink):
    n, l, n_kv, n_grp, hd = q.shape
    nb = l // BLOCK
    qb = q.reshape(n, nb, BLOCK, n_kv, n_grp, hd)
    kb = k.reshape(n, nb, BLOCK, n_kv, hd)
    vb = v.reshape(n, nb, BLOCK, n_kv, hd)
    pad = ((0, 0), (1, 0), (0, 0), (0, 0), (0, 0))
    k_band = jnp.concatenate([jnp.pad(kb[:, :-1], pad), kb], axis=2)
    v_band = jnp.concatenate([jnp.pad(vb[:, :-1], pad), vb], axis=2)
    s = jnp.einsum('nbikgd,nbckd->nbkgic', qb, k_band).astype(jnp.float32) * (hd ** -0.5) + bias
    valid = (jnp.arange(nb)[:, None] > 0) | (jnp.arange(2 * BLOCK)[None, :] >= BLOCK)
    s = jnp.where(valid[None, :, None, None, None, :], s, -jnp.inf)
    p, lse = softmax_lse(s, sink)
    o = jnp.einsum('nbkgic,nbckd->nbikgd', p.astype(v.dtype), v_band).reshape(n, l, n_kv, n_grp, hd)
    lse = jnp.moveaxis(lse, 4, 2).reshape(n, l, n_kv, n_grp)
    return o, lse


def to_residue(x, d):
    n, t = x.shape[:2]
    x = jnp.moveaxis(x.reshape(n, t // d, d, *x.shape[2:]), 2, 1)
    return x.reshape(n * d, t // d, *x.shape[3:])


def from_residue(x, n, d):
    l = x.shape[1]
    x = jnp.moveaxis(x.reshape(n, d, l, *x.shape[2:]), 1, 2)
    return x.reshape(n, l * d, *x.shape[3:])


def dilated_prompt(q, k, v, table_cols, window, dil):
    n, t = q.shape[:2]
    l = t // dil
    lp = -(-l // BLOCK) * BLOCK
    pad = ((0, 0), (0, lp - l), (0, 0), (0, 0))
    qr = jnp.pad(to_residue(q, dil), pad)[:, :, :, None]
    kr = jnp.pad(to_residue(k, dil), pad)
    vr = jnp.pad(to_residue(v, dil), pad)
    bias = band_bias(table_cols, window // dil, dil, B_HEADS_PER_PATTERN, 1)
    o, lse = banded_attention(qr, kr, vr, bias, None)
    return from_residue(o[:, :l, :, 0], n, dil), from_residue(lse[:, :l, :, 0], n, dil)


def window_decode(q, k_new, v_new, cache, table_cols, sink):
    w, s_len, hd = cache.shape[1], q.shape[1], q.shape[-1]
    k_all = jnp.concatenate([cache[:, :, 0], k_new], axis=1)
    v_all = jnp.concatenate([cache[:, :, 1], v_new], axis=1)
    dist = (w + jnp.arange(s_len))[:, None] - jnp.arange(w + s_len)[None, :]
    valid = (dist >= 0) & (dist < A_WINDOW)
    b = table_cols.astype(jnp.float32)[t5_bucket(jnp.clip(dist, 0))]
    b = jnp.where(valid[..., None], b, -jnp.inf)
    b = jnp.transpose(b, (2, 0, 1)).reshape(A_KV_HEADS, A_GROUP, s_len, w + s_len)
    s = jnp.einsum('nskgd,nckd->nkgsc', q, k_all).astype(jnp.float32) * (hd ** -0.5) + b
    p, _ = softmax_lse(s, sink)
    o = jnp.einsum('nkgsc,nckd->nskgd', p.astype(v_all.dtype), v_all)
    new_cache = jnp.concatenate([cache, jnp.stack([k_new, v_new], axis=2)], axis=1)[:, s_len:]
    return o, new_cache


def dilated_decode(q, k_new, v_new, cache, table_cols, window, dil):
    w, s_len, hd = cache.shape[1], q.shape[1], q.shape[-1]
    k_all = jnp.concatenate([cache[:, :, 0], k_new], axis=1)
    v_all = jnp.concatenate([cache[:, :, 1], v_new], axis=1)
    j = jnp.arange(window // dil + 1)
    idx = (w + jnp.arange(s_len))[:, None] - dil * j[None, :]
    valid = idx >= 0
    idx = jnp.maximum(idx, 0)
    k_g = k_all[:, idx]
    v_g = v_all[:, idx]
    b = table_cols.astype(jnp.float32)[t5_bucket(dil * j)]
    b = jnp.transpose(jnp.where(valid[..., None], b[None], -jnp.inf), (2, 0, 1))
    s = jnp.einsum('nshd,nsjhd->nhsj', q, k_g).astype(jnp.float32) * (hd ** -0.5) + b
    p, lse = softmax_lse(s, None)
    o = jnp.einsum('nhsj,nsjhd->nshd', p.astype(v_g.dtype), v_g)
    new_cache = jnp.concatenate([cache, jnp.stack([k_new, v_new], axis=2)], axis=1)[:, s_len:]
    return o, jnp.transpose(lse, (0, 2, 1)), new_cache


def combine_patterns(outs, lses):
    o = jnp.stack(outs).astype(jnp.float32)
    a = jax.nn.softmax(jnp.stack(lses), axis=0)
    return jnp.sum(a[..., None] * o, axis=0).astype(outs[0].dtype)


def mixer_inputs(xn, w_in):
    n, t, _ = xn.shape
    qa, ka, va, qb, kb, vb, ga, gb = jnp.split(xn @ w_in, SPLIT_POINTS, axis=-1)
    return (qa.reshape(n, t, A_KV_HEADS, A_GROUP, HEAD_DIM),
            ka.reshape(n, t, A_KV_HEADS, HEAD_DIM), va.reshape(n, t, A_KV_HEADS, HEAD_DIM),
            qb.reshape(n, t, B_HEADS, HEAD_DIM), kb.reshape(n, t, B_HEADS, HEAD_DIM),
            vb.reshape(n, t, B_HEADS, HEAD_DIM), ga, gb)


def moe(x, w_rg, b_rg, w_re, b_re, w1, w3, w2):
    shp = x.shape
    xf = x.reshape(-1, D_MODEL)
    m = xf.shape[0]
    mp = -(-m // MOE_BLOCK) * MOE_BLOCK
    xb = jnp.pad(xf, ((0, mp - m), (0, 0))).reshape(mp // MOE_BLOCK, MOE_BLOCK, D_MODEL)

    def block(xc):
        lg = (xc @ w_rg + b_rg).astype(jnp.float32)
        g_sel = jnp.argmax(lg, axis=-1)
        p_g = jnp.take_along_axis(jax.nn.softmax(lg, axis=-1), g_sel[:, None], axis=1)[:, 0]
        le = (xc @ w_re + b_re).astype(jnp.float32).reshape(-1, MOE_GROUPS, EXPERTS_PER_GROUP)
        le = jnp.take_along_axis(le, g_sel[:, None, None], axis=1)[:, 0]
        top_v, top_i = lax.top_k(jax.nn.softmax(le, axis=-1), TOP_K)
        wts = p_g[:, None] * top_v / jnp.sum(top_v, axis=-1, keepdims=True)
        e_idx = g_sel[:, None] * EXPERTS_PER_GROUP + top_i
        gates = jnp.einsum('mk,mke->me', wts, jax.nn.one_hot(e_idx, N_EXPERTS, dtype=jnp.float32))
        a = jnp.einsum('md,edf->mef', xc, w1)
        b = jnp.einsum('md,edf->mef', xc, w3)
        h = jax.nn.silu(a) * b * gates[:, :, None].astype(xc.dtype)
        return jnp.einsum('mef,efd->md', h, w2)

    y = lax.map(block, xb).reshape(mp, D_MODEL)[:m]
    return y.reshape(shp)


def layer_forward(x, caches, rel_bias, ln1, w_in, sinks, w_pa, w_pb, w_out, ln2, w_rg, b_rg, w_re, b_re, w1, w3, w2):
    n, t, _ = x.shape
    xn = rms_norm(x, ln1)
    qa, ka, va, qb, kb, vb, ga, gb = mixer_inputs(xn, w_in)
    sink = sinks.astype(jnp.float32).reshape(A_KV_HEADS, A_GROUP, 1)
    table_a = rel_bias[:, :A_HEADS]
    if caches is None:
        bias_a = band_bias(table_a, A_WINDOW - 1, 1, A_KV_HEADS, A_GROUP)
        oa, _ = banded_attention(qa, ka, va, bias_a, sink)
        st_a = jnp.stack([ka, va], axis=2)[:, t - min(A_WINDOW, t):]
    else:
        oa, st_a = window_decode(qa, ka, va, caches[0], table_a, sink)
    states = [st_a]
    outs, lses = [], []
    for p, (win, dil) in enumerate(B_PATTERNS):
        lo = p * B_HEADS_PER_PATTERN
        hi = lo + B_HEADS_PER_PATTERN
        tab = rel_bias[:, A_HEADS + lo:A_HEADS + hi]
        q_p, k_p, v_p = qb[:, :, lo:hi], kb[:, :, lo:hi], vb[:, :, lo:hi]
        if caches is None:
            o, lse = dilated_prompt(q_p, k_p, v_p, tab, win, dil)
            st = jnp.stack([k_p, v_p], axis=2)[:, t - min(win, t):]
        else:
            o, lse, st = dilated_decode(q_p, k_p, v_p, caches[1 + p], tab, win, dil)
        outs.append(o)
        lses.append(lse)
        states.append(st)
    ob = combine_patterns(outs, lses)
    merged = (jax.nn.sigmoid(ga) * (oa.reshape(n, t, QA_W) @ w_pa)
              + jax.nn.sigmoid(gb) * (ob.reshape(n, t, B_OUT_W) @ w_pb))
    h = x + merged @ w_out
    y = h + moe(rms_norm(h, ln2), w_rg, b_rg, w_re, b_re, w1, w3, w2)
    return y, states


def setup_inputs(seed: int = 0) -> dict:
    key = jax.random.key(seed)
    ks = jax.random.split(key, 24)
    f32 = jnp.float32

    def nrm(k, shape, scale):
        return jax.random.normal(k, shape, f32) * scale

    def cache(k, window, heads):
        return nrm(k, (DEPTH, DEC_BATCH, min(window, PAST_LEN), 2, heads, HEAD_DIM), 1.0)

    return {
        "x_prompt": nrm(ks[0], (BATCH, SEQ, D_MODEL), 1.0),
        "x_sample": nrm(ks[1], (DEC_BATCH, DEC_SEQ, D_MODEL), 1.0),
        "cache_a_kv": cache(ks[2], A_WINDOW, A_KV_HEADS),
        "cache_b1_kv": cache(ks[3], B_PATTERNS[0][0], B_HEADS_PER_PATTERN),
        "cache_b2_kv": cache(ks[4], B_PATTERNS[1][0], B_HEADS_PER_PATTERN),
        "cache_b3_kv": cache(ks[5], B_PATTERNS[2][0], B_HEADS_PER_PATTERN),
        "rel_bias": nrm(ks[6], (NUM_BUCKETS, N_HEADS_TOTAL), 0.5),
        "ln1_g": 1.0 + nrm(ks[7], (DEPTH, D_MODEL), 0.02),
        "w_in": nrm(ks[8], (DEPTH, D_MODEL, IN_W), D_MODEL ** -0.5),
        "sinks": nrm(ks[9], (DEPTH, A_HEADS), 0.5),
        "w_pa": nrm(ks[10], (DEPTH, QA_W, D_MODEL), QA_W ** -0.5),
        "w_pb": nrm(ks[11], (DEPTH, B_OUT_W, D_MODEL), B_OUT_W ** -0.5),
        "w_out": nrm(ks[12], (DEPTH, D_MODEL, D_MODEL), D_MODEL ** -0.5),
        "ln2_g": 1.0 + nrm(ks[13], (DEPTH, D_MODEL), 0.02),
        "w_rg": nrm(ks[14], (DEPTH, D_MODEL, MOE_GROUPS), D_MODEL ** -0.5),
        "b_rg": nrm(ks[15], (DEPTH, MOE_GROUPS), 0.01),
        "w_re": nrm(ks[16], (DEPTH, D_MODEL, N_EXPERTS), D_MODEL ** -0.5),
        "b_re": nrm(ks[17], (DEPTH, N_EXPERTS), 0.01),
        "w1": nrm(ks[18], (DEPTH, N_EXPERTS, D_MODEL, D_EXPERT), D_MODEL ** -0.5),
        "w3": nrm(ks[19], (DEPTH, N_EXPERTS, D_MODEL, D_EXPERT), D_MODEL ** -0.5),
        "w2": nrm(ks[20], (DEPTH, N_EXPERTS, D_EXPERT, D_MODEL), D_EXPERT ** -0.5),
        "lnf_g": 1.0 + nrm(ks[21], (D_MODEL,), 0.02),
    }


def reference(x_prompt, x_sample, cache_a_kv, cache_b1_kv, cache_b2_kv, cache_b3_kv, rel_bias,
              ln1_g, w_in, sinks, w_pa, w_pb, w_out, ln2_g, w_rg, b_rg, w_re, b_re, w1, w3, w2, lnf_g):
    xp, xs = x_prompt, x_sample
    sample_caches = (cache_a_kv, cache_b1_kv, cache_b2_kv, cache_b3_kv)
    new_p = [[], [], [], []]
    new_s = [[], [], [], []]
    for l in range(DEPTH):
        lw = (ln1_g[l], w_in[l], sinks[l], w_pa[l], w_pb[l], w_out[l], ln2_g[l],
              w_rg[l], b_rg[l], w_re[l], b_re[l], w1[l], w3[l], w2[l])
        xp, st_p = layer_forward(xp, None, rel_bias, *lw)
        xs, st_s = layer_forward(xs, (cache_a_kv[l], cache_b1_kv[l], cache_b2_kv[l], cache_b3_kv[l]), rel_bias, *lw)
        for i in range(len(sample_caches)):
            new_p[i].append(st_p[i])
            new_s[i].append(st_s[i])
    y_prompt = rms_norm(xp, lnf_g)
    y_sample = rms_norm(xs, lnf_g)
    a_p, b1_p, b2_p, b3_p = [jnp.stack(v) for v in new_p]
    a_s, b1_s, b2_s, b3_s = [jnp.stack(v) for v in new_s]
    return (y_prompt, y_sample, a_p, a_s, b1_p, b1_s, b2_p, b2_s, b3_p, b3_s)
```

```python
import functools
import math

import jax
import jax.numpy as jnp
from jax import lax
from jax.experimental import pallas as pl
from jax.experimental.pallas import tpu as pltpu

F32 = jnp.float32
BF16 = jnp.bfloat16

HEAD_DIM = 64
A_HEADS = 8
A_KV_HEADS = 2
A_GROUP = A_HEADS // A_KV_HEADS
A_WINDOW = 128
B_PATTERNS = ((128, 1), (512, 4), (2048, 16))
B_HEADS_PER_PATTERN = 4
B_HEADS = B_HEADS_PER_PATTERN * len(B_PATTERNS)
BLOCK = 128
NUM_BUCKETS = 32
MAX_DISTANCE = 2048
MOE_GROUPS = 4
EXPERTS_PER_GROUP = 8
N_EXPERTS = MOE_GROUPS * EXPERTS_PER_GROUP
EPS = 1e-6
NEG = -1e30
LANES = 128

QA_W = A_HEADS * HEAD_DIM
KA_W = A_KV_HEADS * HEAD_DIM
QB_W = B_HEADS * HEAD_DIM
PB_W = B_HEADS_PER_PATTERN * HEAD_DIM
OFF_QA = 0
OFF_KA = OFF_QA + QA_W
OFF_VA = OFF_KA + KA_W
OFF_QB = OFF_VA + KA_W
OFF_KB = OFF_QB + QB_W
OFF_VB = OFF_KB + QB_W
OFF_GA = OFF_VB + QB_W
KV_F32_W = 2 * KA_W + 2 * QB_W

VMEM_LIMIT = 56 * 1024 * 1024


def _cparams(sem):
    return pltpu.CompilerParams(dimension_semantics=sem, vmem_limit_bytes=VMEM_LIMIT)


def _t5_bucket(dist):
    max_exact = NUM_BUCKETS // 2
    df = jnp.maximum(dist, max_exact).astype(F32)
    large = max_exact + (jnp.log(df / max_exact) / math.log(MAX_DISTANCE / max_exact)
                         * (NUM_BUCKETS - max_exact)).astype(jnp.int32)
    return jnp.where(dist < max_exact, dist, jnp.minimum(large, NUM_BUCKETS - 1))


def _band_bias(table_cols, max_dist, dilation):
    qi = jnp.arange(BLOCK)[:, None]
    c = jnp.arange(2 * BLOCK)[None, :]
    dist = qi + BLOCK - c
    valid = (dist >= 0) & (dist <= max_dist)
    b = table_cols.astype(F32)[_t5_bucket(jnp.clip(dist, 0) * dilation)]
    b = jnp.where(valid[..., None], b, NEG)
    return jnp.transpose(b, (2, 0, 1))


def _decode_bias(table_cols, dilation, first_valid):
    c = jnp.arange(BLOCK)
    dist = (BLOCK - c) * dilation
    b = table_cols.astype(F32)[_t5_bucket(dist)]
    b = jnp.where((c >= first_valid)[:, None], b, NEG)
    b0 = table_cols.astype(F32)[_t5_bucket(jnp.zeros((1,), jnp.int32))]
    return b[:, :, None], b0[:, :, None]


def _in_proj_kernel(x_ref, g_ref, w_ref, act_ref, kv_ref):
    x = x_ref[...]
    xn = (x * lax.rsqrt(jnp.mean(x * x, axis=-1, keepdims=True) + EPS)) * g_ref[...]
    xb = xn.astype(BF16)
    scale = HEAD_DIM ** -0.5

    def proj(lo, hi):
        return jnp.dot(xb, w_ref[:, lo:hi], preferred_element_type=F32)

    act_ref[:, OFF_QA:OFF_KA] = (proj(OFF_QA, OFF_KA) * scale).astype(BF16)
    kva = proj(OFF_KA, OFF_QB)
    act_ref[:, OFF_KA:OFF_QB] = kva.astype(BF16)
    kv_ref[:, 0:2 * KA_W] = kva
    act_ref[:, OFF_QB:OFF_KB] = (proj(OFF_QB, OFF_KB) * scale).astype(BF16)
    for lo in range(OFF_KB, OFF_GA, QB_W):
        kvb = proj(lo, lo + QB_W)
        act_ref[:, lo:lo + QB_W] = kvb.astype(BF16)
        kv_ref[:, lo - OFF_KB + 2 * KA_W:lo - OFF_KB + 2 * KA_W + QB_W] = kvb
    in_w = act_ref.shape[1]
    step = 1024
    for lo in range(OFF_GA, in_w, step):
        g = proj(lo, lo + step)
        act_ref[:, lo:lo + step] = (1.0 / (1.0 + jnp.exp(-g))).astype(BF16)


def _in_proj(x2d, ln_g, w_bf16, tm):
    m, d = x2d.shape
    in_w = w_bf16.shape[1]
    return pl.pallas_call(
        _in_proj_kernel,
        grid=(m // tm,),
        in_specs=[pl.BlockSpec((tm, d), lambda i: (i, 0)),
                  pl.BlockSpec((1, d), lambda i: (0, 0)),
                  pl.BlockSpec((d, in_w), lambda i: (0, 0))],
        out_specs=[pl.BlockSpec((tm, in_w), lambda i: (i, 0)),
                   pl.BlockSpec((tm, KV_F32_W), lambda i: (i, 0))],
        out_shape=[jax.ShapeDtypeStruct((m, in_w), BF16),
                   jax.ShapeDtypeStruct((m, KV_F32_W), F32)],
        compiler_params=_cparams(("arbitrary",)),
        name="in_proj",
    )(x2d, ln_g.reshape(1, d), w_bf16)


def _band_attn_kernel(*refs, kv_heads, grp, has_sink, want_lse):
    if has_sink:
        sink_ref, refs = refs[0], refs[1:]
    q_ref, kp_ref, kc_ref, vp_ref, vc_ref, bias_ref, o_ref = refs[:7]
    lse_ref = refs[7] if want_lse else None
    blk = pl.program_id(2)
    hd = HEAD_DIM
    nt = (((1,), (1,)), ((), ()))
    for kv in range(kv_heads):
        ks = slice(kv * hd, (kv + 1) * hd)
        kp, kc, vp, vc = kp_ref[:, ks], kc_ref[:, ks], vp_ref[:, ks], vc_ref[:, ks]
        for g in range(grp):
            h = kv * grp + g
            hs = slice(h * hd, (h + 1) * hd)
            q = q_ref[:, hs]
            sp = lax.dot_general(q, kp, nt, preferred_element_type=F32) + bias_ref[h, :, 0:BLOCK]
            sc = lax.dot_general(q, kc, nt, preferred_element_type=F32) + bias_ref[h, :, BLOCK:2 * BLOCK]
            sp = jnp.where(blk > 0, sp, NEG)
            m = jnp.maximum(jnp.max(sp, axis=-1, keepdims=True), jnp.max(sc, axis=-1, keepdims=True))
            if has_sink:
                m = jnp.maximum(m, sink_ref[h])
            pp = jnp.exp(sp - m)
            pc = jnp.exp(sc - m)
            den = jnp.sum(pp, axis=-1, keepdims=True) + jnp.sum(pc, axis=-1, keepdims=True)
            if has_sink:
                den = den + jnp.exp(sink_ref[h] - m)
            o = (jnp.dot(pp.astype(BF16), vp, preferred_element_type=F32)
                 + jnp.dot(pc.astype(BF16), vc, preferred_element_type=F32))
            o_ref[:, hs] = (o / den).astype(o_ref.dtype)
            if want_lse:
                lse_ref[:, hs] = jnp.broadcast_to(m + jnp.log(den), (BLOCK, hd))


def _band_attn(act, bias, sink, *, n, t, dil, q_off, k_off, v_off, kv_heads, grp, want_lse):
    in_w = act.shape[1]
    l = t // dil
    assert l % BLOCK == 0
    nb = l // BLOCK
    qw = kv_heads * grp * HEAD_DIM
    kw = kv_heads * HEAD_DIM
    assert in_w % qw == 0 and in_w % kw == 0 and q_off % qw == 0 and k_off % kw == 0 and v_off % kw == 0
    view = act.reshape(n, l, dil * in_w)
    qs, ks = in_w // qw, in_w // kw
    qb, kb, vb = q_off // qw, k_off // kw, v_off // kw
    prev = lambda b: jnp.maximum(b - 1, 0)
    in_specs = [
        pl.BlockSpec((None, BLOCK, qw), lambda i, r, b: (i, b, qb + r * qs)),
        pl.BlockSpec((None, BLOCK, kw), lambda i, r, b: (i, prev(b), kb + r * ks)),
        pl.BlockSpec((None, BLOCK, kw), lambda i, r, b: (i, b, kb + r * ks)),
        pl.BlockSpec((None, BLOCK, kw), lambda i, r, b: (i, prev(b), vb + r * ks)),
        pl.BlockSpec((None, BLOCK, kw), lambda i, r, b: (i, b, vb + r * ks)),
        pl.BlockSpec(bias.shape, lambda i, r, b: (0, 0, 0)),
    ]
    args = [view, view, view, view, view, bias]
    has_sink = sink is not None
    if has_sink:
        in_specs = [pl.BlockSpec(memory_space=pltpu.SMEM)] + in_specs
        args = [sink.astype(F32)] + args
    out_specs = [pl.BlockSpec((None, BLOCK, qw), lambda i, r, b: (i, b, r))]
    out_shape = [jax.ShapeDtypeStruct((n, l, dil * qw), BF16)]
    if want_lse:
        out_specs.append(pl.BlockSpec((None, BLOCK, qw), lambda i, r, b: (i, b, r)))
        out_shape.append(jax.ShapeDtypeStruct((n, l, dil * qw), F32))
    outs = pl.pallas_call(
        functools.partial(_band_attn_kernel, kv_heads=kv_heads, grp=grp, has_sink=has_sink, want_lse=want_lse),
        grid=(n, dil, nb),
        in_specs=in_specs,
        out_specs=out_specs,
        out_shape=out_shape,
        compiler_params=_cparams(("arbitrary", "arbitrary", "arbitrary")),
        name=f"band_attn_d{dil}",
    )(*args)
    return [o.reshape(n * t, qw) for o in outs]


def _decode_one(q, knew, vnew, cache_ref, bias_c, bias_n, sink):
    kc = cache_ref[:, :, 0]
    vc = cache_ref[:, :, 1]
    s_c = jnp.sum(q[:, None] * kc, axis=-1, keepdims=True) + bias_c[None]
    s_n = jnp.sum(q * knew, axis=-1, keepdims=True) + bias_n
    m = jnp.maximum(jnp.max(s_c, axis=1), s_n)
    if sink is not None:
        m = jnp.maximum(m, sink)
    p_c = jnp.exp(s_c - m[:, None])
    p_n = jnp.exp(s_n - m)
    den = jnp.sum(p_c, axis=1) + p_n
    if sink is not None:
        den = den + jnp.exp(sink - m)
    o = (jnp.sum(p_c * vc, axis=1) + p_n * vnew) / den
    return o, m + jnp.log(den)


def _decode_attn_kernel(qa_ref, kna_ref, vna_ref, qb_ref, knb_ref, vnb_ref,
                        ca_ref, c1_ref, c2_ref, c3_ref,
                        ba_ref, ba0_ref, bb_ref, bb0_ref, sink_ref,
                        oa_ref, ob_ref, lse_ref):
    kna, vna = kna_ref[...], vna_ref[...]
    for g in range(A_GROUP):
        o, _ = _decode_one(qa_ref[:, g], kna, vna, ca_ref, ba_ref[g], ba0_ref[g], sink_ref[g])
        oa_ref[:, g] = o
    for p, c_ref in enumerate((c1_ref, c2_ref, c3_ref)):
        o, lse = _decode_one(qb_ref[:, p], knb_ref[:, p], vnb_ref[:, p], c_ref, bb_ref[p], bb0_ref[p], None)
        ob_ref[:, p] = o
        lse_ref[:, p] = jnp.broadcast_to(lse, o.shape)


def _decode_attn(act_s, kv_s, caches, rel_bias, sinks, nb):
    n = act_s.shape[0]
    hd = HEAD_DIM
    a32 = act_s.astype(F32)
    qa = a32[:, OFF_QA:OFF_KA].reshape(n, A_KV_HEADS, A_GROUP, hd).transpose(0, 2, 1, 3)
    qb = a32[:, OFF_QB:OFF_KB].reshape(n, len(B_PATTERNS), B_HEADS_PER_PATTERN, hd)
    kna = kv_s[:, 0:KA_W].reshape(n, A_KV_HEADS, hd)
    vna = kv_s[:, KA_W:2 * KA_W].reshape(n, A_KV_HEADS, hd)
    knb = kv_s[:, 2 * KA_W:2 * KA_W + QB_W].reshape(n, len(B_PATTERNS), B_HEADS_PER_PATTERN, hd)
    vnb = kv_s[:, 2 * KA_W + QB_W:].reshape(n, len(B_PATTERNS), B_HEADS_PER_PATTERN, hd)

    table_a = rel_bias[:, :A_HEADS]
    ba, ba0 = _decode_bias(table_a, 1, 1)
    ba = ba.reshape(BLOCK, A_KV_HEADS, A_GROUP, 1).transpose(2, 0, 1, 3)
    ba0 = ba0.reshape(1, A_KV_HEADS, A_GROUP, 1).transpose(2, 0, 1, 3)
    sink = sinks.astype(F32).reshape(A_KV_HEADS, A_GROUP, 1).transpose(1, 0, 2)[:, None]
    bbs, bb0s = [], []
    for p, (win, dil) in enumerate(B_PATTERNS):
        lo = A_HEADS + p * B_HEADS_PER_PATTERN
        b, b0 = _decode_bias(rel_bias[:, lo:lo + B_HEADS_PER_PATTERN], dil, 0)
        bbs.append(b)
        bb0s.append(b0)
    bb, bb0 = jnp.stack(bbs), jnp.stack(bb0s)

    cache_views, cache_specs = [], []
    for c, dil in zip(caches, (1,) + tuple(d for _, d in B_PATTERNS)):
        w, heads = c.shape[1], c.shape[3]
        assert w == BLOCK * dil
        v = c.reshape(n, BLOCK, dil, 2, heads, hd)
        cache_views.append(v)
        cache_specs.append(pl.BlockSpec((nb, BLOCK, None, 2, heads, hd), lambda i: (i, 0, 0, 0, 0, 0)))

    def full(a):
        nd = a.ndim
        return pl.BlockSpec(a.shape, lambda i: (0,) * nd)

    def rows(a):
        nd = a.ndim
        return pl.BlockSpec((nb,) + a.shape[1:], lambda i: (i,) + (0,) * (nd - 1))

    small = [qa, kna, vna, qb, knb, vnb]
    consts = [ba, ba0, bb, bb0, sink]
    oa, ob, lse = pl.pallas_call(
        _decode_attn_kernel,
        grid=(n // nb,),
        in_specs=[rows(a) for a in small] + cache_specs + [full(a) for a in consts],
        out_specs=[pl.BlockSpec((nb, A_GROUP, A_KV_HEADS, hd), lambda i: (i, 0, 0, 0)),
                   pl.BlockSpec((nb, len(B_PATTERNS), B_HEADS_PER_PATTERN, hd), lambda i: (i, 0, 0, 0)),
                   pl.BlockSpec((nb, len(B_PATTERNS), B_HEADS_PER_PATTERN, hd), lambda i: (i, 0, 0, 0))],
        out_shape=[jax.ShapeDtypeStruct((n, A_GROUP, A_KV_HEADS, hd), F32),
                   jax.ShapeDtypeStruct((n, len(B_PATTERNS), B_HEADS_PER_PATTERN, hd), F32),
                   jax.ShapeDtypeStruct((n, len(B_PATTERNS), B_HEADS_PER_PATTERN, hd), F32)],
        compiler_params=_cparams(("arbitrary",)),
        name="decode_attn",
    )(*small, *cache_views, *consts)
    oa = oa.transpose(0, 2, 1, 3).reshape(n, QA_W).astype(BF16)
    obs = [ob[:, p].reshape(n, PB_W).astype(BF16) for p in range(len(B_PATTERNS))]
    lses = [lse[:, p].reshape(n, PB_W) for p in range(len(B_PATTERNS))]
    return oa, obs, lses


def _post_attn_kernel(oa_ref, o1_ref, o2_ref, o3_ref, l1_ref, l2_ref, l3_ref, ga_ref, gb_ref, x_ref,
                      wpa_ref, wpb_ref, wout_ref, ln2_ref, wr_ref, br_ref,
                      h_ref, hn_ref, gates_ref):
    l1, l2, l3 = l1_ref[...], l2_ref[...], l3_ref[...]
    m = jnp.maximum(jnp.maximum(l1, l2), l3)
    a1, a2, a3 = jnp.exp(l1 - m), jnp.exp(l2 - m), jnp.exp(l3 - m)
    ob = (a1 * o1_ref[...].astype(F32) + a2 * o2_ref[...].astype(F32) + a3 * o3_ref[...].astype(F32)) / (a1 + a2 + a3)
    ya = jnp.dot(oa_ref[...], wpa_ref[...], preferred_element_type=F32)
    yb = jnp.dot(ob.astype(BF16), wpb_ref[...], preferred_element_type=F32)
    merged = ga_ref[...].astype(F32) * ya + gb_ref[...].astype(F32) * yb
    h = x_ref[...] + jnp.dot(merged.astype(BF16), wout_ref[...], preferred_element_type=F32)
    h_ref[...] = h
    hn = (h * lax.rsqrt(jnp.mean(h * h, axis=-1, keepdims=True) + EPS)) * ln2_ref[...]
    hn_ref[...] = hn.astype(BF16)

    logits = jnp.dot(hn, wr_ref[...], preferred_element_type=F32, precision=lax.Precision.HIGHEST) + br_ref[...]
    lane = lax.broadcasted_iota(jnp.int32, logits.shape, 1)
    is_grp = (lane >= N_EXPERTS) & (lane < N_EXPERTS + MOE_GROUPS)
    lg = jnp.where(is_grp, logits, NEG)
    gmax = jnp.max(lg, axis=-1, keepdims=True)
    g_lane = jnp.min(jnp.where(lg == gmax, lane, LANES), axis=-1, keepdims=True)
    p_g = 1.0 / jnp.sum(jnp.where(is_grp, jnp.exp(lg - gmax), 0.0), axis=-1, keepdims=True)
    e_lo = (g_lane - N_EXPERTS) * EXPERTS_PER_GROUP
    in_grp = (lane >= e_lo) & (lane < e_lo + EXPERTS_PER_GROUP)
    le = jnp.where(in_grp, logits, NEG)
    v1 = jnp.max(le, axis=-1, keepdims=True)
    i1 = jnp.min(jnp.where(le == v1, lane, LANES), axis=-1, keepdims=True)
    le2 = jnp.where(lane == i1, NEG, le)
    v2 = jnp.max(le2, axis=-1, keepdims=True)
    i2 = jnp.min(jnp.where(le2 == v2, lane, LANES), axis=-1, keepdims=True)
    e2 = jnp.exp(v2 - v1)
    w1 = p_g / (1.0 + e2)
    w2 = p_g * e2 / (1.0 + e2)
    gates_ref[...] = jnp.where(lane == i1, w1, 0.0) + jnp.where(lane == i2, w2, 0.0)


def _post_attn(oa, obs, lses, act, x2d, wpa, wpb, wout, ln2, wr, br, tm):
    m, d = x2d.shape
    in_w = act.shape[1]
    assert OFF_GA % d == 0 and in_w == OFF_GA + 2 * d
    ga_blk = OFF_GA // d

    def tile(w, col=0):
        return pl.BlockSpec((tm, w), lambda i: (i, col))

    def full(a):
        return pl.BlockSpec(a.shape, lambda i: (0, 0))

    weights = [wpa, wpb, wout, ln2.reshape(1, d), wr, br]
    return pl.pallas_call(
        _post_attn_kernel,
        grid=(m // tm,),
        in_specs=[tile(QA_W)] + [tile(PB_W)] * 6 + [tile(d, ga_blk), tile(d, ga_blk + 1), tile(d)]
                 + [full(w) for w in weights],
        out_specs=[tile(d), tile(d), tile(LANES)],
        out_shape=[jax.ShapeDtypeStruct((m, d), F32),
                   jax.ShapeDtypeStruct((m, d), BF16),
                   jax.ShapeDtypeStruct((m, LANES), F32)],
        compiler_params=_cparams(("arbitrary",)),
        name="post_attn",
    )(oa, *obs, *lses, act, act, x2d, *weights)


def _moe_dense_kernel(hn_ref, gates_ref, h_ref, w1_ref, w3_ref, w2_ref, lnf_ref, y_ref, acc_ref, *, final_norm):
    e = pl.program_id(1)

    @pl.when(e == 0)
    def _():
        acc_ref[...] = jnp.zeros_like(acc_ref)

    x = hn_ref[...]
    a = jnp.dot(x, w1_ref[...].astype(BF16), preferred_element_type=F32)
    b = jnp.dot(x, w3_ref[...].astype(BF16), preferred_element_type=F32)
    gates = gates_ref[...]
    lane = lax.broadcasted_iota(jnp.int32, gates.shape, 1)
    gate = jnp.sum(jnp.where(lane == e, gates, 0.0), axis=-1, keepdims=True)
    hh = (a / (1.0 + jnp.exp(-a))) * b * gate
    acc_ref[...] += jnp.dot(hh.astype(BF16), w2_ref[...].astype(BF16), preferred_element_type=F32)

    @pl.when(e == pl.num_programs(1) - 1)
    def _():
        y = h_ref[...] + acc_ref[...]
        if final_norm:
            y = (y * lax.rsqrt(jnp.mean(y * y, axis=-1, keepdims=True) + EPS)) * lnf_ref[...]
        y_ref[...] = y


def _moe_dense(hn, gates, h, w1, w3, w2, lnf, tm, final_norm):
    m, d = h.shape
    ne, _, f = w1.shape
    return pl.pallas_call(
        functools.partial(_moe_dense_kernel, final_norm=final_norm),
        grid=(m // tm, ne),
        in_specs=[pl.BlockSpec((tm, d), lambda i, e: (i, 0)),
                  pl.BlockSpec((tm, LANES), lambda i, e: (i, 0)),
                  pl.BlockSpec((tm, d), lambda i, e: (i, 0)),
                  pl.BlockSpec((None, d, f), lambda i, e: (e, 0, 0)),
                  pl.BlockSpec((None, d, f), lambda i, e: (e, 0, 0)),
                  pl.BlockSpec((None, f, d), lambda i, e: (e, 0, 0)),
                  pl.BlockSpec((1, d), lambda i, e: (0, 0))],
        out_specs=pl.BlockSpec((tm, d), lambda i, e: (i, 0)),
        out_shape=jax.ShapeDtypeStruct((m, d), F32),
        scratch_shapes=[pltpu.VMEM((tm, d), F32)],
        compiler_params=_cparams(("arbitrary", "arbitrary")),
        name="moe_dense",
    )(hn, gates, h, w1, w3, w2, lnf.reshape(1, d))


def _roll_kernel(*refs, n_caches):
    srcs, news, dsts = refs[:n_caches], refs[n_caches:2 * n_caches], refs[2 * n_caches:3 * n_caches]
    sem = refs[3 * n_caches]
    copies = []
    for i in range(n_caches):
        w = srcs[i].shape[1]
        copies.append(pltpu.make_async_copy(srcs[i].at[:, pl.ds(1, w - 1)], dsts[i].at[:, pl.ds(0, w - 1)],
                                            sem.at[2 * i]))
        copies.append(pltpu.make_async_copy(news[i], dsts[i].at[:, pl.ds(w - 1, 1)], sem.at[2 * i + 1]))
    for c in copies:
        c.start()
    for c in copies:
        c.wait()


def _roll_caches(caches, new_rows):
    nc = len(caches)
    any_spec = pl.BlockSpec(memory_space=pl.ANY)
    return pl.pallas_call(
        functools.partial(_roll_kernel, n_caches=nc),
        in_specs=[any_spec] * (2 * nc),
        out_specs=[any_spec] * nc,
        out_shape=[jax.ShapeDtypeStruct(c.shape, c.dtype) for c in caches],
        scratch_shapes=[pltpu.SemaphoreType.DMA((2 * nc,))],
        name="roll_caches",
    )(*caches, *new_rows)


def _split_kv(kv, n, t):
    hd = HEAD_DIM
    kv = kv.reshape(n, t, KV_F32_W)
    ka, va = kv[..., 0:KA_W], kv[..., KA_W:2 * KA_W]
    out = [jnp.stack([ka, va], axis=2).reshape(n, t, 2, A_KV_HEADS, hd)]
    kb, vb = kv[..., 2 * KA_W:2 * KA_W + QB_W], kv[..., 2 * KA_W + QB_W:]
    for p in range(len(B_PATTERNS)):
        sl = slice(p * PB_W, (p + 1) * PB_W)
        out.append(jnp.stack([kb[..., sl], vb[..., sl]], axis=2).reshape(n, t, 2, B_HEADS_PER_PATTERN, hd))
    return out


def _layer(xp, xs, caches, rel_bias, ln1, w_in, sinks, w_pa, w_pb, w_out, ln2, w_rg, b_rg, w_re, b_re,
           w1, w3, w2, lnf, final_norm):
    n, t, d = xp.shape
    ns = xs.shape[0]
    assert xs.shape[1] == 1
    w_in_b = w_in.astype(BF16)
    wpa, wpb, wout = w_pa.astype(BF16), w_pb.astype(BF16), w_out.astype(BF16)
    wr = jnp.zeros((d, LANES), F32).at[:, :N_EXPERTS].set(w_re).at[:, N_EXPERTS:N_EXPERTS + MOE_GROUPS].set(w_rg)
    br = jnp.zeros((1, LANES), F32).at[0, :N_EXPERTS].set(b_re).at[0, N_EXPERTS:N_EXPERTS + MOE_GROUPS].set(b_rg)
    windows = (A_WINDOW,) + tuple(w for w, _ in B_PATTERNS)

    xp2 = xp.reshape(n * t, d)
    act, kv = _in_proj(xp2, ln1, w_in_b, 256)
    bias_a = _band_bias(rel_bias[:, :A_HEADS], A_WINDOW - 1, 1)
    (oa,) = _band_attn(act, bias_a, sinks, n=n, t=t, dil=1, q_off=OFF_QA, k_off=OFF_KA, v_off=OFF_VA,
                       kv_heads=A_KV_HEADS, grp=A_GROUP, want_lse=False)
    obs, lses = [], []
    for p, (win, dil) in enumerate(B_PATTERNS):
        lo = A_HEADS + p * B_HEADS_PER_PATTERN
        bias_p = _band_bias(rel_bias[:, lo:lo + B_HEADS_PER_PATTERN], win // dil, dil)
        o, lse = _band_attn(act, bias_p, None, n=n, t=t, dil=dil, q_off=OFF_QB + p * PB_W,
                            k_off=OFF_KB + p * PB_W, v_off=OFF_VB + p * PB_W,
                            kv_heads=B_HEADS_PER_PATTERN, grp=1, want_lse=True)
        obs.append(o)
        lses.append(lse)
    h, hn, gates = _post_attn(oa, obs, lses, act, xp2, wpa, wpb, wout, ln2, wr, br, 256)
    yp = _moe_dense(hn, gates, h, w1, w3, w2, lnf, 512, final_norm).reshape(n, t, d)
    st_p = [s[:, t - min(w, t):] for s, w in zip(_split_kv(kv, n, t), windows)]

    xs2 = xs.reshape(ns, d)
    act_s, kv_s = _in_proj(xs2, ln1, w_in_b, ns)
    oa_s, obs_s, lses_s = _decode_attn(act_s, kv_s, caches, rel_bias, sinks, 2)
    h_s, hn_s, gates_s = _post_attn(oa_s, obs_s, lses_s, act_s, xs2, wpa, wpb, wout, ln2, wr, br, ns)
    ys = _moe_dense(hn_s, gates_s, h_s, w1, w3, w2, lnf, ns, final_norm).reshape(ns, 1, d)
    st_s = _roll_caches(list(caches), _split_kv(kv_s, ns, 1))
    return yp, ys, st_p, st_s


def kernel(x_prompt, x_sample, cache_a_kv, cache_b1_kv, cache_b2_kv, cache_b3_kv, rel_bias, ln1_g, w_in, sinks,
           w_pa, w_pb, w_out, ln2_g, w_rg, b_rg, w_re, b_re, w1, w3, w2, lnf_g):
    depth = w_in.shape[0]
    xp, xs = x_prompt, x_sample
    new_p = [[] for _ in range(4)]
    new_s = [[] for _ in range(4)]
    for l in range(depth):
        caches = (cache_a_kv[l], cache_b1_kv[l], cache_b2_kv[l], cache_b3_kv[l])
        xp, xs, st_p, st_s = _layer(xp, xs, caches, rel_bias, ln1_g[l], w_in[l], sinks[l], w_pa[l], w_pb[l],
                                    w_out[l], ln2_g[l], w_rg[l], b_rg[l], w_re[l], b_re[l], w1[l], w3[l], w2[l],
                                    lnf_g, l == depth - 1)
        for i in range(4):
            new_p[i].append(st_p[i])
            new_s[i].append(st_s[i])
    if depth == 0:
        raise ValueError("depth must be positive")
    a_p, b1_p, b2_p, b3_p = [jnp.stack(v) for v in new_p]
    a_s, b1_s, b2_s, b3_s = [jnp.stack(v) for v in new_s]
    return (xp, xs, a_p, a_s, b1_p, b1_s, b2_p, b2_s, b3_p, b3_s)
```

```python
import functools
import math

import numpy as np
import jax
import jax.numpy as jnp
from jax import lax
from jax.experimental import pallas as pl
from jax.experimental.pallas import tpu as pltpu

F32 = jnp.float32
BF16 = jnp.bfloat16

HEAD_DIM = 64
A_HEADS = 8
A_KV_HEADS = 2
A_GROUP = A_HEADS // A_KV_HEADS
A_WINDOW = 128
B_PATTERNS = ((128, 1), (512, 4), (2048, 16))
N_PAT = len(B_PATTERNS)
B_HEADS_PER_PATTERN = 4
B_HEADS = B_HEADS_PER_PATTERN * N_PAT
BLOCK = 128
NUM_BUCKETS = 32
MAX_DISTANCE = 2048
MOE_GROUPS = 4
EXPERTS_PER_GROUP = 8
N_EXPERTS = MOE_GROUPS * EXPERTS_PER_GROUP
EPS = 1e-6
NEG = -1e30
LANES = 128
Q_SCALE = HEAD_DIM ** -0.5

QA_W = A_HEADS * HEAD_DIM
KA_W = A_KV_HEADS * HEAD_DIM
QB_W = B_HEADS * HEAD_DIM
PB_W = B_HEADS_PER_PATTERN * HEAD_DIM
QKV_W = 3 * PB_W
OFF_KA = QA_W
OFF_QB = OFF_KA + 2 * KA_W
OFF_KB = OFF_QB + QB_W
OFF_VB = OFF_KB + QB_W
OFF_GA = OFF_VB + QB_W
P_QA = 2048
P_KA = P_QA + QA_W
P_VA = P_KA + KA_W
P_B = P_VA + KA_W
ACT_W = P_B + QKV_W
KVT_ROWS = 2 * KA_W + 2 * QB_W

VMEM_LIMIT = 56 * 1024 * 1024


def _cparams(sem):
    return pltpu.CompilerParams(dimension_semantics=sem, vmem_limit_bytes=VMEM_LIMIT)


def _bucket_np(dist):
    dist = np.asarray(dist, np.int64)
    max_exact = NUM_BUCKETS // 2
    df = np.maximum(dist, max_exact).astype(np.float64)
    large = max_exact + (np.log(df / max_exact) / math.log(MAX_DISTANCE / max_exact)
                         * (NUM_BUCKETS - max_exact)).astype(np.int64)
    return np.where(dist < max_exact, dist, np.minimum(large, NUM_BUCKETS - 1))


def _table_rows(table_cols, dist, valid):
    onehot = (_bucket_np(dist)[:, None] == np.arange(NUM_BUCKETS)[None, :]).astype(np.float32)
    rows = jnp.einsum("ck,kh->hc", jnp.asarray(onehot), table_cols.astype(F32), precision=lax.Precision.HIGHEST)
    return jnp.where(jnp.asarray(valid)[None, :], rows, NEG)


def _band_bias(table_cols, max_dist, dilation):
    period = 3 * BLOCK
    m = np.arange(period)
    k = np.where(m < 2 * BLOCK, m, m - period)
    dist = BLOCK - k
    valid = (dist >= 0) & (dist <= max_dist) & (m != 2 * BLOCK)
    v = _table_rows(table_cols, np.clip(dist, 0, None) * dilation, valid)
    heads = v.shape[0]
    flat = jnp.tile(v, (1, BLOCK))[:, :BLOCK * (period - 1)]
    return flat.reshape(heads, BLOCK, period - 1)[:, :, :2 * BLOCK]


def _decode_bias(table_cols, width, dilation, first_valid):
    c = np.arange(width)
    valid = (c % dilation == 0) & (c >= first_valid)
    rows = _table_rows(table_cols, width - c, valid)
    self_bias = _table_rows(table_cols, np.zeros((1,), np.int64), np.ones((1,), bool))
    return rows[:, None, :], self_bias[:, None, :]


def _rms(x, g):
    return (x * lax.rsqrt(jnp.mean(x * x, axis=-1, keepdims=True) + EPS)) * g


def _sigmoid(x):
    return 1.0 / (1.0 + jnp.exp(-x))


def _in_proj_kernel(x_ref, g_ref, w_ref, wkvt_ref, act_ref, qkv2_ref, qkv3_ref, kvt_ref, ys_ref, *,
                    tiles_per_seq, tail_tiles):
    tm = x_ref.shape[0]
    g = g_ref[...]
    xb = _rms(x_ref[...], g).astype(BF16)

    def proj(xv, lo, hi):
        return jnp.dot(xv, w_ref[:, lo:hi], preferred_element_type=F32)

    for c in range(0, P_QA, 1024):
        act_ref[:, c:c + 1024] = _sigmoid(proj(xb, OFF_GA + c, OFF_GA + c + 1024)).astype(BF16)
    act_ref[:, P_QA:P_KA] = (proj(xb, 0, OFF_KA) * Q_SCALE).astype(BF16)
    act_ref[:, P_KA:P_B] = proj(xb, OFF_KA, OFF_QB).astype(BF16)

    def qkv_parts(p):
        lo = p * PB_W
        return (proj(xb, OFF_QB + lo, OFF_QB + lo + PB_W) * Q_SCALE,
                proj(xb, OFF_KB + lo, OFF_KB + lo + PB_W),
                proj(xb, OFF_VB + lo, OFF_VB + lo + PB_W))

    for j, part in enumerate(qkv_parts(0)):
        act_ref[:, P_B + j * PB_W:P_B + (j + 1) * PB_W] = part.astype(BF16)

    for p, out_ref in ((1, qkv2_ref), (2, qkv3_ref)):
        dil = B_PATTERNS[p][1]
        rows = tm // dil
        for j, part in enumerate(qkv_parts(p)):
            for c in range(PB_W // LANES):
                ys_ref[j * (PB_W // LANES) + c] = part[:, c * LANES:(c + 1) * LANES]
        for c in range(QKV_W // LANES):
            cs = slice(c * LANES, (c + 1) * LANES)
            for r in range(dil):
                out_ref[r, :, cs] = ys_ref[c, pl.ds(r, rows, stride=dil), :].astype(BF16)

    @pl.when(pl.program_id(0) % tiles_per_seq >= tiles_per_seq - tail_tiles)
    def _():
        kvt_ref[...] = lax.dot_general(wkvt_ref[...], xb, (((1,), (1,)), ((), ())), preferred_element_type=F32)


def _in_proj_prompt(x2d, ln_g, w_bf, w_kvt, n, t, tm):
    m, d = x2d.shape
    tps = t // tm
    tail = min(max(w for w, _ in B_PATTERNS), t)
    assert t % tm == 0 and tail % tm == 0 and all(tm % (16 * dl) == 0 for _, dl in B_PATTERNS)
    tail_tiles = tail // tm
    d2, d3 = B_PATTERNS[1][1], B_PATTERNS[2][1]
    return pl.pallas_call(
        functools.partial(_in_proj_kernel, tiles_per_seq=tps, tail_tiles=tail_tiles),
        grid=(m // tm,),
        in_specs=[pl.BlockSpec((tm, d), lambda i: (i, 0)),
                  pl.BlockSpec((1, d), lambda i: (0, 0)),
                  pl.BlockSpec(w_bf.shape, lambda i: (0, 0)),
                  pl.BlockSpec(w_kvt.shape, lambda i: (0, 0))],
        out_specs=[pl.BlockSpec((tm, ACT_W), lambda i: (i, 0)),
                   pl.BlockSpec((None, d2, tm // d2, QKV_W), lambda i: (i // tps, 0, i % tps, 0)),
                   pl.BlockSpec((None, d3, tm // d3, QKV_W), lambda i: (i // tps, 0, i % tps, 0)),
                   pl.BlockSpec((None, KVT_ROWS, tm),
                                lambda i: (i // tps, 0, jnp.maximum(i % tps - (tps - tail_tiles), 0)))],
        out_shape=[jax.ShapeDtypeStruct((m, ACT_W), BF16),
                   jax.ShapeDtypeStruct((n, d2, t // d2, QKV_W), BF16),
                   jax.ShapeDtypeStruct((n, d3, t // d3, QKV_W), BF16),
                   jax.ShapeDtypeStruct((n, KVT_ROWS, tail), F32)],
        scratch_shapes=[pltpu.VMEM((QKV_W // LANES, tm, LANES), F32)],
        compiler_params=_cparams(("arbitrary",)),
        name="in_proj",
    )(x2d, ln_g.reshape(1, d), w_bf, w_kvt)


def _in_proj_sample_kernel(x_ref, g_ref, w_ref, y_ref):
    xb = _rms(x_ref[...], g_ref[...]).astype(BF16)
    y_ref[:, :OFF_GA] = jnp.dot(xb, w_ref[:, :OFF_GA], preferred_element_type=F32)
    for lo, hi in ((0, OFF_KA), (OFF_QB, OFF_KB)):
        y_ref[:, lo:hi] = y_ref[:, lo:hi] * Q_SCALE
    for lo in range(OFF_GA, w_ref.shape[1], 1024):
        y_ref[:, lo:lo + 1024] = _sigmoid(jnp.dot(xb, w_ref[:, lo:lo + 1024], preferred_element_type=F32))


def _in_proj_sample(x2d, ln_g, w_bf):
    m, d = x2d.shape
    return pl.pallas_call(
        _in_proj_sample_kernel,
        grid=(1,),
        in_specs=[pl.BlockSpec((m, d), lambda i: (0, 0)),
                  pl.BlockSpec((1, d), lambda i: (0, 0)),
                  pl.BlockSpec(w_bf.shape, lambda i: (0, 0))],
        out_specs=pl.BlockSpec((m, w_bf.shape[1]), lambda i: (0, 0)),
        out_shape=jax.ShapeDtypeStruct((m, w_bf.shape[1]), F32),
        compiler_params=_cparams(("arbitrary",)),
        name="in_proj_sample",
    )(x2d, ln_g.reshape(1, d), w_bf)


def _band_attn_kernel(*refs, kv_heads, grp, has_sink, want_lse):
    if has_sink:
        sink_ref, refs = refs[0], refs[1:]
    q_ref, kp_ref, kc_ref, vp_ref, vc_ref, bias_ref, o_ref = refs[:7]
    lse_ref = refs[7] if want_lse else None
    blk = pl.program_id(2)
    hd = HEAD_DIM
    nt = (((1,), (1,)), ((), ()))
    for kv in range(kv_heads):
        ks = slice(kv * hd, (kv + 1) * hd)
        kp, kc, vp, vc = kp_ref[:, ks], kc_ref[:, ks], vp_ref[:, ks], vc_ref[:, ks]
        for g in range(grp):
            h = kv * grp + g
            hs = slice(h * hd, (h + 1) * hd)
            q = q_ref[:, hs]
            sp = lax.dot_general(q, kp, nt, preferred_element_type=F32) + bias_ref[h, :, 0:BLOCK]
            sc = lax.dot_general(q, kc, nt, preferred_element_type=F32) + bias_ref[h, :, BLOCK:2 * BLOCK]
            sp = jnp.where(blk > 0, sp, NEG)
            m = jnp.maximum(jnp.max(sp, axis=-1, keepdims=True), jnp.max(sc, axis=-1, keepdims=True))
            if has_sink:
                m = jnp.maximum(m, sink_ref[h])
            pp = jnp.exp(sp - m)
            pc = jnp.exp(sc - m)
            den = jnp.sum(pp, axis=-1, keepdims=True) + jnp.sum(pc, axis=-1, keepdims=True)
            if has_sink:
                den = den + jnp.exp(sink_ref[h] - m)
            o = (jnp.dot(pp.astype(BF16), vp, preferred_element_type=F32)
                 + jnp.dot(pc.astype(BF16), vc, preferred_element_type=F32))
            o_ref[:, hs] = (o / den).astype(o_ref.dtype)
            if want_lse:
                lse_ref[:, hs] = jnp.broadcast_to(m + jnp.log(den), (BLOCK, hd))


def _band_attn(src, bias, sink, *, q_off, k_off, v_off, kv_heads, grp, want_lse):
    n, dil, l, cols = src.shape
    assert l % BLOCK == 0
    nb = l // BLOCK
    qw = kv_heads * grp * HEAD_DIM
    kw = kv_heads * HEAD_DIM
    assert q_off % qw == 0 and k_off % kw == 0 and v_off % kw == 0
    qb, kb, vb = q_off // qw, k_off // kw, v_off // kw
    prev = lambda b: jnp.maximum(b - 1, 0)
    in_specs = [
        pl.BlockSpec((None, None, BLOCK, qw), lambda i, r, b: (i, r, b, qb)),
        pl.BlockSpec((None, None, BLOCK, kw), lambda i, r, b: (i, r, prev(b), kb)),
        pl.BlockSpec((None, None, BLOCK, kw), lambda i, r, b: (i, r, b, kb)),
        pl.BlockSpec((None, None, BLOCK, kw), lambda i, r, b: (i, r, prev(b), vb)),
        pl.BlockSpec((None, None, BLOCK, kw), lambda i, r, b: (i, r, b, vb)),
        pl.BlockSpec(bias.shape, lambda i, r, b: (0, 0, 0)),
    ]
    args = [src, src, src, src, src, bias]
    has_sink = sink is not None
    if has_sink:
        in_specs = [pl.BlockSpec(memory_space=pltpu.SMEM)] + in_specs
        args = [sink.astype(F32)] + args
    out_specs = [pl.BlockSpec((None, None, BLOCK, qw), lambda i, r, b: (i, r, b, 0))]
    out_shape = [jax.ShapeDtypeStruct((n, dil, l, qw), BF16)]
    if want_lse:
        out_specs.append(pl.BlockSpec((None, None, BLOCK, qw), lambda i, r, b: (i, r, b, 0)))
        out_shape.append(jax.ShapeDtypeStruct((n, dil, l, qw), F32))
    return pl.pallas_call(
        functools.partial(_band_attn_kernel, kv_heads=kv_heads, grp=grp, has_sink=has_sink, want_lse=want_lse),
        grid=(n, dil, nb),
        in_specs=in_specs,
        out_specs=out_specs,
        out_shape=out_shape,
        compiler_params=_cparams(("arbitrary", "arbitrary", "arbitrary")),
        name=f"band_attn_d{dil}",
    )(*args)


def _attend_cached(q, cache_ref, kv_head, new_ref, bias, self_bias, sink):
    kt = cache_ref[0, kv_head]
    vt = cache_ref[1, kv_head]
    s = jnp.sum(kt * q, axis=0, keepdims=True) + bias
    s_new = jnp.sum(new_ref[0, kv_head] * q, axis=0, keepdims=True) + self_bias
    m = jnp.maximum(jnp.max(s, axis=1, keepdims=True), s_new)
    if sink is not None:
        m = jnp.maximum(m, sink)
    p = jnp.exp(s - m)
    p_new = jnp.exp(s_new - m)
    den = jnp.sum(p, axis=1, keepdims=True) + p_new
    if sink is not None:
        den = den + jnp.exp(sink - m)
    o = (jnp.sum(vt * p, axis=1, keepdims=True) + new_ref[1, kv_head] * p_new) / den
    return o, m + jnp.log(den)


def _roll_in(cache_ref, new_ref, out_ref):
    w = cache_ref.shape[-1]
    for i in range(cache_ref.shape[0]):
        for h in range(cache_ref.shape[1]):
            x = cache_ref[i, h]
            lane = lax.broadcasted_iota(jnp.int32, x.shape, 1)
            out_ref[i, h] = jnp.where(lane == w - 1, new_ref[i, h], pltpu.roll(x, w - 1, 1))


def _sample_mix_kernel(qa_ref, qb_ref, na_ref, n1_ref, n2_ref, n3_ref, ca_ref, c1_ref, c2_ref, c3_ref,
                       ba_ref, sa_ref, sink_ref, b1_ref, b2_ref, b3_ref, sb_ref,
                       oa_ref, ob_ref, lse_ref, ra_ref, r1_ref, r2_ref, r3_ref):
    for h in range(A_HEADS):
        o, _ = _attend_cached(qa_ref[h], ca_ref, h // A_GROUP, na_ref, ba_ref[h], sa_ref[h], sink_ref[h])
        oa_ref[h] = o
    pats = ((c1_ref, n1_ref, b1_ref), (c2_ref, n2_ref, b2_ref), (c3_ref, n3_ref, b3_ref))
    for p, (c_ref, n_ref, b_ref) in enumerate(pats):
        for h in range(B_HEADS_PER_PATTERN):
            o, lse = _attend_cached(qb_ref[p, h], c_ref, h, n_ref, b_ref[h], sb_ref[p, h], None)
            ob_ref[p, h] = o
            lse_ref[p, h] = lse
    _roll_in(ca_ref, na_ref, ra_ref)
    _roll_in(c1_ref, n1_ref, r1_ref)
    _roll_in(c2_ref, n2_ref, r2_ref)
    _roll_in(c3_ref, n3_ref, r3_ref)


def _sample_mix(proj_s, caches, rel_bias, sinks):
    n = proj_s.shape[0]
    hd = HEAD_DIM
    qa = proj_s[:, :OFF_KA].reshape(n, A_HEADS, hd, 1)
    new_a = proj_s[:, OFF_KA:OFF_QB].reshape(n, 2, A_KV_HEADS, hd, 1)
    qb, kb, vb = (proj_s[:, lo:lo + QB_W].reshape(n, N_PAT, B_HEADS_PER_PATTERN, hd, 1)
                  for lo in (OFF_QB, OFF_KB, OFF_VB))
    news = [new_a] + [jnp.stack([kb[:, p], vb[:, p]], axis=1) for p in range(N_PAT)]
    cts = [jnp.transpose(c, (0, 2, 3, 4, 1)) for c in caches]

    ba, sa = _decode_bias(rel_bias[:, :A_HEADS], A_WINDOW, 1, 1)
    sink = sinks.astype(F32).reshape(A_HEADS, 1, 1)
    bbs, sbs = [], []
    for p, (win, dil) in enumerate(B_PATTERNS):
        lo = A_HEADS + p * B_HEADS_PER_PATTERN
        assert caches[1 + p].shape[1] == win == BLOCK * dil
        b, s = _decode_bias(rel_bias[:, lo:lo + B_HEADS_PER_PATTERN], win, dil, 0)
        bbs.append(b)
        sbs.append(s)
    sb = jnp.stack(sbs)

    def per_seq(a):
        nd = a.ndim
        return pl.BlockSpec((None,) + a.shape[1:], lambda i: (i,) + (0,) * (nd - 1))

    def full(a):
        nd = a.ndim
        return pl.BlockSpec(a.shape, lambda i: (0,) * nd)

    seq_in = [qa, qb] + news + cts
    consts = [ba, sa, sink] + bbs + [sb]
    out_shape = [jax.ShapeDtypeStruct((n, A_HEADS, hd, 1), F32),
                 jax.ShapeDtypeStruct((n, N_PAT, B_HEADS_PER_PATTERN, hd, 1), F32),
                 jax.ShapeDtypeStruct((n, N_PAT, B_HEADS_PER_PATTERN, 1, 1), F32)]
    out_shape += [jax.ShapeDtypeStruct(c.shape, c.dtype) for c in cts]
    outs = pl.pallas_call(
        _sample_mix_kernel,
        grid=(n,),
        in_specs=[per_seq(a) for a in seq_in] + [full(a) for a in consts],
        out_specs=[per_seq(s) for s in out_shape],
        out_shape=out_shape,
        compiler_params=_cparams(("arbitrary",)),
        name="sample_mix",
    )(*seq_in, *consts)
    oa = outs[0].reshape(n, QA_W)
    ob = outs[1].reshape(n, N_PAT, PB_W)
    lse = outs[2].reshape(n, N_PAT, B_HEADS_PER_PATTERN)
    rolled = [jnp.transpose(r, (0, 4, 1, 2, 3)) for r in outs[3:]]
    return oa, ob, lse, rolled


def _post_attn_kernel(oa_ref, o1_ref, o2_ref, o3_ref, l1_ref, l2_ref, l3_ref, ga_ref, gb_ref, x_ref,
                      wpa_ref, wpb_ref, wout_ref, ln2_ref, wr_ref, br_ref,
                      h_ref, hn_ref, gates_ref, scr_ref):
    tm = x_ref.shape[0]
    chunks = PB_W // LANES

    def token_major(ref, slot):
        dil = ref.shape[0]
        if dil == 1:
            return ref[0].astype(F32)
        for c in range(chunks):
            for r in range(dil):
                scr_ref[slot * chunks + c, pl.ds(r, tm // dil, stride=dil), :] = (
                    ref[r, :, c * LANES:(c + 1) * LANES].astype(F32))
        return jnp.concatenate([scr_ref[slot * chunks + c] for c in range(chunks)], axis=1)

    o1, o2, o3 = (token_major(r, s) for s, r in enumerate((o1_ref, o2_ref, o3_ref)))
    l1, l2, l3 = (token_major(r, 3 + s) for s, r in enumerate((l1_ref, l2_ref, l3_ref)))
    m = jnp.maximum(jnp.maximum(l1, l2), l3)
    a1, a2, a3 = jnp.exp(l1 - m), jnp.exp(l2 - m), jnp.exp(l3 - m)
    ob = (a1 * o1 + a2 * o2 + a3 * o3) / (a1 + a2 + a3)
    ya = jnp.dot(oa_ref[...], wpa_ref[...], preferred_element_type=F32)
    yb = jnp.dot(ob.astype(BF16), wpb_ref[...], preferred_element_type=F32)
    merged = ga_ref[...].astype(F32) * ya + gb_ref[...].astype(F32) * yb
    h = x_ref[...] + jnp.dot(merged.astype(BF16), wout_ref[...], preferred_element_type=F32)
    h_ref[...] = h
    hn = _rms(h, ln2_ref[...])
    hn_ref[...] = hn.astype(BF16)

    logits = jnp.dot(hn, wr_ref[...], preferred_element_type=F32, precision=lax.Precision.HIGHEST) + br_ref[...]
    lane = lax.broadcasted_iota(jnp.int32, logits.shape, 1)
    is_grp = (lane >= N_EXPERTS) & (lane < N_EXPERTS + MOE_GROUPS)
    lg = jnp.where(is_grp, logits, NEG)
    gmax = jnp.max(lg, axis=-1, keepdims=True)
    g_lane = jnp.min(jnp.where(lg == gmax, lane, LANES), axis=-1, keepdims=True)
    p_g = 1.0 / jnp.sum(jnp.where(is_grp, jnp.exp(lg - gmax), 0.0), axis=-1, keepdims=True)
    e_lo = (g_lane - N_EXPERTS) * EXPERTS_PER_GROUP
    in_grp = (lane >= e_lo) & (lane < e_lo + EXPERTS_PER_GROUP)
    le = jnp.where(in_grp, logits, NEG)
    v1 = jnp.max(le, axis=-1, keepdims=True)
    i1 = jnp.min(jnp.where(le == v1, lane, LANES), axis=-1, keepdims=True)
    le2 = jnp.where(lane == i1, NEG, le)
    v2 = jnp.max(le2, axis=-1, keepdims=True)
    i2 = jnp.min(jnp.where(le2 == v2, lane, LANES), axis=-1, keepdims=True)
    e2 = jnp.exp(v2 - v1)
    w1 = p_g / (1.0 + e2)
    w2 = p_g * e2 / (1.0 + e2)
    gates_ref[...] = jnp.where(lane == i1, w1, 0.0) + jnp.where(lane == i2, w2, 0.0)


def _post_attn(oa, obs, lses, gates_src, ga_blk, x2d, wpa, wpb, wout, ln2, wr, br, tm, tiles_per_seq):
    m, d = x2d.shape
    tps = tiles_per_seq

    def tile(w, col=0):
        return pl.BlockSpec((tm, w), lambda i: (i, col))

    def full(a):
        return pl.BlockSpec(a.shape, lambda i: (0, 0))

    def residue(a):
        dil = a.shape[1]
        return pl.BlockSpec((None, dil, tm // dil, PB_W), lambda i: (i // tps, 0, i % tps, 0))

    weights = [wpa, wpb, wout, ln2.reshape(1, d), wr, br]
    scratch = [pltpu.VMEM((6 * PB_W // LANES, tm, LANES), F32)]
    return pl.pallas_call(
        _post_attn_kernel,
        grid=(m // tm,),
        in_specs=[tile(QA_W)] + [residue(a) for a in obs] + [residue(a) for a in lses]
                 + [tile(d, ga_blk), tile(d, ga_blk + 1), tile(d)] + [full(w) for w in weights],
        out_specs=[tile(d), tile(d), tile(LANES)],
        out_shape=[jax.ShapeDtypeStruct((m, d), F32),
                   jax.ShapeDtypeStruct((m, d), BF16),
                   jax.ShapeDtypeStruct((m, LANES), F32)],
        scratch_shapes=scratch,
        compiler_params=_cparams(("arbitrary",)),
        name="post_attn",
    )(oa, *obs, *lses, gates_src, gates_src, x2d, *weights)


def _moe_dense_kernel(hn_ref, gates_ref, h_ref, w1_ref, w3_ref, w2_ref, lnf_ref, y_ref, acc_ref, *, final_norm):
    e = pl.program_id(1)

    @pl.when(e == 0)
    def _():
        acc_ref[...] = jnp.zeros_like(acc_ref)

    x = hn_ref[...]
    a = jnp.dot(x, w1_ref[...].astype(BF16), preferred_element_type=F32)
    b = jnp.dot(x, w3_ref[...].astype(BF16), preferred_element_type=F32)
    gates = gates_ref[...]
    lane = lax.broadcasted_iota(jnp.int32, gates.shape, 1)
    gate = jnp.sum(jnp.where(lane == e, gates, 0.0), axis=-1, keepdims=True)
    hh = (a * _sigmoid(a)) * b * gate
    acc_ref[...] += jnp.dot(hh.astype(BF16), w2_ref[...].astype(BF16), preferred_element_type=F32)

    @pl.when(e == pl.num_programs(1) - 1)
    def _():
        y = h_ref[...] + acc_ref[...]
        if final_norm:
            y = _rms(y, lnf_ref[...])
        y_ref[...] = y


def _moe_dense(hn, gates, h, w1, w3, w2, lnf, tm, final_norm):
    m, d = h.shape
    ne, _, f = w1.shape
    return pl.pallas_call(
        functools.partial(_moe_dense_kernel, final_norm=final_norm),
        grid=(m // tm, ne),
        in_specs=[pl.BlockSpec((tm, d), lambda i, e: (i, 0)),
                  pl.BlockSpec((tm, LANES), lambda i, e: (i, 0)),
                  pl.BlockSpec((tm, d), lambda i, e: (i, 0)),
                  pl.BlockSpec((None, d, f), lambda i, e: (e, 0, 0)),
                  pl.BlockSpec((None, d, f), lambda i, e: (e, 0, 0)),
                  pl.BlockSpec((None, f, d), lambda i, e: (e, 0, 0)),
                  pl.BlockSpec((1, d), lambda i, e: (0, 0))],
        out_specs=pl.BlockSpec((tm, d), lambda i, e: (i, 0)),
        out_shape=jax.ShapeDtypeStruct((m, d), F32),
        scratch_shapes=[pltpu.VMEM((tm, d), F32)],
        compiler_params=_cparams(("arbitrary", "arbitrary")),
        name="moe_dense",
    )(hn, gates, h, w1, w3, w2, lnf.reshape(1, d))


def _prompt_states(kvt, n, t):
    hd = HEAD_DIM
    tail = kvt.shape[2]

    def state(k_lo, v_lo, rows, heads, win):
        w = min(win, t)
        kv = jnp.stack([kvt[:, k_lo:k_lo + rows, tail - w:], kvt[:, v_lo:v_lo + rows, tail - w:]], axis=1)
        return jnp.transpose(kv.reshape(n, 2, heads, hd, w), (0, 4, 1, 2, 3))

    out = [state(0, KA_W, KA_W, A_KV_HEADS, A_WINDOW)]
    for p, (win, _) in enumerate(B_PATTERNS):
        out.append(state(2 * KA_W + p * PB_W, 2 * KA_W + QB_W + p * PB_W, PB_W, B_HEADS_PER_PATTERN, win))
    return out


def _layer(xp, xs, caches, rel_bias, ln1, w_in, sinks, w_pa, w_pb, w_out, ln2, w_rg, b_rg, w_re, b_re,
           w1, w3, w2, lnf, final_norm):
    n, t, d = xp.shape
    ns = xs.shape[0]
    assert xs.shape[1] == 1 and OFF_GA + 2 * d == w_in.shape[1] and P_QA == 2 * d
    w_bf = w_in.astype(BF16)
    w_kvt = jnp.concatenate([w_in[:, OFF_KA:OFF_QB], w_in[:, OFF_KB:OFF_GA]], axis=1).T.astype(BF16)
    wpa, wpb, wout = w_pa.astype(BF16), w_pb.astype(BF16), w_out.astype(BF16)
    wr = jnp.zeros((d, LANES), F32).at[:, :N_EXPERTS].set(w_re).at[:, N_EXPERTS:N_EXPERTS + MOE_GROUPS].set(w_rg)
    br = jnp.zeros((1, LANES), F32).at[0, :N_EXPERTS].set(b_re).at[0, N_EXPERTS:N_EXPERTS + MOE_GROUPS].set(b_rg)

    tm = 256
    xp2 = xp.reshape(n * t, d)
    act, qkv2, qkv3, kvt = _in_proj_prompt(xp2, ln1, w_bf, w_kvt, n, t, tm)
    act4 = act.reshape(n, 1, t, ACT_W)
    bias_a = _band_bias(rel_bias[:, :A_HEADS], A_WINDOW - 1, 1)
    (oa,) = _band_attn(act4, bias_a, sinks, q_off=P_QA, k_off=P_KA, v_off=P_VA,
                       kv_heads=A_KV_HEADS, grp=A_GROUP, want_lse=False)
    obs, lses = [], []
    for p, (win, dil) in enumerate(B_PATTERNS):
        lo = A_HEADS + p * B_HEADS_PER_PATTERN
        bias_p = _band_bias(rel_bias[:, lo:lo + B_HEADS_PER_PATTERN], win // dil, dil)
        src, base = ((act4, P_B), (qkv2, 0), (qkv3, 0))[p]
        o, lse = _band_attn(src, bias_p, None, q_off=base, k_off=base + PB_W, v_off=base + 2 * PB_W,
                            kv_heads=B_HEADS_PER_PATTERN, grp=1, want_lse=True)
        obs.append(o)
        lses.append(lse)
    h, hn, gates = _post_attn(oa.reshape(n * t, QA_W), obs, lses, act, 0, xp2, wpa, wpb, wout, ln2, wr, br,
                              tm, t // tm)
    yp = _moe_dense(hn, gates, h, w1, w3, w2, lnf, 512, final_norm).reshape(n, t, d)
    st_p = _prompt_states(kvt, n, t)

    xs2 = xs.reshape(ns, d)
    proj_s = _in_proj_sample(xs2, ln1, w_bf)
    oa_s, ob_s, lse_s, st_s = _sample_mix(proj_s, caches, rel_bias, sinks)
    obs_s = [ob_s[:, p].astype(BF16).reshape(1, 1, ns, PB_W) for p in range(N_PAT)]
    lses_s = [jnp.repeat(lse_s[:, p], HEAD_DIM, axis=-1).reshape(1, 1, ns, PB_W) for p in range(N_PAT)]
    gates_s = proj_s[:, OFF_GA:].astype(BF16)
    h_s, hn_s, gt_s = _post_attn(oa_s.astype(BF16), obs_s, lses_s, gates_s, 0, xs2, wpa, wpb, wout, ln2, wr, br,
                                 ns, 1)
    ys = _moe_dense(hn_s, gt_s, h_s, w1, w3, w2, lnf, ns, final_norm).reshape(ns, 1, d)
    return yp, ys, st_p, st_s


def kernel(x_prompt, x_sample, cache_a_kv, cache_b1_kv, cache_b2_kv, cache_b3_kv, rel_bias, ln1_g, w_in, sinks,
           w_pa, w_pb, w_out, ln2_g, w_rg, b_rg, w_re, b_re, w1, w3, w2, lnf_g):
    depth = w_in.shape[0]
    assert depth >= 1
    xp, xs = x_prompt, x_sample
    new_p = [[] for _ in range(4)]
    new_s = [[] for _ in range(4)]
    for l in range(depth):
        caches = (cache_a_kv[l], cache_b1_kv[l], cache_b2_kv[l], cache_b3_kv[l])
        xp, xs, st_p, st_s = _layer(xp, xs, caches, rel_bias, ln1_g[l], w_in[l], sinks[l], w_pa[l], w_pb[l],
                                    w_out[l], ln2_g[l], w_rg[l], b_rg[l], w_re[l], b_re[l], w1[l], w3[l], w2[l],
                                    lnf_g, l == depth - 1)
        for i in range(4):
            new_p[i].append(st_p[i])
            new_s[i].append(st_s[i])
    a_p, b1_p, b2_p, b3_p = [jnp.stack(v) for v in new_p]
    a_s, b1_s, b2_s, b3_s = [jnp.stack(v) for v in new_s]
    return (xp, xs, a_p, a_s, b1_p, b1_s, b2_p, b2_s, b3_p, b3_s)
```

```python
import functools
import math

import numpy as np
import jax
import jax.numpy as jnp
from jax import lax
from jax.experimental import pallas as pl
from jax.experimental.pallas import tpu as pltpu

F32 = jnp.float32
BF16 = jnp.bfloat16

HEAD_DIM = 64
A_HEADS = 8
A_KV_HEADS = 2
A_GROUP = A_HEADS // A_KV_HEADS
A_WINDOW = 128
B_PATTERNS = ((128, 1), (512, 4), (2048, 16))
N_PAT = len(B_PATTERNS)
B_HEADS_PER_PATTERN = 4
B_HEADS = B_HEADS_PER_PATTERN * N_PAT
BLOCK = 128
NUM_BUCKETS = 32
MAX_DISTANCE = 2048
MOE_GROUPS = 4
EXPERTS_PER_GROUP = 8
N_EXPERTS = MOE_GROUPS * EXPERTS_PER_GROUP
EPS = 1e-6
NEG = -1e30
LANES = 128
SUBLANES = 8
ROUTE_I1, ROUTE_I2, ROUTE_W1, ROUTE_W2 = 0, 1, 2, 3
MOE_TM = 256
MOE_SLOTS = 2 * MOE_TM + N_EXPERTS * SUBLANES
MOE_TMX = 256
Q_SCALE = HEAD_DIM ** -0.5

QA_W = A_HEADS * HEAD_DIM
KA_W = A_KV_HEADS * HEAD_DIM
QB_W = B_HEADS * HEAD_DIM
PB_W = B_HEADS_PER_PATTERN * HEAD_DIM
QKV_W = 3 * PB_W
OFF_KA = QA_W
OFF_QB = OFF_KA + 2 * KA_W
OFF_KB = OFF_QB + QB_W
OFF_VB = OFF_KB + QB_W
OFF_GA = OFF_VB + QB_W
P_QA = 2048
P_KA = P_QA + QA_W
P_VA = P_KA + KA_W
P_B = P_VA + KA_W
ACT_W = P_B + QKV_W
KVT_ROWS = 2 * KA_W + 2 * QB_W

VMEM_LIMIT = 56 * 1024 * 1024


def _cparams(sem):
    return pltpu.CompilerParams(dimension_semantics=sem, vmem_limit_bytes=VMEM_LIMIT)


def _bucket_np(dist):
    dist = np.asarray(dist, np.int64)
    max_exact = NUM_BUCKETS // 2
    df = np.maximum(dist, max_exact).astype(np.float64)
    large = max_exact + (np.log(df / max_exact) / math.log(MAX_DISTANCE / max_exact)
                         * (NUM_BUCKETS - max_exact)).astype(np.int64)
    return np.where(dist < max_exact, dist, np.minimum(large, NUM_BUCKETS - 1))


def _table_rows(table_cols, dist, valid):
    onehot = (_bucket_np(dist)[:, None] == np.arange(NUM_BUCKETS)[None, :]).astype(np.float32)
    rows = jnp.einsum("ck,kh->hc", jnp.asarray(onehot), table_cols.astype(F32), precision=lax.Precision.HIGHEST)
    return jnp.where(jnp.asarray(valid)[None, :], rows, NEG)


def _band_bias(table_cols, max_dist, dilation):
    period = 3 * BLOCK
    m = np.arange(period)
    k = np.where(m < 2 * BLOCK, m, m - period)
    dist = BLOCK - k
    valid = (dist >= 0) & (dist <= max_dist) & (m != 2 * BLOCK)
    v = _table_rows(table_cols, np.clip(dist, 0, None) * dilation, valid)
    heads = v.shape[0]
    flat = jnp.tile(v, (1, BLOCK))[:, :BLOCK * (period - 1)]
    return flat.reshape(heads, BLOCK, period - 1)[:, :, :2 * BLOCK]


def _decode_bias(table_cols, width, dilation, first_valid):
    c = np.arange(width)
    valid = (c % dilation == 0) & (c >= first_valid)
    rows = _table_rows(table_cols, width - c, valid)
    self_bias = _table_rows(table_cols, np.zeros((1,), np.int64), np.ones((1,), bool))
    return rows[:, None, :], self_bias[:, None, :]


def _rms(x, g):
    return (x * lax.rsqrt(jnp.mean(x * x, axis=-1, keepdims=True) + EPS)) * g


def _sigmoid(x):
    return 1.0 / (1.0 + jnp.exp(-x))


def _in_proj_kernel(x_ref, g_ref, w_ref, wkvt_ref, act_ref, qkv2_ref, qkv3_ref, kvt_ref, ys_ref, *,
                    tiles_per_seq, tail_tiles):
    tm = x_ref.shape[0]
    g = g_ref[...]
    xb = _rms(x_ref[...], g).astype(BF16)

    def proj(xv, lo, hi):
        return jnp.dot(xv, w_ref[:, lo:hi], preferred_element_type=F32)

    for c in range(0, P_QA, 1024):
        act_ref[:, c:c + 1024] = _sigmoid(proj(xb, OFF_GA + c, OFF_GA + c + 1024)).astype(BF16)
    act_ref[:, P_QA:P_KA] = (proj(xb, 0, OFF_KA) * Q_SCALE).astype(BF16)
    act_ref[:, P_KA:P_B] = proj(xb, OFF_KA, OFF_QB).astype(BF16)

    def qkv_parts(p):
        lo = p * PB_W
        return (proj(xb, OFF_QB + lo, OFF_QB + lo + PB_W) * Q_SCALE,
                proj(xb, OFF_KB + lo, OFF_KB + lo + PB_W),
                proj(xb, OFF_VB + lo, OFF_VB + lo + PB_W))

    for j, part in enumerate(qkv_parts(0)):
        act_ref[:, P_B + j * PB_W:P_B + (j + 1) * PB_W] = part.astype(BF16)

    for p, out_ref in ((1, qkv2_ref), (2, qkv3_ref)):
        dil = B_PATTERNS[p][1]
        rows = tm // dil
        for j, part in enumerate(qkv_parts(p)):
            for c in range(PB_W // LANES):
                ys_ref[j * (PB_W // LANES) + c] = part[:, c * LANES:(c + 1) * LANES]
        for c in range(QKV_W // LANES):
            cs = slice(c * LANES, (c + 1) * LANES)
            for r in range(dil):
                out_ref[r, :, cs] = ys_ref[c, pl.ds(r, rows, stride=dil), :].astype(BF16)

    @pl.when(pl.program_id(0) % tiles_per_seq >= tiles_per_seq - tail_tiles)
    def _():
        kvt_ref[...] = lax.dot_general(wkvt_ref[...], xb, (((1,), (1,)), ((), ())), preferred_element_type=F32)


def _in_proj_prompt(x2d, ln_g, w_bf, w_kvt, n, t, tm):
    m, d = x2d.shape
    tps = t // tm
    tail = min(max(w for w, _ in B_PATTERNS), t)
    assert t % tm == 0 and tail % tm == 0 and all(tm % (16 * dl) == 0 for _, dl in B_PATTERNS)
    tail_tiles = tail // tm
    d2, d3 = B_PATTERNS[1][1], B_PATTERNS[2][1]
    return pl.pallas_call(
        functools.partial(_in_proj_kernel, tiles_per_seq=tps, tail_tiles=tail_tiles),
        grid=(m // tm,),
        in_specs=[pl.BlockSpec((tm, d), lambda i: (i, 0)),
                  pl.BlockSpec((1, d), lambda i: (0, 0)),
                  pl.BlockSpec(w_bf.shape, lambda i: (0, 0)),
                  pl.BlockSpec(w_kvt.shape, lambda i: (0, 0))],
        out_specs=[pl.BlockSpec((tm, ACT_W), lambda i: (i, 0)),
                   pl.BlockSpec((None, d2, tm // d2, QKV_W), lambda i: (i // tps, 0, i % tps, 0)),
                   pl.BlockSpec((None, d3, tm // d3, QKV_W), lambda i: (i // tps, 0, i % tps, 0)),
                   pl.BlockSpec((None, KVT_ROWS, tm),
                                lambda i: (i // tps, 0, jnp.maximum(i % tps - (tps - tail_tiles), 0)))],
        out_shape=[jax.ShapeDtypeStruct((m, ACT_W), BF16),
                   jax.ShapeDtypeStruct((n, d2, t // d2, QKV_W), BF16),
                   jax.ShapeDtypeStruct((n, d3, t // d3, QKV_W), BF16),
                   jax.ShapeDtypeStruct((n, KVT_ROWS, tail), F32)],
        scratch_shapes=[pltpu.VMEM((QKV_W // LANES, tm, LANES), F32)],
        compiler_params=_cparams(("arbitrary",)),
        name="in_proj",
    )(x2d, ln_g.reshape(1, d), w_bf, w_kvt)


def _in_proj_sample_kernel(x_ref, g_ref, w_ref, y_ref):
    xb = _rms(x_ref[...], g_ref[...]).astype(BF16)
    y_ref[:, :OFF_GA] = jnp.dot(xb, w_ref[:, :OFF_GA], preferred_element_type=F32)
    for lo, hi in ((0, OFF_KA), (OFF_QB, OFF_KB)):
        y_ref[:, lo:hi] = y_ref[:, lo:hi] * Q_SCALE
    for lo in range(OFF_GA, w_ref.shape[1], 1024):
        y_ref[:, lo:lo + 1024] = _sigmoid(jnp.dot(xb, w_ref[:, lo:lo + 1024], preferred_element_type=F32))


def _in_proj_sample(x2d, ln_g, w_bf):
    m, d = x2d.shape
    return pl.pallas_call(
        _in_proj_sample_kernel,
        grid=(1,),
        in_specs=[pl.BlockSpec((m, d), lambda i: (0, 0)),
                  pl.BlockSpec((1, d), lambda i: (0, 0)),
                  pl.BlockSpec(w_bf.shape, lambda i: (0, 0))],
        out_specs=pl.BlockSpec((m, w_bf.shape[1]), lambda i: (0, 0)),
        out_shape=jax.ShapeDtypeStruct((m, w_bf.shape[1]), F32),
        compiler_params=_cparams(("arbitrary",)),
        name="in_proj_sample",
    )(x2d, ln_g.reshape(1, d), w_bf)


def _band_attn_kernel(*refs, kv_heads, grp, has_sink, want_lse):
    if has_sink:
        sink_ref, refs = refs[0], refs[1:]
    q_ref, kp_ref, kc_ref, vp_ref, vc_ref, bias_ref, o_ref = refs[:7]
    lse_ref = refs[7] if want_lse else None
    blk = pl.program_id(2)
    hd = HEAD_DIM
    nt = (((1,), (1,)), ((), ()))
    for kv in range(kv_heads):
        ks = slice(kv * hd, (kv + 1) * hd)
        kp, kc, vp, vc = kp_ref[:, ks], kc_ref[:, ks], vp_ref[:, ks], vc_ref[:, ks]
        for g in range(grp):
            h = kv * grp + g
            hs = slice(h * hd, (h + 1) * hd)
            q = q_ref[:, hs]
            sp = lax.dot_general(q, kp, nt, preferred_element_type=F32) + bias_ref[h, :, 0:BLOCK]
            sc = lax.dot_general(q, kc, nt, preferred_element_type=F32) + bias_ref[h, :, BLOCK:2 * BLOCK]
            sp = jnp.where(blk > 0, sp, NEG)
            m = jnp.maximum(jnp.max(sp, axis=-1, keepdims=True), jnp.max(sc, axis=-1, keepdims=True))
            if has_sink:
                m = jnp.maximum(m, sink_ref[h])
            pp = jnp.exp(sp - m)
            pc = jnp.exp(sc - m)
            den = jnp.sum(pp, axis=-1, keepdims=True) + jnp.sum(pc, axis=-1, keepdims=True)
            if has_sink:
                den = den + jnp.exp(sink_ref[h] - m)
            o = (jnp.dot(pp.astype(BF16), vp, preferred_element_type=F32)
                 + jnp.dot(pc.astype(BF16), vc, preferred_element_type=F32))
            o_ref[:, hs] = (o / den).astype(o_ref.dtype)
            if want_lse:
                lse_ref[:, hs] = jnp.broadcast_to(m + jnp.log(den), (BLOCK, hd))


def _band_attn(src, bias, sink, *, q_off, k_off, v_off, kv_heads, grp, want_lse):
    n, dil, l, cols = src.shape
    assert l % BLOCK == 0
    nb = l // BLOCK
    qw = kv_heads * grp * HEAD_DIM
    kw = kv_heads * HEAD_DIM
    assert q_off % qw == 0 and k_off % kw == 0 and v_off % kw == 0
    qb, kb, vb = q_off // qw, k_off // kw, v_off // kw
    prev = lambda b: jnp.maximum(b - 1, 0)
    in_specs = [
        pl.BlockSpec((None, None, BLOCK, qw), lambda i, r, b: (i, r, b, qb)),
        pl.BlockSpec((None, None, BLOCK, kw), lambda i, r, b: (i, r, prev(b), kb)),
        pl.BlockSpec((None, None, BLOCK, kw), lambda i, r, b: (i, r, b, kb)),
        pl.BlockSpec((None, None, BLOCK, kw), lambda i, r, b: (i, r, prev(b), vb)),
        pl.BlockSpec((None, None, BLOCK, kw), lambda i, r, b: (i, r, b, vb)),
        pl.BlockSpec(bias.shape, lambda i, r, b: (0, 0, 0)),
    ]
    args = [src, src, src, src, src, bias]
    has_sink = sink is not None
    if has_sink:
        in_specs = [pl.BlockSpec(memory_space=pltpu.SMEM)] + in_specs
        args = [sink.astype(F32)] + args
    out_specs = [pl.BlockSpec((None, None, BLOCK, qw), lambda i, r, b: (i, r, b, 0))]
    out_shape = [jax.ShapeDtypeStruct((n, dil, l, qw), BF16)]
    if want_lse:
        out_specs.append(pl.BlockSpec((None, None, BLOCK, qw), lambda i, r, b: (i, r, b, 0)))
        out_shape.append(jax.ShapeDtypeStruct((n, dil, l, qw), F32))
    return pl.pallas_call(
        functools.partial(_band_attn_kernel, kv_heads=kv_heads, grp=grp, has_sink=has_sink, want_lse=want_lse),
        grid=(n, dil, nb),
        in_specs=in_specs,
        out_specs=out_specs,
        out_shape=out_shape,
        compiler_params=_cparams(("arbitrary", "arbitrary", "arbitrary")),
        name=f"band_attn_d{dil}",
    )(*args)


def _attend_cached(q, cache_ref, kv_head, new_ref, bias, self_bias, sink):
    kt = cache_ref[0, kv_head]
    vt = cache_ref[1, kv_head]
    s = jnp.sum(kt * q, axis=0, keepdims=True) + bias
    s_new = jnp.sum(new_ref[0, kv_head] * q, axis=0, keepdims=True) + self_bias
    m = jnp.maximum(jnp.max(s, axis=1, keepdims=True), s_new)
    if sink is not None:
        m = jnp.maximum(m, sink)
    p = jnp.exp(s - m)
    p_new = jnp.exp(s_new - m)
    den = jnp.sum(p, axis=1, keepdims=True) + p_new
    if sink is not None:
        den = den + jnp.exp(sink - m)
    o = (jnp.sum(vt * p, axis=1, keepdims=True) + new_ref[1, kv_head] * p_new) / den
    return o, m + jnp.log(den)


def _roll_in(cache_ref, new_ref, out_ref):
    w = cache_ref.shape[-1]
    for i in range(cache_ref.shape[0]):
        for h in range(cache_ref.shape[1]):
            x = cache_ref[i, h]
            lane = lax.broadcasted_iota(jnp.int32, x.shape, 1)
            out_ref[i, h] = jnp.where(lane == w - 1, new_ref[i, h], pltpu.roll(x, w - 1, 1))


def _sample_mix_kernel(qa_ref, qb_ref, na_ref, n1_ref, n2_ref, n3_ref, ca_ref, c1_ref, c2_ref, c3_ref,
                       ba_ref, sa_ref, sink_ref, b1_ref, b2_ref, b3_ref, sb_ref,
                       oa_ref, ob_ref, lse_ref, ra_ref, r1_ref, r2_ref, r3_ref):
    for h in range(A_HEADS):
        o, _ = _attend_cached(qa_ref[h], ca_ref, h // A_GROUP, na_ref, ba_ref[h], sa_ref[h], sink_ref[h])
        oa_ref[h] = o
    pats = ((c1_ref, n1_ref, b1_ref), (c2_ref, n2_ref, b2_ref), (c3_ref, n3_ref, b3_ref))
    for p, (c_ref, n_ref, b_ref) in enumerate(pats):
        for h in range(B_HEADS_PER_PATTERN):
            o, lse = _attend_cached(qb_ref[p, h], c_ref, h, n_ref, b_ref[h], sb_ref[p, h], None)
            ob_ref[p, h] = o
            lse_ref[p, h] = lse
    _roll_in(ca_ref, na_ref, ra_ref)
    _roll_in(c1_ref, n1_ref, r1_ref)
    _roll_in(c2_ref, n2_ref, r2_ref)
    _roll_in(c3_ref, n3_ref, r3_ref)


def _sample_mix(proj_s, caches, rel_bias, sinks):
    n = proj_s.shape[0]
    hd = HEAD_DIM
    qa = proj_s[:, :OFF_KA].reshape(n, A_HEADS, hd, 1)
    new_a = proj_s[:, OFF_KA:OFF_QB].reshape(n, 2, A_KV_HEADS, hd, 1)
    qb, kb, vb = (proj_s[:, lo:lo + QB_W].reshape(n, N_PAT, B_HEADS_PER_PATTERN, hd, 1)
                  for lo in (OFF_QB, OFF_KB, OFF_VB))
    news = [new_a] + [jnp.stack([kb[:, p], vb[:, p]], axis=1) for p in range(N_PAT)]
    cts = [jnp.transpose(c, (0, 2, 3, 4, 1)) for c in caches]

    ba, sa = _decode_bias(rel_bias[:, :A_HEADS], A_WINDOW, 1, 1)
    sink = sinks.astype(F32).reshape(A_HEADS, 1, 1)
    bbs, sbs = [], []
    for p, (win, dil) in enumerate(B_PATTERNS):
        lo = A_HEADS + p * B_HEADS_PER_PATTERN
        assert caches[1 + p].shape[1] == win == BLOCK * dil
        b, s = _decode_bias(rel_bias[:, lo:lo + B_HEADS_PER_PATTERN], win, dil, 0)
        bbs.append(b)
        sbs.append(s)
    sb = jnp.stack(sbs)

    def per_seq(a):
        nd = a.ndim
        return pl.BlockSpec((None,) + a.shape[1:], lambda i: (i,) + (0,) * (nd - 1))

    def full(a):
        nd = a.ndim
        return pl.BlockSpec(a.shape, lambda i: (0,) * nd)

    seq_in = [qa, qb] + news + cts
    consts = [ba, sa, sink] + bbs + [sb]
    out_shape = [jax.ShapeDtypeStruct((n, A_HEADS, hd, 1), F32),
                 jax.ShapeDtypeStruct((n, N_PAT, B_HEADS_PER_PATTERN, hd, 1), F32),
                 jax.ShapeDtypeStruct((n, N_PAT, B_HEADS_PER_PATTERN, 1, 1), F32)]
    out_shape += [jax.ShapeDtypeStruct(c.shape, c.dtype) for c in cts]
    outs = pl.pallas_call(
        _sample_mix_kernel,
        grid=(n,),
        in_specs=[per_seq(a) for a in seq_in] + [full(a) for a in consts],
        out_specs=[per_seq(s) for s in out_shape],
        out_shape=out_shape,
        compiler_params=_cparams(("arbitrary",)),
        name="sample_mix",
    )(*seq_in, *consts)
    oa = outs[0].reshape(n, QA_W)
    ob = outs[1].reshape(n, N_PAT, PB_W)
    lse = outs[2].reshape(n, N_PAT, B_HEADS_PER_PATTERN)
    rolled = [jnp.transpose(r, (0, 4, 1, 2, 3)) for r in outs[3:]]
    return oa, ob, lse, rolled


def _post_attn_kernel(oa_ref, o1_ref, o2_ref, o3_ref, l1_ref, l2_ref, l3_ref, ga_ref, gb_ref, x_ref,
                      wpa_ref, wpb_ref, wout_ref, ln2_ref, wr_ref, br_ref, *rest, n_extra):
    outs = rest[n_extra:n_extra + 4]
    h_ref, hn_ref, route_ref, cnt_ref = outs
    scr_ref = rest[n_extra + 4]
    if n_extra:
        own_tile = pl.program_id(0) < pl.num_programs(0) - 1

        @pl.when(jnp.logical_not(own_tile))
        def _():
            for dst, src in zip(outs, rest[:n_extra]):
                dst[...] = src[...]

        @pl.when(own_tile)
        def _():
            _post_attn_tile(oa_ref, o1_ref, o2_ref, o3_ref, l1_ref, l2_ref, l3_ref, ga_ref, gb_ref, x_ref,
                            wpa_ref, wpb_ref, wout_ref, ln2_ref, wr_ref, br_ref, *outs, scr_ref)
    else:
        _post_attn_tile(oa_ref, o1_ref, o2_ref, o3_ref, l1_ref, l2_ref, l3_ref, ga_ref, gb_ref, x_ref,
                        wpa_ref, wpb_ref, wout_ref, ln2_ref, wr_ref, br_ref, *outs, scr_ref)


def _post_attn_tile(oa_ref, o1_ref, o2_ref, o3_ref, l1_ref, l2_ref, l3_ref, ga_ref, gb_ref, x_ref,
                    wpa_ref, wpb_ref, wout_ref, ln2_ref, wr_ref, br_ref, h_ref, hn_ref, route_ref, cnt_ref, scr_ref):
    tm = x_ref.shape[0]
    chunks = PB_W // LANES

    def token_major(ref, slot):
        dil = ref.shape[0]
        if dil == 1:
            return ref[0].astype(F32)
        for c in range(chunks):
            for r in range(dil):
                scr_ref[slot * chunks + c, pl.ds(r, tm // dil, stride=dil), :] = (
                    ref[r, :, c * LANES:(c + 1) * LANES].astype(F32))
        return jnp.concatenate([scr_ref[slot * chunks + c] for c in range(chunks)], axis=1)

    o1, o2, o3 = (token_major(r, s) for s, r in enumerate((o1_ref, o2_ref, o3_ref)))
    l1, l2, l3 = (token_major(r, 3 + s) for s, r in enumerate((l1_ref, l2_ref, l3_ref)))
    m = jnp.maximum(jnp.maximum(l1, l2), l3)
    a1, a2, a3 = jnp.exp(l1 - m), jnp.exp(l2 - m), jnp.exp(l3 - m)
    ob = (a1 * o1 + a2 * o2 + a3 * o3) / (a1 + a2 + a3)
    ya = jnp.dot(oa_ref[...], wpa_ref[...], preferred_element_type=F32)
    yb = jnp.dot(ob.astype(BF16), wpb_ref[...], preferred_element_type=F32)
    merged = ga_ref[...].astype(F32) * ya + gb_ref[...].astype(F32) * yb
    h = x_ref[...] + jnp.dot(merged.astype(BF16), wout_ref[...], preferred_element_type=F32)
    h_ref[...] = h
    hn = _rms(h, ln2_ref[...])
    hn_ref[...] = hn.astype(BF16)

    logits = jnp.dot(hn, wr_ref[...], preferred_element_type=F32, precision=lax.Precision.HIGHEST) + br_ref[...]
    lane = lax.broadcasted_iota(jnp.int32, logits.shape, 1)
    is_grp = (lane >= N_EXPERTS) & (lane < N_EXPERTS + MOE_GROUPS)
    lg = jnp.where(is_grp, logits, NEG)
    gmax = jnp.max(lg, axis=-1, keepdims=True)
    g_lane = jnp.min(jnp.where(lg == gmax, lane, LANES), axis=-1, keepdims=True)
    p_g = 1.0 / jnp.sum(jnp.where(is_grp, jnp.exp(lg - gmax), 0.0), axis=-1, keepdims=True)
    e_lo = (g_lane - N_EXPERTS) * EXPERTS_PER_GROUP
    in_grp = (lane >= e_lo) & (lane < e_lo + EXPERTS_PER_GROUP)
    le = jnp.where(in_grp, logits, NEG)
    v1 = jnp.max(le, axis=-1, keepdims=True)
    i1 = jnp.min(jnp.where(le == v1, lane, LANES), axis=-1, keepdims=True)
    le2 = jnp.where(lane == i1, NEG, le)
    v2 = jnp.max(le2, axis=-1, keepdims=True)
    i2 = jnp.min(jnp.where(le2 == v2, lane, LANES), axis=-1, keepdims=True)
    e2 = jnp.exp(v2 - v1)
    w1 = p_g / (1.0 + e2)
    w2 = p_g * e2 / (1.0 + e2)
    route = jnp.where(lane == ROUTE_I1, i1.astype(F32), jnp.where(lane == ROUTE_I2, i2.astype(F32), 0.0))
    route_ref[...] = route + jnp.where(lane == ROUTE_W1, w1, 0.0) + jnp.where(lane == ROUTE_W2, w2, 0.0)
    picks = (lane == i1).astype(F32) + (lane == i2).astype(F32)
    cnt_ref[...] = jnp.broadcast_to(jnp.sum(picks, axis=0, keepdims=True), cnt_ref.shape)


def _post_attn(oa, obs, lses, gates_src, ga_blk, x2d, wpa, wpb, wout, ln2, wr, br, tm, tiles_per_seq, extra):
    m, d = x2d.shape
    tps = tiles_per_seq
    own = m // tm
    n_tiles = own + (extra is not None)
    mine = lambda i: jnp.minimum(i, own - 1)

    def tile(w, col=0):
        return pl.BlockSpec((tm, w), lambda i: (mine(i), col))

    def out_tile(rows, w):
        return pl.BlockSpec((rows, w), lambda i: (i, 0))

    def full(a):
        return pl.BlockSpec(a.shape, lambda i: (0, 0))

    def residue(a):
        dil = a.shape[1]
        return pl.BlockSpec((None, dil, tm // dil, PB_W), lambda i: (mine(i) // tps, 0, mine(i) % tps, 0))

    weights = [wpa, wpb, wout, ln2.reshape(1, d), wr, br]
    scratch = [pltpu.VMEM((6 * PB_W // LANES, tm, LANES), F32)]
    in_specs = ([tile(QA_W)] + [residue(a) for a in obs] + [residue(a) for a in lses]
                + [tile(d, ga_blk), tile(d, ga_blk + 1), tile(d)] + [full(w) for w in weights])
    args = [oa, *obs, *lses, gates_src, gates_src, x2d, *weights]
    if extra is not None:
        in_specs = in_specs + [full(a) for a in extra]
        args = args + list(extra)
    return pl.pallas_call(
        functools.partial(_post_attn_kernel, n_extra=0 if extra is None else len(extra)),
        grid=(n_tiles,),
        in_specs=in_specs,
        out_specs=[out_tile(tm, d), out_tile(tm, d), out_tile(tm, LANES), out_tile(SUBLANES, LANES)],
        out_shape=[jax.ShapeDtypeStruct((n_tiles * tm, d), F32),
                   jax.ShapeDtypeStruct((n_tiles * tm, d), BF16),
                   jax.ShapeDtypeStruct((n_tiles * tm, LANES), F32),
                   jax.ShapeDtypeStruct((n_tiles * SUBLANES, LANES), F32)],
        scratch_shapes=scratch,
        compiler_params=_cparams(("arbitrary",)),
        name="post_attn",
    )(*args)


def _route_plan(cnt_rows, n_tiles, n_xtiles):
    g = SUBLANES
    cnt = cnt_rows.reshape(n_tiles, g, LANES)[:, 0, :N_EXPERTS].astype(jnp.int32)
    cnt8 = (cnt + g - 1) // g * g
    loff = jnp.cumsum(cnt8, axis=1) - cnt8
    boff = jnp.cumsum(cnt8, axis=0) - cnt8
    tot = jnp.sum(cnt8, axis=0)
    region = (tot + MOE_TMX - 1) // MOE_TMX * MOE_TMX
    gend = jnp.cumsum(region)
    gbase = gend - region
    cum_tiles = gend // MOE_TMX
    j = jnp.arange(n_xtiles, dtype=jnp.int32)
    loff_rows = jnp.zeros((n_tiles, g, LANES), F32).at[:, :, :N_EXPERTS].set(loff[:, None, :].astype(F32))
    return dict(
        loff=loff.reshape(-1), dst=(gbase[None, :] + boff).reshape(-1), nch=(cnt8 // g).reshape(-1),
        ntot=jnp.sum(cnt8 // g, axis=1), zst=gbase + tot, znch=(region - tot) // g,
        ztot=jnp.sum((region - tot) // g).reshape(1),
        xtile_expert=jnp.minimum(jnp.sum(j[:, None] >= cum_tiles[None, :], axis=1), N_EXPERTS - 1).astype(jnp.int32),
        n_used=cum_tiles[-1:].astype(jnp.int32),
        loff_rows=loff_rows.reshape(n_tiles * g, LANES))


def _local_slots(route, loff_row):
    tm = route.shape[0]
    lane = lax.broadcasted_iota(jnp.int32, (tm, LANES), 1)
    e1 = lane == route[:, ROUTE_I1:ROUTE_I1 + 1].astype(jnp.int32)
    e2 = lane == route[:, ROUTE_I2:ROUTE_I2 + 1].astype(jnp.int32)
    earlier = (lax.broadcasted_iota(jnp.int32, (tm, tm), 1) < lax.broadcasted_iota(jnp.int32, (tm, tm), 0))
    earlier = earlier.astype(BF16)
    c1 = jnp.dot(earlier, e1.astype(BF16), preferred_element_type=F32)
    c2 = jnp.dot(earlier, e2.astype(BF16), preferred_element_type=F32)
    cnt1 = jnp.sum(e1.astype(F32), axis=0, keepdims=True)
    pos1 = jnp.sum(jnp.where(e1, c1 + loff_row, 0.0), axis=1, keepdims=True)
    pos2 = jnp.sum(jnp.where(e2, c2 + cnt1 + loff_row, 0.0), axis=1, keepdims=True)
    slot = lax.broadcasted_iota(jnp.int32, (tm, MOE_SLOTS), 1)
    return slot == pos1.astype(jnp.int32), slot == pos2.astype(jnp.int32)


def _split3(w):
    hi = w.astype(BF16).astype(F32)
    mid = (w - hi).astype(BF16).astype(F32)
    return hi, mid, (w - hi - mid).astype(BF16).astype(F32)


def _moe_scatter_kernel(loff_s, dst_s, nch_s, ntot_s, zst_s, znch_s, ztot_s, nused_s,
                        hn_ref, route_ref, loffv_ref, xs_hbm, buf_ref, zero_ref, sem, zsem, tsem):
    b = pl.program_id(0)
    nb = pl.num_programs(0)
    slot = b % 2
    g = SUBLANES
    d = hn_ref.shape[1]
    n_xtiles = xs_hbm.shape[0] // MOE_TMX

    def run_copy(s, src_row, dst_row):
        return pltpu.make_async_copy(buf_ref.at[s, pl.ds(src_row, g)], xs_hbm.at[pl.ds(dst_row, g)], sem.at[s])

    def zero_copy(dst_row):
        return pltpu.make_async_copy(zero_ref.at[pl.ds(0, g)], xs_hbm.at[pl.ds(dst_row, g)], zsem)

    def zero_tile_copy(j):
        return pltpu.make_async_copy(zero_ref, xs_hbm.at[pl.ds(pl.multiple_of(j * MOE_TMX, MOE_TMX), MOE_TMX)], tsem)

    def wait_tile(tile, s):
        lax.fori_loop(0, ntot_s[tile], lambda _, c: (run_copy(s, 0, 0).wait(), c)[1], 0)

    @pl.when(b == 0)
    def _():
        zero_ref[...] = jnp.zeros_like(zero_ref)

        def per_expert(e, c):
            def per_chunk(k, c2):
                zero_copy(pl.multiple_of(zst_s[e] + k * g, g)).start()
                return c2
            return lax.fori_loop(0, znch_s[e], per_chunk, c)
        lax.fori_loop(0, N_EXPERTS, per_expert, 0)
        lax.fori_loop(nused_s[0], n_xtiles, lambda j, c: (zero_tile_copy(j).start(), c)[1], 0)

    @pl.when(b >= 2)
    def _():
        wait_tile(b - 2, slot)

    route = route_ref[...]
    p1, p2 = _local_slots(route, loffv_ref[0:1, :])
    tn = (((0,), (0,)), ((), ()))
    picks = (p1 | p2).astype(BF16)
    buf_ref[slot, :, 0:d] = lax.dot_general(picks, hn_ref[...], tn, preferred_element_type=F32)
    lane = lax.broadcasted_iota(jnp.int32, route.shape, 1)
    meta = jnp.zeros((MOE_SLOTS, LANES), F32)
    for p, col in ((p1, ROUTE_W1), (p2, ROUTE_W2)):
        w = route[:, col:col + 1]
        parts = _split3(w)
        wm = sum(jnp.where(lane == k, part, 0.0) for k, part in enumerate(parts))
        meta = meta + lax.dot_general(p.astype(BF16), wm.astype(BF16), tn, preferred_element_type=F32)
    buf_ref[slot, :, d:d + LANES] = meta

    def per_expert(e, c):
        base = b * N_EXPERTS + e

        def per_chunk(k, c2):
            run_copy(slot, pl.multiple_of(loff_s[base] + k * g, g), pl.multiple_of(dst_s[base] + k * g, g)).start()
            return c2
        return lax.fori_loop(0, nch_s[base], per_chunk, c)
    lax.fori_loop(0, N_EXPERTS, per_expert, 0)

    @pl.when(b == nb - 1)
    def _():
        wait_tile(b, slot)

        @pl.when(nb >= 2)
        def _():
            wait_tile(b - 1, 1 - slot)
        lax.fori_loop(0, ztot_s[0], lambda _, c: (zero_copy(0).wait(), c)[1], 0)
        lax.fori_loop(nused_s[0], n_xtiles, lambda j, c: (zero_tile_copy(j).wait(), c)[1], 0)


def _moe_scatter(plan, hn_all, route_all, n_tiles, n_xtiles):
    d = hn_all.shape[1]
    tm = MOE_TM
    grid_spec = pltpu.PrefetchScalarGridSpec(
        num_scalar_prefetch=8,
        grid=(n_tiles,),
        in_specs=[pl.BlockSpec((tm, d), lambda i, *_: (i, 0)),
                  pl.BlockSpec((tm, LANES), lambda i, *_: (i, 0)),
                  pl.BlockSpec((SUBLANES, LANES), lambda i, *_: (i, 0))],
        out_specs=pl.BlockSpec(memory_space=pl.ANY),
        scratch_shapes=[pltpu.VMEM((2, MOE_SLOTS, d + LANES), F32),
                        pltpu.VMEM((MOE_TMX, d + LANES), F32),
                        pltpu.SemaphoreType.DMA((2,)),
                        pltpu.SemaphoreType.DMA(()),
                        pltpu.SemaphoreType.DMA(())])
    return pl.pallas_call(
        _moe_scatter_kernel,
        grid_spec=grid_spec,
        out_shape=jax.ShapeDtypeStruct((n_xtiles * MOE_TMX, d + LANES), F32),
        compiler_params=_cparams(("arbitrary",)),
        name="moe_scatter",
    )(plan["loff"], plan["dst"], plan["nch"], plan["ntot"], plan["zst"], plan["znch"], plan["ztot"], plan["n_used"],
      hn_all, route_all, plan["loff_rows"])


def _moe_experts_kernel(xe_s, nused_s, x_ref, w1_ref, w3_ref, w2_ref, y_ref, w1b_ref, w3b_ref, w2b_ref):
    j = pl.program_id(0)
    d = y_ref.shape[1]

    @pl.when(j < nused_s[0])
    def _():
        @pl.when((j == 0) | (xe_s[j] != xe_s[jnp.maximum(j - 1, 0)]))
        def _():
            w1b_ref[...] = w1_ref[...].astype(BF16)
            w3b_ref[...] = w3_ref[...].astype(BF16)
            w2b_ref[...] = w2_ref[...].astype(BF16)

        x = x_ref[:, 0:d].astype(BF16)
        gate = jnp.sum(x_ref[:, d:d + LANES], axis=1, keepdims=True)
        a = jnp.dot(x, w1b_ref[...], preferred_element_type=F32)
        b = jnp.dot(x, w3b_ref[...], preferred_element_type=F32)
        hh = (a * _sigmoid(a)) * b * gate
        y_ref[...] = jnp.dot(hh.astype(BF16), w2b_ref[...], preferred_element_type=F32)

    @pl.when(j >= nused_s[0])
    def _():
        y_ref[...] = jnp.zeros_like(y_ref)


def _moe_experts(plan, xs, w1, w3, w2, n_xtiles):
    ne, d, f = w1.shape
    last = lambda j, xe, nu: jnp.minimum(j, nu[0] - 1)
    grid_spec = pltpu.PrefetchScalarGridSpec(
        num_scalar_prefetch=2,
        grid=(n_xtiles,),
        in_specs=[pl.BlockSpec((MOE_TMX, d + LANES), lambda j, xe, nu: (last(j, xe, nu), 0)),
                  pl.BlockSpec((None, d, f), lambda j, xe, nu: (xe[j], 0, 0)),
                  pl.BlockSpec((None, d, f), lambda j, xe, nu: (xe[j], 0, 0)),
                  pl.BlockSpec((None, f, d), lambda j, xe, nu: (xe[j], 0, 0))],
        out_specs=pl.BlockSpec((MOE_TMX, d), lambda j, xe, nu: (j, 0)),
        scratch_shapes=[pltpu.VMEM((d, f), BF16), pltpu.VMEM((d, f), BF16), pltpu.VMEM((f, d), BF16)])
    return pl.pallas_call(
        _moe_experts_kernel,
        grid_spec=grid_spec,
        out_shape=jax.ShapeDtypeStruct((n_xtiles * MOE_TMX, d), F32),
        compiler_params=_cparams(("arbitrary",)),
        name="moe_experts",
    )(plan["xtile_expert"], plan["n_used"], xs, w1, w3, w2)


def _moe_combine_kernel(loff_s, dst_s, nch_s, ntot_s, ys_hbm, route_ref, loffv_ref, h_ref, lnf_ref, y_ref,
                        buf_ref, sem, *, tile0, final_norm):
    b = pl.program_id(0)
    nb = pl.num_programs(0)
    slot = b % 2
    g = SUBLANES
    tile = b + tile0

    def run_copy(s, src_row, dst_row):
        return pltpu.make_async_copy(ys_hbm.at[pl.ds(src_row, g)], buf_ref.at[s, pl.ds(dst_row, g)], sem.at[s])

    def fetch(t, s):
        def per_expert(e, c):
            base = t * N_EXPERTS + e

            def per_chunk(k, c2):
                run_copy(s, pl.multiple_of(dst_s[base] + k * g, g), pl.multiple_of(loff_s[base] + k * g, g)).start()
                return c2
            return lax.fori_loop(0, nch_s[base], per_chunk, c)
        lax.fori_loop(0, N_EXPERTS, per_expert, 0)

    @pl.when(b == 0)
    def _():
        fetch(tile, slot)

    @pl.when(b + 1 < nb)
    def _():
        fetch(tile + 1, 1 - slot)

    lax.fori_loop(0, ntot_s[tile], lambda _, c: (run_copy(slot, 0, 0).wait(), c)[1], 0)

    def clear(k, c):
        buf_ref[slot, pl.ds(pl.multiple_of(k * g, g), g), :] = jnp.zeros((g, buf_ref.shape[2]), F32)
        return c
    lax.fori_loop(ntot_s[tile], MOE_SLOTS // g, clear, 0)

    p1, p2 = _local_slots(route_ref[...], loffv_ref[0:1, :])
    picks = (p1 | p2).astype(BF16)
    y = h_ref[...] + jnp.dot(picks, buf_ref[slot].astype(BF16), preferred_element_type=F32)
    if final_norm:
        y = _rms(y, lnf_ref[...])
    y_ref[...] = y


def _moe_combine(plan, ys, route_all, h_all, lnf, tile0, n_tiles, final_norm):
    d = h_all.shape[1]
    tm = MOE_TM
    grid_spec = pltpu.PrefetchScalarGridSpec(
        num_scalar_prefetch=4,
        grid=(n_tiles,),
        in_specs=[pl.BlockSpec(memory_space=pl.ANY),
                  pl.BlockSpec((tm, LANES), lambda i, *_: (i + tile0, 0)),
                  pl.BlockSpec((SUBLANES, LANES), lambda i, *_: (i + tile0, 0)),
                  pl.BlockSpec((tm, d), lambda i, *_: (i + tile0, 0)),
                  pl.BlockSpec((1, d), lambda i, *_: (0, 0))],
        out_specs=pl.BlockSpec((tm, d), lambda i, *_: (i, 0)),
        scratch_shapes=[pltpu.VMEM((2, MOE_SLOTS, d), F32), pltpu.SemaphoreType.DMA((2,))])
    return pl.pallas_call(
        functools.partial(_moe_combine_kernel, tile0=tile0, final_norm=final_norm),
        grid_spec=grid_spec,
        out_shape=jax.ShapeDtypeStruct((n_tiles * tm, d), F32),
        compiler_params=_cparams(("arbitrary",)),
        name="moe_combine",
    )(plan["loff"], plan["dst"], plan["nch"], plan["ntot"], ys, route_all, plan["loff_rows"], h_all,
      lnf.reshape(1, d))


def _prompt_states(kvt, n, t):
    hd = HEAD_DIM
    tail = kvt.shape[2]

    def state(k_lo, v_lo, rows, heads, win):
        w = min(win, t)
        kv = jnp.stack([kvt[:, k_lo:k_lo + rows, tail - w:], kvt[:, v_lo:v_lo + rows, tail - w:]], axis=1)
        return jnp.transpose(kv.reshape(n, 2, heads, hd, w), (0, 4, 1, 2, 3))

    out = [state(0, KA_W, KA_W, A_KV_HEADS, A_WINDOW)]
    for p, (win, _) in enumerate(B_PATTERNS):
        out.append(state(2 * KA_W + p * PB_W, 2 * KA_W + QB_W + p * PB_W, PB_W, B_HEADS_PER_PATTERN, win))
    return out


def _layer(xp, xs, caches, rel_bias, ln1, w_in, sinks, w_pa, w_pb, w_out, ln2, w_rg, b_rg, w_re, b_re,
           w1, w3, w2, lnf, final_norm):
    n, t, d = xp.shape
    ns = xs.shape[0]
    assert xs.shape[1] == 1 and OFF_GA + 2 * d == w_in.shape[1] and P_QA == 2 * d
    w_bf = w_in.astype(BF16)
    w_kvt = jnp.concatenate([w_in[:, OFF_KA:OFF_QB], w_in[:, OFF_KB:OFF_GA]], axis=1).T.astype(BF16)
    wpa, wpb, wout = w_pa.astype(BF16), w_pb.astype(BF16), w_out.astype(BF16)
    wr = jnp.zeros((d, LANES), F32).at[:, :N_EXPERTS].set(w_re).at[:, N_EXPERTS:N_EXPERTS + MOE_GROUPS].set(w_rg)
    br = jnp.zeros((1, LANES), F32).at[0, :N_EXPERTS].set(b_re).at[0, N_EXPERTS:N_EXPERTS + MOE_GROUPS].set(b_rg)

    tm = 256
    xp2 = xp.reshape(n * t, d)
    act, qkv2, qkv3, kvt = _in_proj_prompt(xp2, ln1, w_bf, w_kvt, n, t, tm)
    act4 = act.reshape(n, 1, t, ACT_W)
    bias_a = _band_bias(rel_bias[:, :A_HEADS], A_WINDOW - 1, 1)
    (oa,) = _band_attn(act4, bias_a, sinks, q_off=P_QA, k_off=P_KA, v_off=P_VA,
                       kv_heads=A_KV_HEADS, grp=A_GROUP, want_lse=False)
    obs, lses = [], []
    for p, (win, dil) in enumerate(B_PATTERNS):
        lo = A_HEADS + p * B_HEADS_PER_PATTERN
        bias_p = _band_bias(rel_bias[:, lo:lo + B_HEADS_PER_PATTERN], win // dil, dil)
        src, base = ((act4, P_B), (qkv2, 0), (qkv3, 0))[p]
        o, lse = _band_attn(src, bias_p, None, q_off=base, k_off=base + PB_W, v_off=base + 2 * PB_W,
                            kv_heads=B_HEADS_PER_PATTERN, grp=1, want_lse=True)
        obs.append(o)
        lses.append(lse)
    assert tm == MOE_TM and ns <= MOE_TM
    p_tiles = n * t // MOE_TM
    n_tiles = p_tiles + 1
    m_all = n_tiles * MOE_TM
    st_p = _prompt_states(kvt, n, t)

    xs2 = xs.reshape(ns, d)
    proj_s = _in_proj_sample(xs2, ln1, w_bf)
    oa_s, ob_s, lse_s, st_s = _sample_mix(proj_s, caches, rel_bias, sinks)
    rows = lambda a: jnp.pad(a, ((0, MOE_TM - ns), (0, 0)))
    obs_s = [rows(ob_s[:, p].astype(BF16)).reshape(1, 1, MOE_TM, PB_W) for p in range(N_PAT)]
    lses_s = [rows(jnp.repeat(lse_s[:, p], HEAD_DIM, axis=-1)).reshape(1, 1, MOE_TM, PB_W) for p in range(N_PAT)]
    gates_s = rows(proj_s[:, OFF_GA:].astype(BF16))
    routed_s = _post_attn(rows(oa_s.astype(BF16)), obs_s, lses_s, gates_s, 0, rows(xs2),
                          wpa, wpb, wout, ln2, wr, br, MOE_TM, 1, None)
    h_all, hn_all, route_all, cnt_all = _post_attn(oa.reshape(n * t, QA_W), obs, lses, act, 0, xp2,
                                                   wpa, wpb, wout, ln2, wr, br, tm, t // tm, routed_s)

    max_rows = 2 * m_all + n_tiles * N_EXPERTS * (SUBLANES - 1) + N_EXPERTS * (MOE_TMX - SUBLANES)
    n_xtiles = -(-max_rows // MOE_TMX)
    plan = _route_plan(cnt_all, n_tiles, n_xtiles)
    xs_sorted = _moe_scatter(plan, hn_all, route_all, n_tiles, n_xtiles)
    ys_sorted = _moe_experts(plan, xs_sorted, w1, w3, w2, n_xtiles)
    yp = _moe_combine(plan, ys_sorted, route_all, h_all, lnf, 0, p_tiles, final_norm).reshape(n, t, d)
    ys = _moe_combine(plan, ys_sorted, route_all, h_all, lnf, p_tiles, 1, final_norm)[:ns].reshape(ns, 1, d)
    return yp, ys, st_p, st_s


def kernel(x_prompt, x_sample, cache_a_kv, cache_b1_kv, cache_b2_kv, cache_b3_kv, rel_bias, ln1_g, w_in, sinks,
           w_pa, w_pb, w_out, ln2_g, w_rg, b_rg, w_re, b_re, w1, w3, w2, lnf_g):
    depth = w_in.shape[0]
    assert depth >= 1
    xp, xs = x_prompt, x_sample
    new_p = [[] for _ in range(4)]
    new_s = [[] for _ in range(4)]
    for l in range(depth):
        caches = (cache_a_kv[l], cache_b1_kv[l], cache_b2_kv[l], cache_b3_kv[l])
        xp, xs, st_p, st_s = _layer(xp, xs, caches, rel_bias, ln1_g[l], w_in[l], sinks[l], w_pa[l], w_pb[l],
                                    w_out[l], ln2_g[l], w_rg[l], b_rg[l], w_re[l], b_re[l], w1[l], w3[l], w2[l],
                                    lnf_g, l == depth - 1)
        for i in range(4):
            new_p[i].append(st_p[i])
            new_s[i].append(st_s[i])
    a_p, b1_p, b2_p, b3_p = [jnp.stack(v) for v in new_p]
    a_s, b1_s, b2_s, b3_s = [jnp.stack(v) for v in new_s]
    return (xp, xs, a_p, a_s, b1_p, b1_s, b2_p, b2_s, b3_p, b3_s)
```

```python
import functools
import math

import numpy as np
import jax
import jax.numpy as jnp
from jax import lax
from jax.experimental import pallas as pl
from jax.experimental.pallas import tpu as pltpu

F32 = jnp.float32
BF16 = jnp.bfloat16

HEAD_DIM = 64
A_HEADS = 8
A_KV_HEADS = 2
A_GROUP = A_HEADS // A_KV_HEADS
A_WINDOW = 128
B_PATTERNS = ((128, 1), (512, 4), (2048, 16))
N_PAT = len(B_PATTERNS)
B_HEADS_PER_PATTERN = 4
B_HEADS = B_HEADS_PER_PATTERN * N_PAT
BLOCK = 128
NUM_BUCKETS = 32
MAX_DISTANCE = 2048
MOE_GROUPS = 4
EXPERTS_PER_GROUP = 8
N_EXPERTS = MOE_GROUPS * EXPERTS_PER_GROUP
EPS = 1e-6
NEG = -1e30
LANES = 128
SUBLANES = 8
ROUTE_I1, ROUTE_I2, ROUTE_W1, ROUTE_W2 = 0, 1, 2, 3
MOE_TM = 256
MOE_SLOTS = 2 * MOE_TM + N_EXPERTS * SUBLANES
MOE_TMX = 256
Q_SCALE = HEAD_DIM ** -0.5

QA_W = A_HEADS * HEAD_DIM
KA_W = A_KV_HEADS * HEAD_DIM
QB_W = B_HEADS * HEAD_DIM
PB_W = B_HEADS_PER_PATTERN * HEAD_DIM
QKV_W = 3 * PB_W
OFF_KA = QA_W
OFF_QB = OFF_KA + 2 * KA_W
OFF_KB = OFF_QB + QB_W
OFF_VB = OFF_KB + QB_W
OFF_GA = OFF_VB + QB_W
P_QA = 2048
P_KA = P_QA + QA_W
P_VA = P_KA + KA_W
P_B = P_VA + KA_W
ACT_W = P_B + QKV_W
KVT_ROWS = 2 * KA_W + 2 * QB_W

VMEM_LIMIT = 56 * 1024 * 1024


def _cparams(sem):
    return pltpu.CompilerParams(dimension_semantics=sem, vmem_limit_bytes=VMEM_LIMIT)


def _bucket_np(dist):
    dist = np.asarray(dist, np.int64)
    max_exact = NUM_BUCKETS // 2
    df = np.maximum(dist, max_exact).astype(np.float64)
    large = max_exact + (np.log(df / max_exact) / math.log(MAX_DISTANCE / max_exact)
                         * (NUM_BUCKETS - max_exact)).astype(np.int64)
    return np.where(dist < max_exact, dist, np.minimum(large, NUM_BUCKETS - 1))


def _table_rows(table_cols, dist, valid):
    onehot = (_bucket_np(dist)[:, None] == np.arange(NUM_BUCKETS)[None, :]).astype(np.float32)
    rows = jnp.einsum("ck,kh->hc", jnp.asarray(onehot), table_cols.astype(F32), precision=lax.Precision.HIGHEST)
    return jnp.where(jnp.asarray(valid)[None, :], rows, NEG)


def _band_bias(table_cols, max_dist, dilation):
    period = 3 * BLOCK
    m = np.arange(period)
    k = np.where(m < 2 * BLOCK, m, m - period)
    dist = BLOCK - k
    valid = (dist >= 0) & (dist <= max_dist) & (m != 2 * BLOCK)
    v = _table_rows(table_cols, np.clip(dist, 0, None) * dilation, valid)
    heads = v.shape[0]
    flat = jnp.tile(v, (1, BLOCK))[:, :BLOCK * (period - 1)]
    return flat.reshape(heads, BLOCK, period - 1)[:, :, :2 * BLOCK]


def _decode_bias(table_cols, width, dilation, first_valid):
    c = np.arange(width)
    valid = (c % dilation == 0) & (c >= first_valid)
    rows = _table_rows(table_cols, width - c, valid)
    self_bias = _table_rows(table_cols, np.zeros((1,), np.int64), np.ones((1,), bool))
    return rows[:, None, :], self_bias[:, None, :]


def _rms(x, g):
    return (x * lax.rsqrt(jnp.mean(x * x, axis=-1, keepdims=True) + EPS)) * g


def _sigmoid(x):
    return 1.0 / (1.0 + jnp.exp(-x))


def _in_proj_kernel(x_ref, g_ref, w_ref, wkvt_ref, act_ref, qkv2_ref, qkv3_ref, kvt_ref, ys_ref, *,
                    tiles_per_seq, tail_tiles):
    tm = x_ref.shape[0]
    g = g_ref[...]
    xb = _rms(x_ref[...], g).astype(BF16)

    def proj(xv, lo, hi):
        return jnp.dot(xv, w_ref[:, lo:hi], preferred_element_type=F32)

    for c in range(0, P_QA, 1024):
        act_ref[:, c:c + 1024] = _sigmoid(proj(xb, OFF_GA + c, OFF_GA + c + 1024)).astype(BF16)
    act_ref[:, P_QA:P_KA] = (proj(xb, 0, OFF_KA) * Q_SCALE).astype(BF16)
    act_ref[:, P_KA:P_B] = proj(xb, OFF_KA, OFF_QB).astype(BF16)

    def qkv_parts(p):
        lo = p * PB_W
        return (proj(xb, OFF_QB + lo, OFF_QB + lo + PB_W) * Q_SCALE,
                proj(xb, OFF_KB + lo, OFF_KB + lo + PB_W),
                proj(xb, OFF_VB + lo, OFF_VB + lo + PB_W))

    for j, part in enumerate(qkv_parts(0)):
        act_ref[:, P_B + j * PB_W:P_B + (j + 1) * PB_W] = part.astype(BF16)

    for p, out_ref in ((1, qkv2_ref), (2, qkv3_ref)):
        dil = B_PATTERNS[p][1]
        rows = tm // dil
        for j, part in enumerate(qkv_parts(p)):
            for c in range(PB_W // LANES):
                ys_ref[j * (PB_W // LANES) + c] = part[:, c * LANES:(c + 1) * LANES]
        for c in range(QKV_W // LANES):
            cs = slice(c * LANES, (c + 1) * LANES)
            for r in range(dil):
                out_ref[r, :, cs] = ys_ref[c, pl.ds(r, rows, stride=dil), :].astype(BF16)

    @pl.when(pl.program_id(0) % tiles_per_seq >= tiles_per_seq - tail_tiles)
    def _():
        kvt_ref[...] = lax.dot_general(wkvt_ref[...], xb, (((1,), (1,)), ((), ())), preferred_element_type=F32)


def _in_proj_prompt(x2d, ln_g, w_bf, w_kvt, n, t, tm):
    m, d = x2d.shape
    tps = t // tm
    tail = min(max(w for w, _ in B_PATTERNS), t)
    assert t % tm == 0 and tail % tm == 0 and all(tm % (16 * dl) == 0 for _, dl in B_PATTERNS)
    tail_tiles = tail // tm
    d2, d3 = B_PATTERNS[1][1], B_PATTERNS[2][1]
    return pl.pallas_call(
        functools.partial(_in_proj_kernel, tiles_per_seq=tps, tail_tiles=tail_tiles),
        grid=(m // tm,),
        in_specs=[pl.BlockSpec((tm, d), lambda i: (i, 0)),
                  pl.BlockSpec((1, d), lambda i: (0, 0)),
                  pl.BlockSpec(w_bf.shape, lambda i: (0, 0)),
                  pl.BlockSpec(w_kvt.shape, lambda i: (0, 0))],
        out_specs=[pl.BlockSpec((tm, ACT_W), lambda i: (i, 0)),
                   pl.BlockSpec((None, d2, tm // d2, QKV_W), lambda i: (i // tps, 0, i % tps, 0)),
                   pl.BlockSpec((None, d3, tm // d3, QKV_W), lambda i: (i // tps, 0, i % tps, 0)),
                   pl.BlockSpec((None, KVT_ROWS, tm),
                                lambda i: (i // tps, 0, jnp.maximum(i % tps - (tps - tail_tiles), 0)))],
        out_shape=[jax.ShapeDtypeStruct((m, ACT_W), BF16),
                   jax.ShapeDtypeStruct((n, d2, t // d2, QKV_W), BF16),
                   jax.ShapeDtypeStruct((n, d3, t // d3, QKV_W), BF16),
                   jax.ShapeDtypeStruct((n, KVT_ROWS, tail), F32)],
        scratch_shapes=[pltpu.VMEM((QKV_W // LANES, tm, LANES), F32)],
        compiler_params=_cparams(("arbitrary",)),
        name="in_proj",
    )(x2d, ln_g.reshape(1, d), w_bf, w_kvt)


def _in_proj_sample_kernel(x_ref, g_ref, w_ref, y_ref):
    xb = _rms(x_ref[...], g_ref[...]).astype(BF16)
    y_ref[:, :OFF_GA] = jnp.dot(xb, w_ref[:, :OFF_GA], preferred_element_type=F32)
    for lo, hi in ((0, OFF_KA), (OFF_QB, OFF_KB)):
        y_ref[:, lo:hi] = y_ref[:, lo:hi] * Q_SCALE
    for lo in range(OFF_GA, w_ref.shape[1], 1024):
        y_ref[:, lo:lo + 1024] = _sigmoid(jnp.dot(xb, w_ref[:, lo:lo + 1024], preferred_element_type=F32))


def _in_proj_sample(x2d, ln_g, w_bf):
    m, d = x2d.shape
    return pl.pallas_call(
        _in_proj_sample_kernel,
        grid=(1,),
        in_specs=[pl.BlockSpec((m, d), lambda i: (0, 0)),
                  pl.BlockSpec((1, d), lambda i: (0, 0)),
                  pl.BlockSpec(w_bf.shape, lambda i: (0, 0))],
        out_specs=pl.BlockSpec((m, w_bf.shape[1]), lambda i: (0, 0)),
        out_shape=jax.ShapeDtypeStruct((m, w_bf.shape[1]), F32),
        compiler_params=_cparams(("arbitrary",)),
        name="in_proj_sample",
    )(x2d, ln_g.reshape(1, d), w_bf)


def _band_attn_kernel(*refs, kv_heads, grp, has_sink, want_lse):
    if has_sink:
        sink_ref, refs = refs[0], refs[1:]
    q_ref, kp_ref, kc_ref, vp_ref, vc_ref, bias_ref, o_ref = refs[:7]
    lse_ref = refs[7] if want_lse else None
    blk = pl.program_id(2)
    hd = HEAD_DIM
    nt = (((1,), (1,)), ((), ()))
    for kv in range(kv_heads):
        ks = slice(kv * hd, (kv + 1) * hd)
        kp, kc, vp, vc = kp_ref[:, ks], kc_ref[:, ks], vp_ref[:, ks], vc_ref[:, ks]
        for g in range(grp):
            h = kv * grp + g
            hs = slice(h * hd, (h + 1) * hd)
            q = q_ref[:, hs]
            sp = lax.dot_general(q, kp, nt, preferred_element_type=F32) + bias_ref[h, :, 0:BLOCK]
            sc = lax.dot_general(q, kc, nt, preferred_element_type=F32) + bias_ref[h, :, BLOCK:2 * BLOCK]
            sp = jnp.where(blk > 0, sp, NEG)
            m = jnp.maximum(jnp.max(sp, axis=-1, keepdims=True), jnp.max(sc, axis=-1, keepdims=True))
            if has_sink:
                m = jnp.maximum(m, sink_ref[h])
            pp = jnp.exp(sp - m)
            pc = jnp.exp(sc - m)
            den = jnp.sum(pp, axis=-1, keepdims=True) + jnp.sum(pc, axis=-1, keepdims=True)
            if has_sink:
                den = den + jnp.exp(sink_ref[h] - m)
            o = (jnp.dot(pp.astype(BF16), vp, preferred_element_type=F32)
                 + jnp.dot(pc.astype(BF16), vc, preferred_element_type=F32))
            o_ref[:, hs] = (o / den).astype(o_ref.dtype)
            if want_lse:
                lse_ref[:, hs] = jnp.broadcast_to(m + jnp.log(den), (BLOCK, hd))


def _band_attn(src, bias, sink, *, q_off, k_off, v_off, kv_heads, grp, want_lse):
    n, dil, l, cols = src.shape
    assert l % BLOCK == 0
    nb = l // BLOCK
    qw = kv_heads * grp * HEAD_DIM
    kw = kv_heads * HEAD_DIM
    assert q_off % qw == 0 and k_off % kw == 0 and v_off % kw == 0
    qb, kb, vb = q_off // qw, k_off // kw, v_off // kw
    prev = lambda b: jnp.maximum(b - 1, 0)
    in_specs = [
        pl.BlockSpec((None, None, BLOCK, qw), lambda i, r, b: (i, r, b, qb)),
        pl.BlockSpec((None, None, BLOCK, kw), lambda i, r, b: (i, r, prev(b), kb)),
        pl.BlockSpec((None, None, BLOCK, kw), lambda i, r, b: (i, r, b, kb)),
        pl.BlockSpec((None, None, BLOCK, kw), lambda i, r, b: (i, r, prev(b), vb)),
        pl.BlockSpec((None, None, BLOCK, kw), lambda i, r, b: (i, r, b, vb)),
        pl.BlockSpec(bias.shape, lambda i, r, b: (0, 0, 0)),
    ]
    args = [src, src, src, src, src, bias]
    has_sink = sink is not None
    if has_sink:
        in_specs = [pl.BlockSpec(memory_space=pltpu.SMEM)] + in_specs
        args = [sink.astype(F32)] + args
    out_specs = [pl.BlockSpec((None, None, BLOCK, qw), lambda i, r, b: (i, r, b, 0))]
    out_shape = [jax.ShapeDtypeStruct((n, dil, l, qw), BF16)]
    if want_lse:
        out_specs.append(pl.BlockSpec((None, None, BLOCK, qw), lambda i, r, b: (i, r, b, 0)))
        out_shape.append(jax.ShapeDtypeStruct((n, dil, l, qw), F32))
    return pl.pallas_call(
        functools.partial(_band_attn_kernel, kv_heads=kv_heads, grp=grp, has_sink=has_sink, want_lse=want_lse),
        grid=(n, dil, nb),
        in_specs=in_specs,
        out_specs=out_specs,
        out_shape=out_shape,
        compiler_params=_cparams(("arbitrary", "arbitrary", "arbitrary")),
        name=f"band_attn_d{dil}",
    )(*args)


COL_QA = 0
COL_QB = COL_QA + A_HEADS
COL_KA = COL_QB + B_HEADS
COL_VA = COL_KA + A_KV_HEADS
COL_KB = COL_VA + A_KV_HEADS
COL_VB = COL_KB + B_HEADS
N_COLS = COL_VB + B_HEADS


def _attend_cached(kt, vt, q, k_new, v_new, bias, self_bias, sink, write):
    s = jnp.sum(kt * q, axis=1, keepdims=True) + bias
    s_new = jnp.sum(k_new * q, axis=1, keepdims=True) + self_bias
    yield
    m = jnp.maximum(jnp.max(s, axis=2, keepdims=True), s_new)
    if sink is not None:
        m = jnp.maximum(m, sink)
    yield
    p = jnp.exp(s - m)
    p_new = jnp.exp(s_new - m)
    den = jnp.sum(p, axis=2, keepdims=True) + p_new
    if sink is not None:
        den = den + jnp.exp(sink - m)
    yield
    o = (jnp.sum(vt * p, axis=2, keepdims=True) + v_new * p_new) / den
    write(o, m + jnp.log(den))
    yield


def _sample_mix_kernel(cols_ref, ca_ref, c1_ref, c2_ref, c3_ref, ba_ref, sa_ref, sink_ref, b1_ref, b2_ref, b3_ref,
                       sb_ref, o_ref, lse_ref, ra_ref, r1_ref, r2_ref, r3_ref):
    cols = cols_ref[...]

    def stack(js):
        return jnp.stack([cols[:, j:j + 1] for j in js])

    o_ref[...] = jnp.zeros_like(o_ref)
    lse_ref[...] = jnp.zeros_like(lse_ref)

    def write_a(o, _):
        for h in range(A_HEADS):
            o_ref[:, COL_QA + h:COL_QA + h + 1] = o[h]

    def write_b(p):
        def write(o, lse):
            for h in range(B_HEADS_PER_PATTERN):
                j = p * B_HEADS_PER_PATTERN + h
                o_ref[:, COL_QB + j:COL_QB + j + 1] = o[h]
                lse_ref[0:1, j:j + 1] = lse[h]
        return write

    kv_of = [h // A_GROUP for h in range(A_HEADS)]
    stages = [_attend_cached(jnp.stack([ca_ref[0, kv] for kv in kv_of]), jnp.stack([ca_ref[1, kv] for kv in kv_of]),
                             stack(range(COL_QA, COL_QA + A_HEADS)), stack([COL_KA + kv for kv in kv_of]),
                             stack([COL_VA + kv for kv in kv_of]), ba_ref[...], sa_ref[...], sink_ref[...], write_a)]
    for p, (c_ref, b_ref) in enumerate(((c1_ref, b1_ref), (c2_ref, b2_ref), (c3_ref, b3_ref))):
        js = range(p * B_HEADS_PER_PATTERN, (p + 1) * B_HEADS_PER_PATTERN)
        stages.append(_attend_cached(c_ref[0], c_ref[1], stack([COL_QB + j for j in js]),
                                     stack([COL_KB + j for j in js]), stack([COL_VB + j for j in js]),
                                     b_ref[...], sb_ref[p], None, write_b(p)))

    def roll_plane(c_ref, r_ref, i, h, new_col):
        w = c_ref.shape[-1]
        x = c_ref[i, h]
        lane = lax.broadcasted_iota(jnp.int32, x.shape, 1)
        r_ref[i, h] = jnp.where(lane == w - 1, cols[:, new_col:new_col + 1], pltpu.roll(x, w - 1, 1))

    planes = []
    for ci, (c_ref, r_ref) in enumerate(((ca_ref, ra_ref), (c1_ref, r1_ref), (c2_ref, r2_ref), (c3_ref, r3_ref))):
        for i, (first_a, first_b) in enumerate(((COL_KA, COL_KB), (COL_VA, COL_VB))):
            for h in range(c_ref.shape[1]):
                new_col = first_a + h if ci == 0 else first_b + (ci - 1) * B_HEADS_PER_PATTERN + h
                planes.append((c_ref.shape[-1], functools.partial(roll_plane, c_ref, r_ref, i, h, new_col)))

    total = sum(w for w, _ in planes)
    n_slots = 4 * len(stages)
    done, k = 0, 0
    for slot in range(n_slots):
        next(stages[slot // 4])
        while k < len(planes) and done < total * (slot + 1) // n_slots:
            done += planes[k][0]
            planes[k][1]()
            k += 1
    assert k == len(planes)


def _sample_mix(proj_s, caches, rel_bias, sinks):
    n = proj_s.shape[0]
    hd = HEAD_DIM
    vecs = jnp.concatenate([proj_s[:, :OFF_KA], proj_s[:, OFF_QB:OFF_KB], proj_s[:, OFF_KA:OFF_QB],
                            proj_s[:, OFF_KB:OFF_GA]], axis=1).reshape(n, N_COLS, hd)
    cols = jnp.pad(jnp.transpose(vecs, (0, 2, 1)), ((0, 0), (0, 0), (0, LANES - N_COLS)))
    cts = [jnp.transpose(c, (0, 2, 3, 4, 1)) for c in caches]

    ba, sa = _decode_bias(rel_bias[:, :A_HEADS], A_WINDOW, 1, 1)
    sink = sinks.astype(F32).reshape(A_HEADS, 1, 1)
    bbs, sbs = [], []
    for p, (win, dil) in enumerate(B_PATTERNS):
        lo = A_HEADS + p * B_HEADS_PER_PATTERN
        assert caches[1 + p].shape[1] == win == BLOCK * dil
        b, s = _decode_bias(rel_bias[:, lo:lo + B_HEADS_PER_PATTERN], win, dil, 0)
        bbs.append(b)
        sbs.append(s)
    sb = jnp.stack(sbs)

    def per_seq(a):
        nd = a.ndim
        return pl.BlockSpec((None,) + a.shape[1:], lambda i: (i,) + (0,) * (nd - 1))

    def full(a):
        nd = a.ndim
        return pl.BlockSpec(a.shape, lambda i: (0,) * nd)

    seq_in = [cols] + cts
    consts = [ba, sa, sink] + bbs + [sb]
    out_shape = [jax.ShapeDtypeStruct((n, hd, LANES), F32), jax.ShapeDtypeStruct((n, SUBLANES, LANES), F32)]
    out_shape += [jax.ShapeDtypeStruct(c.shape, c.dtype) for c in cts]
    outs = pl.pallas_call(
        _sample_mix_kernel,
        grid=(n,),
        in_specs=[per_seq(a) for a in seq_in] + [full(a) for a in consts],
        out_specs=[per_seq(s) for s in out_shape],
        out_shape=out_shape,
        compiler_params=_cparams(("arbitrary",)),
        name="sample_mix",
    )(*seq_in, *consts)
    o_rows = jnp.transpose(outs[0][:, :, :COL_KA], (0, 2, 1))
    oa = o_rows[:, COL_QA:COL_QB].reshape(n, QA_W)
    ob = o_rows[:, COL_QB:COL_KA].reshape(n, N_PAT, PB_W)
    lse = outs[1][:, 0, :B_HEADS].reshape(n, N_PAT, B_HEADS_PER_PATTERN)
    rolled = [jnp.transpose(r, (0, 4, 1, 2, 3)) for r in outs[2:]]
    return oa, ob, lse, rolled


def _post_attn_kernel(oa_ref, o1_ref, o2_ref, o3_ref, l1_ref, l2_ref, l3_ref, ga_ref, gb_ref, x_ref,
                      wpa_ref, wpb_ref, wout_ref, ln2_ref, wr_ref, br_ref, *rest, n_extra):
    outs = rest[n_extra:n_extra + 4]
    h_ref, hn_ref, route_ref, cnt_ref = outs
    scr_ref = rest[n_extra + 4]
    if n_extra:
        own_tile = pl.program_id(0) < pl.num_programs(0) - 1

        @pl.when(jnp.logical_not(own_tile))
        def _():
            for dst, src in zip(outs, rest[:n_extra]):
                dst[...] = src[...]

        @pl.when(own_tile)
        def _():
            _post_attn_tile(oa_ref, o1_ref, o2_ref, o3_ref, l1_ref, l2_ref, l3_ref, ga_ref, gb_ref, x_ref,
                            wpa_ref, wpb_ref, wout_ref, ln2_ref, wr_ref, br_ref, *outs, scr_ref)
    else:
        _post_attn_tile(oa_ref, o1_ref, o2_ref, o3_ref, l1_ref, l2_ref, l3_ref, ga_ref, gb_ref, x_ref,
                        wpa_ref, wpb_ref, wout_ref, ln2_ref, wr_ref, br_ref, *outs, scr_ref)


def _post_attn_tile(oa_ref, o1_ref, o2_ref, o3_ref, l1_ref, l2_ref, l3_ref, ga_ref, gb_ref, x_ref,
                    wpa_ref, wpb_ref, wout_ref, ln2_ref, wr_ref, br_ref, h_ref, hn_ref, route_ref, cnt_ref, scr_ref):
    tm = x_ref.shape[0]
    chunks = PB_W // LANES

    def token_major(ref, slot):
        dil = ref.shape[0]
        if dil == 1:
            return ref[0].astype(F32)
        for c in range(chunks):
            for r in range(dil):
                scr_ref[slot * chunks + c, pl.ds(r, tm // dil, stride=dil), :] = (
                    ref[r, :, c * LANES:(c + 1) * LANES].astype(F32))
        return jnp.concatenate([scr_ref[slot * chunks + c] for c in range(chunks)], axis=1)

    o1, o2, o3 = (token_major(r, s) for s, r in enumerate((o1_ref, o2_ref, o3_ref)))
    l1, l2, l3 = (token_major(r, 3 + s) for s, r in enumerate((l1_ref, l2_ref, l3_ref)))
    m = jnp.maximum(jnp.maximum(l1, l2), l3)
    a1, a2, a3 = jnp.exp(l1 - m), jnp.exp(l2 - m), jnp.exp(l3 - m)
    ob = (a1 * o1 + a2 * o2 + a3 * o3) / (a1 + a2 + a3)
    ya = jnp.dot(oa_ref[...], wpa_ref[...], preferred_element_type=F32)
    yb = jnp.dot(ob.astype(BF16), wpb_ref[...], preferred_element_type=F32)
    merged = ga_ref[...].astype(F32) * ya + gb_ref[...].astype(F32) * yb
    h = x_ref[...] + jnp.dot(merged.astype(BF16), wout_ref[...], preferred_element_type=F32)
    h_ref[...] = h
    hn = _rms(h, ln2_ref[...])
    hn_ref[...] = hn.astype(BF16)

    logits = jnp.dot(hn, wr_ref[...], preferred_element_type=F32, precision=lax.Precision.HIGHEST) + br_ref[...]
    lane = lax.broadcasted_iota(jnp.int32, logits.shape, 1)
    is_grp = (lane >= N_EXPERTS) & (lane < N_EXPERTS + MOE_GROUPS)
    lg = jnp.where(is_grp, logits, NEG)
    gmax = jnp.max(lg, axis=-1, keepdims=True)
    g_lane = jnp.min(jnp.where(lg == gmax, lane, LANES), axis=-1, keepdims=True)
    p_g = 1.0 / jnp.sum(jnp.where(is_grp, jnp.exp(lg - gmax), 0.0), axis=-1, keepdims=True)
    e_lo = (g_lane - N_EXPERTS) * EXPERTS_PER_GROUP
    in_grp = (lane >= e_lo) & (lane < e_lo + EXPERTS_PER_GROUP)
    le = jnp.where(in_grp, logits, NEG)
    v1 = jnp.max(le, axis=-1, keepdims=True)
    i1 = jnp.min(jnp.where(le == v1, lane, LANES), axis=-1, keepdims=True)
    le2 = jnp.where(lane == i1, NEG, le)
    v2 = jnp.max(le2, axis=-1, keepdims=True)
    i2 = jnp.min(jnp.where(le2 == v2, lane, LANES), axis=-1, keepdims=True)
    e2 = jnp.exp(v2 - v1)
    w1 = p_g / (1.0 + e2)
    w2 = p_g * e2 / (1.0 + e2)
    route = jnp.where(lane == ROUTE_I1, i1.astype(F32), jnp.where(lane == ROUTE_I2, i2.astype(F32), 0.0))
    route_ref[...] = route + jnp.where(lane == ROUTE_W1, w1, 0.0) + jnp.where(lane == ROUTE_W2, w2, 0.0)
    picks = (lane == i1).astype(F32) + (lane == i2).astype(F32)
    cnt_ref[...] = jnp.broadcast_to(jnp.sum(picks, axis=0, keepdims=True), cnt_ref.shape)


def _post_attn(oa, obs, lses, gates_src, ga_blk, x2d, wpa, wpb, wout, ln2, wr, br, tm, tiles_per_seq, extra):
    m, d = x2d.shape
    tps = tiles_per_seq
    own = m // tm
    n_tiles = own + (extra is not None)
    mine = lambda i: jnp.minimum(i, own - 1)

    def tile(w, col=0):
        return pl.BlockSpec((tm, w), lambda i: (mine(i), col))

    def out_tile(rows, w):
        return pl.BlockSpec((rows, w), lambda i: (i, 0))

    def full(a):
        return pl.BlockSpec(a.shape, lambda i: (0, 0))

    def residue(a):
        dil = a.shape[1]
        return pl.BlockSpec((None, dil, tm // dil, PB_W), lambda i: (mine(i) // tps, 0, mine(i) % tps, 0))

    weights = [wpa, wpb, wout, ln2.reshape(1, d), wr, br]
    scratch = [pltpu.VMEM((6 * PB_W // LANES, tm, LANES), F32)]
    in_specs = ([tile(QA_W)] + [residue(a) for a in obs] + [residue(a) for a in lses]
                + [tile(d, ga_blk), tile(d, ga_blk + 1), tile(d)] + [full(w) for w in weights])
    args = [oa, *obs, *lses, gates_src, gates_src, x2d, *weights]
    if extra is not None:
        in_specs = in_specs + [full(a) for a in extra]
        args = args + list(extra)
    return pl.pallas_call(
        functools.partial(_post_attn_kernel, n_extra=0 if extra is None else len(extra)),
        grid=(n_tiles,),
        in_specs=in_specs,
        out_specs=[out_tile(tm, d), out_tile(tm, d), out_tile(tm, LANES), out_tile(SUBLANES, LANES)],
        out_shape=[jax.ShapeDtypeStruct((n_tiles * tm, d), F32),
                   jax.ShapeDtypeStruct((n_tiles * tm, d), BF16),
                   jax.ShapeDtypeStruct((n_tiles * tm, LANES), F32),
                   jax.ShapeDtypeStruct((n_tiles * SUBLANES, LANES), F32)],
        scratch_shapes=scratch,
        compiler_params=_cparams(("arbitrary",)),
        name="post_attn",
    )(*args)


def _route_plan(cnt_rows, n_tiles, n_xtiles):
    g = SUBLANES
    cnt = cnt_rows.reshape(n_tiles, g, LANES)[:, 0, :N_EXPERTS].astype(jnp.int32)
    cnt8 = (cnt + g - 1) // g * g
    loff = jnp.cumsum(cnt8, axis=1) - cnt8
    boff = jnp.cumsum(cnt8, axis=0) - cnt8
    tot = jnp.sum(cnt8, axis=0)
    region = (tot + MOE_TMX - 1) // MOE_TMX * MOE_TMX
    gend = jnp.cumsum(region)
    gbase = gend - region
    cum_tiles = gend // MOE_TMX
    j = jnp.arange(n_xtiles, dtype=jnp.int32)
    loff_rows = jnp.zeros((n_tiles, g, LANES), F32).at[:, :, :N_EXPERTS].set(loff[:, None, :].astype(F32))
    return dict(
        loff=loff.reshape(-1), dst=(gbase[None, :] + boff).reshape(-1), nch=(cnt8 // g).reshape(-1),
        ntot=jnp.sum(cnt8 // g, axis=1), zst=gbase + tot, znch=(region - tot) // g,
        ztot=jnp.sum((region - tot) // g).reshape(1),
        xtile_expert=jnp.minimum(jnp.sum(j[:, None] >= cum_tiles[None, :], axis=1), N_EXPERTS - 1).astype(jnp.int32),
        n_used=cum_tiles[-1:].astype(jnp.int32),
        loff_rows=loff_rows.reshape(n_tiles * g, LANES))


def _local_slots(route, loff_row):
    tm = route.shape[0]
    lane = lax.broadcasted_iota(jnp.int32, (tm, LANES), 1)
    e1 = lane == route[:, ROUTE_I1:ROUTE_I1 + 1].astype(jnp.int32)
    e2 = lane == route[:, ROUTE_I2:ROUTE_I2 + 1].astype(jnp.int32)
    earlier = (lax.broadcasted_iota(jnp.int32, (tm, tm), 1) < lax.broadcasted_iota(jnp.int32, (tm, tm), 0))
    earlier = earlier.astype(BF16)
    c1 = jnp.dot(earlier, e1.astype(BF16), preferred_element_type=F32)
    c2 = jnp.dot(earlier, e2.astype(BF16), preferred_element_type=F32)
    cnt1 = jnp.sum(e1.astype(F32), axis=0, keepdims=True)
    pos1 = jnp.sum(jnp.where(e1, c1 + loff_row, 0.0), axis=1, keepdims=True)
    pos2 = jnp.sum(jnp.where(e2, c2 + cnt1 + loff_row, 0.0), axis=1, keepdims=True)
    slot = lax.broadcasted_iota(jnp.int32, (tm, MOE_SLOTS), 1)
    return slot == pos1.astype(jnp.int32), slot == pos2.astype(jnp.int32)


def _split3(w):
    hi = w.astype(BF16).astype(F32)
    mid = (w - hi).astype(BF16).astype(F32)
    return hi, mid, (w - hi - mid).astype(BF16).astype(F32)


def _moe_scatter_kernel(loff_s, dst_s, nch_s, ntot_s, zst_s, znch_s, ztot_s, nused_s,
                        hn_ref, route_ref, loffv_ref, xs_hbm, buf_ref, zero_ref, sem, zsem, tsem):
    b = pl.program_id(0)
    nb = pl.num_programs(0)
    slot = b % 2
    g = SUBLANES
    d = hn_ref.shape[1]
    n_xtiles = xs_hbm.shape[0] // MOE_TMX

    def run_copy(s, src_row, dst_row):
        return pltpu.make_async_copy(buf_ref.at[s, pl.ds(src_row, g)], xs_hbm.at[pl.ds(dst_row, g)], sem.at[s])

    def zero_copy(dst_row):
        return pltpu.make_async_copy(zero_ref.at[pl.ds(0, g)], xs_hbm.at[pl.ds(dst_row, g)], zsem)

    def zero_tile_copy(j):
        return pltpu.make_async_copy(zero_ref, xs_hbm.at[pl.ds(pl.multiple_of(j * MOE_TMX, MOE_TMX), MOE_TMX)], tsem)

    def wait_tile(tile, s):
        lax.fori_loop(0, ntot_s[tile], lambda _, c: (run_copy(s, 0, 0).wait(), c)[1], 0)

    @pl.when(b == 0)
    def _():
        zero_ref[...] = jnp.zeros_like(zero_ref)

        def per_expert(e, c):
            def per_chunk(k, c2):
                zero_copy(pl.multiple_of(zst_s[e] + k * g, g)).start()
                return c2
            return lax.fori_loop(0, znch_s[e], per_chunk, c)
        lax.fori_loop(0, N_EXPERTS, per_expert, 0)
        lax.fori_loop(nused_s[0], n_xtiles, lambda j, c: (zero_tile_copy(j).start(), c)[1], 0)

    @pl.when(b >= 2)
    def _():
        wait_tile(b - 2, slot)

    route = route_ref[...]
    p1, p2 = _local_slots(route, loffv_ref[0:1, :])
    tn = (((0,), (0,)), ((), ()))
    picks = (p1 | p2).astype(BF16)
    buf_ref[slot, :, 0:d] = lax.dot_general(picks, hn_ref[...], tn, preferred_element_type=F32)
    lane = lax.broadcasted_iota(jnp.int32, route.shape, 1)
    meta = jnp.zeros((MOE_SLOTS, LANES), F32)
    for p, col in ((p1, ROUTE_W1), (p2, ROUTE_W2)):
        w = route[:, col:col + 1]
        parts = _split3(w)
        wm = sum(jnp.where(lane == k, part, 0.0) for k, part in enumerate(parts))
        meta = meta + lax.dot_general(p.astype(BF16), wm.astype(BF16), tn, preferred_element_type=F32)
    buf_ref[slot, :, d:d + LANES] = meta

    def per_expert(e, c):
        base = b * N_EXPERTS + e

        def per_chunk(k, c2):
            run_copy(slot, pl.multiple_of(loff_s[base] + k * g, g), pl.multiple_of(dst_s[base] + k * g, g)).start()
            return c2
        return lax.fori_loop(0, nch_s[base], per_chunk, c)
    lax.fori_loop(0, N_EXPERTS, per_expert, 0)

    @pl.when(b == nb - 1)
    def _():
        wait_tile(b, slot)

        @pl.when(nb >= 2)
        def _():
            wait_tile(b - 1, 1 - slot)
        lax.fori_loop(0, ztot_s[0], lambda _, c: (zero_copy(0).wait(), c)[1], 0)
        lax.fori_loop(nused_s[0], n_xtiles, lambda j, c: (zero_tile_copy(j).wait(), c)[1], 0)


def _moe_scatter(plan, hn_all, route_all, n_tiles, n_xtiles):
    d = hn_all.shape[1]
    tm = MOE_TM
    grid_spec = pltpu.PrefetchScalarGridSpec(
        num_scalar_prefetch=8,
        grid=(n_tiles,),
        in_specs=[pl.BlockSpec((tm, d), lambda i, *_: (i, 0)),
                  pl.BlockSpec((tm, LANES), lambda i, *_: (i, 0)),
                  pl.BlockSpec((SUBLANES, LANES), lambda i, *_: (i, 0))],
        out_specs=pl.BlockSpec(memory_space=pl.ANY),
        scratch_shapes=[pltpu.VMEM((2, MOE_SLOTS, d + LANES), F32),
                        pltpu.VMEM((MOE_TMX, d + LANES), F32),
                        pltpu.SemaphoreType.DMA((2,)),
                        pltpu.SemaphoreType.DMA(()),
                        pltpu.SemaphoreType.DMA(())])
    return pl.pallas_call(
        _moe_scatter_kernel,
        grid_spec=grid_spec,
        out_shape=jax.ShapeDtypeStruct((n_xtiles * MOE_TMX, d + LANES), F32),
        compiler_params=_cparams(("arbitrary",)),
        name="moe_scatter",
    )(plan["loff"], plan["dst"], plan["nch"], plan["ntot"], plan["zst"], plan["znch"], plan["ztot"], plan["n_used"],
      hn_all, route_all, plan["loff_rows"])


def _moe_experts_kernel(xe_s, nused_s, x_ref, w1_ref, w3_ref, w2_ref, y_ref, w1b_ref, w3b_ref, w2b_ref):
    j = pl.program_id(0)
    d = y_ref.shape[1]

    @pl.when(j < nused_s[0])
    def _():
        @pl.when((j == 0) | (xe_s[j] != xe_s[jnp.maximum(j - 1, 0)]))
        def _():
            w1b_ref[...] = w1_ref[...].astype(BF16)
            w3b_ref[...] = w3_ref[...].astype(BF16)
            w2b_ref[...] = w2_ref[...].astype(BF16)

        x = x_ref[:, 0:d].astype(BF16)
        gate = jnp.sum(x_ref[:, d:d + LANES], axis=1, keepdims=True)
        a = jnp.dot(x, w1b_ref[...], preferred_element_type=F32)
        b = jnp.dot(x, w3b_ref[...], preferred_element_type=F32)
        hh = (a * _sigmoid(a)) * b * gate
        y_ref[...] = jnp.dot(hh.astype(BF16), w2b_ref[...], preferred_element_type=F32)

    @pl.when(j >= nused_s[0])
    def _():
        y_ref[...] = jnp.zeros_like(y_ref)


def _moe_experts(plan, xs, w1, w3, w2, n_xtiles):
    ne, d, f = w1.shape
    last = lambda j, xe, nu: jnp.maximum(jnp.minimum(j, nu[0] - 1), 0)
    grid_spec = pltpu.PrefetchScalarGridSpec(
        num_scalar_prefetch=2,
        grid=(n_xtiles,),
        in_specs=[pl.BlockSpec((MOE_TMX, d + LANES), lambda j, xe, nu: (last(j, xe, nu), 0)),
                  pl.BlockSpec((None, d, f), lambda j, xe, nu: (xe[j], 0, 0)),
                  pl.BlockSpec((None, d, f), lambda j, xe, nu: (xe[j], 0, 0)),
                  pl.BlockSpec((None, f, d), lambda j, xe, nu: (xe[j], 0, 0))],
        out_specs=pl.BlockSpec((MOE_TMX, d), lambda j, xe, nu: (j, 0)),
        scratch_shapes=[pltpu.VMEM((d, f), BF16), pltpu.VMEM((d, f), BF16), pltpu.VMEM((f, d), BF16)])
    return pl.pallas_call(
        _moe_experts_kernel,
        grid_spec=grid_spec,
        out_shape=jax.ShapeDtypeStruct((n_xtiles * MOE_TMX, d), F32),
        compiler_params=_cparams(("arbitrary",)),
        name="moe_experts",
    )(plan["xtile_expert"], plan["n_used"], xs, w1, w3, w2)


def _moe_combine_kernel(loff_s, dst_s, nch_s, ntot_s, ys_hbm, route_ref, loffv_ref, h_ref, lnf_ref, y_ref,
                        buf_ref, sem, *, tile0, final_norm):
    b = pl.program_id(0)
    nb = pl.num_programs(0)
    slot = b % 2
    g = SUBLANES
    tile = b + tile0

    def run_copy(s, src_row, dst_row):
        return pltpu.make_async_copy(ys_hbm.at[pl.ds(src_row, g)], buf_ref.at[s, pl.ds(dst_row, g)], sem.at[s])

    def fetch(t, s):
        def per_expert(e, c):
            base = t * N_EXPERTS + e

            def per_chunk(k, c2):
                run_copy(s, pl.multiple_of(dst_s[base] + k * g, g), pl.multiple_of(loff_s[base] + k * g, g)).start()
                return c2
            return lax.fori_loop(0, nch_s[base], per_chunk, c)
        lax.fori_loop(0, N_EXPERTS, per_expert, 0)

    @pl.when(b == 0)
    def _():
        fetch(tile, slot)

    @pl.when(b + 1 < nb)
    def _():
        fetch(tile + 1, 1 - slot)

    lax.fori_loop(0, ntot_s[tile], lambda _, c: (run_copy(slot, 0, 0).wait(), c)[1], 0)

    def clear(k, c):
        buf_ref[slot, pl.ds(pl.multiple_of(k * g, g), g), :] = jnp.zeros((g, buf_ref.shape[2]), F32)
        return c
    lax.fori_loop(ntot_s[tile], MOE_SLOTS // g, clear, 0)

    p1, p2 = _local_slots(route_ref[...], loffv_ref[0:1, :])
    picks = (p1 | p2).astype(BF16)
    y = h_ref[...] + jnp.dot(picks, buf_ref[slot].astype(BF16), preferred_element_type=F32)
    if final_norm:
        y = _rms(y, lnf_ref[...])
    y_ref[...] = y


def _moe_combine(plan, ys, route_all, h_all, lnf, tile0, n_tiles, final_norm):
    d = h_all.shape[1]
    tm = MOE_TM
    grid_spec = pltpu.PrefetchScalarGridSpec(
        num_scalar_prefetch=4,
        grid=(n_tiles,),
        in_specs=[pl.BlockSpec(memory_space=pl.ANY),
                  pl.BlockSpec((tm, LANES), lambda i, *_: (i + tile0, 0)),
                  pl.BlockSpec((SUBLANES, LANES), lambda i, *_: (i + tile0, 0)),
                  pl.BlockSpec((tm, d), lambda i, *_: (i + tile0, 0)),
                  pl.BlockSpec((1, d), lambda i, *_: (0, 0))],
        out_specs=pl.BlockSpec((tm, d), lambda i, *_: (i, 0)),
        scratch_shapes=[pltpu.VMEM((2, MOE_SLOTS, d), F32), pltpu.SemaphoreType.DMA((2,))])
    return pl.pallas_call(
        functools.partial(_moe_combine_kernel, tile0=tile0, final_norm=final_norm),
        grid_spec=grid_spec,
        out_shape=jax.ShapeDtypeStruct((n_tiles * tm, d), F32),
        compiler_params=_cparams(("arbitrary",)),
        name="moe_combine",
    )(plan["loff"], plan["dst"], plan["nch"], plan["ntot"], ys, route_all, plan["loff_rows"], h_all,
      lnf.reshape(1, d))


def _prompt_states(kvt, n, t):
    hd = HEAD_DIM
    tail = kvt.shape[2]

    def state(k_lo, v_lo, rows, heads, win):
        w = min(win, t)
        kv = jnp.stack([kvt[:, k_lo:k_lo + rows, tail - w:], kvt[:, v_lo:v_lo + rows, tail - w:]], axis=1)
        return jnp.transpose(kv.reshape(n, 2, heads, hd, w), (0, 4, 1, 2, 3))

    out = [state(0, KA_W, KA_W, A_KV_HEADS, A_WINDOW)]
    for p, (win, _) in enumerate(B_PATTERNS):
        out.append(state(2 * KA_W + p * PB_W, 2 * KA_W + QB_W + p * PB_W, PB_W, B_HEADS_PER_PATTERN, win))
    return out


def _layer(xp, xs, caches, rel_bias, ln1, w_in, sinks, w_pa, w_pb, w_out, ln2, w_rg, b_rg, w_re, b_re,
           w1, w3, w2, lnf, final_norm):
    n, t, d = xp.shape
    ns = xs.shape[0]
    assert xs.shape[1] == 1 and OFF_GA + 2 * d == w_in.shape[1] and P_QA == 2 * d
    w_bf = w_in.astype(BF16)
    w_kvt = jnp.concatenate([w_in[:, OFF_KA:OFF_QB], w_in[:, OFF_KB:OFF_GA]], axis=1).T.astype(BF16)
    wpa, wpb, wout = w_pa.astype(BF16), w_pb.astype(BF16), w_out.astype(BF16)
    wr = jnp.zeros((d, LANES), F32).at[:, :N_EXPERTS].set(w_re).at[:, N_EXPERTS:N_EXPERTS + MOE_GROUPS].set(w_rg)
    br = jnp.zeros((1, LANES), F32).at[0, :N_EXPERTS].set(b_re).at[0, N_EXPERTS:N_EXPERTS + MOE_GROUPS].set(b_rg)

    tm = 256
    xp2 = xp.reshape(n * t, d)
    act, qkv2, qkv3, kvt = _in_proj_prompt(xp2, ln1, w_bf, w_kvt, n, t, tm)
    act4 = act.reshape(n, 1, t, ACT_W)
    bias_a = _band_bias(rel_bias[:, :A_HEADS], A_WINDOW - 1, 1)
    (oa,) = _band_attn(act4, bias_a, sinks, q_off=P_QA, k_off=P_KA, v_off=P_VA,
                       kv_heads=A_KV_HEADS, grp=A_GROUP, want_lse=False)
    obs, lses = [], []
    for p, (win, dil) in enumerate(B_PATTERNS):
        lo = A_HEADS + p * B_HEADS_PER_PATTERN
        bias_p = _band_bias(rel_bias[:, lo:lo + B_HEADS_PER_PATTERN], win // dil, dil)
        src, base = ((act4, P_B), (qkv2, 0), (qkv3, 0))[p]
        o, lse = _band_attn(src, bias_p, None, q_off=base, k_off=base + PB_W, v_off=base + 2 * PB_W,
                            kv_heads=B_HEADS_PER_PATTERN, grp=1, want_lse=True)
        obs.append(o)
        lses.append(lse)
    assert tm == MOE_TM and ns <= MOE_TM
    p_tiles = n * t // MOE_TM
    n_tiles = p_tiles + 1
    m_all = n_tiles * MOE_TM
    st_p = _prompt_states(kvt, n, t)

    xs2 = xs.reshape(ns, d)
    proj_s = _in_proj_sample(xs2, ln1, w_bf)
    oa_s, ob_s, lse_s, st_s = _sample_mix(proj_s, caches, rel_bias, sinks)
    rows = lambda a: jnp.pad(a, ((0, MOE_TM - ns), (0, 0)))
    obs_s = [rows(ob_s[:, p].astype(BF16)).reshape(1, 1, MOE_TM, PB_W) for p in range(N_PAT)]
    lses_s = [rows(jnp.repeat(lse_s[:, p], HEAD_DIM, axis=-1)).reshape(1, 1, MOE_TM, PB_W) for p in range(N_PAT)]
    gates_s = rows(proj_s[:, OFF_GA:].astype(BF16))
    routed_s = _post_attn(rows(oa_s.astype(BF16)), obs_s, lses_s, gates_s, 0, rows(xs2),
                          wpa, wpb, wout, ln2, wr, br, MOE_TM, 1, None)
    h_all, hn_all, route_all, cnt_all = _post_attn(oa.reshape(n * t, QA_W), obs, lses, act, 0, xp2,
                                                   wpa, wpb, wout, ln2, wr, br, tm, t // tm, routed_s)

    max_rows = 2 * m_all + n_tiles * N_EXPERTS * (SUBLANES - 1) + N_EXPERTS * (MOE_TMX - SUBLANES)
    n_xtiles = -(-max_rows // MOE_TMX)
    plan = _route_plan(cnt_all, n_tiles, n_xtiles)
    xs_sorted = _moe_scatter(plan, hn_all, route_all, n_tiles, n_xtiles)
    ys_sorted = _moe_experts(plan, xs_sorted, w1, w3, w2, n_xtiles)
    yp = _moe_combine(plan, ys_sorted, route_all, h_all, lnf, 0, p_tiles, final_norm).reshape(n, t, d)
    ys = _moe_combine(plan, ys_sorted, route_all, h_all, lnf, p_tiles, 1, final_norm)[:ns].reshape(ns, 1, d)
    return yp, ys, st_p, st_s


def kernel(x_prompt, x_sample, cache_a_kv, cache_b1_kv, cache_b2_kv, cache_b3_kv, rel_bias, ln1_g, w_in, sinks,
           w_pa, w_pb, w_out, ln2_g, w_rg, b_rg, w_re, b_re, w1, w3, w2, lnf_g):
    depth = w_in.shape[0]
    assert depth >= 1
    xp, xs = x_prompt, x_sample
    new_p = [[] for _ in range(4)]
    new_s = [[] for _ in range(4)]
    for l in range(depth):
        caches = (cache_a_kv[l], cache_b1_kv[l], cache_b2_kv[l], cache_b3_kv[l])
        xp, xs, st_p, st_s = _layer(xp, xs, caches, rel_bias, ln1_g[l], w_in[l], sinks[l], w_pa[l], w_pb[l],
                                    w_out[l], ln2_g[l], w_rg[l], b_rg[l], w_re[l], b_re[l], w1[l], w3[l], w2[l],
                                    lnf_g, l == depth - 1)
        for i in range(4):
            new_p[i].append(st_p[i])
            new_s[i].append(st_s[i])
    a_p, b1_p, b2_p, b3_p = [jnp.stack(v) for v in new_p]
    a_s, b1_s, b2_s, b3_s = [jnp.stack(v) for v in new_s]
    return (xp, xs, a_p, a_s, b1_p, b1_s, b2_p, b2_s, b3_p, b3_s)
```

```python
import functools
import math

import numpy as np
import jax
import jax.numpy as jnp
from jax import lax
from jax.experimental import pallas as pl
from jax.experimental.pallas import tpu as pltpu

F32 = jnp.float32
BF16 = jnp.bfloat16

HEAD_DIM = 64
A_HEADS = 8
A_KV_HEADS = 2
A_GROUP = A_HEADS // A_KV_HEADS
A_WINDOW = 128
B_PATTERNS = ((128, 1), (512, 4), (2048, 16))
N_PAT = len(B_PATTERNS)
B_HEADS_PER_PATTERN = 4
B_HEADS = B_HEADS_PER_PATTERN * N_PAT
BLOCK = 128
ATTN_CHAINS = 16
NUM_BUCKETS = 32
MAX_DISTANCE = 2048
MOE_GROUPS = 4
EXPERTS_PER_GROUP = 8
N_EXPERTS = MOE_GROUPS * EXPERTS_PER_GROUP
EPS = 1e-6
NEG = -1e30
LANES = 128
SUBLANES = 8
ROUTE_I1, ROUTE_I2, ROUTE_W1, ROUTE_W2 = 0, 1, 2, 3
MOE_TM = 256
MOE_SLOTS = 2 * MOE_TM + N_EXPERTS * SUBLANES
MOE_TMX = 256
Q_SCALE = HEAD_DIM ** -0.5

QA_W = A_HEADS * HEAD_DIM
KA_W = A_KV_HEADS * HEAD_DIM
QB_W = B_HEADS * HEAD_DIM
PB_W = B_HEADS_PER_PATTERN * HEAD_DIM
QKV_W = 3 * PB_W
OFF_KA = QA_W
OFF_QB = OFF_KA + 2 * KA_W
OFF_KB = OFF_QB + QB_W
OFF_VB = OFF_KB + QB_W
OFF_GA = OFF_VB + QB_W
P_QA = 2048
P_KA = P_QA + QA_W
P_VA = P_KA + KA_W
P_B = P_VA + KA_W
ACT_W = P_B + QKV_W
KVT_ROWS = 2 * KA_W + 2 * QB_W

VMEM_LIMIT = 56 * 1024 * 1024


def _cparams(sem):
    return pltpu.CompilerParams(dimension_semantics=sem, vmem_limit_bytes=VMEM_LIMIT)


def _bucket_np(dist):
    dist = np.asarray(dist, np.int64)
    max_exact = NUM_BUCKETS // 2
    df = np.maximum(dist, max_exact).astype(np.float64)
    large = max_exact + (np.log(df / max_exact) / math.log(MAX_DISTANCE / max_exact)
                         * (NUM_BUCKETS - max_exact)).astype(np.int64)
    return np.where(dist < max_exact, dist, np.minimum(large, NUM_BUCKETS - 1))


def _table_rows(table_cols, dist, valid):
    onehot = (_bucket_np(dist)[:, None] == np.arange(NUM_BUCKETS)[None, :]).astype(np.float32)
    rows = jnp.einsum("ck,kh->hc", jnp.asarray(onehot), table_cols.astype(F32), precision=lax.Precision.HIGHEST)
    return jnp.where(jnp.asarray(valid)[None, :], rows, NEG)


def _band_bias(table_cols, max_dist, dilation):
    period = 3 * BLOCK
    m = np.arange(period)
    k = np.where(m < 2 * BLOCK, m, m - period)
    dist = BLOCK - k
    valid = (dist >= 0) & (dist <= max_dist) & (m != 2 * BLOCK)
    v = _table_rows(table_cols, np.clip(dist, 0, None) * dilation, valid)
    heads = v.shape[0]
    flat = jnp.tile(v, (1, BLOCK))[:, :BLOCK * (period - 1)]
    return flat.reshape(heads, BLOCK, period - 1)[:, :, :2 * BLOCK]


def _decode_bias(table_cols, width, dilation, first_valid):
    c = np.arange(width)
    valid = (c % dilation == 0) & (c >= first_valid)
    rows = _table_rows(table_cols, width - c, valid)
    self_bias = _table_rows(table_cols, np.zeros((1,), np.int64), np.ones((1,), bool))
    return rows[:, None, :], self_bias[:, None, :]


def _rms(x, g):
    return (x * lax.rsqrt(jnp.mean(x * x, axis=-1, keepdims=True) + EPS)) * g


def _sigmoid(x):
    return 1.0 / (1.0 + jnp.exp(-x))


def _in_proj_kernel(x_ref, g_ref, w_ref, wkvt_ref, act_ref, qkv2_ref, qkv3_ref, kvt_ref, ys_ref, *,
                    tiles_per_seq, tail_tiles):
    tm = x_ref.shape[0]
    g = g_ref[...]
    xb = _rms(x_ref[...], g).astype(BF16)

    def proj(xv, lo, hi):
        return jnp.dot(xv, w_ref[:, lo:hi], preferred_element_type=F32)

    for c in range(0, P_QA, 1024):
        act_ref[:, c:c + 1024] = _sigmoid(proj(xb, OFF_GA + c, OFF_GA + c + 1024)).astype(BF16)
    act_ref[:, P_QA:P_KA] = (proj(xb, 0, OFF_KA) * Q_SCALE).astype(BF16)
    act_ref[:, P_KA:P_B] = proj(xb, OFF_KA, OFF_QB).astype(BF16)

    def qkv_parts(p):
        lo = p * PB_W
        return (proj(xb, OFF_QB + lo, OFF_QB + lo + PB_W) * Q_SCALE,
                proj(xb, OFF_KB + lo, OFF_KB + lo + PB_W),
                proj(xb, OFF_VB + lo, OFF_VB + lo + PB_W))

    for j, part in enumerate(qkv_parts(0)):
        act_ref[:, P_B + j * PB_W:P_B + (j + 1) * PB_W] = part.astype(BF16)

    for p, out_ref in ((1, qkv2_ref), (2, qkv3_ref)):
        dil = B_PATTERNS[p][1]
        rows = tm // dil
        for j, part in enumerate(qkv_parts(p)):
            for c in range(PB_W // LANES):
                ys_ref[j * (PB_W // LANES) + c] = part[:, c * LANES:(c + 1) * LANES]
        for c in range(QKV_W // LANES):
            cs = slice(c * LANES, (c + 1) * LANES)
            for r in range(dil):
                out_ref[r, :, cs] = ys_ref[c, pl.ds(r, rows, stride=dil), :].astype(BF16)

    @pl.when(pl.program_id(0) % tiles_per_seq >= tiles_per_seq - tail_tiles)
    def _():
        kvt_ref[...] = lax.dot_general(wkvt_ref[...], xb, (((1,), (1,)), ((), ())), preferred_element_type=F32)


def _in_proj_prompt(x2d, ln_g, w_bf, w_kvt, n, t, tm):
    m, d = x2d.shape
    tps = t // tm
    tail = min(max(w for w, _ in B_PATTERNS), t)
    assert t % tm == 0 and tail % tm == 0 and all(tm % (16 * dl) == 0 for _, dl in B_PATTERNS)
    tail_tiles = tail // tm
    d2, d3 = B_PATTERNS[1][1], B_PATTERNS[2][1]
    return pl.pallas_call(
        functools.partial(_in_proj_kernel, tiles_per_seq=tps, tail_tiles=tail_tiles),
        grid=(m // tm,),
        in_specs=[pl.BlockSpec((tm, d), lambda i: (i, 0)),
                  pl.BlockSpec((1, d), lambda i: (0, 0)),
                  pl.BlockSpec(w_bf.shape, lambda i: (0, 0)),
                  pl.BlockSpec(w_kvt.shape, lambda i: (0, 0))],
        out_specs=[pl.BlockSpec((tm, ACT_W), lambda i: (i, 0)),
                   pl.BlockSpec((None, d2, tm // d2, QKV_W), lambda i: (i // tps, 0, i % tps, 0)),
                   pl.BlockSpec((None, d3, tm // d3, QKV_W), lambda i: (i // tps, 0, i % tps, 0)),
                   pl.BlockSpec((None, KVT_ROWS, tm),
                                lambda i: (i // tps, 0, jnp.maximum(i % tps - (tps - tail_tiles), 0)))],
        out_shape=[jax.ShapeDtypeStruct((m, ACT_W), BF16),
                   jax.ShapeDtypeStruct((n, d2, t // d2, QKV_W), BF16),
                   jax.ShapeDtypeStruct((n, d3, t // d3, QKV_W), BF16),
                   jax.ShapeDtypeStruct((n, KVT_ROWS, tail), F32)],
        scratch_shapes=[pltpu.VMEM((QKV_W // LANES, tm, LANES), F32)],
        compiler_params=_cparams(("arbitrary",)),
        name="in_proj",
    )(x2d, ln_g.reshape(1, d), w_bf, w_kvt)


def _in_proj_sample_kernel(x_ref, g_ref, w_ref, y_ref):
    xb = _rms(x_ref[...], g_ref[...]).astype(BF16)
    y_ref[:, :OFF_GA] = jnp.dot(xb, w_ref[:, :OFF_GA], preferred_element_type=F32)
    for lo, hi in ((0, OFF_KA), (OFF_QB, OFF_KB)):
        y_ref[:, lo:hi] = y_ref[:, lo:hi] * Q_SCALE
    for lo in range(OFF_GA, w_ref.shape[1], 1024):
        y_ref[:, lo:lo + 1024] = _sigmoid(jnp.dot(xb, w_ref[:, lo:lo + 1024], preferred_element_type=F32))


def _in_proj_sample(x2d, ln_g, w_bf):
    m, d = x2d.shape
    return pl.pallas_call(
        _in_proj_sample_kernel,
        grid=(1,),
        in_specs=[pl.BlockSpec((m, d), lambda i: (0, 0)),
                  pl.BlockSpec((1, d), lambda i: (0, 0)),
                  pl.BlockSpec(w_bf.shape, lambda i: (0, 0))],
        out_specs=pl.BlockSpec((m, w_bf.shape[1]), lambda i: (0, 0)),
        out_shape=jax.ShapeDtypeStruct((m, w_bf.shape[1]), F32),
        compiler_params=_cparams(("arbitrary",)),
        name="in_proj_sample",
    )(x2d, ln_g.reshape(1, d), w_bf)


def _band_attn_kernel(*refs, kv_heads, grp, has_sink, want_lse):
    if has_sink:
        sink_ref, refs = refs[0], refs[1:]
    q_ref, kp_ref, kc_ref, vp_ref, vc_ref, bias_ref, o_ref = refs[:7]
    lse_ref = refs[7] if want_lse else None
    step = pl.program_id(2)
    hd = HEAD_DIM
    nt = (((1,), (1,)), ((), ()))
    chains = [(sub, kv * grp + g, slice(kv * hd, (kv + 1) * hd))
              for sub in range(q_ref.shape[0] // BLOCK) for kv in range(kv_heads) for g in range(grp)]
    scores = []
    for sub, h, ks in chains:
        rows = slice(sub * BLOCK, (sub + 1) * BLOCK)
        q = q_ref[rows, h * hd:(h + 1) * hd]
        kp = kp_ref[:, ks] if sub == 0 else kc_ref[(sub - 1) * BLOCK:sub * BLOCK, ks]
        sp = lax.dot_general(q, kp, nt, preferred_element_type=F32) + bias_ref[h, :, 0:BLOCK]
        sc = lax.dot_general(q, kc_ref[rows, ks], nt, preferred_element_type=F32) + bias_ref[h, :, BLOCK:2 * BLOCK]
        if sub == 0:
            sp = jnp.where(step > 0, sp, NEG)
        scores.append((sp, sc))
    maxes = []
    for (sub, h, ks), (sp, sc) in zip(chains, scores):
        m = jnp.maximum(jnp.max(sp, axis=-1, keepdims=True), jnp.max(sc, axis=-1, keepdims=True))
        maxes.append(jnp.maximum(m, sink_ref[h]) if has_sink else m)
    probs = []
    for (sub, h, ks), (sp, sc), m in zip(chains, scores, maxes):
        pp, pc = jnp.exp(sp - m), jnp.exp(sc - m)
        den = jnp.sum(pp, axis=-1, keepdims=True) + jnp.sum(pc, axis=-1, keepdims=True)
        if has_sink:
            den = den + jnp.exp(sink_ref[h] - m)
        probs.append((pp.astype(BF16), pc.astype(BF16), den))
    for (sub, h, ks), (pp, pc, den), m in zip(chains, probs, maxes):
        rows = slice(sub * BLOCK, (sub + 1) * BLOCK)
        vp = vp_ref[:, ks] if sub == 0 else vc_ref[(sub - 1) * BLOCK:sub * BLOCK, ks]
        o = (jnp.dot(pp, vp, preferred_element_type=F32)
             + jnp.dot(pc, vc_ref[rows, ks], preferred_element_type=F32))
        o_ref[rows, h * hd:(h + 1) * hd] = (o / den).astype(o_ref.dtype)
        if want_lse:
            lse_ref[rows, h * hd:(h + 1) * hd] = jnp.broadcast_to(m + jnp.log(den), (BLOCK, hd))


def _band_attn(src, bias, sink, *, q_off, k_off, v_off, kv_heads, grp, want_lse):
    n, dil, l, cols = src.shape
    sub = max(1, ATTN_CHAINS // (kv_heads * grp))
    while l % (sub * BLOCK):
        sub //= 2
    rows = sub * BLOCK
    assert sub >= 1 and l % rows == 0
    nb = l // rows
    qw = kv_heads * grp * HEAD_DIM
    kw = kv_heads * HEAD_DIM
    assert q_off % qw == 0 and k_off % kw == 0 and v_off % kw == 0
    qb, kb, vb = q_off // qw, k_off // kw, v_off // kw
    prev = lambda b: jnp.maximum(sub * b - 1, 0)
    in_specs = [
        pl.BlockSpec((None, None, rows, qw), lambda i, r, b: (i, r, b, qb)),
        pl.BlockSpec((None, None, BLOCK, kw), lambda i, r, b: (i, r, prev(b), kb)),
        pl.BlockSpec((None, None, rows, kw), lambda i, r, b: (i, r, b, kb)),
        pl.BlockSpec((None, None, BLOCK, kw), lambda i, r, b: (i, r, prev(b), vb)),
        pl.BlockSpec((None, None, rows, kw), lambda i, r, b: (i, r, b, vb)),
        pl.BlockSpec(bias.shape, lambda i, r, b: (0, 0, 0)),
    ]
    args = [src, src, src, src, src, bias]
    has_sink = sink is not None
    if has_sink:
        in_specs = [pl.BlockSpec(memory_space=pltpu.SMEM)] + in_specs
        args = [sink.astype(F32)] + args
    out_specs = [pl.BlockSpec((None, None, rows, qw), lambda i, r, b: (i, r, b, 0))]
    out_shape = [jax.ShapeDtypeStruct((n, dil, l, qw), BF16)]
    if want_lse:
        out_specs.append(pl.BlockSpec((None, None, rows, qw), lambda i, r, b: (i, r, b, 0)))
        out_shape.append(jax.ShapeDtypeStruct((n, dil, l, qw), F32))
    return pl.pallas_call(
        functools.partial(_band_attn_kernel, kv_heads=kv_heads, grp=grp, has_sink=has_sink, want_lse=want_lse),
        grid=(n, dil, nb),
        in_specs=in_specs,
        out_specs=out_specs,
        out_shape=out_shape,
        compiler_params=_cparams(("arbitrary", "arbitrary", "arbitrary")),
        name=f"band_attn_d{dil}",
    )(*args)


COL_QA = 0
COL_QB = COL_QA + A_HEADS
COL_KA = COL_QB + B_HEADS
COL_VA = COL_KA + A_KV_HEADS
COL_KB = COL_VA + A_KV_HEADS
COL_VB = COL_KB + B_HEADS
N_COLS = COL_VB + B_HEADS


def _attend_cached(kt, vt, q, k_new, v_new, bias, self_bias, sink, write):
    s = jnp.sum(kt * q, axis=1, keepdims=True) + bias
    s_new = jnp.sum(k_new * q, axis=1, keepdims=True) + self_bias
    yield
    m = jnp.maximum(jnp.max(s, axis=2, keepdims=True), s_new)
    if sink is not None:
        m = jnp.maximum(m, sink)
    yield
    p = jnp.exp(s - m)
    p_new = jnp.exp(s_new - m)
    den = jnp.sum(p, axis=2, keepdims=True) + p_new
    if sink is not None:
        den = den + jnp.exp(sink - m)
    yield
    o = (jnp.sum(vt * p, axis=2, keepdims=True) + v_new * p_new) / den
    write(o, m + jnp.log(den))
    yield


def _sample_mix_kernel(cols_ref, ca_ref, c1_ref, c2_ref, c3_ref, ba_ref, sa_ref, sink_ref, b1_ref, b2_ref, b3_ref,
                       sb_ref, o_ref, lse_ref, ra_ref, r1_ref, r2_ref, r3_ref):
    cols = cols_ref[...]

    def stack(js):
        return jnp.stack([cols[:, j:j + 1] for j in js])

    o_ref[...] = jnp.zeros_like(o_ref)
    lse_ref[...] = jnp.zeros_like(lse_ref)

    def write_a(o, _):
        for h in range(A_HEADS):
            o_ref[:, COL_QA + h:COL_QA + h + 1] = o[h]

    def write_b(p):
        def write(o, lse):
            for h in range(B_HEADS_PER_PATTERN):
                j = p * B_HEADS_PER_PATTERN + h
                o_ref[:, COL_QB + j:COL_QB + j + 1] = o[h]
                lse_ref[0:1, j:j + 1] = lse[h]
        return write

    kv_of = [h // A_GROUP for h in range(A_HEADS)]
    stages = [_attend_cached(jnp.stack([ca_ref[0, kv] for kv in kv_of]), jnp.stack([ca_ref[1, kv] for kv in kv_of]),
                             stack(range(COL_QA, COL_QA + A_HEADS)), stack([COL_KA + kv for kv in kv_of]),
                             stack([COL_VA + kv for kv in kv_of]), ba_ref[...], sa_ref[...], sink_ref[...], write_a)]
    for p, (c_ref, b_ref) in enumerate(((c1_ref, b1_ref), (c2_ref, b2_ref), (c3_ref, b3_ref))):
        js = range(p * B_HEADS_PER_PATTERN, (p + 1) * B_HEADS_PER_PATTERN)
        stages.append(_attend_cached(c_ref[0], c_ref[1], stack([COL_QB + j for j in js]),
                                     stack([COL_KB + j for j in js]), stack([COL_VB + j for j in js]),
                                     b_ref[...], sb_ref[p], None, write_b(p)))

    def roll_plane(c_ref, r_ref, i, h, new_col):
        w = c_ref.shape[-1]
        x = c_ref[i, h]
        lane = lax.broadcasted_iota(jnp.int32, x.shape, 1)
        r_ref[i, h] = jnp.where(lane == w - 1, cols[:, new_col:new_col + 1], pltpu.roll(x, w - 1, 1))

    planes = []
    for ci, (c_ref, r_ref) in enumerate(((ca_ref, ra_ref), (c1_ref, r1_ref), (c2_ref, r2_ref), (c3_ref, r3_ref))):
        for i, (first_a, first_b) in enumerate(((COL_KA, COL_KB), (COL_VA, COL_VB))):
            for h in range(c_ref.shape[1]):
                new_col = first_a + h if ci == 0 else first_b + (ci - 1) * B_HEADS_PER_PATTERN + h
                planes.append((c_ref.shape[-1], functools.partial(roll_plane, c_ref, r_ref, i, h, new_col)))

    total = sum(w for w, _ in planes)
    n_slots = 4 * len(stages)
    done, k = 0, 0
    for slot in range(n_slots):
        next(stages[slot // 4])
        while k < len(planes) and done < total * (slot + 1) // n_slots:
            done += planes[k][0]
            planes[k][1]()
            k += 1
    assert k == len(planes)


def _sample_mix(proj_s, caches, rel_bias, sinks):
    n = proj_s.shape[0]
    hd = HEAD_DIM
    vecs = jnp.concatenate([proj_s[:, :OFF_KA], proj_s[:, OFF_QB:OFF_KB], proj_s[:, OFF_KA:OFF_QB],
                            proj_s[:, OFF_KB:OFF_GA]], axis=1).reshape(n, N_COLS, hd)
    cols = jnp.pad(jnp.transpose(vecs, (0, 2, 1)), ((0, 0), (0, 0), (0, LANES - N_COLS)))
    cts = [jnp.transpose(c, (0, 2, 3, 4, 1)) for c in caches]

    ba, sa = _decode_bias(rel_bias[:, :A_HEADS], A_WINDOW, 1, 1)
    sink = sinks.astype(F32).reshape(A_HEADS, 1, 1)
    bbs, sbs = [], []
    for p, (win, dil) in enumerate(B_PATTERNS):
        lo = A_HEADS + p * B_HEADS_PER_PATTERN
        assert caches[1 + p].shape[1] == win == BLOCK * dil
        b, s = _decode_bias(rel_bias[:, lo:lo + B_HEADS_PER_PATTERN], win, dil, 0)
        bbs.append(b)
        sbs.append(s)
    sb = jnp.stack(sbs)

    def per_seq(a):
        nd = a.ndim
        return pl.BlockSpec((None,) + a.shape[1:], lambda i: (i,) + (0,) * (nd - 1))

    def full(a):
        nd = a.ndim
        return pl.BlockSpec(a.shape, lambda i: (0,) * nd)

    seq_in = [cols] + cts
    consts = [ba, sa, sink] + bbs + [sb]
    out_shape = [jax.ShapeDtypeStruct((n, hd, LANES), F32), jax.ShapeDtypeStruct((n, SUBLANES, LANES), F32)]
    out_shape += [jax.ShapeDtypeStruct(c.shape, c.dtype) for c in cts]
    outs = pl.pallas_call(
        _sample_mix_kernel,
        grid=(n,),
        in_specs=[per_seq(a) for a in seq_in] + [full(a) for a in consts],
        out_specs=[per_seq(s) for s in out_shape],
        out_shape=out_shape,
        compiler_params=_cparams(("arbitrary",)),
        name="sample_mix",
    )(*seq_in, *consts)
    o_rows = jnp.transpose(outs[0][:, :, :COL_KA], (0, 2, 1))
    oa = o_rows[:, COL_QA:COL_QB].reshape(n, QA_W)
    ob = o_rows[:, COL_QB:COL_KA].reshape(n, N_PAT, PB_W)
    lse = outs[1][:, 0, :B_HEADS].reshape(n, N_PAT, B_HEADS_PER_PATTERN)
    rolled = [jnp.transpose(r, (0, 4, 1, 2, 3)) for r in outs[2:]]
    return oa, ob, lse, rolled


def _post_attn_kernel(oa_ref, o1_ref, o2_ref, o3_ref, l1_ref, l2_ref, l3_ref, ga_ref, gb_ref, x_ref,
                      wpa_ref, wpb_ref, wout_ref, ln2_ref, wr_ref, br_ref, *rest, n_extra):
    outs = rest[n_extra:n_extra + 4]
    h_ref, hn_ref, route_ref, cnt_ref = outs
    scr_ref = rest[n_extra + 4]
    if n_extra:
        own_tile = pl.program_id(0) < pl.num_programs(0) - 1

        @pl.when(jnp.logical_not(own_tile))
        def _():
            for dst, src in zip(outs, rest[:n_extra]):
                dst[...] = src[...]

        @pl.when(own_tile)
        def _():
            _post_attn_tile(oa_ref, o1_ref, o2_ref, o3_ref, l1_ref, l2_ref, l3_ref, ga_ref, gb_ref, x_ref,
                            wpa_ref, wpb_ref, wout_ref, ln2_ref, wr_ref, br_ref, *outs, scr_ref)
    else:
        _post_attn_tile(oa_ref, o1_ref, o2_ref, o3_ref, l1_ref, l2_ref, l3_ref, ga_ref, gb_ref, x_ref,
                        wpa_ref, wpb_ref, wout_ref, ln2_ref, wr_ref, br_ref, *outs, scr_ref)


def _post_attn_tile(oa_ref, o1_ref, o2_ref, o3_ref, l1_ref, l2_ref, l3_ref, ga_ref, gb_ref, x_ref,
                    wpa_ref, wpb_ref, wout_ref, ln2_ref, wr_ref, br_ref, h_ref, hn_ref, route_ref, cnt_ref, scr_ref):
    tm = x_ref.shape[0]
    chunks = PB_W // LANES

    def token_major(ref, slot):
        dil = ref.shape[0]
        if dil == 1:
            return ref[0].astype(F32)
        for c in range(chunks):
            for r in range(dil):
                scr_ref[slot * chunks + c, pl.ds(r, tm // dil, stride=dil), :] = (
                    ref[r, :, c * LANES:(c + 1) * LANES].astype(F32))
        return jnp.concatenate([scr_ref[slot * chunks + c] for c in range(chunks)], axis=1)

    o1, o2, o3 = (token_major(r, s) for s, r in enumerate((o1_ref, o2_ref, o3_ref)))
    l1, l2, l3 = (token_major(r, 3 + s) for s, r in enumerate((l1_ref, l2_ref, l3_ref)))
    m = jnp.maximum(jnp.maximum(l1, l2), l3)
    a1, a2, a3 = jnp.exp(l1 - m), jnp.exp(l2 - m), jnp.exp(l3 - m)
    ob = (a1 * o1 + a2 * o2 + a3 * o3) / (a1 + a2 + a3)
    ya = jnp.dot(oa_ref[...], wpa_ref[...], preferred_element_type=F32)
    yb = jnp.dot(ob.astype(BF16), wpb_ref[...], preferred_element_type=F32)
    merged = ga_ref[...].astype(F32) * ya + gb_ref[...].astype(F32) * yb
    h = x_ref[...] + jnp.dot(merged.astype(BF16), wout_ref[...], preferred_element_type=F32)
    h_ref[...] = h
    hn = _rms(h, ln2_ref[...])
    hn_ref[...] = hn.astype(BF16)

    hn_hi = hn.astype(BF16)
    hn_lo = (hn - hn_hi.astype(F32)).astype(BF16)
    logits = (jnp.dot(hn_hi, wr_ref[0], preferred_element_type=F32)
              + jnp.dot(hn_lo, wr_ref[0], preferred_element_type=F32)
              + jnp.dot(hn_hi, wr_ref[1], preferred_element_type=F32)) + br_ref[...]
    lane = lax.broadcasted_iota(jnp.int32, logits.shape, 1)
    is_grp = (lane >= N_EXPERTS) & (lane < N_EXPERTS + MOE_GROUPS)
    lg = jnp.where(is_grp, logits, NEG)
    gmax = jnp.max(lg, axis=-1, keepdims=True)
    g_lane = jnp.min(jnp.where(lg == gmax, lane, LANES), axis=-1, keepdims=True)
    p_g = 1.0 / jnp.sum(jnp.where(is_grp, jnp.exp(lg - gmax), 0.0), axis=-1, keepdims=True)
    e_lo = (g_lane - N_EXPERTS) * EXPERTS_PER_GROUP
    in_grp = (lane >= e_lo) & (lane < e_lo + EXPERTS_PER_GROUP)
    le = jnp.where(in_grp, logits, NEG)
    v1 = jnp.max(le, axis=-1, keepdims=True)
    i1 = jnp.min(jnp.where(le == v1, lane, LANES), axis=-1, keepdims=True)
    le2 = jnp.where(lane == i1, NEG, le)
    v2 = jnp.max(le2, axis=-1, keepdims=True)
    i2 = jnp.min(jnp.where(le2 == v2, lane, LANES), axis=-1, keepdims=True)
    e2 = jnp.exp(v2 - v1)
    w1 = p_g / (1.0 + e2)
    w2 = p_g * e2 / (1.0 + e2)
    route = jnp.where(lane == ROUTE_I1, i1.astype(F32), jnp.where(lane == ROUTE_I2, i2.astype(F32), 0.0))
    route_ref[...] = route + jnp.where(lane == ROUTE_W1, w1, 0.0) + jnp.where(lane == ROUTE_W2, w2, 0.0)
    picks = (lane == i1).astype(F32) + (lane == i2).astype(F32)
    cnt_ref[...] = jnp.broadcast_to(jnp.sum(picks, axis=0, keepdims=True), cnt_ref.shape)


def _post_attn(oa, obs, lses, gates_src, ga_blk, x2d, wpa, wpb, wout, ln2, wr, br, tm, tiles_per_seq, extra):
    m, d = x2d.shape
    tps = tiles_per_seq
    own = m // tm
    n_tiles = own + (extra is not None)
    mine = lambda i: jnp.minimum(i, own - 1)

    def tile(w, col=0):
        return pl.BlockSpec((tm, w), lambda i: (mine(i), col))

    def out_tile(rows, w):
        return pl.BlockSpec((rows, w), lambda i: (i, 0))

    def full(a):
        nd = a.ndim
        return pl.BlockSpec(a.shape, lambda i: (0,) * nd)

    def residue(a):
        dil = a.shape[1]
        return pl.BlockSpec((None, dil, tm // dil, PB_W), lambda i: (mine(i) // tps, 0, mine(i) % tps, 0))

    weights = [wpa, wpb, wout, ln2.reshape(1, d), wr, br]
    scratch = [pltpu.VMEM((6 * PB_W // LANES, tm, LANES), F32)]
    in_specs = ([tile(QA_W)] + [residue(a) for a in obs] + [residue(a) for a in lses]
                + [tile(d, ga_blk), tile(d, ga_blk + 1), tile(d)] + [full(w) for w in weights])
    args = [oa, *obs, *lses, gates_src, gates_src, x2d, *weights]
    if extra is not None:
        in_specs = in_specs + [full(a) for a in extra]
        args = args + list(extra)
    return pl.pallas_call(
        functools.partial(_post_attn_kernel, n_extra=0 if extra is None else len(extra)),
        grid=(n_tiles,),
        in_specs=in_specs,
        out_specs=[out_tile(tm, d), out_tile(tm, d), out_tile(tm, LANES), out_tile(SUBLANES, LANES)],
        out_shape=[jax.ShapeDtypeStruct((n_tiles * tm, d), F32),
                   jax.ShapeDtypeStruct((n_tiles * tm, d), BF16),
                   jax.ShapeDtypeStruct((n_tiles * tm, LANES), F32),
                   jax.ShapeDtypeStruct((n_tiles * SUBLANES, LANES), F32)],
        scratch_shapes=scratch,
        compiler_params=_cparams(("arbitrary",)),
        name="post_attn",
    )(*args)


def _route_plan(cnt_rows, n_tiles, n_xtiles):
    g = SUBLANES
    cnt = cnt_rows.reshape(n_tiles, g, LANES)[:, 0, :N_EXPERTS].astype(jnp.int32)
    cnt8 = (cnt + g - 1) // g * g
    loff = jnp.cumsum(cnt8, axis=1) - cnt8
    boff = jnp.cumsum(cnt8, axis=0) - cnt8
    tot = jnp.sum(cnt8, axis=0)
    region = (tot + MOE_TMX - 1) // MOE_TMX * MOE_TMX
    gend = jnp.cumsum(region)
    gbase = gend - region
    cum_tiles = gend // MOE_TMX
    j = jnp.arange(n_xtiles, dtype=jnp.int32)
    loff_rows = jnp.zeros((n_tiles, g, LANES), F32).at[:, :, :N_EXPERTS].set(loff[:, None, :].astype(F32))
    k8 = g * jnp.arange(MOE_SLOTS // g, dtype=jnp.int32)
    run_end = loff + cnt8
    e_of_k = jnp.sum(k8[None, :, None] >= run_end[:, None, :], axis=2)
    shift = gbase[None, :] + boff - loff
    picked = jnp.sum(jnp.where(e_of_k[:, :, None] == jnp.arange(N_EXPERTS)[None, None, :], shift[:, None, :], 0), axis=2)
    return dict(
        chunk_dst=(picked + k8[None, :]).reshape(-1).astype(jnp.int32),
        ntot=jnp.sum(cnt8 // g, axis=1), zst=gbase + tot, znch=(region - tot) // g,
        ztot=jnp.sum((region - tot) // g).reshape(1),
        xtile_expert=jnp.minimum(jnp.sum(j[:, None] >= cum_tiles[None, :], axis=1), N_EXPERTS - 1).astype(jnp.int32),
        n_used=cum_tiles[-1:].astype(jnp.int32),
        loff_rows=loff_rows.reshape(n_tiles * g, LANES))


def _local_slots(route, loff_row):
    tm = route.shape[0]
    lane = lax.broadcasted_iota(jnp.int32, (tm, LANES), 1)
    e1 = lane == route[:, ROUTE_I1:ROUTE_I1 + 1].astype(jnp.int32)
    e2 = lane == route[:, ROUTE_I2:ROUTE_I2 + 1].astype(jnp.int32)
    earlier = (lax.broadcasted_iota(jnp.int32, (tm, tm), 1) < lax.broadcasted_iota(jnp.int32, (tm, tm), 0))
    earlier = earlier.astype(BF16)
    c1 = jnp.dot(earlier, e1.astype(BF16), preferred_element_type=F32)
    c2 = jnp.dot(earlier, e2.astype(BF16), preferred_element_type=F32)
    cnt1 = jnp.sum(e1.astype(F32), axis=0, keepdims=True)
    pos1 = jnp.sum(jnp.where(e1, c1 + loff_row, 0.0), axis=1, keepdims=True)
    pos2 = jnp.sum(jnp.where(e2, c2 + cnt1 + loff_row, 0.0), axis=1, keepdims=True)
    slot = lax.broadcasted_iota(jnp.int32, (tm, MOE_SLOTS), 1)
    return slot == pos1.astype(jnp.int32), slot == pos2.astype(jnp.int32)


def _split3(w):
    hi = w.astype(BF16).astype(F32)
    mid = (w - hi).astype(BF16).astype(F32)
    return hi, mid, (w - hi - mid).astype(BF16).astype(F32)


def _moe_scatter_kernel(cdst_s, ntot_s, zst_s, znch_s, ztot_s, nused_s,
                        hn_ref, route_ref, loffv_ref, xs_hbm, buf_ref, zero_ref, sem, zsem, tsem):
    b = pl.program_id(0)
    nb = pl.num_programs(0)
    slot = b % 2
    g = SUBLANES
    d = hn_ref.shape[1]
    n_xtiles = xs_hbm.shape[0] // MOE_TMX

    def run_copy(s, src_row, dst_row):
        return pltpu.make_async_copy(buf_ref.at[s, pl.ds(src_row, g)], xs_hbm.at[pl.ds(dst_row, g)], sem.at[s])

    def zero_copy(dst_row):
        return pltpu.make_async_copy(zero_ref.at[pl.ds(0, g)], xs_hbm.at[pl.ds(dst_row, g)], zsem)

    def zero_tile_copy(j):
        return pltpu.make_async_copy(zero_ref, xs_hbm.at[pl.ds(pl.multiple_of(j * MOE_TMX, MOE_TMX), MOE_TMX)], tsem)

    def wait_tile(tile, s):
        lax.fori_loop(0, ntot_s[tile], lambda _, c: (run_copy(s, 0, 0).wait(), c)[1], 0)

    @pl.when(b == 0)
    def _():
        zero_ref[...] = jnp.zeros_like(zero_ref)

        def per_expert(e, c):
            def per_chunk(k, c2):
                zero_copy(pl.multiple_of(zst_s[e] + k * g, g)).start()
                return c2
            return lax.fori_loop(0, znch_s[e], per_chunk, c)
        lax.fori_loop(0, N_EXPERTS, per_expert, 0)
        lax.fori_loop(nused_s[0], n_xtiles, lambda j, c: (zero_tile_copy(j).start(), c)[1], 0)

    @pl.when(b >= 2)
    def _():
        wait_tile(b - 2, slot)

    route = route_ref[...]
    p1, p2 = _local_slots(route, loffv_ref[0:1, :])
    tn = (((0,), (0,)), ((), ()))
    picks = (p1 | p2).astype(BF16)
    buf_ref[slot, :, 0:d] = lax.dot_general(picks, hn_ref[...], tn, preferred_element_type=F32)
    lane = lax.broadcasted_iota(jnp.int32, route.shape, 1)
    meta = jnp.zeros((MOE_SLOTS, LANES), F32)
    for p, col in ((p1, ROUTE_W1), (p2, ROUTE_W2)):
        w = route[:, col:col + 1]
        parts = _split3(w)
        wm = sum(jnp.where(lane == k, part, 0.0) for k, part in enumerate(parts))
        meta = meta + lax.dot_general(p.astype(BF16), wm.astype(BF16), tn, preferred_element_type=F32)
    buf_ref[slot, :, d:d + LANES] = meta

    def per_chunk(k, c):
        run_copy(slot, pl.multiple_of(k * g, g), pl.multiple_of(cdst_s[b * (MOE_SLOTS // g) + k], g)).start()
        return c
    lax.fori_loop(0, ntot_s[b], per_chunk, 0)

    @pl.when(b == nb - 1)
    def _():
        wait_tile(b, slot)

        @pl.when(nb >= 2)
        def _():
            wait_tile(b - 1, 1 - slot)
        lax.fori_loop(0, ztot_s[0], lambda _, c: (zero_copy(0).wait(), c)[1], 0)
        lax.fori_loop(nused_s[0], n_xtiles, lambda j, c: (zero_tile_copy(j).wait(), c)[1], 0)


def _moe_scatter(plan, hn_all, route_all, n_tiles, n_xtiles):
    d = hn_all.shape[1]
    tm = MOE_TM
    grid_spec = pltpu.PrefetchScalarGridSpec(
        num_scalar_prefetch=6,
        grid=(n_tiles,),
        in_specs=[pl.BlockSpec((tm, d), lambda i, *_: (i, 0)),
                  pl.BlockSpec((tm, LANES), lambda i, *_: (i, 0)),
                  pl.BlockSpec((SUBLANES, LANES), lambda i, *_: (i, 0))],
        out_specs=pl.BlockSpec(memory_space=pl.ANY),
        scratch_shapes=[pltpu.VMEM((2, MOE_SLOTS, d + LANES), F32),
                        pltpu.VMEM((MOE_TMX, d + LANES), F32),
                        pltpu.SemaphoreType.DMA((2,)),
                        pltpu.SemaphoreType.DMA(()),
                        pltpu.SemaphoreType.DMA(())])
    return pl.pallas_call(
        _moe_scatter_kernel,
        grid_spec=grid_spec,
        out_shape=jax.ShapeDtypeStruct((n_xtiles * MOE_TMX, d + LANES), F32),
        compiler_params=_cparams(("arbitrary",)),
        name="moe_scatter",
    )(plan["chunk_dst"], plan["ntot"], plan["zst"], plan["znch"], plan["ztot"], plan["n_used"],
      hn_all, route_all, plan["loff_rows"])


def _moe_experts_kernel(xe_s, nused_s, x_ref, w1_ref, w3_ref, w2_ref, y_ref, w1b_ref, w3b_ref, w2b_ref):
    j = pl.program_id(0)
    d = y_ref.shape[1]

    @pl.when(j < nused_s[0])
    def _():
        @pl.when((j == 0) | (xe_s[j] != xe_s[jnp.maximum(j - 1, 0)]))
        def _():
            w1b_ref[...] = w1_ref[...].astype(BF16)
            w3b_ref[...] = w3_ref[...].astype(BF16)
            w2b_ref[...] = w2_ref[...].astype(BF16)

        x = x_ref[:, 0:d].astype(BF16)
        gate = jnp.sum(x_ref[:, d:d + LANES], axis=1, keepdims=True)
        a = jnp.dot(x, w1b_ref[...], preferred_element_type=F32)
        b = jnp.dot(x, w3b_ref[...], preferred_element_type=F32)
        hh = (a * _sigmoid(a)) * b * gate
        y_ref[...] = jnp.dot(hh.astype(BF16), w2b_ref[...], preferred_element_type=F32)

    @pl.when(j >= nused_s[0])
    def _():
        y_ref[...] = jnp.zeros_like(y_ref)


def _moe_experts(plan, xs, w1, w3, w2, n_xtiles):
    ne, d, f = w1.shape
    last = lambda j, xe, nu: jnp.maximum(jnp.minimum(j, nu[0] - 1), 0)
    grid_spec = pltpu.PrefetchScalarGridSpec(
        num_scalar_prefetch=2,
        grid=(n_xtiles,),
        in_specs=[pl.BlockSpec((MOE_TMX, d + LANES), lambda j, xe, nu: (last(j, xe, nu), 0)),
                  pl.BlockSpec((None, d, f), lambda j, xe, nu: (xe[j], 0, 0)),
                  pl.BlockSpec((None, d, f), lambda j, xe, nu: (xe[j], 0, 0)),
                  pl.BlockSpec((None, f, d), lambda j, xe, nu: (xe[j], 0, 0))],
        out_specs=pl.BlockSpec((MOE_TMX, d), lambda j, xe, nu: (j, 0)),
        scratch_shapes=[pltpu.VMEM((d, f), BF16), pltpu.VMEM((d, f), BF16), pltpu.VMEM((f, d), BF16)])
    return pl.pallas_call(
        _moe_experts_kernel,
        grid_spec=grid_spec,
        out_shape=jax.ShapeDtypeStruct((n_xtiles * MOE_TMX, d), F32),
        compiler_params=_cparams(("arbitrary",)),
        name="moe_experts",
    )(plan["xtile_expert"], plan["n_used"], xs, w1, w3, w2)


def _moe_combine_kernel(cdst_s, ntot_s, ys_hbm, route_ref, loffv_ref, h_ref, lnf_ref, y_ref,
                        buf_ref, sem, *, tile0, final_norm):
    b = pl.program_id(0)
    nb = pl.num_programs(0)
    slot = b % 2
    g = SUBLANES
    tile = b + tile0

    def run_copy(s, src_row, dst_row):
        return pltpu.make_async_copy(ys_hbm.at[pl.ds(src_row, g)], buf_ref.at[s, pl.ds(dst_row, g)], sem.at[s])

    def fetch(t, s):
        def per_chunk(k, c):
            run_copy(s, pl.multiple_of(cdst_s[t * (MOE_SLOTS // g) + k], g), pl.multiple_of(k * g, g)).start()
            return c
        lax.fori_loop(0, ntot_s[t], per_chunk, 0)

    @pl.when(b == 0)
    def _():
        fetch(tile, slot)

    @pl.when(b + 1 < nb)
    def _():
        fetch(tile + 1, 1 - slot)

    lax.fori_loop(0, ntot_s[tile], lambda _, c: (run_copy(slot, 0, 0).wait(), c)[1], 0)

    def clear(k, c):
        buf_ref[slot, pl.ds(pl.multiple_of(k * g, g), g), :] = jnp.zeros((g, buf_ref.shape[2]), F32)
        return c
    lax.fori_loop(ntot_s[tile], MOE_SLOTS // g, clear, 0)

    p1, p2 = _local_slots(route_ref[...], loffv_ref[0:1, :])
    picks = (p1 | p2).astype(BF16)
    y = h_ref[...] + jnp.dot(picks, buf_ref[slot].astype(BF16), preferred_element_type=F32)
    if final_norm:
        y = _rms(y, lnf_ref[...])
    y_ref[...] = y


def _moe_combine(plan, ys, route_all, h_all, lnf, tile0, n_tiles, final_norm):
    d = h_all.shape[1]
    tm = MOE_TM
    grid_spec = pltpu.PrefetchScalarGridSpec(
        num_scalar_prefetch=2,
        grid=(n_tiles,),
        in_specs=[pl.BlockSpec(memory_space=pl.ANY),
                  pl.BlockSpec((tm, LANES), lambda i, *_: (i + tile0, 0)),
                  pl.BlockSpec((SUBLANES, LANES), lambda i, *_: (i + tile0, 0)),
                  pl.BlockSpec((tm, d), lambda i, *_: (i + tile0, 0)),
                  pl.BlockSpec((1, d), lambda i, *_: (0, 0))],
        out_specs=pl.BlockSpec((tm, d), lambda i, *_: (i, 0)),
        scratch_shapes=[pltpu.VMEM((2, MOE_SLOTS, d), F32), pltpu.SemaphoreType.DMA((2,))])
    return pl.pallas_call(
        functools.partial(_moe_combine_kernel, tile0=tile0, final_norm=final_norm),
        grid_spec=grid_spec,
        out_shape=jax.ShapeDtypeStruct((n_tiles * tm, d), F32),
        compiler_params=_cparams(("arbitrary",)),
        name="moe_combine",
    )(plan["chunk_dst"], plan["ntot"], ys, route_all, plan["loff_rows"], h_all, lnf.reshape(1, d))


def _prompt_states(kvt, n, t):
    hd = HEAD_DIM
    tail = kvt.shape[2]

    def state(k_lo, v_lo, rows, heads, win):
        w = min(win, t)
        kv = jnp.stack([kvt[:, k_lo:k_lo + rows, tail - w:], kvt[:, v_lo:v_lo + rows, tail - w:]], axis=1)
        return jnp.transpose(kv.reshape(n, 2, heads, hd, w), (0, 4, 1, 2, 3))

    out = [state(0, KA_W, KA_W, A_KV_HEADS, A_WINDOW)]
    for p, (win, _) in enumerate(B_PATTERNS):
        out.append(state(2 * KA_W + p * PB_W, 2 * KA_W + QB_W + p * PB_W, PB_W, B_HEADS_PER_PATTERN, win))
    return out


def _layer(xp, xs, caches, rel_bias, ln1, w_in, sinks, w_pa, w_pb, w_out, ln2, w_rg, b_rg, w_re, b_re,
           w1, w3, w2, lnf, final_norm):
    n, t, d = xp.shape
    ns = xs.shape[0]
    assert xs.shape[1] == 1 and OFF_GA + 2 * d == w_in.shape[1] and P_QA == 2 * d
    w_bf = w_in.astype(BF16)
    w_kvt = jnp.concatenate([w_in[:, OFF_KA:OFF_QB], w_in[:, OFF_KB:OFF_GA]], axis=1).T.astype(BF16)
    wpa, wpb, wout = w_pa.astype(BF16), w_pb.astype(BF16), w_out.astype(BF16)
    wr = jnp.zeros((d, LANES), F32).at[:, :N_EXPERTS].set(w_re).at[:, N_EXPERTS:N_EXPERTS + MOE_GROUPS].set(w_rg)
    br = jnp.zeros((1, LANES), F32).at[0, :N_EXPERTS].set(b_re).at[0, N_EXPERTS:N_EXPERTS + MOE_GROUPS].set(b_rg)
    wr_hi = wr.astype(BF16)
    wr = jnp.stack([wr_hi, (wr - wr_hi.astype(F32)).astype(BF16)])

    tm = 256
    xp2 = xp.reshape(n * t, d)
    act, qkv2, qkv3, kvt = _in_proj_prompt(xp2, ln1, w_bf, w_kvt, n, t, tm)
    act4 = act.reshape(n, 1, t, ACT_W)
    bias_a = _band_bias(rel_bias[:, :A_HEADS], A_WINDOW - 1, 1)
    (oa,) = _band_attn(act4, bias_a, sinks, q_off=P_QA, k_off=P_KA, v_off=P_VA,
                       kv_heads=A_KV_HEADS, grp=A_GROUP, want_lse=False)
    obs, lses = [], []
    for p, (win, dil) in enumerate(B_PATTERNS):
        lo = A_HEADS + p * B_HEADS_PER_PATTERN
        bias_p = _band_bias(rel_bias[:, lo:lo + B_HEADS_PER_PATTERN], win // dil, dil)
        src, base = ((act4, P_B), (qkv2, 0), (qkv3, 0))[p]
        o, lse = _band_attn(src, bias_p, None, q_off=base, k_off=base + PB_W, v_off=base + 2 * PB_W,
                            kv_heads=B_HEADS_PER_PATTERN, grp=1, want_lse=True)
        obs.append(o)
        lses.append(lse)
    assert tm == MOE_TM and ns <= MOE_TM
    p_tiles = n * t // MOE_TM
    n_tiles = p_tiles + 1
    m_all = n_tiles * MOE_TM
    st_p = _prompt_states(kvt, n, t)

    xs2 = xs.reshape(ns, d)
    proj_s = _in_proj_sample(xs2, ln1, w_bf)
    oa_s, ob_s, lse_s, st_s = _sample_mix(proj_s, caches, rel_bias, sinks)
    rows = lambda a: jnp.pad(a, ((0, MOE_TM - ns), (0, 0)))
    obs_s = [rows(ob_s[:, p].astype(BF16)).reshape(1, 1, MOE_TM, PB_W) for p in range(N_PAT)]
    lses_s = [rows(jnp.repeat(lse_s[:, p], HEAD_DIM, axis=-1)).reshape(1, 1, MOE_TM, PB_W) for p in range(N_PAT)]
    gates_s = rows(proj_s[:, OFF_GA:].astype(BF16))
    routed_s = _post_attn(rows(oa_s.astype(BF16)), obs_s, lses_s, gates_s, 0, rows(xs2),
                          wpa, wpb, wout, ln2, wr, br, MOE_TM, 1, None)
    h_all, hn_all, route_all, cnt_all = _post_attn(oa.reshape(n * t, QA_W), obs, lses, act, 0, xp2,
                                                   wpa, wpb, wout, ln2, wr, br, tm, t // tm, routed_s)

    max_rows = 2 * m_all + n_tiles * N_EXPERTS * (SUBLANES - 1) + N_EXPERTS * (MOE_TMX - SUBLANES)
    n_xtiles = -(-max_rows // MOE_TMX)
    plan = _route_plan(cnt_all, n_tiles, n_xtiles)
    xs_sorted = _moe_scatter(plan, hn_all, route_all, n_tiles, n_xtiles)
    ys_sorted = _moe_experts(plan, xs_sorted, w1, w3, w2, n_xtiles)
    yp = _moe_combine(plan, ys_sorted, route_all, h_all, lnf, 0, p_tiles, final_norm).reshape(n, t, d)
    ys = _moe_combine(plan, ys_sorted, route_all, h_all, lnf, p_tiles, 1, final_norm)[:ns].reshape(ns, 1, d)
    return yp, ys, st_p, st_s


def kernel(x_prompt, x_sample, cache_a_kv, cache_b1_kv, cache_b2_kv, cache_b3_kv, rel_bias, ln1_g, w_in, sinks,
           w_pa, w_pb, w_out, ln2_g, w_rg, b_rg, w_re, b_re, w1, w3, w2, lnf_g):
    depth = w_in.shape[0]
    assert depth >= 1
    xp, xs = x_prompt, x_sample
    new_p = [[] for _ in range(4)]
    new_s = [[] for _ in range(4)]
    for l in range(depth):
        caches = (cache_a_kv[l], cache_b1_kv[l], cache_b2_kv[l], cache_b3_kv[l])
        xp, xs, st_p, st_s = _layer(xp, xs, caches, rel_bias, ln1_g[l], w_in[l], sinks[l], w_pa[l], w_pb[l],
                                    w_out[l], ln2_g[l], w_rg[l], b_rg[l], w_re[l], b_re[l], w1[l], w3[l], w2[l],
                                    lnf_g, l == depth - 1)
        for i in range(4):
            new_p[i].append(st_p[i])
            new_s[i].append(st_s[i])
    a_p, b1_p, b2_p, b3_p = [jnp.stack(v) for v in new_p]
    a_s, b1_s, b2_s, b3_s = [jnp.stack(v) for v in new_s]
    return (xp, xs, a_p, a_s, b1_p, b1_s, b2_p, b2_s, b3_p, b3_s)
```

```python
import functools
import math

import numpy as np
import jax
import jax.numpy as jnp
from jax import lax
from jax.experimental import pallas as pl
from jax.experimental.pallas import tpu as pltpu

F32 = jnp.float32
BF16 = jnp.bfloat16

HEAD_DIM = 64
A_HEADS = 8
A_KV_HEADS = 2
A_GROUP = A_HEADS // A_KV_HEADS
A_WINDOW = 128
B_PATTERNS = ((128, 1), (512, 4), (2048, 16))
N_PAT = len(B_PATTERNS)
B_HEADS_PER_PATTERN = 4
B_HEADS = B_HEADS_PER_PATTERN * N_PAT
BLOCK = 128
ATTN_CHAINS = 16
NUM_BUCKETS = 32
MAX_DISTANCE = 2048
MOE_GROUPS = 4
EXPERTS_PER_GROUP = 8
N_EXPERTS = MOE_GROUPS * EXPERTS_PER_GROUP
EPS = 1e-6
NEG = -1e30
LANES = 128
SUBLANES = 8
ROUTE_I1, ROUTE_I2, ROUTE_W1, ROUTE_W2 = 0, 1, 2, 3
IN_PROJ_TM = 512
MOE_TM = 256
MOE_SLOTS = 2 * MOE_TM + N_EXPERTS * SUBLANES
MOE_TMX = 256
Q_SCALE = HEAD_DIM ** -0.5

QA_W = A_HEADS * HEAD_DIM
KA_W = A_KV_HEADS * HEAD_DIM
QB_W = B_HEADS * HEAD_DIM
PB_W = B_HEADS_PER_PATTERN * HEAD_DIM
QKV_W = 3 * PB_W
OFF_KA = QA_W
OFF_QB = OFF_KA + 2 * KA_W
OFF_KB = OFF_QB + QB_W
OFF_VB = OFF_KB + QB_W
OFF_GA = OFF_VB + QB_W
P_QA = 2048
P_KA = P_QA + QA_W
P_VA = P_KA + KA_W
P_B = P_VA + KA_W
ACT_W = P_B + QKV_W
KVT_ROWS = 2 * KA_W + 2 * QB_W

VMEM_LIMIT = 56 * 1024 * 1024


def _cparams(sem):
    return pltpu.CompilerParams(dimension_semantics=sem, vmem_limit_bytes=VMEM_LIMIT)


def _bucket_np(dist):
    dist = np.asarray(dist, np.int64)
    max_exact = NUM_BUCKETS // 2
    df = np.maximum(dist, max_exact).astype(np.float64)
    large = max_exact + (np.log(df / max_exact) / math.log(MAX_DISTANCE / max_exact)
                         * (NUM_BUCKETS - max_exact)).astype(np.int64)
    return np.where(dist < max_exact, dist, np.minimum(large, NUM_BUCKETS - 1))


def _table_rows(table_cols, dist, valid):
    onehot = (_bucket_np(dist)[:, None] == np.arange(NUM_BUCKETS)[None, :]).astype(np.float32)
    rows = jnp.einsum("ck,kh->hc", jnp.asarray(onehot), table_cols.astype(F32), precision=lax.Precision.HIGHEST)
    return jnp.where(jnp.asarray(valid)[None, :], rows, NEG)


def _band_bias(table_cols, max_dist, dilation):
    period = 3 * BLOCK
    m = np.arange(period)
    k = np.where(m < 2 * BLOCK, m, m - period)
    dist = BLOCK - k
    valid = (dist >= 0) & (dist <= max_dist) & (m != 2 * BLOCK)
    v = _table_rows(table_cols, np.clip(dist, 0, None) * dilation, valid)
    heads = v.shape[0]
    flat = jnp.tile(v, (1, BLOCK))[:, :BLOCK * (period - 1)]
    return flat.reshape(heads, BLOCK, period - 1)[:, :, :2 * BLOCK]


def _decode_bias(table_cols, width, dilation, first_valid):
    c = np.arange(width)
    valid = (c % dilation == 0) & (c >= first_valid)
    rows = _table_rows(table_cols, width - c, valid)
    self_bias = _table_rows(table_cols, np.zeros((1,), np.int64), np.ones((1,), bool))
    return rows[:, None, :], self_bias[:, None, :]


def _rms(x, g):
    return (x * lax.rsqrt(jnp.mean(x * x, axis=-1, keepdims=True) + EPS)) * g


def _sigmoid(x):
    return 1.0 / (1.0 + jnp.exp(-x))


def _in_proj_kernel(x_ref, g_ref, w_ref, wkvt_ref, act_ref, qkv2_ref, qkv3_ref, kvt_ref, ys_ref, *,
                    tiles_per_seq, tail_tiles):
    tm = x_ref.shape[0]
    g = g_ref[...]
    xb = _rms(x_ref[...], g).astype(BF16)

    def proj(xv, lo, hi):
        return jnp.dot(xv, w_ref[:, lo:hi], preferred_element_type=F32)

    for c in range(0, P_QA, 1024):
        act_ref[:, c:c + 1024] = _sigmoid(proj(xb, OFF_GA + c, OFF_GA + c + 1024)).astype(BF16)
    act_ref[:, P_QA:P_KA] = (proj(xb, 0, OFF_KA) * Q_SCALE).astype(BF16)
    act_ref[:, P_KA:P_B] = proj(xb, OFF_KA, OFF_QB).astype(BF16)

    def qkv_parts(p):
        lo = p * PB_W
        return (proj(xb, OFF_QB + lo, OFF_QB + lo + PB_W) * Q_SCALE,
                proj(xb, OFF_KB + lo, OFF_KB + lo + PB_W),
                proj(xb, OFF_VB + lo, OFF_VB + lo + PB_W))

    for j, part in enumerate(qkv_parts(0)):
        act_ref[:, P_B + j * PB_W:P_B + (j + 1) * PB_W] = part.astype(BF16)

    for p, out_ref in ((1, qkv2_ref), (2, qkv3_ref)):
        dil = B_PATTERNS[p][1]
        rows = tm // dil
        for j, part in enumerate(qkv_parts(p)):
            for c in range(PB_W // LANES):
                ys_ref[j * (PB_W // LANES) + c] = part[:, c * LANES:(c + 1) * LANES]
        for c in range(QKV_W // LANES):
            cs = slice(c * LANES, (c + 1) * LANES)
            for r in range(dil):
                out_ref[r, :, cs] = ys_ref[c, pl.ds(r, rows, stride=dil), :].astype(BF16)

    @pl.when(pl.program_id(0) % tiles_per_seq >= tiles_per_seq - tail_tiles)
    def _():
        kvt_ref[...] = lax.dot_general(wkvt_ref[...], xb, (((1,), (1,)), ((), ())), preferred_element_type=F32)


def _in_proj_prompt(x2d, ln_g, w_bf, w_kvt, n, t, tm):
    m, d = x2d.shape
    tps = t // tm
    tail = min(max(w for w, _ in B_PATTERNS), t)
    assert t % tm == 0 and tail % tm == 0 and all(tm % (16 * dl) == 0 for _, dl in B_PATTERNS)
    tail_tiles = tail // tm
    d2, d3 = B_PATTERNS[1][1], B_PATTERNS[2][1]
    return pl.pallas_call(
        functools.partial(_in_proj_kernel, tiles_per_seq=tps, tail_tiles=tail_tiles),
        grid=(m // tm,),
        in_specs=[pl.BlockSpec((tm, d), lambda i: (i, 0)),
                  pl.BlockSpec((1, d), lambda i: (0, 0)),
                  pl.BlockSpec(w_bf.shape, lambda i: (0, 0), pipeline_mode=pl.Buffered(1)),
                  pl.BlockSpec(w_kvt.shape, lambda i: (0, 0), pipeline_mode=pl.Buffered(1))],
        out_specs=[pl.BlockSpec((tm, ACT_W), lambda i: (i, 0)),
                   pl.BlockSpec((None, d2, tm // d2, QKV_W), lambda i: (i // tps, 0, i % tps, 0)),
                   pl.BlockSpec((None, d3, tm // d3, QKV_W), lambda i: (i // tps, 0, i % tps, 0)),
                   pl.BlockSpec((None, KVT_ROWS, tm),
                                lambda i: (i // tps, 0, jnp.maximum(i % tps - (tps - tail_tiles), 0)))],
        out_shape=[jax.ShapeDtypeStruct((m, ACT_W), BF16),
                   jax.ShapeDtypeStruct((n, d2, t // d2, QKV_W), BF16),
                   jax.ShapeDtypeStruct((n, d3, t // d3, QKV_W), BF16),
                   jax.ShapeDtypeStruct((n, KVT_ROWS, tail), F32)],
        scratch_shapes=[pltpu.VMEM((QKV_W // LANES, tm, LANES), F32)],
        compiler_params=_cparams(("arbitrary",)),
        name="in_proj",
    )(x2d, ln_g.reshape(1, d), w_bf, w_kvt)


def _in_proj_sample_kernel(x_ref, g_ref, w_ref, y_ref):
    xb = _rms(x_ref[...], g_ref[...]).astype(BF16)
    y_ref[:, :OFF_GA] = jnp.dot(xb, w_ref[:, :OFF_GA], preferred_element_type=F32)
    for lo, hi in ((0, OFF_KA), (OFF_QB, OFF_KB)):
        y_ref[:, lo:hi] = y_ref[:, lo:hi] * Q_SCALE
    for lo in range(OFF_GA, w_ref.shape[1], 1024):
        y_ref[:, lo:lo + 1024] = _sigmoid(jnp.dot(xb, w_ref[:, lo:lo + 1024], preferred_element_type=F32))


def _in_proj_sample(x2d, ln_g, w_bf):
    m, d = x2d.shape
    return pl.pallas_call(
        _in_proj_sample_kernel,
        grid=(1,),
        in_specs=[pl.BlockSpec((m, d), lambda i: (0, 0)),
                  pl.BlockSpec((1, d), lambda i: (0, 0)),
                  pl.BlockSpec(w_bf.shape, lambda i: (0, 0))],
        out_specs=pl.BlockSpec((m, w_bf.shape[1]), lambda i: (0, 0)),
        out_shape=jax.ShapeDtypeStruct((m, w_bf.shape[1]), F32),
        compiler_params=_cparams(("arbitrary",)),
        name="in_proj_sample",
    )(x2d, ln_g.reshape(1, d), w_bf)


def _band_attn_kernel(*refs, kv_heads, grp, has_sink, want_lse):
    if has_sink:
        sink_ref, refs = refs[0], refs[1:]
    q_ref, kp_ref, kc_ref, vp_ref, vc_ref, bias_ref, o_ref = refs[:7]
    lse_ref = refs[7] if want_lse else None
    step = pl.program_id(2)
    hd = HEAD_DIM
    nt = (((1,), (1,)), ((), ()))
    chains = [(sub, kv * grp + g, slice(kv * hd, (kv + 1) * hd))
              for sub in range(q_ref.shape[0] // BLOCK) for kv in range(kv_heads) for g in range(grp)]
    scores = []
    for sub, h, ks in chains:
        rows = slice(sub * BLOCK, (sub + 1) * BLOCK)
        q = q_ref[rows, h * hd:(h + 1) * hd]
        kp = kp_ref[:, ks] if sub == 0 else kc_ref[(sub - 1) * BLOCK:sub * BLOCK, ks]
        sp = lax.dot_general(q, kp, nt, preferred_element_type=F32) + bias_ref[h, :, 0:BLOCK]
        sc = lax.dot_general(q, kc_ref[rows, ks], nt, preferred_element_type=F32) + bias_ref[h, :, BLOCK:2 * BLOCK]
        if sub == 0:
            sp = jnp.where(step > 0, sp, NEG)
        scores.append((sp, sc))
    maxes = []
    for (sub, h, ks), (sp, sc) in zip(chains, scores):
        m = jnp.maximum(jnp.max(sp, axis=-1, keepdims=True), jnp.max(sc, axis=-1, keepdims=True))
        maxes.append(jnp.maximum(m, sink_ref[h]) if has_sink else m)
    probs = []
    for (sub, h, ks), (sp, sc), m in zip(chains, scores, maxes):
        pp, pc = jnp.exp(sp - m), jnp.exp(sc - m)
        den = jnp.sum(pp, axis=-1, keepdims=True) + jnp.sum(pc, axis=-1, keepdims=True)
        if has_sink:
            den = den + jnp.exp(sink_ref[h] - m)
        probs.append((pp.astype(BF16), pc.astype(BF16), den))
    for (sub, h, ks), (pp, pc, den), m in zip(chains, probs, maxes):
        rows = slice(sub * BLOCK, (sub + 1) * BLOCK)
        vp = vp_ref[:, ks] if sub == 0 else vc_ref[(sub - 1) * BLOCK:sub * BLOCK, ks]
        o = (jnp.dot(pp, vp, preferred_element_type=F32)
             + jnp.dot(pc, vc_ref[rows, ks], preferred_element_type=F32))
        o_ref[rows, h * hd:(h + 1) * hd] = (o / den).astype(o_ref.dtype)
        if want_lse:
            lse_ref[rows, h * hd:(h + 1) * hd] = jnp.broadcast_to(m + jnp.log(den), (BLOCK, hd))


def _band_attn(src, bias, sink, *, q_off, k_off, v_off, kv_heads, grp, want_lse):
    n, dil, l, cols = src.shape
    sub = max(1, ATTN_CHAINS // (kv_heads * grp))
    while l % (sub * BLOCK):
        sub //= 2
    rows = sub * BLOCK
    assert sub >= 1 and l % rows == 0
    nb = l // rows
    qw = kv_heads * grp * HEAD_DIM
    kw = kv_heads * HEAD_DIM
    assert q_off % qw == 0 and k_off % kw == 0 and v_off % kw == 0
    qb, kb, vb = q_off // qw, k_off // kw, v_off // kw
    prev = lambda b: jnp.maximum(sub * b - 1, 0)
    in_specs = [
        pl.BlockSpec((None, None, rows, qw), lambda i, r, b: (i, r, b, qb)),
        pl.BlockSpec((None, None, BLOCK, kw), lambda i, r, b: (i, r, prev(b), kb)),
        pl.BlockSpec((None, None, rows, kw), lambda i, r, b: (i, r, b, kb)),
        pl.BlockSpec((None, None, BLOCK, kw), lambda i, r, b: (i, r, prev(b), vb)),
        pl.BlockSpec((None, None, rows, kw), lambda i, r, b: (i, r, b, vb)),
        pl.BlockSpec(bias.shape, lambda i, r, b: (0, 0, 0)),
    ]
    args = [src, src, src, src, src, bias]
    has_sink = sink is not None
    if has_sink:
        in_specs = [pl.BlockSpec(memory_space=pltpu.SMEM)] + in_specs
        args = [sink.astype(F32)] + args
    out_specs = [pl.BlockSpec((None, None, rows, qw), lambda i, r, b: (i, r, b, 0))]
    out_shape = [jax.ShapeDtypeStruct((n, dil, l, qw), BF16)]
    if want_lse:
        out_specs.append(pl.BlockSpec((None, None, rows, qw), lambda i, r, b: (i, r, b, 0)))
        out_shape.append(jax.ShapeDtypeStruct((n, dil, l, qw), F32))
    return pl.pallas_call(
        functools.partial(_band_attn_kernel, kv_heads=kv_heads, grp=grp, has_sink=has_sink, want_lse=want_lse),
        grid=(n, dil, nb),
        in_specs=in_specs,
        out_specs=out_specs,
        out_shape=out_shape,
        compiler_params=_cparams(("arbitrary", "arbitrary", "arbitrary")),
        name=f"band_attn_d{dil}",
    )(*args)


COL_QA = 0
COL_QB = COL_QA + A_HEADS
COL_KA = COL_QB + B_HEADS
COL_VA = COL_KA + A_KV_HEADS
COL_KB = COL_VA + A_KV_HEADS
COL_VB = COL_KB + B_HEADS
N_COLS = COL_VB + B_HEADS


def _attend_cached(kt, vt, q, k_new, v_new, bias, self_bias, sink, write):
    s = jnp.sum(kt * q, axis=1, keepdims=True) + bias
    s_new = jnp.sum(k_new * q, axis=1, keepdims=True) + self_bias
    yield
    m = jnp.maximum(jnp.max(s, axis=2, keepdims=True), s_new)
    if sink is not None:
        m = jnp.maximum(m, sink)
    yield
    p = jnp.exp(s - m)
    p_new = jnp.exp(s_new - m)
    den = jnp.sum(p, axis=2, keepdims=True) + p_new
    if sink is not None:
        den = den + jnp.exp(sink - m)
    yield
    o = (jnp.sum(vt * p, axis=2, keepdims=True) + v_new * p_new) / den
    write(o, m + jnp.log(den))
    yield


def _sample_mix_kernel(cols_ref, ca_ref, c1_ref, c2_ref, c3_ref, ba_ref, sa_ref, sink_ref, b1_ref, b2_ref, b3_ref,
                       sb_ref, o_ref, lse_ref, ra_ref, r1_ref, r2_ref, r3_ref):
    cols = cols_ref[...]

    def stack(js):
        return jnp.stack([cols[:, j:j + 1] for j in js])

    o_ref[...] = jnp.zeros_like(o_ref)
    lse_ref[...] = jnp.zeros_like(lse_ref)

    def write_a(o, _):
        for h in range(A_HEADS):
            o_ref[:, COL_QA + h:COL_QA + h + 1] = o[h]

    def write_b(p):
        def write(o, lse):
            for h in range(B_HEADS_PER_PATTERN):
                j = p * B_HEADS_PER_PATTERN + h
                o_ref[:, COL_QB + j:COL_QB + j + 1] = o[h]
                lse_ref[0:1, j:j + 1] = lse[h]
        return write

    kv_of = [h // A_GROUP for h in range(A_HEADS)]
    stages = [_attend_cached(jnp.stack([ca_ref[0, kv] for kv in kv_of]), jnp.stack([ca_ref[1, kv] for kv in kv_of]),
                             stack(range(COL_QA, COL_QA + A_HEADS)), stack([COL_KA + kv for kv in kv_of]),
                             stack([COL_VA + kv for kv in kv_of]), ba_ref[...], sa_ref[...], sink_ref[...], write_a)]
    for p, (c_ref, b_ref) in enumerate(((c1_ref, b1_ref), (c2_ref, b2_ref), (c3_ref, b3_ref))):
        js = range(p * B_HEADS_PER_PATTERN, (p + 1) * B_HEADS_PER_PATTERN)
        stages.append(_attend_cached(c_ref[0], c_ref[1], stack([COL_QB + j for j in js]),
                                     stack([COL_KB + j for j in js]), stack([COL_VB + j for j in js]),
                                     b_ref[...], sb_ref[p], None, write_b(p)))

    def roll_plane(c_ref, r_ref, i, h, new_col):
        w = c_ref.shape[-1]
        x = c_ref[i, h]
        lane = lax.broadcasted_iota(jnp.int32, x.shape, 1)
        r_ref[i, h] = jnp.where(lane == w - 1, cols[:, new_col:new_col + 1], pltpu.roll(x, w - 1, 1))

    planes = []
    for ci, (c_ref, r_ref) in enumerate(((ca_ref, ra_ref), (c1_ref, r1_ref), (c2_ref, r2_ref), (c3_ref, r3_ref))):
        for i, (first_a, first_b) in enumerate(((COL_KA, COL_KB), (COL_VA, COL_VB))):
            for h in range(c_ref.shape[1]):
                new_col = first_a + h if ci == 0 else first_b + (ci - 1) * B_HEADS_PER_PATTERN + h
                planes.append((c_ref.shape[-1], functools.partial(roll_plane, c_ref, r_ref, i, h, new_col)))

    total = sum(w for w, _ in planes)
    n_slots = 4 * len(stages)
    done, k = 0, 0
    for slot in range(n_slots):
        next(stages[slot // 4])
        while k < len(planes) and done < total * (slot + 1) // n_slots:
            done += planes[k][0]
            planes[k][1]()
            k += 1
    assert k == len(planes)


def _sample_mix(proj_s, caches, rel_bias, sinks):
    n = proj_s.shape[0]
    hd = HEAD_DIM
    vecs = jnp.concatenate([proj_s[:, :OFF_KA], proj_s[:, OFF_QB:OFF_KB], proj_s[:, OFF_KA:OFF_QB],
                            proj_s[:, OFF_KB:OFF_GA]], axis=1).reshape(n, N_COLS, hd)
    cols = jnp.pad(jnp.transpose(vecs, (0, 2, 1)), ((0, 0), (0, 0), (0, LANES - N_COLS)))
    cts = [jnp.transpose(c, (0, 2, 3, 4, 1)) for c in caches]

    ba, sa = _decode_bias(rel_bias[:, :A_HEADS], A_WINDOW, 1, 1)
    sink = sinks.astype(F32).reshape(A_HEADS, 1, 1)
    bbs, sbs = [], []
    for p, (win, dil) in enumerate(B_PATTERNS):
        lo = A_HEADS + p * B_HEADS_PER_PATTERN
        assert caches[1 + p].shape[1] == win == BLOCK * dil
        b, s = _decode_bias(rel_bias[:, lo:lo + B_HEADS_PER_PATTERN], win, dil, 0)
        bbs.append(b)
        sbs.append(s)
    sb = jnp.stack(sbs)

    def per_seq(a):
        nd = a.ndim
        return pl.BlockSpec((None,) + a.shape[1:], lambda i: (i,) + (0,) * (nd - 1))

    def full(a):
        nd = a.ndim
        return pl.BlockSpec(a.shape, lambda i: (0,) * nd)

    seq_in = [cols] + cts
    consts = [ba, sa, sink] + bbs + [sb]
    out_shape = [jax.ShapeDtypeStruct((n, hd, LANES), F32), jax.ShapeDtypeStruct((n, SUBLANES, LANES), F32)]
    out_shape += [jax.ShapeDtypeStruct(c.shape, c.dtype) for c in cts]
    outs = pl.pallas_call(
        _sample_mix_kernel,
        grid=(n,),
        in_specs=[per_seq(a) for a in seq_in] + [full(a) for a in consts],
        out_specs=[per_seq(s) for s in out_shape],
        out_shape=out_shape,
        compiler_params=_cparams(("arbitrary",)),
        name="sample_mix",
    )(*seq_in, *consts)
    o_rows = jnp.transpose(outs[0][:, :, :COL_KA], (0, 2, 1))
    oa = o_rows[:, COL_QA:COL_QB].reshape(n, QA_W)
    ob = o_rows[:, COL_QB:COL_KA].reshape(n, N_PAT, PB_W)
    lse = outs[1][:, 0, :B_HEADS].reshape(n, N_PAT, B_HEADS_PER_PATTERN)
    rolled = [jnp.transpose(r, (0, 4, 1, 2, 3)) for r in outs[2:]]
    return oa, ob, lse, rolled


def _post_attn_kernel(oa_ref, o1_ref, o2_ref, o3_ref, l1_ref, l2_ref, l3_ref, ga_ref, gb_ref, x_ref,
                      wpa_ref, wpb_ref, wout_ref, ln2_ref, wr_ref, br_ref, *rest, n_extra):
    outs = rest[n_extra:n_extra + 4]
    h_ref, hn_ref, route_ref, cnt_ref = outs
    scr_ref = rest[n_extra + 4]
    if n_extra:
        own_tile = pl.program_id(0) < pl.num_programs(0) - 1

        @pl.when(jnp.logical_not(own_tile))
        def _():
            for dst, src in zip(outs, rest[:n_extra]):
                dst[...] = src[...]

        @pl.when(own_tile)
        def _():
            _post_attn_tile(oa_ref, o1_ref, o2_ref, o3_ref, l1_ref, l2_ref, l3_ref, ga_ref, gb_ref, x_ref,
                            wpa_ref, wpb_ref, wout_ref, ln2_ref, wr_ref, br_ref, *outs, scr_ref)
    else:
        _post_attn_tile(oa_ref, o1_ref, o2_ref, o3_ref, l1_ref, l2_ref, l3_ref, ga_ref, gb_ref, x_ref,
                        wpa_ref, wpb_ref, wout_ref, ln2_ref, wr_ref, br_ref, *outs, scr_ref)


def _post_attn_tile(oa_ref, o1_ref, o2_ref, o3_ref, l1_ref, l2_ref, l3_ref, ga_ref, gb_ref, x_ref,
                    wpa_ref, wpb_ref, wout_ref, ln2_ref, wr_ref, br_ref, h_ref, hn_ref, route_ref, cnt_ref, scr_ref):
    tm = x_ref.shape[0]
    chunks = PB_W // LANES

    def token_major(ref, slot):
        dil = ref.shape[0]
        if dil == 1:
            return ref[0].astype(F32)
        for c in range(chunks):
            for r in range(dil):
                scr_ref[slot * chunks + c, pl.ds(r, tm // dil, stride=dil), :] = (
                    ref[r, :, c * LANES:(c + 1) * LANES].astype(F32))
        return jnp.concatenate([scr_ref[slot * chunks + c] for c in range(chunks)], axis=1)

    o1, o2, o3 = (token_major(r, s) for s, r in enumerate((o1_ref, o2_ref, o3_ref)))
    l1, l2, l3 = (token_major(r, 3 + s) for s, r in enumerate((l1_ref, l2_ref, l3_ref)))
    m = jnp.maximum(jnp.maximum(l1, l2), l3)
    a1, a2, a3 = jnp.exp(l1 - m), jnp.exp(l2 - m), jnp.exp(l3 - m)
    ob = (a1 * o1 + a2 * o2 + a3 * o3) / (a1 + a2 + a3)
    ya = jnp.dot(oa_ref[...], wpa_ref[...], preferred_element_type=F32)
    yb = jnp.dot(ob.astype(BF16), wpb_ref[...], preferred_element_type=F32)
    merged = ga_ref[...].astype(F32) * ya + gb_ref[...].astype(F32) * yb
    h = x_ref[...] + jnp.dot(merged.astype(BF16), wout_ref[...], preferred_element_type=F32)
    h_ref[...] = h
    hn = _rms(h, ln2_ref[...])
    hn_ref[...] = hn.astype(BF16)

    hn_hi = hn.astype(BF16)
    hn_lo = (hn - hn_hi.astype(F32)).astype(BF16)
    logits = (jnp.dot(hn_hi, wr_ref[0], preferred_element_type=F32)
              + jnp.dot(hn_lo, wr_ref[0], preferred_element_type=F32)
              + jnp.dot(hn_hi, wr_ref[1], preferred_element_type=F32)) + br_ref[...]
    lane = lax.broadcasted_iota(jnp.int32, logits.shape, 1)
    is_grp = (lane >= N_EXPERTS) & (lane < N_EXPERTS + MOE_GROUPS)
    lg = jnp.where(is_grp, logits, NEG)
    gmax = jnp.max(lg, axis=-1, keepdims=True)
    g_lane = jnp.min(jnp.where(lg == gmax, lane, LANES), axis=-1, keepdims=True)
    p_g = 1.0 / jnp.sum(jnp.where(is_grp, jnp.exp(lg - gmax), 0.0), axis=-1, keepdims=True)
    e_lo = (g_lane - N_EXPERTS) * EXPERTS_PER_GROUP
    in_grp = (lane >= e_lo) & (lane < e_lo + EXPERTS_PER_GROUP)
    le = jnp.where(in_grp, logits, NEG)
    v1 = jnp.max(le, axis=-1, keepdims=True)
    i1 = jnp.min(jnp.where(le == v1, lane, LANES), axis=-1, keepdims=True)
    le2 = jnp.where(lane == i1, NEG, le)
    v2 = jnp.max(le2, axis=-1, keepdims=True)
    i2 = jnp.min(jnp.where(le2 == v2, lane, LANES), axis=-1, keepdims=True)
    e2 = jnp.exp(v2 - v1)
    w1 = p_g / (1.0 + e2)
    w2 = p_g * e2 / (1.0 + e2)
    route = jnp.where(lane == ROUTE_I1, i1.astype(F32), jnp.where(lane == ROUTE_I2, i2.astype(F32), 0.0))
    route_ref[...] = route + jnp.where(lane == ROUTE_W1, w1, 0.0) + jnp.where(lane == ROUTE_W2, w2, 0.0)
    picks = (lane == i1).astype(F32) + (lane == i2).astype(F32)
    cnt_ref[...] = jnp.broadcast_to(jnp.sum(picks, axis=0, keepdims=True), cnt_ref.shape)


def _post_attn(oa, obs, lses, gates_src, ga_blk, x2d, wpa, wpb, wout, ln2, wr, br, tm, tiles_per_seq, extra):
    m, d = x2d.shape
    tps = tiles_per_seq
    own = m // tm
    n_tiles = own + (extra is not None)
    mine = lambda i: jnp.minimum(i, own - 1)

    def tile(w, col=0):
        return pl.BlockSpec((tm, w), lambda i: (mine(i), col))

    def out_tile(rows, w):
        return pl.BlockSpec((rows, w), lambda i: (i, 0))

    def full(a):
        nd = a.ndim
        return pl.BlockSpec(a.shape, lambda i: (0,) * nd)

    def residue(a):
        dil = a.shape[1]
        return pl.BlockSpec((None, dil, tm // dil, PB_W), lambda i: (mine(i) // tps, 0, mine(i) % tps, 0))

    weights = [wpa, wpb, wout, ln2.reshape(1, d), wr, br]
    scratch = [pltpu.VMEM((6 * PB_W // LANES, tm, LANES), F32)]
    in_specs = ([tile(QA_W)] + [residue(a) for a in obs] + [residue(a) for a in lses]
                + [tile(d, ga_blk), tile(d, ga_blk + 1), tile(d)] + [full(w) for w in weights])
    args = [oa, *obs, *lses, gates_src, gates_src, x2d, *weights]
    if extra is not None:
        in_specs = in_specs + [full(a) for a in extra]
        args = args + list(extra)
    return pl.pallas_call(
        functools.partial(_post_attn_kernel, n_extra=0 if extra is None else len(extra)),
        grid=(n_tiles,),
        in_specs=in_specs,
        out_specs=[out_tile(tm, d), out_tile(tm, d), out_tile(tm, LANES), out_tile(SUBLANES, LANES)],
        out_shape=[jax.ShapeDtypeStruct((n_tiles * tm, d), F32),
                   jax.ShapeDtypeStruct((n_tiles * tm, d), BF16),
                   jax.ShapeDtypeStruct((n_tiles * tm, LANES), F32),
                   jax.ShapeDtypeStruct((n_tiles * SUBLANES, LANES), F32)],
        scratch_shapes=scratch,
        compiler_params=_cparams(("arbitrary",)),
        name="post_attn",
    )(*args)


def _route_plan(cnt_rows, n_tiles, n_xtiles):
    g = SUBLANES
    cnt = cnt_rows.reshape(n_tiles, g, LANES)[:, 0, :N_EXPERTS].astype(jnp.int32)
    cnt8 = (cnt + g - 1) // g * g
    loff = jnp.cumsum(cnt8, axis=1) - cnt8
    boff = jnp.cumsum(cnt8, axis=0) - cnt8
    tot = jnp.sum(cnt8, axis=0)
    region = (tot + MOE_TMX - 1) // MOE_TMX * MOE_TMX
    gend = jnp.cumsum(region)
    gbase = gend - region
    cum_tiles = gend // MOE_TMX
    j = jnp.arange(n_xtiles, dtype=jnp.int32)
    loff_rows = jnp.zeros((n_tiles, g, LANES), F32).at[:, :, :N_EXPERTS].set(loff[:, None, :].astype(F32))
    k8 = g * jnp.arange(MOE_SLOTS // g, dtype=jnp.int32)
    run_end = loff + cnt8
    e_of_k = jnp.sum(k8[None, :, None] >= run_end[:, None, :], axis=2)
    shift = gbase[None, :] + boff - loff
    picked = jnp.sum(jnp.where(e_of_k[:, :, None] == jnp.arange(N_EXPERTS)[None, None, :], shift[:, None, :], 0), axis=2)
    return dict(
        chunk_dst=(picked + k8[None, :]).reshape(-1).astype(jnp.int32),
        ntot=jnp.sum(cnt8 // g, axis=1), zst=gbase + tot, znch=(region - tot) // g,
        ztot=jnp.sum((region - tot) // g).reshape(1),
        xtile_expert=jnp.minimum(jnp.sum(j[:, None] >= cum_tiles[None, :], axis=1), N_EXPERTS - 1).astype(jnp.int32),
        n_used=cum_tiles[-1:].astype(jnp.int32),
        loff_rows=loff_rows.reshape(n_tiles * g, LANES))


def _local_slots(route, loff_row):
    tm = route.shape[0]
    lane = lax.broadcasted_iota(jnp.int32, (tm, LANES), 1)
    e1 = lane == route[:, ROUTE_I1:ROUTE_I1 + 1].astype(jnp.int32)
    e2 = lane == route[:, ROUTE_I2:ROUTE_I2 + 1].astype(jnp.int32)
    earlier = (lax.broadcasted_iota(jnp.int32, (tm, tm), 1) < lax.broadcasted_iota(jnp.int32, (tm, tm), 0))
    earlier = earlier.astype(BF16)
    c1 = jnp.dot(earlier, e1.astype(BF16), preferred_element_type=F32)
    c2 = jnp.dot(earlier, e2.astype(BF16), preferred_element_type=F32)
    cnt1 = jnp.sum(e1.astype(F32), axis=0, keepdims=True)
    pos1 = jnp.sum(jnp.where(e1, c1 + loff_row, 0.0), axis=1, keepdims=True)
    pos2 = jnp.sum(jnp.where(e2, c2 + cnt1 + loff_row, 0.0), axis=1, keepdims=True)
    slot = lax.broadcasted_iota(jnp.int32, (tm, MOE_SLOTS), 1)
    return slot == pos1.astype(jnp.int32), slot == pos2.astype(jnp.int32)


def _pack_bf16_pairs(x):
    c = x.shape[1] // 2
    bits = lambda v: lax.bitcast_convert_type(v.astype(BF16).astype(F32), jnp.uint32)
    return bits(x[:, :c]) | (bits(x[:, c:]) >> 16)


def _unpack_bf16_pairs(w):
    hi = lax.bitcast_convert_type(w & jnp.uint32(0xFFFF0000), F32).astype(BF16)
    lo = lax.bitcast_convert_type(w << 16, F32).astype(BF16)
    return jnp.concatenate([hi, lo], axis=1)


def _split3(w):
    hi = w.astype(BF16).astype(F32)
    mid = (w - hi).astype(BF16).astype(F32)
    return hi, mid, (w - hi - mid).astype(BF16).astype(F32)


def _moe_scatter_kernel(cdst_s, ntot_s, zst_s, znch_s, ztot_s, nused_s,
                        hn_ref, route_ref, loffv_ref, xs_hbm, buf_ref, zero_ref, sem, zsem, tsem):
    b = pl.program_id(0)
    nb = pl.num_programs(0)
    slot = b % 2
    g = SUBLANES
    d = hn_ref.shape[1]
    n_xtiles = xs_hbm.shape[0] // MOE_TMX

    def run_copy(s, src_row, dst_row):
        return pltpu.make_async_copy(buf_ref.at[s, pl.ds(src_row, g)], xs_hbm.at[pl.ds(dst_row, g)], sem.at[s])

    def zero_copy(dst_row):
        return pltpu.make_async_copy(zero_ref.at[pl.ds(0, g)], xs_hbm.at[pl.ds(dst_row, g)], zsem)

    def zero_tile_copy(j):
        return pltpu.make_async_copy(zero_ref, xs_hbm.at[pl.ds(pl.multiple_of(j * MOE_TMX, MOE_TMX), MOE_TMX)], tsem)

    def wait_tile(tile, s):
        lax.fori_loop(0, ntot_s[tile], lambda _, c: (run_copy(s, 0, 0).wait(), c)[1], 0)

    @pl.when(b == 0)
    def _():
        zero_ref[...] = jnp.zeros_like(zero_ref)

        def per_expert(e, c):
            def per_chunk(k, c2):
                zero_copy(pl.multiple_of(zst_s[e] + k * g, g)).start()
                return c2
            return lax.fori_loop(0, znch_s[e], per_chunk, c)
        lax.fori_loop(0, N_EXPERTS, per_expert, 0)
        lax.fori_loop(nused_s[0], n_xtiles, lambda j, c: (zero_tile_copy(j).start(), c)[1], 0)

    @pl.when(b >= 2)
    def _():
        wait_tile(b - 2, slot)

    route = route_ref[...]
    p1, p2 = _local_slots(route, loffv_ref[0:1, :])
    tn = (((0,), (0,)), ((), ()))
    picks = (p1 | p2).astype(BF16)
    buf_ref[slot, :, 0:d // 2] = _pack_bf16_pairs(lax.dot_general(picks, hn_ref[...], tn, preferred_element_type=F32))
    lane = lax.broadcasted_iota(jnp.int32, route.shape, 1)
    meta = jnp.zeros((MOE_SLOTS, LANES), F32)
    for p, col in ((p1, ROUTE_W1), (p2, ROUTE_W2)):
        w = route[:, col:col + 1]
        parts = _split3(w)
        wm = sum(jnp.where(lane == k, part, 0.0) for k, part in enumerate(parts))
        meta = meta + lax.dot_general(p.astype(BF16), wm.astype(BF16), tn, preferred_element_type=F32)
    buf_ref[slot, :, d // 2:d // 2 + LANES] = lax.bitcast_convert_type(meta, jnp.uint32)

    def per_chunk(k, c):
        run_copy(slot, pl.multiple_of(k * g, g), pl.multiple_of(cdst_s[b * (MOE_SLOTS // g) + k], g)).start()
        return c
    lax.fori_loop(0, ntot_s[b], per_chunk, 0)

    @pl.when(b == nb - 1)
    def _():
        wait_tile(b, slot)

        @pl.when(nb >= 2)
        def _():
            wait_tile(b - 1, 1 - slot)
        lax.fori_loop(0, ztot_s[0], lambda _, c: (zero_copy(0).wait(), c)[1], 0)
        lax.fori_loop(nused_s[0], n_xtiles, lambda j, c: (zero_tile_copy(j).wait(), c)[1], 0)


def _moe_scatter(plan, hn_all, route_all, n_tiles, n_xtiles):
    d = hn_all.shape[1]
    tm = MOE_TM
    grid_spec = pltpu.PrefetchScalarGridSpec(
        num_scalar_prefetch=6,
        grid=(n_tiles,),
        in_specs=[pl.BlockSpec((tm, d), lambda i, *_: (i, 0)),
                  pl.BlockSpec((tm, LANES), lambda i, *_: (i, 0)),
                  pl.BlockSpec((SUBLANES, LANES), lambda i, *_: (i, 0))],
        out_specs=pl.BlockSpec(memory_space=pl.ANY),
        scratch_shapes=[pltpu.VMEM((2, MOE_SLOTS, d // 2 + LANES), jnp.uint32),
                        pltpu.VMEM((MOE_TMX, d // 2 + LANES), jnp.uint32),
                        pltpu.SemaphoreType.DMA((2,)),
                        pltpu.SemaphoreType.DMA(()),
                        pltpu.SemaphoreType.DMA(())])
    return pl.pallas_call(
        _moe_scatter_kernel,
        grid_spec=grid_spec,
        out_shape=jax.ShapeDtypeStruct((n_xtiles * MOE_TMX, d // 2 + LANES), jnp.uint32),
        compiler_params=_cparams(("arbitrary",)),
        name="moe_scatter",
    )(plan["chunk_dst"], plan["ntot"], plan["zst"], plan["znch"], plan["ztot"], plan["n_used"],
      hn_all, route_all, plan["loff_rows"])


def _moe_experts_kernel(xe_s, nused_s, x_ref, w1_ref, w3_ref, w2_ref, y_ref, w1b_ref, w3b_ref, w2b_ref):
    j = pl.program_id(0)
    d = w1_ref.shape[0]

    @pl.when(j < nused_s[0])
    def _():
        @pl.when((j == 0) | (xe_s[j] != xe_s[jnp.maximum(j - 1, 0)]))
        def _():
            w1b_ref[...] = w1_ref[...].astype(BF16)
            w3b_ref[...] = w3_ref[...].astype(BF16)
            w2b_ref[...] = w2_ref[...].astype(BF16)

        x = _unpack_bf16_pairs(x_ref[:, 0:d // 2])
        gate = jnp.sum(lax.bitcast_convert_type(x_ref[:, d // 2:d // 2 + LANES], F32), axis=1, keepdims=True)
        a = jnp.dot(x, w1b_ref[...], preferred_element_type=F32)
        b = jnp.dot(x, w3b_ref[...], preferred_element_type=F32)
        hh = (a * _sigmoid(a)) * b * gate
        y_ref[...] = _pack_bf16_pairs(jnp.dot(hh.astype(BF16), w2b_ref[...], preferred_element_type=F32))

    @pl.when(j >= nused_s[0])
    def _():
        y_ref[...] = jnp.zeros_like(y_ref)


def _moe_experts(plan, xs, w1, w3, w2, n_xtiles):
    ne, d, f = w1.shape
    last = lambda j, xe, nu: jnp.maximum(jnp.minimum(j, nu[0] - 1), 0)
    grid_spec = pltpu.PrefetchScalarGridSpec(
        num_scalar_prefetch=2,
        grid=(n_xtiles,),
        in_specs=[pl.BlockSpec((MOE_TMX, d // 2 + LANES), lambda j, xe, nu: (last(j, xe, nu), 0)),
                  pl.BlockSpec((None, d, f), lambda j, xe, nu: (xe[j], 0, 0)),
                  pl.BlockSpec((None, d, f), lambda j, xe, nu: (xe[j], 0, 0)),
                  pl.BlockSpec((None, f, d), lambda j, xe, nu: (xe[j], 0, 0))],
        out_specs=pl.BlockSpec((MOE_TMX, d // 2), lambda j, xe, nu: (j, 0)),
        scratch_shapes=[pltpu.VMEM((d, f), BF16), pltpu.VMEM((d, f), BF16), pltpu.VMEM((f, d), BF16)])
    return pl.pallas_call(
        _moe_experts_kernel,
        grid_spec=grid_spec,
        out_shape=jax.ShapeDtypeStruct((n_xtiles * MOE_TMX, d // 2), jnp.uint32),
        compiler_params=_cparams(("arbitrary",)),
        name="moe_experts",
    )(plan["xtile_expert"], plan["n_used"], xs, w1, w3, w2)


def _moe_combine_kernel(cdst_s, ntot_s, ys_hbm, route_ref, loffv_ref, h_ref, lnf_ref, y_ref,
                        buf_ref, sem, *, tile0, final_norm):
    b = pl.program_id(0)
    nb = pl.num_programs(0)
    slot = b % 2
    g = SUBLANES
    tile = b + tile0

    def run_copy(s, src_row, dst_row):
        return pltpu.make_async_copy(ys_hbm.at[pl.ds(src_row, g)], buf_ref.at[s, pl.ds(dst_row, g)], sem.at[s])

    def fetch(t, s):
        def per_chunk(k, c):
            run_copy(s, pl.multiple_of(cdst_s[t * (MOE_SLOTS // g) + k], g), pl.multiple_of(k * g, g)).start()
            return c
        lax.fori_loop(0, ntot_s[t], per_chunk, 0)

    @pl.when(b == 0)
    def _():
        fetch(tile, slot)

    @pl.when(b + 1 < nb)
    def _():
        fetch(tile + 1, 1 - slot)

    lax.fori_loop(0, ntot_s[tile], lambda _, c: (run_copy(slot, 0, 0).wait(), c)[1], 0)

    def clear(k, c):
        buf_ref[slot, pl.ds(pl.multiple_of(k * g, g), g), :] = jnp.zeros((g, buf_ref.shape[2]), jnp.uint32)
        return c
    lax.fori_loop(ntot_s[tile], MOE_SLOTS // g, clear, 0)

    p1, p2 = _local_slots(route_ref[...], loffv_ref[0:1, :])
    picks = (p1 | p2).astype(BF16)
    y = h_ref[...] + jnp.dot(picks, _unpack_bf16_pairs(buf_ref[slot]), preferred_element_type=F32)
    if final_norm:
        y = _rms(y, lnf_ref[...])
    y_ref[...] = y


def _moe_combine(plan, ys, route_all, h_all, lnf, tile0, n_tiles, final_norm):
    d = h_all.shape[1]
    tm = MOE_TM
    grid_spec = pltpu.PrefetchScalarGridSpec(
        num_scalar_prefetch=2,
        grid=(n_tiles,),
        in_specs=[pl.BlockSpec(memory_space=pl.ANY),
                  pl.BlockSpec((tm, LANES), lambda i, *_: (i + tile0, 0)),
                  pl.BlockSpec((SUBLANES, LANES), lambda i, *_: (i + tile0, 0)),
                  pl.BlockSpec((tm, d), lambda i, *_: (i + tile0, 0)),
                  pl.BlockSpec((1, d), lambda i, *_: (0, 0))],
        out_specs=pl.BlockSpec((tm, d), lambda i, *_: (i, 0)),
        scratch_shapes=[pltpu.VMEM((2, MOE_SLOTS, d // 2), jnp.uint32), pltpu.SemaphoreType.DMA((2,))])
    return pl.pallas_call(
        functools.partial(_moe_combine_kernel, tile0=tile0, final_norm=final_norm),
        grid_spec=grid_spec,
        out_shape=jax.ShapeDtypeStruct((n_tiles * tm, d), F32),
        compiler_params=_cparams(("arbitrary",)),
        name="moe_combine",
    )(plan["chunk_dst"], plan["ntot"], ys, route_all, plan["loff_rows"], h_all, lnf.reshape(1, d))


def _prompt_states(kvt, n, t):
    hd = HEAD_DIM
    tail = kvt.shape[2]

    def state(k_lo, v_lo, rows, heads, win):
        w = min(win, t)
        kv = jnp.stack([kvt[:, k_lo:k_lo + rows, tail - w:], kvt[:, v_lo:v_lo + rows, tail - w:]], axis=1)
        return jnp.transpose(kv.reshape(n, 2, heads, hd, w), (0, 4, 1, 2, 3))

    out = [state(0, KA_W, KA_W, A_KV_HEADS, A_WINDOW)]
    for p, (win, _) in enumerate(B_PATTERNS):
        out.append(state(2 * KA_W + p * PB_W, 2 * KA_W + QB_W + p * PB_W, PB_W, B_HEADS_PER_PATTERN, win))
    return out


def _layer(xp, xs, caches, rel_bias, ln1, w_in, sinks, w_pa, w_pb, w_out, ln2, w_rg, b_rg, w_re, b_re,
           w1, w3, w2, lnf, final_norm):
    n, t, d = xp.shape
    ns = xs.shape[0]
    assert xs.shape[1] == 1 and OFF_GA + 2 * d == w_in.shape[1] and P_QA == 2 * d
    w_bf = w_in.astype(BF16)
    w_kvt = jnp.concatenate([w_in[:, OFF_KA:OFF_QB], w_in[:, OFF_KB:OFF_GA]], axis=1).T.astype(BF16)
    wpa, wpb, wout = w_pa.astype(BF16), w_pb.astype(BF16), w_out.astype(BF16)
    wr = jnp.zeros((d, LANES), F32).at[:, :N_EXPERTS].set(w_re).at[:, N_EXPERTS:N_EXPERTS + MOE_GROUPS].set(w_rg)
    br = jnp.zeros((1, LANES), F32).at[0, :N_EXPERTS].set(b_re).at[0, N_EXPERTS:N_EXPERTS + MOE_GROUPS].set(b_rg)
    wr_hi = wr.astype(BF16)
    wr = jnp.stack([wr_hi, (wr - wr_hi.astype(F32)).astype(BF16)])

    tm = MOE_TM
    xp2 = xp.reshape(n * t, d)
    act, qkv2, qkv3, kvt = _in_proj_prompt(xp2, ln1, w_bf, w_kvt, n, t, IN_PROJ_TM)
    act4 = act.reshape(n, 1, t, ACT_W)
    bias_a = _band_bias(rel_bias[:, :A_HEADS], A_WINDOW - 1, 1)
    (oa,) = _band_attn(act4, bias_a, sinks, q_off=P_QA, k_off=P_KA, v_off=P_VA,
                       kv_heads=A_KV_HEADS, grp=A_GROUP, want_lse=False)
    obs, lses = [], []
    for p, (win, dil) in enumerate(B_PATTERNS):
        lo = A_HEADS + p * B_HEADS_PER_PATTERN
        bias_p = _band_bias(rel_bias[:, lo:lo + B_HEADS_PER_PATTERN], win // dil, dil)
        src, base = ((act4, P_B), (qkv2, 0), (qkv3, 0))[p]
        o, lse = _band_attn(src, bias_p, None, q_off=base, k_off=base + PB_W, v_off=base + 2 * PB_W,
                            kv_heads=B_HEADS_PER_PATTERN, grp=1, want_lse=True)
        obs.append(o)
        lses.append(lse)
    assert ns <= MOE_TM
    p_tiles = n * t // MOE_TM
    n_tiles = p_tiles + 1
    m_all = n_tiles * MOE_TM
    st_p = _prompt_states(kvt, n, t)

    xs2 = xs.reshape(ns, d)
    proj_s = _in_proj_sample(xs2, ln1, w_bf)
    oa_s, ob_s, lse_s, st_s = _sample_mix(proj_s, caches, rel_bias, sinks)
    rows = lambda a: jnp.pad(a, ((0, MOE_TM - ns), (0, 0)))
    obs_s = [rows(ob_s[:, p].astype(BF16)).reshape(1, 1, MOE_TM, PB_W) for p in range(N_PAT)]
    lses_s = [rows(jnp.repeat(lse_s[:, p], HEAD_DIM, axis=-1)).reshape(1, 1, MOE_TM, PB_W) for p in range(N_PAT)]
    gates_s = rows(proj_s[:, OFF_GA:].astype(BF16))
    routed_s = _post_attn(rows(oa_s.astype(BF16)), obs_s, lses_s, gates_s, 0, rows(xs2),
                          wpa, wpb, wout, ln2, wr, br, MOE_TM, 1, None)
    h_all, hn_all, route_all, cnt_all = _post_attn(oa.reshape(n * t, QA_W), obs, lses, act, 0, xp2,
                                                   wpa, wpb, wout, ln2, wr, br, tm, t // tm, routed_s)

    max_rows = 2 * m_all + n_tiles * N_EXPERTS * (SUBLANES - 1) + N_EXPERTS * (MOE_TMX - SUBLANES)
    n_xtiles = -(-max_rows // MOE_TMX)
    plan = _route_plan(cnt_all, n_tiles, n_xtiles)
    xs_sorted = _moe_scatter(plan, hn_all, route_all, n_tiles, n_xtiles)
    ys_sorted = _moe_experts(plan, xs_sorted, w1, w3, w2, n_xtiles)
    yp = _moe_combine(plan, ys_sorted, route_all, h_all, lnf, 0, p_tiles, final_norm).reshape(n, t, d)
    ys = _moe_combine(plan, ys_sorted, route_all, h_all, lnf, p_tiles, 1, final_norm)[:ns].reshape(ns, 1, d)
    return yp, ys, st_p, st_s


def kernel(x_prompt, x_sample, cache_a_kv, cache_b1_kv, cache_b2_kv, cache_b3_kv, rel_bias, ln1_g, w_in, sinks,
           w_pa, w_pb, w_out, ln2_g, w_rg, b_rg, w_re, b_re, w1, w3, w2, lnf_g):
    depth = w_in.shape[0]
    assert depth >= 1
    xp, xs = x_prompt, x_sample
    new_p = [[] for _ in range(4)]
    new_s = [[] for _ in range(4)]
    for l in range(depth):
        caches = (cache_a_kv[l], cache_b1_kv[l], cache_b2_kv[l], cache_b3_kv[l])
        xp, xs, st_p, st_s = _layer(xp, xs, caches, rel_bias, ln1_g[l], w_in[l], sinks[l], w_pa[l], w_pb[l],
                                    w_out[l], ln2_g[l], w_rg[l], b_rg[l], w_re[l], b_re[l], w1[l], w3[l], w2[l],
                                    lnf_g, l == depth - 1)
        for i in range(4):
            new_p[i].append(st_p[i])
            new_s[i].append(st_s[i])
    a_p, b1_p, b2_p, b3_p = [jnp.stack(v) for v in new_p]
    a_s, b1_s, b2_s, b3_s = [jnp.stack(v) for v in new_s]
    return (xp, xs, a_p, a_s, b1_p, b1_s, b2_p, b2_s, b3_p, b3_s)
```

```python
import functools
import math

import numpy as np
import jax
import jax.numpy as jnp
from jax import lax
from jax.experimental import pallas as pl
from jax.experimental.pallas import tpu as pltpu

F32 = jnp.float32
BF16 = jnp.bfloat16

HEAD_DIM = 64
A_HEADS = 8
A_KV_HEADS = 2
A_GROUP = A_HEADS // A_KV_HEADS
A_WINDOW = 128
B_PATTERNS = ((128, 1), (512, 4), (2048, 16))
N_PAT = len(B_PATTERNS)
B_HEADS_PER_PATTERN = 4
B_HEADS = B_HEADS_PER_PATTERN * N_PAT
BLOCK = 128
ATTN_CHAINS = 16
NUM_BUCKETS = 32
MAX_DISTANCE = 2048
MOE_GROUPS = 4
EXPERTS_PER_GROUP = 8
N_EXPERTS = MOE_GROUPS * EXPERTS_PER_GROUP
EPS = 1e-6
NEG = -1e30
LANES = 128
SUBLANES = 8
ROUTE_I1, ROUTE_I2, ROUTE_W1, ROUTE_W2 = 0, 1, 2, 3
IN_PROJ_TM = 512
MIX_TM = 256
MOE_TM = 256
MOE_SLOTS = 2 * MOE_TM + N_EXPERTS * SUBLANES
MOE_TMX = 256
Q_SCALE = HEAD_DIM ** -0.5

QA_W = A_HEADS * HEAD_DIM
KA_W = A_KV_HEADS * HEAD_DIM
QB_W = B_HEADS * HEAD_DIM
PB_W = B_HEADS_PER_PATTERN * HEAD_DIM
QKV_W = 3 * PB_W
OFF_KA = QA_W
OFF_QB = OFF_KA + 2 * KA_W
OFF_KB = OFF_QB + QB_W
OFF_VB = OFF_KB + QB_W
OFF_GA = OFF_VB + QB_W
P_QA = 2048
P_KA = P_QA + QA_W
P_VA = P_KA + KA_W
P_B = P_VA + KA_W
ACT_W = P_B + QKV_W
KVT_ROWS = 2 * KA_W + 2 * QB_W

VMEM_LIMIT = 56 * 1024 * 1024


def _cparams(sem):
    return pltpu.CompilerParams(dimension_semantics=sem, vmem_limit_bytes=VMEM_LIMIT)


def _bucket_np(dist):
    dist = np.asarray(dist, np.int64)
    max_exact = NUM_BUCKETS // 2
    df = np.maximum(dist, max_exact).astype(np.float64)
    large = max_exact + (np.log(df / max_exact) / math.log(MAX_DISTANCE / max_exact)
                         * (NUM_BUCKETS - max_exact)).astype(np.int64)
    return np.where(dist < max_exact, dist, np.minimum(large, NUM_BUCKETS - 1))


def _table_rows(table_cols, dist, valid):
    onehot = (_bucket_np(dist)[:, None] == np.arange(NUM_BUCKETS)[None, :]).astype(np.float32)
    rows = jnp.einsum("ck,kh->hc", jnp.asarray(onehot), table_cols.astype(F32), precision=lax.Precision.HIGHEST)
    return jnp.where(jnp.asarray(valid)[None, :], rows, NEG)


def _band_bias(table_cols, max_dist, dilation):
    period = 3 * BLOCK
    m = np.arange(period)
    k = np.where(m < 2 * BLOCK, m, m - period)
    dist = BLOCK - k
    valid = (dist >= 0) & (dist <= max_dist) & (m != 2 * BLOCK)
    v = _table_rows(table_cols, np.clip(dist, 0, None) * dilation, valid)
    heads = v.shape[0]
    flat = jnp.tile(v, (1, BLOCK))[:, :BLOCK * (period - 1)]
    return flat.reshape(heads, BLOCK, period - 1)[:, :, :2 * BLOCK]


def _decode_bias(table_cols, width, dilation, first_valid):
    c = np.arange(width)
    valid = (c % dilation == 0) & (c >= first_valid)
    rows = _table_rows(table_cols, width - c, valid)
    self_bias = _table_rows(table_cols, np.zeros((1,), np.int64), np.ones((1,), bool))
    return rows[:, None, :], self_bias[:, None, :]


def _rms(x, g):
    return (x * lax.rsqrt(jnp.mean(x * x, axis=-1, keepdims=True) + EPS)) * g


def _sigmoid(x):
    return 1.0 / (1.0 + jnp.exp(-x))


YS_CHUNKS = QKV_W // LANES


def _in_proj_work(x_ref, g_ref, w_ref, act_ref, qkv2_ref, qkv3_ref, ys_ref, phase):
    tm = x_ref.shape[0]
    cache = {}

    def xb():
        if "xb" not in cache:
            cache["xb"] = _rms(x_ref[...], g_ref[...]).astype(BF16)
        return cache["xb"]

    def proj(lo, hi):
        return jnp.dot(xb(), w_ref[:, lo:hi], preferred_element_type=F32)

    def gates(c):
        act_ref[:, c:c + 512] = _sigmoid(proj(OFF_GA + c, OFF_GA + c + 512)).astype(BF16)

    def mixer_a_q():
        act_ref[:, P_QA:P_KA] = (proj(0, OFF_KA) * Q_SCALE).astype(BF16)

    def mixer_a_kv():
        act_ref[:, P_KA:P_B] = proj(OFF_KA, OFF_QB).astype(BF16)

    def pattern_part(p, j):
        lo = (OFF_QB, OFF_KB, OFF_VB)[j] + p * PB_W
        part = proj(lo, lo + PB_W)
        if j == 0:
            part = part * Q_SCALE
        if p == 0:
            act_ref[:, P_B + j * PB_W:P_B + (j + 1) * PB_W] = part.astype(BF16)
        else:
            for c in range(PB_W // LANES):
                ys_ref[(p - 1) * YS_CHUNKS + j * (PB_W // LANES) + c] = part[:, c * LANES:(c + 1) * LANES]

    def regroup(p, out_ref):
        dil = B_PATTERNS[p][1]
        for c in range(YS_CHUNKS):
            for r in range(dil):
                out_ref[r, :, c * LANES:(c + 1) * LANES] = (
                    ys_ref[(p - 1) * YS_CHUNKS + c, pl.ds(r, tm // dil, stride=dil), :].astype(BF16))

    work = []
    if phase in (None, 0):
        work += [functools.partial(gates, c) for c in range(0, P_QA, 512)] + [mixer_a_q, mixer_a_kv]
    if phase in (None, 1):
        for p in range(N_PAT):
            work += [functools.partial(pattern_part, p, j) for j in range(3)]
            if p > 0:
                work.append(functools.partial(regroup, p, (qkv2_ref, qkv3_ref)[p - 1]))
    return work


def _kv_tail(x_ref, g_ref, wkvt_ref, kvt_ref):
    xb = _rms(x_ref[...], g_ref[...]).astype(BF16)
    kvt_ref[...] = lax.dot_general(wkvt_ref[...], xb, (((1,), (1,)), ((), ())), preferred_element_type=F32)


def _in_proj_kernel(x_ref, g_ref, w_ref, wkvt_ref, act_ref, qkv2_ref, qkv3_ref, kvt_ref, ys_ref, *,
                    tiles_per_seq, tail_tiles):
    for piece in _in_proj_work(x_ref, g_ref, w_ref, act_ref, qkv2_ref, qkv3_ref, ys_ref, None):
        piece()

    @pl.when(pl.program_id(0) % tiles_per_seq >= tiles_per_seq - tail_tiles)
    def _():
        _kv_tail(x_ref, g_ref, wkvt_ref, kvt_ref)


def _in_proj_prompt(x2d, ln_g, w_bf, w_kvt, n, t, tm):
    m, d = x2d.shape
    tps = t // tm
    tail = min(max(w for w, _ in B_PATTERNS), t)
    assert t % tm == 0 and tail % tm == 0 and all(tm % (16 * dl) == 0 for _, dl in B_PATTERNS)
    tail_tiles = tail // tm
    d2, d3 = B_PATTERNS[1][1], B_PATTERNS[2][1]
    return pl.pallas_call(
        functools.partial(_in_proj_kernel, tiles_per_seq=tps, tail_tiles=tail_tiles),
        grid=(m // tm,),
        in_specs=[pl.BlockSpec((tm, d), lambda i: (i, 0)),
                  pl.BlockSpec((1, d), lambda i: (0, 0)),
                  pl.BlockSpec(w_bf.shape, lambda i: (0, 0), pipeline_mode=pl.Buffered(1)),
                  pl.BlockSpec(w_kvt.shape, lambda i: (0, 0), pipeline_mode=pl.Buffered(1))],
        out_specs=[pl.BlockSpec((tm, ACT_W), lambda i: (i, 0)),
                   pl.BlockSpec((None, d2, tm // d2, QKV_W), lambda i: (i // tps, 0, i % tps, 0)),
                   pl.BlockSpec((None, d3, tm // d3, QKV_W), lambda i: (i // tps, 0, i % tps, 0)),
                   pl.BlockSpec((None, KVT_ROWS, tm),
                                lambda i: (i // tps, 0, jnp.maximum(i % tps - (tps - tail_tiles), 0)))],
        out_shape=[jax.ShapeDtypeStruct((m, ACT_W), BF16),
                   jax.ShapeDtypeStruct((n, d2, t // d2, QKV_W), BF16),
                   jax.ShapeDtypeStruct((n, d3, t // d3, QKV_W), BF16),
                   jax.ShapeDtypeStruct((n, KVT_ROWS, tail), F32)],
        scratch_shapes=[pltpu.VMEM(((N_PAT - 1) * YS_CHUNKS, tm, LANES), F32)],
        compiler_params=_cparams(("arbitrary",)),
        name="in_proj",
    )(x2d, ln_g.reshape(1, d), w_bf, w_kvt)


def _in_proj_sample_kernel(x_ref, g_ref, w_ref, y_ref):
    xb = _rms(x_ref[...], g_ref[...]).astype(BF16)
    y_ref[:, :OFF_GA] = jnp.dot(xb, w_ref[:, :OFF_GA], preferred_element_type=F32)
    for lo, hi in ((0, OFF_KA), (OFF_QB, OFF_KB)):
        y_ref[:, lo:hi] = y_ref[:, lo:hi] * Q_SCALE
    for lo in range(OFF_GA, w_ref.shape[1], 1024):
        y_ref[:, lo:lo + 1024] = _sigmoid(jnp.dot(xb, w_ref[:, lo:lo + 1024], preferred_element_type=F32))


def _in_proj_sample(x2d, ln_g, w_bf):
    m, d = x2d.shape
    return pl.pallas_call(
        _in_proj_sample_kernel,
        grid=(1,),
        in_specs=[pl.BlockSpec((m, d), lambda i: (0, 0)),
                  pl.BlockSpec((1, d), lambda i: (0, 0)),
                  pl.BlockSpec(w_bf.shape, lambda i: (0, 0))],
        out_specs=pl.BlockSpec((m, w_bf.shape[1]), lambda i: (0, 0)),
        out_shape=jax.ShapeDtypeStruct((m, w_bf.shape[1]), F32),
        compiler_params=_cparams(("arbitrary",)),
        name="in_proj_sample",
    )(x2d, ln_g.reshape(1, d), w_bf)


def _band_attn_kernel(*refs, kv_heads, grp, has_sink, want_lse):
    if has_sink:
        sink_ref, refs = refs[0], refs[1:]
    q_ref, kp_ref, kc_ref, vp_ref, vc_ref, bias_ref, o_ref = refs[:7]
    lse_ref = refs[7] if want_lse else None
    step = pl.program_id(2)
    hd = HEAD_DIM
    nt = (((1,), (1,)), ((), ()))
    chains = [(sub, kv * grp + g, slice(kv * hd, (kv + 1) * hd))
              for sub in range(q_ref.shape[0] // BLOCK) for kv in range(kv_heads) for g in range(grp)]
    scores = []
    for sub, h, ks in chains:
        rows = slice(sub * BLOCK, (sub + 1) * BLOCK)
        q = q_ref[rows, h * hd:(h + 1) * hd]
        kp = kp_ref[:, ks] if sub == 0 else kc_ref[(sub - 1) * BLOCK:sub * BLOCK, ks]
        sp = lax.dot_general(q, kp, nt, preferred_element_type=F32) + bias_ref[h, :, 0:BLOCK]
        sc = lax.dot_general(q, kc_ref[rows, ks], nt, preferred_element_type=F32) + bias_ref[h, :, BLOCK:2 * BLOCK]
        if sub == 0:
            sp = jnp.where(step > 0, sp, NEG)
        scores.append((sp, sc))
    maxes = []
    for (sub, h, ks), (sp, sc) in zip(chains, scores):
        m = jnp.maximum(jnp.max(sp, axis=-1, keepdims=True), jnp.max(sc, axis=-1, keepdims=True))
        maxes.append(jnp.maximum(m, sink_ref[h]) if has_sink else m)
    probs = []
    for (sub, h, ks), (sp, sc), m in zip(chains, scores, maxes):
        pp, pc = jnp.exp(sp - m), jnp.exp(sc - m)
        den = jnp.sum(pp, axis=-1, keepdims=True) + jnp.sum(pc, axis=-1, keepdims=True)
        if has_sink:
            den = den + jnp.exp(sink_ref[h] - m)
        probs.append((pp.astype(BF16), pc.astype(BF16), den))
    for (sub, h, ks), (pp, pc, den), m in zip(chains, probs, maxes):
        rows = slice(sub * BLOCK, (sub + 1) * BLOCK)
        vp = vp_ref[:, ks] if sub == 0 else vc_ref[(sub - 1) * BLOCK:sub * BLOCK, ks]
        o = (jnp.dot(pp, vp, preferred_element_type=F32)
             + jnp.dot(pc, vc_ref[rows, ks], preferred_element_type=F32))
        o_ref[rows, h * hd:(h + 1) * hd] = (o / den).astype(o_ref.dtype)
        if want_lse:
            lse_ref[rows, h * hd:(h + 1) * hd] = jnp.broadcast_to(m + jnp.log(den), (BLOCK, hd))


def _band_attn(src, bias, sink, *, q_off, k_off, v_off, kv_heads, grp, want_lse):
    n, dil, l, cols = src.shape
    sub = max(1, ATTN_CHAINS // (kv_heads * grp))
    while l % (sub * BLOCK):
        sub //= 2
    rows = sub * BLOCK
    assert sub >= 1 and l % rows == 0
    nb = l // rows
    qw = kv_heads * grp * HEAD_DIM
    kw = kv_heads * HEAD_DIM
    assert q_off % qw == 0 and k_off % kw == 0 and v_off % kw == 0
    qb, kb, vb = q_off // qw, k_off // kw, v_off // kw
    prev = lambda b: jnp.maximum(sub * b - 1, 0)
    in_specs = [
        pl.BlockSpec((None, None, rows, qw), lambda i, r, b: (i, r, b, qb)),
        pl.BlockSpec((None, None, BLOCK, kw), lambda i, r, b: (i, r, prev(b), kb)),
        pl.BlockSpec((None, None, rows, kw), lambda i, r, b: (i, r, b, kb)),
        pl.BlockSpec((None, None, BLOCK, kw), lambda i, r, b: (i, r, prev(b), vb)),
        pl.BlockSpec((None, None, rows, kw), lambda i, r, b: (i, r, b, vb)),
        pl.BlockSpec(bias.shape, lambda i, r, b: (0, 0, 0)),
    ]
    args = [src, src, src, src, src, bias]
    has_sink = sink is not None
    if has_sink:
        in_specs = [pl.BlockSpec(memory_space=pltpu.SMEM)] + in_specs
        args = [sink.astype(F32)] + args
    out_specs = [pl.BlockSpec((None, None, rows, qw), lambda i, r, b: (i, r, b, 0))]
    out_shape = [jax.ShapeDtypeStruct((n, dil, l, qw), BF16)]
    if want_lse:
        out_specs.append(pl.BlockSpec((None, None, rows, qw), lambda i, r, b: (i, r, b, 0)))
        out_shape.append(jax.ShapeDtypeStruct((n, dil, l, qw), F32))
    return pl.pallas_call(
        functools.partial(_band_attn_kernel, kv_heads=kv_heads, grp=grp, has_sink=has_sink, want_lse=want_lse),
        grid=(n, dil, nb),
        in_specs=in_specs,
        out_specs=out_specs,
        out_shape=out_shape,
        compiler_params=_cparams(("arbitrary", "arbitrary", "arbitrary")),
        name=f"band_attn_d{dil}",
    )(*args)


COL_QA = 0
COL_QB = COL_QA + A_HEADS
COL_KA = COL_QB + B_HEADS
COL_VA = COL_KA + A_KV_HEADS
COL_KB = COL_VA + A_KV_HEADS
COL_VB = COL_KB + B_HEADS
N_COLS = COL_VB + B_HEADS


def _attend_cached(kt, vt, q, k_new, v_new, bias, self_bias, sink, write):
    s = jnp.sum(kt * q, axis=1, keepdims=True) + bias
    s_new = jnp.sum(k_new * q, axis=1, keepdims=True) + self_bias
    yield
    m = jnp.maximum(jnp.max(s, axis=2, keepdims=True), s_new)
    if sink is not None:
        m = jnp.maximum(m, sink)
    yield
    p = jnp.exp(s - m)
    p_new = jnp.exp(s_new - m)
    den = jnp.sum(p, axis=2, keepdims=True) + p_new
    if sink is not None:
        den = den + jnp.exp(sink - m)
    yield
    o = (jnp.sum(vt * p, axis=2, keepdims=True) + v_new * p_new) / den
    write(o, m + jnp.log(den))
    yield


def _sample_work(cols_ref, ca_ref, c1_ref, c2_ref, c3_ref, ba_ref, sa_ref, sink_ref, b1_ref, b2_ref, b3_ref,
                 sb_ref, o_ref, lse_ref, ra_ref, r1_ref, r2_ref, r3_ref):
    cols = cols_ref[...]

    def stack(js):
        return jnp.stack([cols[:, j:j + 1] for j in js])

    o_ref[...] = jnp.zeros_like(o_ref)
    lse_ref[...] = jnp.zeros_like(lse_ref)

    def write_a(o, _):
        for h in range(A_HEADS):
            o_ref[:, COL_QA + h:COL_QA + h + 1] = o[h]

    def write_b(p):
        def write(o, lse):
            for h in range(B_HEADS_PER_PATTERN):
                j = p * B_HEADS_PER_PATTERN + h
                o_ref[:, COL_QB + j:COL_QB + j + 1] = o[h]
                lse_ref[0:1, j:j + 1] = lse[h]
        return write

    kv_of = [h // A_GROUP for h in range(A_HEADS)]
    stages = [_attend_cached(jnp.stack([ca_ref[0, kv] for kv in kv_of]), jnp.stack([ca_ref[1, kv] for kv in kv_of]),
                             stack(range(COL_QA, COL_QA + A_HEADS)), stack([COL_KA + kv for kv in kv_of]),
                             stack([COL_VA + kv for kv in kv_of]), ba_ref[...], sa_ref[...], sink_ref[...], write_a)]
    for p, (c_ref, b_ref) in enumerate(((c1_ref, b1_ref), (c2_ref, b2_ref), (c3_ref, b3_ref))):
        js = range(p * B_HEADS_PER_PATTERN, (p + 1) * B_HEADS_PER_PATTERN)
        stages.append(_attend_cached(c_ref[0], c_ref[1], stack([COL_QB + j for j in js]),
                                     stack([COL_KB + j for j in js]), stack([COL_VB + j for j in js]),
                                     b_ref[...], sb_ref[p], None, write_b(p)))

    def roll_plane(c_ref, r_ref, i, h, new_col):
        w = c_ref.shape[-1]
        x = c_ref[i, h]
        lane = lax.broadcasted_iota(jnp.int32, x.shape, 1)
        r_ref[i, h] = jnp.where(lane == w - 1, cols[:, new_col:new_col + 1], pltpu.roll(x, w - 1, 1))

    planes = []
    for ci, (c_ref, r_ref) in enumerate(((ca_ref, ra_ref), (c1_ref, r1_ref), (c2_ref, r2_ref), (c3_ref, r3_ref))):
        for i, (first_a, first_b) in enumerate(((COL_KA, COL_KB), (COL_VA, COL_VB))):
            for h in range(c_ref.shape[1]):
                new_col = first_a + h if ci == 0 else first_b + (ci - 1) * B_HEADS_PER_PATTERN + h
                planes.append((c_ref.shape[-1], functools.partial(roll_plane, c_ref, r_ref, i, h, new_col)))

    work = []
    total = sum(w for w, _ in planes)
    n_slots = 4 * len(stages)
    done, k = 0, 0
    for slot in range(n_slots):
        work.append(functools.partial(next, stages[slot // 4]))
        while k < len(planes) and done < total * (slot + 1) // n_slots:
            done += planes[k][0]
            work.append(planes[k][1])
            k += 1
    assert k == len(planes)
    return work


def _sample_mix_kernel(*refs):
    for piece in _sample_work(*refs):
        piece()


def _run_interleaved(primary, secondary):
    k = 0
    for i, piece in enumerate(primary):
        piece()
        while k < len(secondary) and k < len(secondary) * (i + 1) // len(primary):
            secondary[k]()
            k += 1
    for piece in secondary[k:]:
        piece()


def _in_proj_mix_kernel(x_ref, g_ref, w_ref, wkvt_ref, *rest, tiles_per_seq, tail_tiles, n_sample_in):
    sample_in = rest[:n_sample_in]
    act_ref, qkv2_ref, qkv3_ref, kvt_ref = rest[n_sample_in:n_sample_in + 4]
    sample_out = rest[n_sample_in + 4:-1]
    ys_ref = rest[-1]
    step = pl.program_id(0)
    tile = step // 2
    for phase in (0, 1):
        @pl.when(step % 2 == phase)
        def _():
            _run_interleaved(_sample_work(*sample_in, *sample_out),
                             _in_proj_work(x_ref, g_ref, w_ref, act_ref, qkv2_ref, qkv3_ref, ys_ref, phase))
            if phase == 1:
                @pl.when(tile % tiles_per_seq >= tiles_per_seq - tail_tiles)
                def _():
                    _kv_tail(x_ref, g_ref, wkvt_ref, kvt_ref)


def _sample_operands(proj_s, caches, rel_bias, sinks):
    n = proj_s.shape[0]
    hd = HEAD_DIM
    vecs = jnp.concatenate([proj_s[:, :OFF_KA], proj_s[:, OFF_QB:OFF_KB], proj_s[:, OFF_KA:OFF_QB],
                            proj_s[:, OFF_KB:OFF_GA]], axis=1).reshape(n, N_COLS, hd)
    cols = jnp.pad(jnp.transpose(vecs, (0, 2, 1)), ((0, 0), (0, 0), (0, LANES - N_COLS)))
    cts = [jnp.transpose(c, (0, 2, 3, 4, 1)) for c in caches]

    ba, sa = _decode_bias(rel_bias[:, :A_HEADS], A_WINDOW, 1, 1)
    sink = sinks.astype(F32).reshape(A_HEADS, 1, 1)
    bbs, sbs = [], []
    for p, (win, dil) in enumerate(B_PATTERNS):
        lo = A_HEADS + p * B_HEADS_PER_PATTERN
        assert caches[1 + p].shape[1] == win == BLOCK * dil
        b, s = _decode_bias(rel_bias[:, lo:lo + B_HEADS_PER_PATTERN], win, dil, 0)
        bbs.append(b)
        sbs.append(s)
    sb = jnp.stack(sbs)

    seq_in = [cols] + cts
    consts = [ba, sa, sink] + bbs + [sb]
    out_shape = [jax.ShapeDtypeStruct((n, hd, LANES), F32), jax.ShapeDtypeStruct((n, SUBLANES, LANES), F32)]
    out_shape += [jax.ShapeDtypeStruct(c.shape, c.dtype) for c in cts]
    return seq_in, consts, out_shape


def _per_seq_spec(a):
    nd = len(a.shape)
    return pl.BlockSpec((None,) + tuple(a.shape[1:]), lambda i: (i,) + (0,) * (nd - 1))


def _const_spec(a):
    nd = a.ndim
    return pl.BlockSpec(a.shape, lambda i: (0,) * nd)


def _sample_results(outs):
    n = outs[0].shape[0]
    o_rows = jnp.transpose(outs[0][:, :, :COL_KA], (0, 2, 1))
    oa = o_rows[:, COL_QA:COL_QB].reshape(n, QA_W)
    ob = o_rows[:, COL_QB:COL_KA].reshape(n, N_PAT, PB_W)
    lse = outs[1][:, 0, :B_HEADS].reshape(n, N_PAT, B_HEADS_PER_PATTERN)
    rolled = [jnp.transpose(r, (0, 4, 1, 2, 3)) for r in outs[2:]]
    return oa, ob, lse, rolled


def _sample_mix(proj_s, caches, rel_bias, sinks):
    seq_in, consts, out_shape = _sample_operands(proj_s, caches, rel_bias, sinks)
    outs = pl.pallas_call(
        _sample_mix_kernel,
        grid=(proj_s.shape[0],),
        in_specs=[_per_seq_spec(a) for a in seq_in] + [_const_spec(a) for a in consts],
        out_specs=[_per_seq_spec(s) for s in out_shape],
        out_shape=out_shape,
        compiler_params=_cparams(("arbitrary",)),
        name="sample_mix",
    )(*seq_in, *consts)
    return _sample_results(outs)


def _in_proj_mix(x2d, ln_g, w_bf, w_kvt, n, t, tm, proj_s, caches, rel_bias, sinks):
    m, d = x2d.shape
    tps = t // tm
    tail = min(max(w for w, _ in B_PATTERNS), t)
    assert t % tm == 0 and tail % tm == 0 and all(tm % (16 * dl) == 0 for _, dl in B_PATTERNS)
    assert proj_s.shape[0] == 2 * (m // tm)
    tail_tiles = tail // tm
    d2, d3 = B_PATTERNS[1][1], B_PATTERNS[2][1]
    seq_in, consts, sample_shape = _sample_operands(proj_s, caches, rel_bias, sinks)
    tile = lambda i: i // 2
    outs = pl.pallas_call(
        functools.partial(_in_proj_mix_kernel, tiles_per_seq=tps, tail_tiles=tail_tiles,
                          n_sample_in=len(seq_in) + len(consts)),
        grid=(2 * (m // tm),),
        in_specs=[pl.BlockSpec((tm, d), lambda i: (tile(i), 0)),
                  pl.BlockSpec((1, d), lambda i: (0, 0)),
                  pl.BlockSpec(w_bf.shape, lambda i: (0, 0), pipeline_mode=pl.Buffered(1)),
                  pl.BlockSpec(w_kvt.shape, lambda i: (0, 0), pipeline_mode=pl.Buffered(1))]
                 + [_per_seq_spec(a) for a in seq_in] + [_const_spec(a) for a in consts],
        out_specs=[pl.BlockSpec((tm, ACT_W), lambda i: (tile(i), 0)),
                   pl.BlockSpec((None, d2, tm // d2, QKV_W), lambda i: (tile(i) // tps, 0, tile(i) % tps, 0)),
                   pl.BlockSpec((None, d3, tm // d3, QKV_W), lambda i: (tile(i) // tps, 0, tile(i) % tps, 0)),
                   pl.BlockSpec((None, KVT_ROWS, tm),
                                lambda i: (tile(i) // tps, 0, jnp.maximum(tile(i) % tps - (tps - tail_tiles), 0)))]
                  + [_per_seq_spec(s) for s in sample_shape],
        out_shape=[jax.ShapeDtypeStruct((m, ACT_W), BF16),
                   jax.ShapeDtypeStruct((n, d2, t // d2, QKV_W), BF16),
                   jax.ShapeDtypeStruct((n, d3, t // d3, QKV_W), BF16),
                   jax.ShapeDtypeStruct((n, KVT_ROWS, tail), F32)] + sample_shape,
        scratch_shapes=[pltpu.VMEM(((N_PAT - 1) * YS_CHUNKS, tm, LANES), F32)],
        compiler_params=_cparams(("arbitrary",)),
        name="in_proj_mix",
    )(x2d, ln_g.reshape(1, d), w_bf, w_kvt, *seq_in, *consts)
    return outs[:4], _sample_results(outs[4:])


def _post_attn_kernel(oa_ref, o1_ref, o2_ref, o3_ref, l1_ref, l2_ref, l3_ref, ga_ref, gb_ref, x_ref,
                      wpa_ref, wpb_ref, wout_ref, ln2_ref, wr_ref, br_ref, *rest, n_extra):
    outs = rest[n_extra:n_extra + 4]
    h_ref, hn_ref, route_ref, cnt_ref = outs
    scr_ref = rest[n_extra + 4]
    if n_extra:
        own_tile = pl.program_id(0) < pl.num_programs(0) - 1

        @pl.when(jnp.logical_not(own_tile))
        def _():
            for dst, src in zip(outs, rest[:n_extra]):
                dst[...] = src[...]

        @pl.when(own_tile)
        def _():
            _post_attn_tile(oa_ref, o1_ref, o2_ref, o3_ref, l1_ref, l2_ref, l3_ref, ga_ref, gb_ref, x_ref,
                            wpa_ref, wpb_ref, wout_ref, ln2_ref, wr_ref, br_ref, *outs, scr_ref)
    else:
        _post_attn_tile(oa_ref, o1_ref, o2_ref, o3_ref, l1_ref, l2_ref, l3_ref, ga_ref, gb_ref, x_ref,
                        wpa_ref, wpb_ref, wout_ref, ln2_ref, wr_ref, br_ref, *outs, scr_ref)


def _post_attn_tile(oa_ref, o1_ref, o2_ref, o3_ref, l1_ref, l2_ref, l3_ref, ga_ref, gb_ref, x_ref,
                    wpa_ref, wpb_ref, wout_ref, ln2_ref, wr_ref, br_ref, h_ref, hn_ref, route_ref, cnt_ref, scr_ref):
    tm = x_ref.shape[0]
    chunks = PB_W // LANES

    def token_major(ref, slot):
        dil = ref.shape[0]
        if dil == 1:
            return ref[0].astype(F32)
        for c in range(chunks):
            for r in range(dil):
                scr_ref[slot * chunks + c, pl.ds(r, tm // dil, stride=dil), :] = (
                    ref[r, :, c * LANES:(c + 1) * LANES].astype(F32))
        return jnp.concatenate([scr_ref[slot * chunks + c] for c in range(chunks)], axis=1)

    o1, o2, o3 = (token_major(r, s) for s, r in enumerate((o1_ref, o2_ref, o3_ref)))
    l1, l2, l3 = (token_major(r, 3 + s) for s, r in enumerate((l1_ref, l2_ref, l3_ref)))
    m = jnp.maximum(jnp.maximum(l1, l2), l3)
    a1, a2, a3 = jnp.exp(l1 - m), jnp.exp(l2 - m), jnp.exp(l3 - m)
    ob = (a1 * o1 + a2 * o2 + a3 * o3) / (a1 + a2 + a3)
    ya = jnp.dot(oa_ref[...], wpa_ref[...], preferred_element_type=F32)
    yb = jnp.dot(ob.astype(BF16), wpb_ref[...], preferred_element_type=F32)
    merged = ga_ref[...].astype(F32) * ya + gb_ref[...].astype(F32) * yb
    h = x_ref[...] + jnp.dot(merged.astype(BF16), wout_ref[...], preferred_element_type=F32)
    h_ref[...] = h
    hn = _rms(h, ln2_ref[...])
    hn_ref[...] = hn.astype(BF16)

    hn_hi = hn.astype(BF16)
    hn_lo = (hn - hn_hi.astype(F32)).astype(BF16)
    logits = (jnp.dot(hn_hi, wr_ref[0], preferred_element_type=F32)
              + jnp.dot(hn_lo, wr_ref[0], preferred_element_type=F32)
              + jnp.dot(hn_hi, wr_ref[1], preferred_element_type=F32)) + br_ref[...]
    lane = lax.broadcasted_iota(jnp.int32, logits.shape, 1)
    is_grp = (lane >= N_EXPERTS) & (lane < N_EXPERTS + MOE_GROUPS)
    lg = jnp.where(is_grp, logits, NEG)
    gmax = jnp.max(lg, axis=-1, keepdims=True)
    g_lane = jnp.min(jnp.where(lg == gmax, lane, LANES), axis=-1, keepdims=True)
    p_g = 1.0 / jnp.sum(jnp.where(is_grp, jnp.exp(lg - gmax), 0.0), axis=-1, keepdims=True)
    e_lo = (g_lane - N_EXPERTS) * EXPERTS_PER_GROUP
    in_grp = (lane >= e_lo) & (lane < e_lo + EXPERTS_PER_GROUP)
    le = jnp.where(in_grp, logits, NEG)
    v1 = jnp.max(le, axis=-1, keepdims=True)
    i1 = jnp.min(jnp.where(le == v1, lane, LANES), axis=-1, keepdims=True)
    le2 = jnp.where(lane == i1, NEG, le)
    v2 = jnp.max(le2, axis=-1, keepdims=True)
    i2 = jnp.min(jnp.where(le2 == v2, lane, LANES), axis=-1, keepdims=True)
    e2 = jnp.exp(v2 - v1)
    w1 = p_g / (1.0 + e2)
    w2 = p_g * e2 / (1.0 + e2)
    route = jnp.where(lane == ROUTE_I1, i1.astype(F32), jnp.where(lane == ROUTE_I2, i2.astype(F32), 0.0))
    route_ref[...] = route + jnp.where(lane == ROUTE_W1, w1, 0.0) + jnp.where(lane == ROUTE_W2, w2, 0.0)
    picks = (lane == i1).astype(F32) + (lane == i2).astype(F32)
    cnt_ref[...] = jnp.broadcast_to(jnp.sum(picks, axis=0, keepdims=True), cnt_ref.shape)


def _post_attn(oa, obs, lses, gates_src, ga_blk, x2d, wpa, wpb, wout, ln2, wr, br, tm, tiles_per_seq, extra):
    m, d = x2d.shape
    tps = tiles_per_seq
    own = m // tm
    n_tiles = own + (extra is not None)
    mine = lambda i: jnp.minimum(i, own - 1)

    def tile(w, col=0):
        return pl.BlockSpec((tm, w), lambda i: (mine(i), col))

    def out_tile(rows, w):
        return pl.BlockSpec((rows, w), lambda i: (i, 0))

    def full(a):
        nd = a.ndim
        return pl.BlockSpec(a.shape, lambda i: (0,) * nd)

    def residue(a):
        dil = a.shape[1]
        return pl.BlockSpec((None, dil, tm // dil, PB_W), lambda i: (mine(i) // tps, 0, mine(i) % tps, 0))

    weights = [wpa, wpb, wout, ln2.reshape(1, d), wr, br]
    scratch = [pltpu.VMEM((6 * PB_W // LANES, tm, LANES), F32)]
    in_specs = ([tile(QA_W)] + [residue(a) for a in obs] + [residue(a) for a in lses]
                + [tile(d, ga_blk), tile(d, ga_blk + 1), tile(d)] + [full(w) for w in weights])
    args = [oa, *obs, *lses, gates_src, gates_src, x2d, *weights]
    if extra is not None:
        in_specs = in_specs + [full(a) for a in extra]
        args = args + list(extra)
    return pl.pallas_call(
        functools.partial(_post_attn_kernel, n_extra=0 if extra is None else len(extra)),
        grid=(n_tiles,),
        in_specs=in_specs,
        out_specs=[out_tile(tm, d), out_tile(tm, d), out_tile(tm, LANES), out_tile(SUBLANES, LANES)],
        out_shape=[jax.ShapeDtypeStruct((n_tiles * tm, d), F32),
                   jax.ShapeDtypeStruct((n_tiles * tm, d), BF16),
                   jax.ShapeDtypeStruct((n_tiles * tm, LANES), F32),
                   jax.ShapeDtypeStruct((n_tiles * SUBLANES, LANES), F32)],
        scratch_shapes=scratch,
        compiler_params=_cparams(("arbitrary",)),
        name="post_attn",
    )(*args)


def _route_plan(cnt_rows, n_tiles, n_xtiles):
    g = SUBLANES
    cnt = cnt_rows.reshape(n_tiles, g, LANES)[:, 0, :N_EXPERTS].astype(jnp.int32)
    cnt8 = (cnt + g - 1) // g * g
    loff = jnp.cumsum(cnt8, axis=1) - cnt8
    boff = jnp.cumsum(cnt8, axis=0) - cnt8
    tot = jnp.sum(cnt8, axis=0)
    region = (tot + MOE_TMX - 1) // MOE_TMX * MOE_TMX
    gend = jnp.cumsum(region)
    gbase = gend - region
    cum_tiles = gend // MOE_TMX
    j = jnp.arange(n_xtiles, dtype=jnp.int32)
    loff_rows = jnp.zeros((n_tiles, g, LANES), F32).at[:, :, :N_EXPERTS].set(loff[:, None, :].astype(F32))
    k8 = g * jnp.arange(MOE_SLOTS // g, dtype=jnp.int32)
    run_end = loff + cnt8
    e_of_k = jnp.sum(k8[None, :, None] >= run_end[:, None, :], axis=2)
    shift = gbase[None, :] + boff - loff
    picked = jnp.sum(jnp.where(e_of_k[:, :, None] == jnp.arange(N_EXPERTS)[None, None, :], shift[:, None, :], 0), axis=2)
    return dict(
        chunk_dst=(picked + k8[None, :]).reshape(-1).astype(jnp.int32),
        ntot=jnp.sum(cnt8 // g, axis=1), zst=gbase + tot, znch=(region - tot) // g,
        ztot=jnp.sum((region - tot) // g).reshape(1),
        xtile_expert=jnp.minimum(jnp.sum(j[:, None] >= cum_tiles[None, :], axis=1), N_EXPERTS - 1).astype(jnp.int32),
        n_used=cum_tiles[-1:].astype(jnp.int32),
        loff_rows=loff_rows.reshape(n_tiles * g, LANES))


def _local_slots(route, loff_row):
    tm = route.shape[0]
    lane = lax.broadcasted_iota(jnp.int32, (tm, LANES), 1)
    e1 = lane == route[:, ROUTE_I1:ROUTE_I1 + 1].astype(jnp.int32)
    e2 = lane == route[:, ROUTE_I2:ROUTE_I2 + 1].astype(jnp.int32)
    earlier = (lax.broadcasted_iota(jnp.int32, (tm, tm), 1) < lax.broadcasted_iota(jnp.int32, (tm, tm), 0))
    earlier = earlier.astype(BF16)
    c1 = jnp.dot(earlier, e1.astype(BF16), preferred_element_type=F32)
    c2 = jnp.dot(earlier, e2.astype(BF16), preferred_element_type=F32)
    cnt1 = jnp.sum(e1.astype(F32), axis=0, keepdims=True)
    pos1 = jnp.sum(jnp.where(e1, c1 + loff_row, 0.0), axis=1, keepdims=True)
    pos2 = jnp.sum(jnp.where(e2, c2 + cnt1 + loff_row, 0.0), axis=1, keepdims=True)
    slot = lax.broadcasted_iota(jnp.int32, (tm, MOE_SLOTS), 1)
    return slot == pos1.astype(jnp.int32), slot == pos2.astype(jnp.int32)


def _pack_bf16_pairs(x):
    c = x.shape[1] // 2
    bits = lambda v: lax.bitcast_convert_type(v.astype(BF16).astype(F32), jnp.uint32)
    return bits(x[:, :c]) | (bits(x[:, c:]) >> 16)


def _unpack_bf16_pairs(w):
    hi = lax.bitcast_convert_type(w & jnp.uint32(0xFFFF0000), F32).astype(BF16)
    lo = lax.bitcast_convert_type(w << 16, F32).astype(BF16)
    return jnp.concatenate([hi, lo], axis=1)


def _split3(w):
    hi = w.astype(BF16).astype(F32)
    mid = (w - hi).astype(BF16).astype(F32)
    return hi, mid, (w - hi - mid).astype(BF16).astype(F32)


def _moe_scatter_kernel(cdst_s, ntot_s, zst_s, znch_s, ztot_s, nused_s,
                        hn_ref, route_ref, loffv_ref, xs_hbm, buf_ref, zero_ref, sem, zsem, tsem):
    b = pl.program_id(0)
    nb = pl.num_programs(0)
    slot = b % 2
    g = SUBLANES
    d = hn_ref.shape[1]
    n_xtiles = xs_hbm.shape[0] // MOE_TMX

    def run_copy(s, src_row, dst_row):
        return pltpu.make_async_copy(buf_ref.at[s, pl.ds(src_row, g)], xs_hbm.at[pl.ds(dst_row, g)], sem.at[s])

    def zero_copy(dst_row):
        return pltpu.make_async_copy(zero_ref.at[pl.ds(0, g)], xs_hbm.at[pl.ds(dst_row, g)], zsem)

    def zero_tile_copy(j):
        return pltpu.make_async_copy(zero_ref, xs_hbm.at[pl.ds(pl.multiple_of(j * MOE_TMX, MOE_TMX), MOE_TMX)], tsem)

    def wait_tile(tile, s):
        lax.fori_loop(0, ntot_s[tile], lambda _, c: (run_copy(s, 0, 0).wait(), c)[1], 0)

    @pl.when(b == 0)
    def _():
        zero_ref[...] = jnp.zeros_like(zero_ref)

        def per_expert(e, c):
            def per_chunk(k, c2):
                zero_copy(pl.multiple_of(zst_s[e] + k * g, g)).start()
                return c2
            return lax.fori_loop(0, znch_s[e], per_chunk, c)
        lax.fori_loop(0, N_EXPERTS, per_expert, 0)
        lax.fori_loop(nused_s[0], n_xtiles, lambda j, c: (zero_tile_copy(j).start(), c)[1], 0)

    @pl.when(b >= 2)
    def _():
        wait_tile(b - 2, slot)

    route = route_ref[...]
    p1, p2 = _local_slots(route, loffv_ref[0:1, :])
    tn = (((0,), (0,)), ((), ()))
    picks = (p1 | p2).astype(BF16)
    buf_ref[slot, :, 0:d // 2] = _pack_bf16_pairs(lax.dot_general(picks, hn_ref[...], tn, preferred_element_type=F32))
    lane = lax.broadcasted_iota(jnp.int32, route.shape, 1)
    meta = jnp.zeros((MOE_SLOTS, LANES), F32)
    for p, col in ((p1, ROUTE_W1), (p2, ROUTE_W2)):
        w = route[:, col:col + 1]
        parts = _split3(w)
        wm = sum(jnp.where(lane == k, part, 0.0) for k, part in enumerate(parts))
        meta = meta + lax.dot_general(p.astype(BF16), wm.astype(BF16), tn, preferred_element_type=F32)
    buf_ref[slot, :, d // 2:d // 2 + LANES] = lax.bitcast_convert_type(meta, jnp.uint32)

    def per_chunk(k, c):
        run_copy(slot, pl.multiple_of(k * g, g), pl.multiple_of(cdst_s[b * (MOE_SLOTS // g) + k], g)).start()
        return c
    lax.fori_loop(0, ntot_s[b], per_chunk, 0)

    @pl.when(b == nb - 1)
    def _():
        wait_tile(b, slot)

        @pl.when(nb >= 2)
        def _():
            wait_tile(b - 1, 1 - slot)
        lax.fori_loop(0, ztot_s[0], lambda _, c: (zero_copy(0).wait(), c)[1], 0)
        lax.fori_loop(nused_s[0], n_xtiles, lambda j, c: (zero_tile_copy(j).wait(), c)[1], 0)


def _moe_scatter(plan, hn_all, route_all, n_tiles, n_xtiles):
    d = hn_all.shape[1]
    tm = MOE_TM
    grid_spec = pltpu.PrefetchScalarGridSpec(
        num_scalar_prefetch=6,
        grid=(n_tiles,),
        in_specs=[pl.BlockSpec((tm, d), lambda i, *_: (i, 0)),
                  pl.BlockSpec((tm, LANES), lambda i, *_: (i, 0)),
                  pl.BlockSpec((SUBLANES, LANES), lambda i, *_: (i, 0))],
        out_specs=pl.BlockSpec(memory_space=pl.ANY),
        scratch_shapes=[pltpu.VMEM((2, MOE_SLOTS, d // 2 + LANES), jnp.uint32),
                        pltpu.VMEM((MOE_TMX, d // 2 + LANES), jnp.uint32),
                        pltpu.SemaphoreType.DMA((2,)),
                        pltpu.SemaphoreType.DMA(()),
                        pltpu.SemaphoreType.DMA(())])
    return pl.pallas_call(
        _moe_scatter_kernel,
        grid_spec=grid_spec,
        out_shape=jax.ShapeDtypeStruct((n_xtiles * MOE_TMX, d // 2 + LANES), jnp.uint32),
        compiler_params=_cparams(("arbitrary",)),
        name="moe_scatter",
    )(plan["chunk_dst"], plan["ntot"], plan["zst"], plan["znch"], plan["ztot"], plan["n_used"],
      hn_all, route_all, plan["loff_rows"])


def _moe_experts_kernel(xe_s, nused_s, x_ref, w1_ref, w3_ref, w2_ref, y_ref, w1b_ref, w3b_ref, w2b_ref):
    j = pl.program_id(0)
    d = w1_ref.shape[0]

    @pl.when(j < nused_s[0])
    def _():
        @pl.when((j == 0) | (xe_s[j] != xe_s[jnp.maximum(j - 1, 0)]))
        def _():
            w1b_ref[...] = w1_ref[...].astype(BF16)
            w3b_ref[...] = w3_ref[...].astype(BF16)
            w2b_ref[...] = w2_ref[...].astype(BF16)

        x = _unpack_bf16_pairs(x_ref[:, 0:d // 2])
        gate = jnp.sum(lax.bitcast_convert_type(x_ref[:, d // 2:d // 2 + LANES], F32), axis=1, keepdims=True)
        a = jnp.dot(x, w1b_ref[...], preferred_element_type=F32)
        b = jnp.dot(x, w3b_ref[...], preferred_element_type=F32)
        hh = (a * _sigmoid(a)) * b * gate
        y_ref[...] = _pack_bf16_pairs(jnp.dot(hh.astype(BF16), w2b_ref[...], preferred_element_type=F32))

    @pl.when(j >= nused_s[0])
    def _():
        y_ref[...] = jnp.zeros_like(y_ref)


def _moe_experts(plan, xs, w1, w3, w2, n_xtiles):
    ne, d, f = w1.shape
    last = lambda j, xe, nu: jnp.maximum(jnp.minimum(j, nu[0] - 1), 0)
    grid_spec = pltpu.PrefetchScalarGridSpec(
        num_scalar_prefetch=2,
        grid=(n_xtiles,),
        in_specs=[pl.BlockSpec((MOE_TMX, d // 2 + LANES), lambda j, xe, nu: (last(j, xe, nu), 0)),
                  pl.BlockSpec((None, d, f), lambda j, xe, nu: (xe[j], 0, 0)),
                  pl.BlockSpec((None, d, f), lambda j, xe, nu: (xe[j], 0, 0)),
                  pl.BlockSpec((None, f, d), lambda j, xe, nu: (xe[j], 0, 0))],
        out_specs=pl.BlockSpec((MOE_TMX, d // 2), lambda j, xe, nu: (j, 0)),
        scratch_shapes=[pltpu.VMEM((d, f), BF16), pltpu.VMEM((d, f), BF16), pltpu.VMEM((f, d), BF16)])
    return pl.pallas_call(
        _moe_experts_kernel,
        grid_spec=grid_spec,
        out_shape=jax.ShapeDtypeStruct((n_xtiles * MOE_TMX, d // 2), jnp.uint32),
        compiler_params=_cparams(("arbitrary",)),
        name="moe_experts",
    )(plan["xtile_expert"], plan["n_used"], xs, w1, w3, w2)


def _moe_combine_kernel(cdst_s, ntot_s, ys_hbm, route_ref, loffv_ref, h_ref, lnf_ref, y_ref,
                        buf_ref, sem, *, tile0, final_norm):
    b = pl.program_id(0)
    nb = pl.num_programs(0)
    slot = b % 2
    g = SUBLANES
    tile = b + tile0

    def run_copy(s, src_row, dst_row):
        return pltpu.make_async_copy(ys_hbm.at[pl.ds(src_row, g)], buf_ref.at[s, pl.ds(dst_row, g)], sem.at[s])

    def fetch(t, s):
        def per_chunk(k, c):
            run_copy(s, pl.multiple_of(cdst_s[t * (MOE_SLOTS // g) + k], g), pl.multiple_of(k * g, g)).start()
            return c
        lax.fori_loop(0, ntot_s[t], per_chunk, 0)

    @pl.when(b == 0)
    def _():
        fetch(tile, slot)

    @pl.when(b + 1 < nb)
    def _():
        fetch(tile + 1, 1 - slot)

    lax.fori_loop(0, ntot_s[tile], lambda _, c: (run_copy(slot, 0, 0).wait(), c)[1], 0)

    def clear(k, c):
        buf_ref[slot, pl.ds(pl.multiple_of(k * g, g), g), :] = jnp.zeros((g, buf_ref.shape[2]), jnp.uint32)
        return c
    lax.fori_loop(ntot_s[tile], MOE_SLOTS // g, clear, 0)

    p1, p2 = _local_slots(route_ref[...], loffv_ref[0:1, :])
    picks = (p1 | p2).astype(BF16)
    y = h_ref[...] + jnp.dot(picks, _unpack_bf16_pairs(buf_ref[slot]), preferred_element_type=F32)
    if final_norm:
        y = _rms(y, lnf_ref[...])
    y_ref[...] = y


def _moe_combine(plan, ys, route_all, h_all, lnf, tile0, n_tiles, final_norm):
    d = h_all.shape[1]
    tm = MOE_TM
    grid_spec = pltpu.PrefetchScalarGridSpec(
        num_scalar_prefetch=2,
        grid=(n_tiles,),
        in_specs=[pl.BlockSpec(memory_space=pl.ANY),
                  pl.BlockSpec((tm, LANES), lambda i, *_: (i + tile0, 0)),
                  pl.BlockSpec((SUBLANES, LANES), lambda i, *_: (i + tile0, 0)),
                  pl.BlockSpec((tm, d), lambda i, *_: (i + tile0, 0)),
                  pl.BlockSpec((1, d), lambda i, *_: (0, 0))],
        out_specs=pl.BlockSpec((tm, d), lambda i, *_: (i, 0)),
        scratch_shapes=[pltpu.VMEM((2, MOE_SLOTS, d // 2), jnp.uint32), pltpu.SemaphoreType.DMA((2,))])
    return pl.pallas_call(
        functools.partial(_moe_combine_kernel, tile0=tile0, final_norm=final_norm),
        grid_spec=grid_spec,
        out_shape=jax.ShapeDtypeStruct((n_tiles * tm, d), F32),
        compiler_params=_cparams(("arbitrary",)),
        name="moe_combine",
    )(plan["chunk_dst"], plan["ntot"], ys, route_all, plan["loff_rows"], h_all, lnf.reshape(1, d))


def _prompt_states(kvt, n, t):
    hd = HEAD_DIM
    tail = kvt.shape[2]

    def state(k_lo, v_lo, rows, heads, win):
        w = min(win, t)
        kv = jnp.stack([kvt[:, k_lo:k_lo + rows, tail - w:], kvt[:, v_lo:v_lo + rows, tail - w:]], axis=1)
        return jnp.transpose(kv.reshape(n, 2, heads, hd, w), (0, 4, 1, 2, 3))

    out = [state(0, KA_W, KA_W, A_KV_HEADS, A_WINDOW)]
    for p, (win, _) in enumerate(B_PATTERNS):
        out.append(state(2 * KA_W + p * PB_W, 2 * KA_W + QB_W + p * PB_W, PB_W, B_HEADS_PER_PATTERN, win))
    return out


def _layer(xp, xs, caches, rel_bias, ln1, w_in, sinks, w_pa, w_pb, w_out, ln2, w_rg, b_rg, w_re, b_re,
           w1, w3, w2, lnf, final_norm):
    n, t, d = xp.shape
    ns = xs.shape[0]
    assert xs.shape[1] == 1 and OFF_GA + 2 * d == w_in.shape[1] and P_QA == 2 * d
    w_bf = w_in.astype(BF16)
    w_kvt = jnp.concatenate([w_in[:, OFF_KA:OFF_QB], w_in[:, OFF_KB:OFF_GA]], axis=1).T.astype(BF16)
    wpa, wpb, wout = w_pa.astype(BF16), w_pb.astype(BF16), w_out.astype(BF16)
    wr = jnp.zeros((d, LANES), F32).at[:, :N_EXPERTS].set(w_re).at[:, N_EXPERTS:N_EXPERTS + MOE_GROUPS].set(w_rg)
    br = jnp.zeros((1, LANES), F32).at[0, :N_EXPERTS].set(b_re).at[0, N_EXPERTS:N_EXPERTS + MOE_GROUPS].set(b_rg)
    wr_hi = wr.astype(BF16)
    wr = jnp.stack([wr_hi, (wr - wr_hi.astype(F32)).astype(BF16)])

    tm = MOE_TM
    xp2 = xp.reshape(n * t, d)
    xs2 = xs.reshape(ns, d)
    proj_s = _in_proj_sample(xs2, ln1, w_bf)
    if ns == 2 * (n * t // MIX_TM):
        (act, qkv2, qkv3, kvt), sampled = _in_proj_mix(xp2, ln1, w_bf, w_kvt, n, t, MIX_TM, proj_s, caches,
                                                       rel_bias, sinks)
    else:
        act, qkv2, qkv3, kvt = _in_proj_prompt(xp2, ln1, w_bf, w_kvt, n, t, IN_PROJ_TM)
        sampled = _sample_mix(proj_s, caches, rel_bias, sinks)
    oa_s, ob_s, lse_s, st_s = sampled
    act4 = act.reshape(n, 1, t, ACT_W)
    bias_a = _band_bias(rel_bias[:, :A_HEADS], A_WINDOW - 1, 1)
    (oa,) = _band_attn(act4, bias_a, sinks, q_off=P_QA, k_off=P_KA, v_off=P_VA,
                       kv_heads=A_KV_HEADS, grp=A_GROUP, want_lse=False)
    obs, lses = [], []
    for p, (win, dil) in enumerate(B_PATTERNS):
        lo = A_HEADS + p * B_HEADS_PER_PATTERN
        bias_p = _band_bias(rel_bias[:, lo:lo + B_HEADS_PER_PATTERN], win // dil, dil)
        src, base = ((act4, P_B), (qkv2, 0), (qkv3, 0))[p]
        o, lse = _band_attn(src, bias_p, None, q_off=base, k_off=base + PB_W, v_off=base + 2 * PB_W,
                            kv_heads=B_HEADS_PER_PATTERN, grp=1, want_lse=True)
        obs.append(o)
        lses.append(lse)
    assert ns <= MOE_TM
    p_tiles = n * t // MOE_TM
    n_tiles = p_tiles + 1
    m_all = n_tiles * MOE_TM
    st_p = _prompt_states(kvt, n, t)

    rows = lambda a: jnp.pad(a, ((0, MOE_TM - ns), (0, 0)))
    obs_s = [rows(ob_s[:, p].astype(BF16)).reshape(1, 1, MOE_TM, PB_W) for p in range(N_PAT)]
    lses_s = [rows(jnp.repeat(lse_s[:, p], HEAD_DIM, axis=-1)).reshape(1, 1, MOE_TM, PB_W) for p in range(N_PAT)]
    gates_s = rows(proj_s[:, OFF_GA:].astype(BF16))
    routed_s = _post_attn(rows(oa_s.astype(BF16)), obs_s, lses_s, gates_s, 0, rows(xs2),
                          wpa, wpb, wout, ln2, wr, br, MOE_TM, 1, None)
    h_all, hn_all, route_all, cnt_all = _post_attn(oa.reshape(n * t, QA_W), obs, lses, act, 0, xp2,
                                                   wpa, wpb, wout, ln2, wr, br, tm, t // tm, routed_s)

    max_rows = 2 * m_all + n_tiles * N_EXPERTS * (SUBLANES - 1) + N_EXPERTS * (MOE_TMX - SUBLANES)
    n_xtiles = -(-max_rows // MOE_TMX)
    plan = _route_plan(cnt_all, n_tiles, n_xtiles)
    xs_sorted = _moe_scatter(plan, hn_all, route_all, n_tiles, n_xtiles)
    ys_sorted = _moe_experts(plan, xs_sorted, w1, w3, w2, n_xtiles)
    yp = _moe_combine(plan, ys_sorted, route_all, h_all, lnf, 0, p_tiles, final_norm).reshape(n, t, d)
    ys = _moe_combine(plan, ys_sorted, route_all, h_all, lnf, p_tiles, 1, final_norm)[:ns].reshape(ns, 1, d)
    return yp, ys, st_p, st_s


def kernel(x_prompt, x_sample, cache_a_kv, cache_b1_kv, cache_b2_kv, cache_b3_kv, rel_bias, ln1_g, w_in, sinks,
           w_pa, w_pb, w_out, ln2_g, w_rg, b_rg, w_re, b_re, w1, w3, w2, lnf_g):
    depth = w_in.shape[0]
    assert depth >= 1
    xp, xs = x_prompt, x_sample
    new_p = [[] for _ in range(4)]
    new_s = [[] for _ in range(4)]
    for l in range(depth):
        caches = (cache_a_kv[l], cache_b1_kv[l], cache_b2_kv[l], cache_b3_kv[l])
        xp, xs, st_p, st_s = _layer(xp, xs, caches, rel_bias, ln1_g[l], w_in[l], sinks[l], w_pa[l], w_pb[l],
                                    w_out[l], ln2_g[l], w_rg[l], b_rg[l], w_re[l], b_re[l], w1[l], w3[l], w2[l],
                                    lnf_g, l == depth - 1)
        for i in range(4):
            new_p[i].append(st_p[i])
            new_s[i].append(st_s[i])
    a_p, b1_p, b2_p, b3_p = [jnp.stack(v) for v in new_p]
    a_s, b1_s, b2_s, b3_s = [jnp.stack(v) for v in new_s]
    return (xp, xs, a_p, a_s, b1_p, b1_s, b2_p, b2_s, b3_p, b3_s)
```

```python
import functools
import math

import numpy as np
import jax
import jax.numpy as jnp
from jax import lax
from jax.experimental import pallas as pl
from jax.experimental.pallas import tpu as pltpu

F32 = jnp.float32
BF16 = jnp.bfloat16

HEAD_DIM = 64
A_HEADS = 8
A_KV_HEADS = 2
A_GROUP = A_HEADS // A_KV_HEADS
A_WINDOW = 128
B_PATTERNS = ((128, 1), (512, 4), (2048, 16))
N_PAT = len(B_PATTERNS)
B_HEADS_PER_PATTERN = 4
B_HEADS = B_HEADS_PER_PATTERN * N_PAT
BLOCK = 128
ATTN_CHAINS = 16
NUM_BUCKETS = 32
MAX_DISTANCE = 2048
MOE_GROUPS = 4
EXPERTS_PER_GROUP = 8
N_EXPERTS = MOE_GROUPS * EXPERTS_PER_GROUP
EPS = 1e-6
NEG = -1e30
LANES = 128
SUBLANES = 8
ROUTE_I1, ROUTE_I2, ROUTE_W1, ROUTE_W2 = 0, 1, 2, 3
IN_PROJ_TM = 512
MIX_TM = 256
MOE_TM = 256
MOE_SLOTS = 2 * MOE_TM + N_EXPERTS * SUBLANES
MOE_TMX = 256
Q_SCALE = HEAD_DIM ** -0.5

QA_W = A_HEADS * HEAD_DIM
KA_W = A_KV_HEADS * HEAD_DIM
QB_W = B_HEADS * HEAD_DIM
PB_W = B_HEADS_PER_PATTERN * HEAD_DIM
QKV_W = 3 * PB_W
OFF_KA = QA_W
OFF_QB = OFF_KA + 2 * KA_W
OFF_KB = OFF_QB + QB_W
OFF_VB = OFF_KB + QB_W
OFF_GA = OFF_VB + QB_W
P_QA = 2048
P_KA = P_QA + QA_W
P_VA = P_KA + KA_W
P_B = P_VA + KA_W
ACT_W = P_B + QKV_W
KVT_ROWS = 2 * KA_W + 2 * QB_W

VMEM_LIMIT = 56 * 1024 * 1024


def _cparams(sem):
    return pltpu.CompilerParams(dimension_semantics=sem, vmem_limit_bytes=VMEM_LIMIT)


def _bucket_np(dist):
    dist = np.asarray(dist, np.int64)
    max_exact = NUM_BUCKETS // 2
    df = np.maximum(dist, max_exact).astype(np.float64)
    large = max_exact + (np.log(df / max_exact) / math.log(MAX_DISTANCE / max_exact)
                         * (NUM_BUCKETS - max_exact)).astype(np.int64)
    return np.where(dist < max_exact, dist, np.minimum(large, NUM_BUCKETS - 1))


def _table_rows(table_cols, dist, valid):
    onehot = (_bucket_np(dist)[:, None] == np.arange(NUM_BUCKETS)[None, :]).astype(np.float32)
    rows = jnp.einsum("ck,kh->hc", jnp.asarray(onehot), table_cols.astype(F32), precision=lax.Precision.HIGHEST)
    return jnp.where(jnp.asarray(valid)[None, :], rows, NEG)


def _band_bias(table_cols, max_dist, dilation):
    period = 3 * BLOCK
    m = np.arange(period)
    k = np.where(m < 2 * BLOCK, m, m - period)
    dist = BLOCK - k
    valid = (dist >= 0) & (dist <= max_dist) & (m != 2 * BLOCK)
    v = _table_rows(table_cols, np.clip(dist, 0, None) * dilation, valid)
    heads = v.shape[0]
    flat = jnp.tile(v, (1, BLOCK))[:, :BLOCK * (period - 1)]
    return flat.reshape(heads, BLOCK, period - 1)[:, :, :2 * BLOCK]


def _decode_bias(table_cols, width, dilation, first_valid):
    c = np.arange(width)
    valid = (c % dilation == 0) & (c >= first_valid)
    rows = _table_rows(table_cols, width - c, valid)
    self_bias = _table_rows(table_cols, np.zeros((1,), np.int64), np.ones((1,), bool))
    return rows[:, None, :], self_bias[:, None, :]


def _rms(x, g):
    return (x * lax.rsqrt(jnp.mean(x * x, axis=-1, keepdims=True) + EPS)) * g


def _sigmoid(x):
    return 1.0 / (1.0 + jnp.exp(-x))


YS_CHUNKS = QKV_W // LANES


def _in_proj_work(x_ref, g_ref, w_ref, act_ref, qkv2_ref, qkv3_ref, ys_ref, phase):
    tm = x_ref.shape[0]
    cache = {}

    def xb():
        if "xb" not in cache:
            cache["xb"] = _rms(x_ref[...], g_ref[...]).astype(BF16)
        return cache["xb"]

    def proj(lo, hi):
        return jnp.dot(xb(), w_ref[:, lo:hi], preferred_element_type=F32)

    def gates(c):
        act_ref[:, c:c + 512] = _sigmoid(proj(OFF_GA + c, OFF_GA + c + 512)).astype(BF16)

    def mixer_a_q():
        act_ref[:, P_QA:P_KA] = (proj(0, OFF_KA) * Q_SCALE).astype(BF16)

    def mixer_a_kv():
        act_ref[:, P_KA:P_B] = proj(OFF_KA, OFF_QB).astype(BF16)

    def pattern_part(p, j):
        lo = (OFF_QB, OFF_KB, OFF_VB)[j] + p * PB_W
        part = proj(lo, lo + PB_W)
        if j == 0:
            part = part * Q_SCALE
        if p == 0:
            act_ref[:, P_B + j * PB_W:P_B + (j + 1) * PB_W] = part.astype(BF16)
        else:
            for c in range(PB_W // LANES):
                ys_ref[(p - 1) * YS_CHUNKS + j * (PB_W // LANES) + c] = part[:, c * LANES:(c + 1) * LANES]

    def regroup(p, out_ref):
        dil = B_PATTERNS[p][1]
        for c in range(YS_CHUNKS):
            for r in range(dil):
                out_ref[r, :, c * LANES:(c + 1) * LANES] = (
                    ys_ref[(p - 1) * YS_CHUNKS + c, pl.ds(r, tm // dil, stride=dil), :].astype(BF16))

    work = []
    if phase in (None, 0):
        work += [functools.partial(gates, c) for c in range(0, P_QA, 512)] + [mixer_a_q, mixer_a_kv]
    if phase in (None, 1):
        for p in range(N_PAT):
            work += [functools.partial(pattern_part, p, j) for j in range(3)]
            if p > 0:
                work.append(functools.partial(regroup, p, (qkv2_ref, qkv3_ref)[p - 1]))
    return work


def _transpose_cast_kernel(w_ref, o_ref):
    o_ref[...] = w_ref[...].T.astype(o_ref.dtype)


def _kv_weights_transposed(w_in):
    d = w_in.shape[0]
    blk = 2 * KA_W
    assert OFF_KA % blk == 0 and OFF_KB % blk == 0 and KVT_ROWS % blk == 0
    first, rest = OFF_KA // blk, OFF_KB // blk - 1
    return pl.pallas_call(
        _transpose_cast_kernel,
        grid=(KVT_ROWS // blk,),
        in_specs=[pl.BlockSpec((d, blk), lambda i: (0, jnp.where(i == 0, first, rest + i)))],
        out_specs=pl.BlockSpec((blk, d), lambda i: (i, 0)),
        out_shape=jax.ShapeDtypeStruct((KVT_ROWS, d), BF16),
        compiler_params=_cparams(("arbitrary",)),
        name="kv_weights_t",
    )(w_in)


def _kv_tail(x_ref, g_ref, wkvt_ref, kvt_ref):
    xb = _rms(x_ref[...], g_ref[...]).astype(BF16)
    kvt_ref[...] = lax.dot_general(wkvt_ref[...], xb, (((1,), (1,)), ((), ())), preferred_element_type=F32)


def _in_proj_kernel(x_ref, g_ref, w_ref, wkvt_ref, act_ref, qkv2_ref, qkv3_ref, kvt_ref, ys_ref, *,
                    tiles_per_seq, tail_tiles):
    for piece in _in_proj_work(x_ref, g_ref, w_ref, act_ref, qkv2_ref, qkv3_ref, ys_ref, None):
        piece()

    @pl.when(pl.program_id(0) % tiles_per_seq >= tiles_per_seq - tail_tiles)
    def _():
        _kv_tail(x_ref, g_ref, wkvt_ref, kvt_ref)


def _in_proj_prompt(x2d, ln_g, w_bf, w_kvt, n, t, tm):
    m, d = x2d.shape
    tps = t // tm
    tail = min(max(w for w, _ in B_PATTERNS), t)
    assert t % tm == 0 and tail % tm == 0 and all(tm % (16 * dl) == 0 for _, dl in B_PATTERNS)
    tail_tiles = tail // tm
    d2, d3 = B_PATTERNS[1][1], B_PATTERNS[2][1]
    return pl.pallas_call(
        functools.partial(_in_proj_kernel, tiles_per_seq=tps, tail_tiles=tail_tiles),
        grid=(m // tm,),
        in_specs=[pl.BlockSpec((tm, d), lambda i: (i, 0)),
                  pl.BlockSpec((1, d), lambda i: (0, 0)),
                  pl.BlockSpec(w_bf.shape, lambda i: (0, 0), pipeline_mode=pl.Buffered(1)),
                  pl.BlockSpec(w_kvt.shape, lambda i: (0, 0), pipeline_mode=pl.Buffered(1))],
        out_specs=[pl.BlockSpec((tm, ACT_W), lambda i: (i, 0)),
                   pl.BlockSpec((None, d2, tm // d2, QKV_W), lambda i: (i // tps, 0, i % tps, 0)),
                   pl.BlockSpec((None, d3, tm // d3, QKV_W), lambda i: (i // tps, 0, i % tps, 0)),
                   pl.BlockSpec((None, KVT_ROWS, tm),
                                lambda i: (i // tps, 0, jnp.maximum(i % tps - (tps - tail_tiles), 0)))],
        out_shape=[jax.ShapeDtypeStruct((m, ACT_W), BF16),
                   jax.ShapeDtypeStruct((n, d2, t // d2, QKV_W), BF16),
                   jax.ShapeDtypeStruct((n, d3, t // d3, QKV_W), BF16),
                   jax.ShapeDtypeStruct((n, KVT_ROWS, tail), F32)],
        scratch_shapes=[pltpu.VMEM(((N_PAT - 1) * YS_CHUNKS, tm, LANES), F32)],
        compiler_params=_cparams(("arbitrary",)),
        name="in_proj",
    )(x2d, ln_g.reshape(1, d), w_bf, w_kvt)


def _in_proj_sample_kernel(x_ref, g_ref, w_ref, y_ref):
    xb = _rms(x_ref[...], g_ref[...]).astype(BF16)
    y_ref[:, :OFF_GA] = jnp.dot(xb, w_ref[:, :OFF_GA], preferred_element_type=F32)
    for lo, hi in ((0, OFF_KA), (OFF_QB, OFF_KB)):
        y_ref[:, lo:hi] = y_ref[:, lo:hi] * Q_SCALE
    for lo in range(OFF_GA, w_ref.shape[1], 1024):
        y_ref[:, lo:lo + 1024] = _sigmoid(jnp.dot(xb, w_ref[:, lo:lo + 1024], preferred_element_type=F32))


def _in_proj_sample(x2d, ln_g, w_bf):
    m, d = x2d.shape
    return pl.pallas_call(
        _in_proj_sample_kernel,
        grid=(1,),
        in_specs=[pl.BlockSpec((m, d), lambda i: (0, 0)),
                  pl.BlockSpec((1, d), lambda i: (0, 0)),
                  pl.BlockSpec(w_bf.shape, lambda i: (0, 0))],
        out_specs=pl.BlockSpec((m, w_bf.shape[1]), lambda i: (0, 0)),
        out_shape=jax.ShapeDtypeStruct((m, w_bf.shape[1]), F32),
        compiler_params=_cparams(("arbitrary",)),
        name="in_proj_sample",
    )(x2d, ln_g.reshape(1, d), w_bf)


def _band_attn_kernel(*refs, kv_heads, grp, has_sink, want_lse):
    if has_sink:
        sink_ref, refs = refs[0], refs[1:]
    q_ref, kp_ref, kc_ref, vp_ref, vc_ref, bias_ref, o_ref = refs[:7]
    lse_ref = refs[7] if want_lse else None
    step = pl.program_id(2)
    hd = HEAD_DIM
    nt = (((1,), (1,)), ((), ()))
    chains = [(sub, kv * grp + g, slice(kv * hd, (kv + 1) * hd))
              for sub in range(q_ref.shape[0] // BLOCK) for kv in range(kv_heads) for g in range(grp)]
    scores = []
    for sub, h, ks in chains:
        rows = slice(sub * BLOCK, (sub + 1) * BLOCK)
        q = q_ref[rows, h * hd:(h + 1) * hd]
        kp = kp_ref[:, ks] if sub == 0 else kc_ref[(sub - 1) * BLOCK:sub * BLOCK, ks]
        sp = lax.dot_general(q, kp, nt, preferred_element_type=F32) + bias_ref[h, :, 0:BLOCK]
        sc = lax.dot_general(q, kc_ref[rows, ks], nt, preferred_element_type=F32) + bias_ref[h, :, BLOCK:2 * BLOCK]
        if sub == 0:
            sp = jnp.where(step > 0, sp, NEG)
        scores.append((sp, sc))
    maxes = []
    for (sub, h, ks), (sp, sc) in zip(chains, scores):
        m = jnp.maximum(jnp.max(sp, axis=-1, keepdims=True), jnp.max(sc, axis=-1, keepdims=True))
        maxes.append(jnp.maximum(m, sink_ref[h]) if has_sink else m)
    probs = []
    for (sub, h, ks), (sp, sc), m in zip(chains, scores, maxes):
        pp, pc = jnp.exp(sp - m), jnp.exp(sc - m)
        den = jnp.sum(pp, axis=-1, keepdims=True) + jnp.sum(pc, axis=-1, keepdims=True)
        if has_sink:
            den = den + jnp.exp(sink_ref[h] - m)
        probs.append((pp.astype(BF16), pc.astype(BF16), den))
    for (sub, h, ks), (pp, pc, den), m in zip(chains, probs, maxes):
        rows = slice(sub * BLOCK, (sub + 1) * BLOCK)
        vp = vp_ref[:, ks] if sub == 0 else vc_ref[(sub - 1) * BLOCK:sub * BLOCK, ks]
        o = (jnp.dot(pp, vp, preferred_element_type=F32)
             + jnp.dot(pc, vc_ref[rows, ks], preferred_element_type=F32))
        o_ref[rows, h * hd:(h + 1) * hd] = (o / den).astype(o_ref.dtype)
        if want_lse:
            lse_ref[rows, h * hd:(h + 1) * hd] = jnp.broadcast_to(m + jnp.log(den), (BLOCK, hd))


def _band_attn(src, bias, sink, *, q_off, k_off, v_off, kv_heads, grp, want_lse):
    n, dil, l, cols = src.shape
    sub = max(1, ATTN_CHAINS // (kv_heads * grp))
    while l % (sub * BLOCK):
        sub //= 2
    rows = sub * BLOCK
    assert sub >= 1 and l % rows == 0
    nb = l // rows
    qw = kv_heads * grp * HEAD_DIM
    kw = kv_heads * HEAD_DIM
    assert q_off % qw == 0 and k_off % kw == 0 and v_off % kw == 0
    qb, kb, vb = q_off // qw, k_off // kw, v_off // kw
    prev = lambda b: jnp.maximum(sub * b - 1, 0)
    in_specs = [
        pl.BlockSpec((None, None, rows, qw), lambda i, r, b: (i, r, b, qb)),
        pl.BlockSpec((None, None, BLOCK, kw), lambda i, r, b: (i, r, prev(b), kb)),
        pl.BlockSpec((None, None, rows, kw), lambda i, r, b: (i, r, b, kb)),
        pl.BlockSpec((None, None, BLOCK, kw), lambda i, r, b: (i, r, prev(b), vb)),
        pl.BlockSpec((None, None, rows, kw), lambda i, r, b: (i, r, b, vb)),
        pl.BlockSpec(bias.shape, lambda i, r, b: (0, 0, 0)),
    ]
    args = [src, src, src, src, src, bias]
    has_sink = sink is not None
    if has_sink:
        in_specs = [pl.BlockSpec(memory_space=pltpu.SMEM)] + in_specs
        args = [sink.astype(F32)] + args
    out_specs = [pl.BlockSpec((None, None, rows, qw), lambda i, r, b: (i, r, b, 0))]
    out_shape = [jax.ShapeDtypeStruct((n, dil, l, qw), BF16)]
    if want_lse:
        out_specs.append(pl.BlockSpec((None, None, rows, qw), lambda i, r, b: (i, r, b, 0)))
        out_shape.append(jax.ShapeDtypeStruct((n, dil, l, qw), F32))
    return pl.pallas_call(
        functools.partial(_band_attn_kernel, kv_heads=kv_heads, grp=grp, has_sink=has_sink, want_lse=want_lse),
        grid=(n, dil, nb),
        in_specs=in_specs,
        out_specs=out_specs,
        out_shape=out_shape,
        compiler_params=_cparams(("arbitrary", "arbitrary", "arbitrary")),
        name=f"band_attn_d{dil}",
    )(*args)


COL_QA = 0
COL_QB = COL_QA + A_HEADS
COL_KA = COL_QB + B_HEADS
COL_VA = COL_KA + A_KV_HEADS
COL_KB = COL_VA + A_KV_HEADS
COL_VB = COL_KB + B_HEADS
N_COLS = COL_VB + B_HEADS


def _attend_cached(kt, vt, q, k_new, v_new, bias, self_bias, sink, write):
    s = jnp.sum(kt * q, axis=1, keepdims=True) + bias
    s_new = jnp.sum(k_new * q, axis=1, keepdims=True) + self_bias
    yield
    m = jnp.maximum(jnp.max(s, axis=2, keepdims=True), s_new)
    if sink is not None:
        m = jnp.maximum(m, sink)
    yield
    p = jnp.exp(s - m)
    p_new = jnp.exp(s_new - m)
    den = jnp.sum(p, axis=2, keepdims=True) + p_new
    if sink is not None:
        den = den + jnp.exp(sink - m)
    yield
    o = (jnp.sum(vt * p, axis=2, keepdims=True) + v_new * p_new) / den
    write(o, m + jnp.log(den))
    yield


def _sample_work(cols_ref, ca_ref, c1_ref, c2_ref, c3_ref, ba_ref, sa_ref, sink_ref, b1_ref, b2_ref, b3_ref,
                 sb_ref, o_ref, lse_ref, ra_ref, r1_ref, r2_ref, r3_ref):
    cols = cols_ref[...]

    def stack(js):
        return jnp.stack([cols[:, j:j + 1] for j in js])

    o_ref[...] = jnp.zeros_like(o_ref)
    lse_ref[...] = jnp.zeros_like(lse_ref)

    def write_a(o, _):
        for h in range(A_HEADS):
            o_ref[:, COL_QA + h:COL_QA + h + 1] = o[h]

    def write_b(p):
        def write(o, lse):
            for h in range(B_HEADS_PER_PATTERN):
                j = p * B_HEADS_PER_PATTERN + h
                o_ref[:, COL_QB + j:COL_QB + j + 1] = o[h]
                lse_ref[0:1, j:j + 1] = lse[h]
        return write

    kv_of = [h // A_GROUP for h in range(A_HEADS)]
    stages = [_attend_cached(jnp.stack([ca_ref[0, kv] for kv in kv_of]), jnp.stack([ca_ref[1, kv] for kv in kv_of]),
                             stack(range(COL_QA, COL_QA + A_HEADS)), stack([COL_KA + kv for kv in kv_of]),
                             stack([COL_VA + kv for kv in kv_of]), ba_ref[...], sa_ref[...], sink_ref[...], write_a)]
    for p, (c_ref, b_ref) in enumerate(((c1_ref, b1_ref), (c2_ref, b2_ref), (c3_ref, b3_ref))):
        js = range(p * B_HEADS_PER_PATTERN, (p + 1) * B_HEADS_PER_PATTERN)
        stages.append(_attend_cached(c_ref[0], c_ref[1], stack([COL_QB + j for j in js]),
                                     stack([COL_KB + j for j in js]), stack([COL_VB + j for j in js]),
                                     b_ref[...], sb_ref[p], None, write_b(p)))

    def roll_plane(c_ref, r_ref, i, h, new_col):
        w = c_ref.shape[-1]
        x = c_ref[i, h]
        lane = lax.broadcasted_iota(jnp.int32, x.shape, 1)
        r_ref[i, h] = jnp.where(lane == w - 1, cols[:, new_col:new_col + 1], pltpu.roll(x, w - 1, 1))

    planes = []
    for ci, (c_ref, r_ref) in enumerate(((ca_ref, ra_ref), (c1_ref, r1_ref), (c2_ref, r2_ref), (c3_ref, r3_ref))):
        for i, (first_a, first_b) in enumerate(((COL_KA, COL_KB), (COL_VA, COL_VB))):
            for h in range(c_ref.shape[1]):
                new_col = first_a + h if ci == 0 else first_b + (ci - 1) * B_HEADS_PER_PATTERN + h
                planes.append((c_ref.shape[-1], functools.partial(roll_plane, c_ref, r_ref, i, h, new_col)))

    work = []
    total = sum(w for w, _ in planes)
    n_slots = 4 * len(stages)
    done, k = 0, 0
    for slot in range(n_slots):
        work.append(functools.partial(next, stages[slot // 4]))
        while k < len(planes) and done < total * (slot + 1) // n_slots:
            done += planes[k][0]
            work.append(planes[k][1])
            k += 1
    assert k == len(planes)
    return work


def _sample_mix_kernel(*refs):
    for piece in _sample_work(*refs):
        piece()


def _run_interleaved(primary, secondary):
    k = 0
    for i, piece in enumerate(primary):
        piece()
        while k < len(secondary) and k < len(secondary) * (i + 1) // len(primary):
            secondary[k]()
            k += 1
    for piece in secondary[k:]:
        piece()


def _in_proj_mix_kernel(x_ref, g_ref, w_ref, wkvt_ref, *rest, tiles_per_seq, tail_tiles, n_sample_in):
    sample_in = rest[:n_sample_in]
    act_ref, qkv2_ref, qkv3_ref, kvt_ref = rest[n_sample_in:n_sample_in + 4]
    sample_out = rest[n_sample_in + 4:-1]
    ys_ref = rest[-1]
    step = pl.program_id(0)
    tile = step // 2
    for phase in (0, 1):
        @pl.when(step % 2 == phase)
        def _():
            _run_interleaved(_sample_work(*sample_in, *sample_out),
                             _in_proj_work(x_ref, g_ref, w_ref, act_ref, qkv2_ref, qkv3_ref, ys_ref, phase))
            if phase == 1:
                @pl.when(tile % tiles_per_seq >= tiles_per_seq - tail_tiles)
                def _():
                    _kv_tail(x_ref, g_ref, wkvt_ref, kvt_ref)


def _sample_operands(proj_s, caches, rel_bias, sinks):
    n = proj_s.shape[0]
    hd = HEAD_DIM
    vecs = jnp.concatenate([proj_s[:, :OFF_KA], proj_s[:, OFF_QB:OFF_KB], proj_s[:, OFF_KA:OFF_QB],
                            proj_s[:, OFF_KB:OFF_GA]], axis=1).reshape(n, N_COLS, hd)
    cols = jnp.pad(jnp.transpose(vecs, (0, 2, 1)), ((0, 0), (0, 0), (0, LANES - N_COLS)))
    cts = [jnp.transpose(c, (0, 2, 3, 4, 1)) for c in caches]

    ba, sa = _decode_bias(rel_bias[:, :A_HEADS], A_WINDOW, 1, 1)
    sink = sinks.astype(F32).reshape(A_HEADS, 1, 1)
    bbs, sbs = [], []
    for p, (win, dil) in enumerate(B_PATTERNS):
        lo = A_HEADS + p * B_HEADS_PER_PATTERN
        assert caches[1 + p].shape[1] == win == BLOCK * dil
        b, s = _decode_bias(rel_bias[:, lo:lo + B_HEADS_PER_PATTERN], win, dil, 0)
        bbs.append(b)
        sbs.append(s)
    sb = jnp.stack(sbs)

    seq_in = [cols] + cts
    consts = [ba, sa, sink] + bbs + [sb]
    out_shape = [jax.ShapeDtypeStruct((n, hd, LANES), F32), jax.ShapeDtypeStruct((n, SUBLANES, LANES), F32)]
    out_shape += [jax.ShapeDtypeStruct(c.shape, c.dtype) for c in cts]
    return seq_in, consts, out_shape


def _per_seq_spec(a):
    nd = len(a.shape)
    return pl.BlockSpec((None,) + tuple(a.shape[1:]), lambda i: (i,) + (0,) * (nd - 1))


def _const_spec(a):
    nd = a.ndim
    return pl.BlockSpec(a.shape, lambda i: (0,) * nd)


def _sample_results(outs):
    n = outs[0].shape[0]
    o_rows = jnp.transpose(outs[0][:, :, :COL_KA], (0, 2, 1))
    oa = o_rows[:, COL_QA:COL_QB].reshape(n, QA_W)
    ob = o_rows[:, COL_QB:COL_KA].reshape(n, N_PAT, PB_W)
    lse = outs[1][:, 0, :B_HEADS].reshape(n, N_PAT, B_HEADS_PER_PATTERN)
    rolled = [jnp.transpose(r, (0, 4, 1, 2, 3)) for r in outs[2:]]
    return oa, ob, lse, rolled


def _sample_mix(proj_s, caches, rel_bias, sinks):
    seq_in, consts, out_shape = _sample_operands(proj_s, caches, rel_bias, sinks)
    outs = pl.pallas_call(
        _sample_mix_kernel,
        grid=(proj_s.shape[0],),
        in_specs=[_per_seq_spec(a) for a in seq_in] + [_const_spec(a) for a in consts],
        out_specs=[_per_seq_spec(s) for s in out_shape],
        out_shape=out_shape,
        compiler_params=_cparams(("arbitrary",)),
        name="sample_mix",
    )(*seq_in, *consts)
    return _sample_results(outs)


def _in_proj_mix(x2d, ln_g, w_bf, w_kvt, n, t, tm, proj_s, caches, rel_bias, sinks):
    m, d = x2d.shape
    tps = t // tm
    tail = min(max(w for w, _ in B_PATTERNS), t)
    assert t % tm == 0 and tail % tm == 0 and all(tm % (16 * dl) == 0 for _, dl in B_PATTERNS)
    assert proj_s.shape[0] == 2 * (m // tm)
    tail_tiles = tail // tm
    d2, d3 = B_PATTERNS[1][1], B_PATTERNS[2][1]
    seq_in, consts, sample_shape = _sample_operands(proj_s, caches, rel_bias, sinks)
    tile = lambda i: i // 2
    outs = pl.pallas_call(
        functools.partial(_in_proj_mix_kernel, tiles_per_seq=tps, tail_tiles=tail_tiles,
                          n_sample_in=len(seq_in) + len(consts)),
        grid=(2 * (m // tm),),
        in_specs=[pl.BlockSpec((tm, d), lambda i: (tile(i), 0)),
                  pl.BlockSpec((1, d), lambda i: (0, 0)),
                  pl.BlockSpec(w_bf.shape, lambda i: (0, 0), pipeline_mode=pl.Buffered(1)),
                  pl.BlockSpec(w_kvt.shape, lambda i: (0, 0), pipeline_mode=pl.Buffered(1))]
                 + [_per_seq_spec(a) for a in seq_in] + [_const_spec(a) for a in consts],
        out_specs=[pl.BlockSpec((tm, ACT_W), lambda i: (tile(i), 0)),
                   pl.BlockSpec((None, d2, tm // d2, QKV_W), lambda i: (tile(i) // tps, 0, tile(i) % tps, 0)),
                   pl.BlockSpec((None, d3, tm // d3, QKV_W), lambda i: (tile(i) // tps, 0, tile(i) % tps, 0)),
                   pl.BlockSpec((None, KVT_ROWS, tm),
                                lambda i: (tile(i) // tps, 0, jnp.maximum(tile(i) % tps - (tps - tail_tiles), 0)))]
                  + [_per_seq_spec(s) for s in sample_shape],
        out_shape=[jax.ShapeDtypeStruct((m, ACT_W), BF16),
                   jax.ShapeDtypeStruct((n, d2, t // d2, QKV_W), BF16),
                   jax.ShapeDtypeStruct((n, d3, t // d3, QKV_W), BF16),
                   jax.ShapeDtypeStruct((n, KVT_ROWS, tail), F32)] + sample_shape,
        scratch_shapes=[pltpu.VMEM(((N_PAT - 1) * YS_CHUNKS, tm, LANES), F32)],
        compiler_params=_cparams(("arbitrary",)),
        name="in_proj_mix",
    )(x2d, ln_g.reshape(1, d), w_bf, w_kvt, *seq_in, *consts)
    return outs[:4], _sample_results(outs[4:])


def _post_attn_kernel(oa_ref, o1_ref, o2_ref, o3_ref, l1_ref, l2_ref, l3_ref, ga_ref, gb_ref, x_ref,
                      wpa_ref, wpb_ref, wout_ref, ln2_ref, wr_ref, br_ref, *rest, n_extra):
    outs = rest[n_extra:n_extra + 4]
    h_ref, hn_ref, route_ref, cnt_ref = outs
    scr_ref = rest[n_extra + 4]
    if n_extra:
        own_tile = pl.program_id(0) < pl.num_programs(0) - 1

        @pl.when(jnp.logical_not(own_tile))
        def _():
            for dst, src in zip(outs, rest[:n_extra]):
                dst[...] = src[...]

        @pl.when(own_tile)
        def _():
            _post_attn_tile(oa_ref, o1_ref, o2_ref, o3_ref, l1_ref, l2_ref, l3_ref, ga_ref, gb_ref, x_ref,
                            wpa_ref, wpb_ref, wout_ref, ln2_ref, wr_ref, br_ref, *outs, scr_ref)
    else:
        _post_attn_tile(oa_ref, o1_ref, o2_ref, o3_ref, l1_ref, l2_ref, l3_ref, ga_ref, gb_ref, x_ref,
                        wpa_ref, wpb_ref, wout_ref, ln2_ref, wr_ref, br_ref, *outs, scr_ref)


def _post_attn_tile(oa_ref, o1_ref, o2_ref, o3_ref, l1_ref, l2_ref, l3_ref, ga_ref, gb_ref, x_ref,
                    wpa_ref, wpb_ref, wout_ref, ln2_ref, wr_ref, br_ref, h_ref, hn_ref, route_ref, cnt_ref, scr_ref):
    tm = x_ref.shape[0]
    chunks = PB_W // LANES

    def token_major(ref, slot):
        dil = ref.shape[0]
        if dil == 1:
            return ref[0].astype(F32)
        for c in range(chunks):
            for r in range(dil):
                scr_ref[slot * chunks + c, pl.ds(r, tm // dil, stride=dil), :] = (
                    ref[r, :, c * LANES:(c + 1) * LANES].astype(F32))
        return jnp.concatenate([scr_ref[slot * chunks + c] for c in range(chunks)], axis=1)

    o1, o2, o3 = (token_major(r, s) for s, r in enumerate((o1_ref, o2_ref, o3_ref)))
    l1, l2, l3 = (token_major(r, 3 + s) for s, r in enumerate((l1_ref, l2_ref, l3_ref)))
    m = jnp.maximum(jnp.maximum(l1, l2), l3)
    a1, a2, a3 = jnp.exp(l1 - m), jnp.exp(l2 - m), jnp.exp(l3 - m)
    ob = (a1 * o1 + a2 * o2 + a3 * o3) / (a1 + a2 + a3)
    ya = jnp.dot(oa_ref[...], wpa_ref[...], preferred_element_type=F32)
    yb = jnp.dot(ob.astype(BF16), wpb_ref[...], preferred_element_type=F32)
    merged = ga_ref[...].astype(F32) * ya + gb_ref[...].astype(F32) * yb
    h = x_ref[...] + jnp.dot(merged.astype(BF16), wout_ref[...], preferred_element_type=F32)
    h_ref[...] = h
    hn = _rms(h, ln2_ref[...])
    hn_ref[...] = hn.astype(BF16)

    hn_hi = hn.astype(BF16)
    hn_lo = (hn - hn_hi.astype(F32)).astype(BF16)
    logits = (jnp.dot(hn_hi, wr_ref[0], preferred_element_type=F32)
              + jnp.dot(hn_lo, wr_ref[0], preferred_element_type=F32)
              + jnp.dot(hn_hi, wr_ref[1], preferred_element_type=F32)) + br_ref[...]
    lane = lax.broadcasted_iota(jnp.int32, logits.shape, 1)
    is_grp = (lane >= N_EXPERTS) & (lane < N_EXPERTS + MOE_GROUPS)
    lg = jnp.where(is_grp, logits, NEG)
    gmax = jnp.max(lg, axis=-1, keepdims=True)
    g_lane = jnp.min(jnp.where(lg == gmax, lane, LANES), axis=-1, keepdims=True)
    p_g = 1.0 / jnp.sum(jnp.where(is_grp, jnp.exp(lg - gmax), 0.0), axis=-1, keepdims=True)
    e_lo = (g_lane - N_EXPERTS) * EXPERTS_PER_GROUP
    in_grp = (lane >= e_lo) & (lane < e_lo + EXPERTS_PER_GROUP)
    le = jnp.where(in_grp, logits, NEG)
    v1 = jnp.max(le, axis=-1, keepdims=True)
    i1 = jnp.min(jnp.where(le == v1, lane, LANES), axis=-1, keepdims=True)
    le2 = jnp.where(lane == i1, NEG, le)
    v2 = jnp.max(le2, axis=-1, keepdims=True)
    i2 = jnp.min(jnp.where(le2 == v2, lane, LANES), axis=-1, keepdims=True)
    e2 = jnp.exp(v2 - v1)
    w1 = p_g / (1.0 + e2)
    w2 = p_g * e2 / (1.0 + e2)
    route = jnp.where(lane == ROUTE_I1, i1.astype(F32), jnp.where(lane == ROUTE_I2, i2.astype(F32), 0.0))
    route_ref[...] = route + jnp.where(lane == ROUTE_W1, w1, 0.0) + jnp.where(lane == ROUTE_W2, w2, 0.0)
    picks = (lane == i1).astype(F32) + (lane == i2).astype(F32)
    cnt_ref[...] = jnp.broadcast_to(jnp.sum(picks, axis=0, keepdims=True), cnt_ref.shape)


def _post_attn(oa, obs, lses, gates_src, ga_blk, x2d, wpa, wpb, wout, ln2, wr, br, tm, tiles_per_seq, extra):
    m, d = x2d.shape
    tps = tiles_per_seq
    own = m // tm
    n_tiles = own + (extra is not None)
    mine = lambda i: jnp.minimum(i, own - 1)

    def tile(w, col=0):
        return pl.BlockSpec((tm, w), lambda i: (mine(i), col))

    def out_tile(rows, w):
        return pl.BlockSpec((rows, w), lambda i: (i, 0))

    def full(a):
        nd = a.ndim
        return pl.BlockSpec(a.shape, lambda i: (0,) * nd)

    def residue(a):
        dil = a.shape[1]
        return pl.BlockSpec((None, dil, tm // dil, PB_W), lambda i: (mine(i) // tps, 0, mine(i) % tps, 0))

    weights = [wpa, wpb, wout, ln2.reshape(1, d), wr, br]
    scratch = [pltpu.VMEM((6 * PB_W // LANES, tm, LANES), F32)]
    in_specs = ([tile(QA_W)] + [residue(a) for a in obs] + [residue(a) for a in lses]
                + [tile(d, ga_blk), tile(d, ga_blk + 1), tile(d)] + [full(w) for w in weights])
    args = [oa, *obs, *lses, gates_src, gates_src, x2d, *weights]
    if extra is not None:
        in_specs = in_specs + [full(a) for a in extra]
        args = args + list(extra)
    return pl.pallas_call(
        functools.partial(_post_attn_kernel, n_extra=0 if extra is None else len(extra)),
        grid=(n_tiles,),
        in_specs=in_specs,
        out_specs=[out_tile(tm, d), out_tile(tm, d), out_tile(tm, LANES), out_tile(SUBLANES, LANES)],
        out_shape=[jax.ShapeDtypeStruct((n_tiles * tm, d), F32),
                   jax.ShapeDtypeStruct((n_tiles * tm, d), BF16),
                   jax.ShapeDtypeStruct((n_tiles * tm, LANES), F32),
                   jax.ShapeDtypeStruct((n_tiles * SUBLANES, LANES), F32)],
        scratch_shapes=scratch,
        compiler_params=_cparams(("arbitrary",)),
        name="post_attn",
    )(*args)


def _route_plan(cnt_rows, n_tiles, n_xtiles):
    g = SUBLANES
    cnt = cnt_rows.reshape(n_tiles, g, LANES)[:, 0, :N_EXPERTS].astype(jnp.int32)
    cnt8 = (cnt + g - 1) // g * g
    loff = jnp.cumsum(cnt8, axis=1) - cnt8
    boff = jnp.cumsum(cnt8, axis=0) - cnt8
    tot = jnp.sum(cnt8, axis=0)
    region = (tot + MOE_TMX - 1) // MOE_TMX * MOE_TMX
    gend = jnp.cumsum(region)
    gbase = gend - region
    cum_tiles = gend // MOE_TMX
    j = jnp.arange(n_xtiles, dtype=jnp.int32)
    loff_rows = jnp.zeros((n_tiles, g, LANES), F32).at[:, :, :N_EXPERTS].set(loff[:, None, :].astype(F32))
    k8 = g * jnp.arange(MOE_SLOTS // g, dtype=jnp.int32)
    run_end = loff + cnt8
    e_of_k = jnp.sum(k8[None, :, None] >= run_end[:, None, :], axis=2)
    shift = gbase[None, :] + boff - loff
    picked = jnp.sum(jnp.where(e_of_k[:, :, None] == jnp.arange(N_EXPERTS)[None, None, :], shift[:, None, :], 0), axis=2)
    return dict(
        chunk_dst=(picked + k8[None, :]).reshape(-1).astype(jnp.int32),
        ntot=jnp.sum(cnt8 // g, axis=1), zst=gbase + tot, znch=(region - tot) // g,
        ztot=jnp.sum((region - tot) // g).reshape(1),
        xtile_expert=jnp.minimum(jnp.sum(j[:, None] >= cum_tiles[None, :], axis=1), N_EXPERTS - 1).astype(jnp.int32),
        n_used=cum_tiles[-1:].astype(jnp.int32),
        loff_rows=loff_rows.reshape(n_tiles * g, LANES))


def _local_slots(route, loff_row):
    tm = route.shape[0]
    lane = lax.broadcasted_iota(jnp.int32, (tm, LANES), 1)
    e1 = lane == route[:, ROUTE_I1:ROUTE_I1 + 1].astype(jnp.int32)
    e2 = lane == route[:, ROUTE_I2:ROUTE_I2 + 1].astype(jnp.int32)
    earlier = (lax.broadcasted_iota(jnp.int32, (tm, tm), 1) < lax.broadcasted_iota(jnp.int32, (tm, tm), 0))
    earlier = earlier.astype(BF16)
    c1 = jnp.dot(earlier, e1.astype(BF16), preferred_element_type=F32)
    c2 = jnp.dot(earlier, e2.astype(BF16), preferred_element_type=F32)
    cnt1 = jnp.sum(e1.astype(F32), axis=0, keepdims=True)
    pos1 = jnp.sum(jnp.where(e1, c1 + loff_row, 0.0), axis=1, keepdims=True)
    pos2 = jnp.sum(jnp.where(e2, c2 + cnt1 + loff_row, 0.0), axis=1, keepdims=True)
    slot = lax.broadcasted_iota(jnp.int32, (tm, MOE_SLOTS), 1)
    return slot == pos1.astype(jnp.int32), slot == pos2.astype(jnp.int32)


def _pack_bf16_pairs(x):
    c = x.shape[1] // 2
    bits = lambda v: lax.bitcast_convert_type(v.astype(BF16).astype(F32), jnp.uint32)
    return bits(x[:, :c]) | (bits(x[:, c:]) >> 16)


def _unpack_bf16_pairs(w):
    hi = lax.bitcast_convert_type(w & jnp.uint32(0xFFFF0000), F32).astype(BF16)
    lo = lax.bitcast_convert_type(w << 16, F32).astype(BF16)
    return jnp.concatenate([hi, lo], axis=1)


def _split3(w):
    hi = w.astype(BF16).astype(F32)
    mid = (w - hi).astype(BF16).astype(F32)
    return hi, mid, (w - hi - mid).astype(BF16).astype(F32)


def _moe_scatter_kernel(cdst_s, ntot_s, zst_s, znch_s, ztot_s, nused_s,
                        hn_ref, route_ref, loffv_ref, xs_hbm, buf_ref, zero_ref, sem, zsem, tsem):
    b = pl.program_id(0)
    nb = pl.num_programs(0)
    slot = b % 2
    g = SUBLANES
    d = hn_ref.shape[1]
    n_xtiles = xs_hbm.shape[0] // MOE_TMX

    def run_copy(s, src_row, dst_row):
        return pltpu.make_async_copy(buf_ref.at[s, pl.ds(src_row, g)], xs_hbm.at[pl.ds(dst_row, g)], sem.at[s])

    def zero_copy(dst_row):
        return pltpu.make_async_copy(zero_ref.at[pl.ds(0, g)], xs_hbm.at[pl.ds(dst_row, g)], zsem)

    def zero_tile_copy(j):
        return pltpu.make_async_copy(zero_ref, xs_hbm.at[pl.ds(pl.multiple_of(j * MOE_TMX, MOE_TMX), MOE_TMX)], tsem)

    def wait_tile(tile, s):
        lax.fori_loop(0, ntot_s[tile], lambda _, c: (run_copy(s, 0, 0).wait(), c)[1], 0)

    @pl.when(b == 0)
    def _():
        zero_ref[...] = jnp.zeros_like(zero_ref)

        def per_expert(e, c):
            def per_chunk(k, c2):
                zero_copy(pl.multiple_of(zst_s[e] + k * g, g)).start()
                return c2
            return lax.fori_loop(0, znch_s[e], per_chunk, c)
        lax.fori_loop(0, N_EXPERTS, per_expert, 0)
        lax.fori_loop(nused_s[0], n_xtiles, lambda j, c: (zero_tile_copy(j).start(), c)[1], 0)

    @pl.when(b >= 2)
    def _():
        wait_tile(b - 2, slot)

    route = route_ref[...]
    p1, p2 = _local_slots(route, loffv_ref[0:1, :])
    tn = (((0,), (0,)), ((), ()))
    picks = (p1 | p2).astype(BF16)
    buf_ref[slot, :, 0:d // 2] = _pack_bf16_pairs(lax.dot_general(picks, hn_ref[...], tn, preferred_element_type=F32))
    lane = lax.broadcasted_iota(jnp.int32, route.shape, 1)
    meta = jnp.zeros((MOE_SLOTS, LANES), F32)
    for p, col in ((p1, ROUTE_W1), (p2, ROUTE_W2)):
        w = route[:, col:col + 1]
        parts = _split3(w)
        wm = sum(jnp.where(lane == k, part, 0.0) for k, part in enumerate(parts))
        meta = meta + lax.dot_general(p.astype(BF16), wm.astype(BF16), tn, preferred_element_type=F32)
    buf_ref[slot, :, d // 2:d // 2 + LANES] = lax.bitcast_convert_type(meta, jnp.uint32)

    def per_chunk(k, c):
        run_copy(slot, pl.multiple_of(k * g, g), pl.multiple_of(cdst_s[b * (MOE_SLOTS // g) + k], g)).start()
        return c
    lax.fori_loop(0, ntot_s[b], per_chunk, 0)

    @pl.when(b == nb - 1)
    def _():
        wait_tile(b, slot)

        @pl.when(nb >= 2)
        def _():
            wait_tile(b - 1, 1 - slot)
        lax.fori_loop(0, ztot_s[0], lambda _, c: (zero_copy(0).wait(), c)[1], 0)
        lax.fori_loop(nused_s[0], n_xtiles, lambda j, c: (zero_tile_copy(j).wait(), c)[1], 0)


def _moe_scatter(plan, hn_all, route_all, n_tiles, n_xtiles):
    d = hn_all.shape[1]
    tm = MOE_TM
    grid_spec = pltpu.PrefetchScalarGridSpec(
        num_scalar_prefetch=6,
        grid=(n_tiles,),
        in_specs=[pl.BlockSpec((tm, d), lambda i, *_: (i, 0)),
                  pl.BlockSpec((tm, LANES), lambda i, *_: (i, 0)),
                  pl.BlockSpec((SUBLANES, LANES), lambda i, *_: (i, 0))],
        out_specs=pl.BlockSpec(memory_space=pl.ANY),
        scratch_shapes=[pltpu.VMEM((2, MOE_SLOTS, d // 2 + LANES), jnp.uint32),
                        pltpu.VMEM((MOE_TMX, d // 2 + LANES), jnp.uint32),
                        pltpu.SemaphoreType.DMA((2,)),
                        pltpu.SemaphoreType.DMA(()),
                        pltpu.SemaphoreType.DMA(())])
    return pl.pallas_call(
        _moe_scatter_kernel,
        grid_spec=grid_spec,
        out_shape=jax.ShapeDtypeStruct((n_xtiles * MOE_TMX, d // 2 + LANES), jnp.uint32),
        compiler_params=_cparams(("arbitrary",)),
        name="moe_scatter",
    )(plan["chunk_dst"], plan["ntot"], plan["zst"], plan["znch"], plan["ztot"], plan["n_used"],
      hn_all, route_all, plan["loff_rows"])


def _moe_experts_kernel(xe_s, nused_s, x_ref, w1_ref, w3_ref, w2_ref, y_ref, w1b_ref, w3b_ref, w2b_ref):
    j = pl.program_id(0)
    d = w1_ref.shape[0]

    @pl.when(j < nused_s[0])
    def _():
        @pl.when((j == 0) | (xe_s[j] != xe_s[jnp.maximum(j - 1, 0)]))
        def _():
            w1b_ref[...] = w1_ref[...].astype(BF16)
            w3b_ref[...] = w3_ref[...].astype(BF16)
            w2b_ref[...] = w2_ref[...].astype(BF16)

        x = _unpack_bf16_pairs(x_ref[:, 0:d // 2])
        gate = jnp.sum(lax.bitcast_convert_type(x_ref[:, d // 2:d // 2 + LANES], F32), axis=1, keepdims=True)
        a = jnp.dot(x, w1b_ref[...], preferred_element_type=F32)
        b = jnp.dot(x, w3b_ref[...], preferred_element_type=F32)
        hh = (a * _sigmoid(a)) * b * gate
        y_ref[...] = _pack_bf16_pairs(jnp.dot(hh.astype(BF16), w2b_ref[...], preferred_element_type=F32))

    @pl.when(j >= nused_s[0])
    def _():
        y_ref[...] = jnp.zeros_like(y_ref)


def _moe_experts(plan, xs, w1, w3, w2, n_xtiles):
    ne, d, f = w1.shape
    last = lambda j, xe, nu: jnp.maximum(jnp.minimum(j, nu[0] - 1), 0)
    grid_spec = pltpu.PrefetchScalarGridSpec(
        num_scalar_prefetch=2,
        grid=(n_xtiles,),
        in_specs=[pl.BlockSpec((MOE_TMX, d // 2 + LANES), lambda j, xe, nu: (last(j, xe, nu), 0)),
                  pl.BlockSpec((None, d, f), lambda j, xe, nu: (xe[j], 0, 0)),
                  pl.BlockSpec((None, d, f), lambda j, xe, nu: (xe[j], 0, 0)),
                  pl.BlockSpec((None, f, d), lambda j, xe, nu: (xe[j], 0, 0))],
        out_specs=pl.BlockSpec((MOE_TMX, d // 2), lambda j, xe, nu: (j, 0)),
        scratch_shapes=[pltpu.VMEM((d, f), BF16), pltpu.VMEM((d, f), BF16), pltpu.VMEM((f, d), BF16)])
    return pl.pallas_call(
        _moe_experts_kernel,
        grid_spec=grid_spec,
        out_shape=jax.ShapeDtypeStruct((n_xtiles * MOE_TMX, d // 2), jnp.uint32),
        compiler_params=_cparams(("arbitrary",)),
        name="moe_experts",
    )(plan["xtile_expert"], plan["n_used"], xs, w1, w3, w2)


def _moe_combine_kernel(cdst_s, ntot_s, ys_hbm, route_ref, loffv_ref, h_ref, lnf_ref, y_ref,
                        buf_ref, sem, *, tile0, final_norm):
    b = pl.program_id(0)
    nb = pl.num_programs(0)
    slot = b % 2
    g = SUBLANES
    tile = b + tile0

    def run_copy(s, src_row, dst_row):
        return pltpu.make_async_copy(ys_hbm.at[pl.ds(src_row, g)], buf_ref.at[s, pl.ds(dst_row, g)], sem.at[s])

    def fetch(t, s):
        def per_chunk(k, c):
            run_copy(s, pl.multiple_of(cdst_s[t * (MOE_SLOTS // g) + k], g), pl.multiple_of(k * g, g)).start()
            return c
        lax.fori_loop(0, ntot_s[t], per_chunk, 0)

    @pl.when(b == 0)
    def _():
        fetch(tile, slot)

    @pl.when(b + 1 < nb)
    def _():
        fetch(tile + 1, 1 - slot)

    lax.fori_loop(0, ntot_s[tile], lambda _, c: (run_copy(slot, 0, 0).wait(), c)[1], 0)

    def clear(k, c):
        buf_ref[slot, pl.ds(pl.multiple_of(k * g, g), g), :] = jnp.zeros((g, buf_ref.shape[2]), jnp.uint32)
        return c
    lax.fori_loop(ntot_s[tile], MOE_SLOTS // g, clear, 0)

    p1, p2 = _local_slots(route_ref[...], loffv_ref[0:1, :])
    picks = (p1 | p2).astype(BF16)
    y = h_ref[...] + jnp.dot(picks, _unpack_bf16_pairs(buf_ref[slot]), preferred_element_type=F32)
    if final_norm:
        y = _rms(y, lnf_ref[...])
    y_ref[...] = y


def _moe_combine(plan, ys, route_all, h_all, lnf, tile0, n_tiles, final_norm):
    d = h_all.shape[1]
    tm = MOE_TM
    grid_spec = pltpu.PrefetchScalarGridSpec(
        num_scalar_prefetch=2,
        grid=(n_tiles,),
        in_specs=[pl.BlockSpec(memory_space=pl.ANY),
                  pl.BlockSpec((tm, LANES), lambda i, *_: (i + tile0, 0)),
                  pl.BlockSpec((SUBLANES, LANES), lambda i, *_: (i + tile0, 0)),
                  pl.BlockSpec((tm, d), lambda i, *_: (i + tile0, 0)),
                  pl.BlockSpec((1, d), lambda i, *_: (0, 0))],
        out_specs=pl.BlockSpec((tm, d), lambda i, *_: (i, 0)),
        scratch_shapes=[pltpu.VMEM((2, MOE_SLOTS, d // 2), jnp.uint32), pltpu.SemaphoreType.DMA((2,))])
    return pl.pallas_call(
        functools.partial(_moe_combine_kernel, tile0=tile0, final_norm=final_norm),
        grid_spec=grid_spec,
        out_shape=jax.ShapeDtypeStruct((n_tiles * tm, d), F32),
        compiler_params=_cparams(("arbitrary",)),
        name="moe_combine",
    )(plan["chunk_dst"], plan["ntot"], ys, route_all, plan["loff_rows"], h_all, lnf.reshape(1, d))


def _prompt_states(kvt, n, t):
    hd = HEAD_DIM
    tail = kvt.shape[2]

    def state(k_lo, v_lo, rows, heads, win):
        w = min(win, t)
        kv = jnp.stack([kvt[:, k_lo:k_lo + rows, tail - w:], kvt[:, v_lo:v_lo + rows, tail - w:]], axis=1)
        return jnp.transpose(kv.reshape(n, 2, heads, hd, w), (0, 4, 1, 2, 3))

    out = [state(0, KA_W, KA_W, A_KV_HEADS, A_WINDOW)]
    for p, (win, _) in enumerate(B_PATTERNS):
        out.append(state(2 * KA_W + p * PB_W, 2 * KA_W + QB_W + p * PB_W, PB_W, B_HEADS_PER_PATTERN, win))
    return out


def _layer(xp, xs, caches, rel_bias, ln1, w_in, sinks, w_pa, w_pb, w_out, ln2, w_rg, b_rg, w_re, b_re,
           w1, w3, w2, lnf, final_norm):
    n, t, d = xp.shape
    ns = xs.shape[0]
    assert xs.shape[1] == 1 and OFF_GA + 2 * d == w_in.shape[1] and P_QA == 2 * d
    w_bf = w_in.astype(BF16)
    w_kvt = _kv_weights_transposed(w_in)
    wpa, wpb, wout = w_pa.astype(BF16), w_pb.astype(BF16), w_out.astype(BF16)
    wr = jnp.zeros((d, LANES), F32).at[:, :N_EXPERTS].set(w_re).at[:, N_EXPERTS:N_EXPERTS + MOE_GROUPS].set(w_rg)
    br = jnp.zeros((1, LANES), F32).at[0, :N_EXPERTS].set(b_re).at[0, N_EXPERTS:N_EXPERTS + MOE_GROUPS].set(b_rg)
    wr_hi = wr.astype(BF16)
    wr = jnp.stack([wr_hi, (wr - wr_hi.astype(F32)).astype(BF16)])

    tm = MOE_TM
    xp2 = xp.reshape(n * t, d)
    xs2 = xs.reshape(ns, d)
    proj_s = _in_proj_sample(xs2, ln1, w_bf)
    if ns == 2 * (n * t // MIX_TM):
        (act, qkv2, qkv3, kvt), sampled = _in_proj_mix(xp2, ln1, w_bf, w_kvt, n, t, MIX_TM, proj_s, caches,
                                                       rel_bias, sinks)
    else:
        act, qkv2, qkv3, kvt = _in_proj_prompt(xp2, ln1, w_bf, w_kvt, n, t, IN_PROJ_TM)
        sampled = _sample_mix(proj_s, caches, rel_bias, sinks)
    oa_s, ob_s, lse_s, st_s = sampled
    act4 = act.reshape(n, 1, t, ACT_W)
    bias_a = _band_bias(rel_bias[:, :A_HEADS], A_WINDOW - 1, 1)
    (oa,) = _band_attn(act4, bias_a, sinks, q_off=P_QA, k_off=P_KA, v_off=P_VA,
                       kv_heads=A_KV_HEADS, grp=A_GROUP, want_lse=False)
    obs, lses = [], []
    for p, (win, dil) in enumerate(B_PATTERNS):
        lo = A_HEADS + p * B_HEADS_PER_PATTERN
        bias_p = _band_bias(rel_bias[:, lo:lo + B_HEADS_PER_PATTERN], win // dil, dil)
        src, base = ((act4, P_B), (qkv2, 0), (qkv3, 0))[p]
        o, lse = _band_attn(src, bias_p, None, q_off=base, k_off=base + PB_W, v_off=base + 2 * PB_W,
                            kv_heads=B_HEADS_PER_PATTERN, grp=1, want_lse=True)
        obs.append(o)
        lses.append(lse)
    assert ns <= MOE_TM
    p_tiles = n * t // MOE_TM
    n_tiles = p_tiles + 1
    m_all = n_tiles * MOE_TM
    st_p = _prompt_states(kvt, n, t)

    rows = lambda a: jnp.pad(a, ((0, MOE_TM - ns), (0, 0)))
    obs_s = [rows(ob_s[:, p].astype(BF16)).reshape(1, 1, MOE_TM, PB_W) for p in range(N_PAT)]
    lses_s = [rows(jnp.repeat(lse_s[:, p], HEAD_DIM, axis=-1)).reshape(1, 1, MOE_TM, PB_W) for p in range(N_PAT)]
    gates_s = rows(proj_s[:, OFF_GA:].astype(BF16))
    routed_s = _post_attn(rows(oa_s.astype(BF16)), obs_s, lses_s, gates_s, 0, rows(xs2),
                          wpa, wpb, wout, ln2, wr, br, MOE_TM, 1, None)
    h_all, hn_all, route_all, cnt_all = _post_attn(oa.reshape(n * t, QA_W), obs, lses, act, 0, xp2,
                                                   wpa, wpb, wout, ln2, wr, br, tm, t // tm, routed_s)

    max_rows = 2 * m_all + n_tiles * N_EXPERTS * (SUBLANES - 1) + N_EXPERTS * (MOE_TMX - SUBLANES)
    n_xtiles = -(-max_rows // MOE_TMX)
    plan = _route_plan(cnt_all, n_tiles, n_xtiles)
    xs_sorted = _moe_scatter(plan, hn_all, route_all, n_tiles, n_xtiles)
    ys_sorted = _moe_experts(plan, xs_sorted, w1, w3, w2, n_xtiles)
    yp = _moe_combine(plan, ys_sorted, route_all, h_all, lnf, 0, p_tiles, final_norm).reshape(n, t, d)
    ys = _moe_combine(plan, ys_sorted, route_all, h_all, lnf, p_tiles, 1, final_norm)[:ns].reshape(ns, 1, d)
    return yp, ys, st_p, st_s


def kernel(x_prompt, x_sample, cache_a_kv, cache_b1_kv, cache_b2_kv, cache_b3_kv, rel_bias, ln1_g, w_in, sinks,
           w_pa, w_pb, w_out, ln2_g, w_rg, b_rg, w_re, b_re, w1, w3, w2, lnf_g):
    depth = w_in.shape[0]
    assert depth >= 1
    xp, xs = x_prompt, x_sample
    new_p = [[] for _ in range(4)]
    new_s = [[] for _ in range(4)]
    for l in range(depth):
        caches = (cache_a_kv[l], cache_b1_kv[l], cache_b2_kv[l], cache_b3_kv[l])
        xp, xs, st_p, st_s = _layer(xp, xs, caches, rel_bias, ln1_g[l], w_in[l], sinks[l], w_pa[l], w_pb[l],
                                    w_out[l], ln2_g[l], w_rg[l], b_rg[l], w_re[l], b_re[l], w1[l], w3[l], w2[l],
                                    lnf_g, l == depth - 1)
        for i in range(4):
            new_p[i].append(st_p[i])
            new_s[i].append(st_s[i])
    a_p, b1_p, b2_p, b3_p = [jnp.stack(v) for v in new_p]
    a_s, b1_s, b2_s, b3_s = [jnp.stack(v) for v in new_s]
    return (xp, xs, a_p, a_s, b1_p, b1_s, b2_p, b2_s, b3_p, b3_s)
```

```python
import functools
import math

import numpy as np
import jax
import jax.numpy as jnp
from jax import lax
from jax.experimental import pallas as pl
from jax.experimental.pallas import tpu as pltpu

F32 = jnp.float32
BF16 = jnp.bfloat16

HEAD_DIM = 64
A_HEADS = 8
A_KV_HEADS = 2
A_GROUP = A_HEADS // A_KV_HEADS
A_WINDOW = 128
B_PATTERNS = ((128, 1), (512, 4), (2048, 16))
N_PAT = len(B_PATTERNS)
B_HEADS_PER_PATTERN = 4
B_HEADS = B_HEADS_PER_PATTERN * N_PAT
BLOCK = 128
ATTN_CHAINS = 16
NUM_BUCKETS = 32
MAX_DISTANCE = 2048
MOE_GROUPS = 4
EXPERTS_PER_GROUP = 8
N_EXPERTS = MOE_GROUPS * EXPERTS_PER_GROUP
EPS = 1e-6
NEG = -1e30
LANES = 128
SUBLANES = 8
ROUTE_I1, ROUTE_I2, ROUTE_W1, ROUTE_W2 = 0, 1, 2, 3
IN_PROJ_TM = 512
MIX_TM = 256
MOE_TM = 256
MOE_SLOTS = 2 * MOE_TM + N_EXPERTS * SUBLANES
MOE_TMX = 256
MOE_SPILL_TILES = -(-MOE_SLOTS // MOE_TMX)
Q_SCALE = HEAD_DIM ** -0.5

QA_W = A_HEADS * HEAD_DIM
KA_W = A_KV_HEADS * HEAD_DIM
QB_W = B_HEADS * HEAD_DIM
PB_W = B_HEADS_PER_PATTERN * HEAD_DIM
QKV_W = 3 * PB_W
OFF_KA = QA_W
OFF_QB = OFF_KA + 2 * KA_W
OFF_KB = OFF_QB + QB_W
OFF_VB = OFF_KB + QB_W
OFF_GA = OFF_VB + QB_W
P_QA = 2048
P_KA = P_QA + QA_W
P_VA = P_KA + KA_W
P_B = P_VA + KA_W
ACT_W = P_B + QKV_W
KVT_ROWS = 2 * KA_W + 2 * QB_W

VMEM_LIMIT = 56 * 1024 * 1024


def _cparams(sem):
    return pltpu.CompilerParams(dimension_semantics=sem, vmem_limit_bytes=VMEM_LIMIT)


def _bucket_np(dist):
    dist = np.asarray(dist, np.int64)
    max_exact = NUM_BUCKETS // 2
    df = np.maximum(dist, max_exact).astype(np.float64)
    large = max_exact + (np.log(df / max_exact) / math.log(MAX_DISTANCE / max_exact)
                         * (NUM_BUCKETS - max_exact)).astype(np.int64)
    return np.where(dist < max_exact, dist, np.minimum(large, NUM_BUCKETS - 1))


def _table_rows(table_cols, dist, valid):
    onehot = (_bucket_np(dist)[:, None] == np.arange(NUM_BUCKETS)[None, :]).astype(np.float32)
    rows = jnp.einsum("ck,kh->hc", jnp.asarray(onehot), table_cols.astype(F32), precision=lax.Precision.HIGHEST)
    return jnp.where(jnp.asarray(valid)[None, :], rows, NEG)


def _band_bias(table_cols, max_dist, dilation):
    period = 3 * BLOCK
    m = np.arange(period)
    k = np.where(m < 2 * BLOCK, m, m - period)
    dist = BLOCK - k
    valid = (dist >= 0) & (dist <= max_dist) & (m != 2 * BLOCK)
    v = _table_rows(table_cols, np.clip(dist, 0, None) * dilation, valid)
    heads = v.shape[0]
    flat = jnp.tile(v, (1, BLOCK))[:, :BLOCK * (period - 1)]
    return flat.reshape(heads, BLOCK, period - 1)[:, :, :2 * BLOCK]


def _decode_bias(table_cols, width, dilation, first_valid):
    c = np.arange(width)
    valid = (c % dilation == 0) & (c >= first_valid)
    rows = _table_rows(table_cols, width - c, valid)
    self_bias = _table_rows(table_cols, np.zeros((1,), np.int64), np.ones((1,), bool))
    return rows[:, None, :], self_bias[:, None, :]


def _rms(x, g):
    return (x * lax.rsqrt(jnp.mean(x * x, axis=-1, keepdims=True) + EPS)) * g


def _sigmoid(x):
    return 1.0 / (1.0 + jnp.exp(-x))


YS_CHUNKS = QKV_W // LANES


def _in_proj_work(x_ref, g_ref, w_ref, act_ref, qkv2_ref, qkv3_ref, ys_ref, phase):
    tm = x_ref.shape[0]
    cache = {}

    def xb():
        if "xb" not in cache:
            cache["xb"] = _rms(x_ref[...], g_ref[...]).astype(BF16)
        return cache["xb"]

    def proj(lo, hi):
        return jnp.dot(xb(), w_ref[:, lo:hi], preferred_element_type=F32)

    def gates(c):
        act_ref[:, c:c + 512] = _sigmoid(proj(OFF_GA + c, OFF_GA + c + 512)).astype(BF16)

    def mixer_a_q():
        act_ref[:, P_QA:P_KA] = (proj(0, OFF_KA) * Q_SCALE).astype(BF16)

    def mixer_a_kv():
        act_ref[:, P_KA:P_B] = proj(OFF_KA, OFF_QB).astype(BF16)

    def pattern_part(p, j):
        lo = (OFF_QB, OFF_KB, OFF_VB)[j] + p * PB_W
        part = proj(lo, lo + PB_W)
        if j == 0:
            part = part * Q_SCALE
        if p == 0:
            act_ref[:, P_B + j * PB_W:P_B + (j + 1) * PB_W] = part.astype(BF16)
        else:
            for c in range(PB_W // LANES):
                ys_ref[(p - 1) * YS_CHUNKS + j * (PB_W // LANES) + c] = part[:, c * LANES:(c + 1) * LANES]

    def regroup(p, out_ref):
        dil = B_PATTERNS[p][1]
        for c in range(YS_CHUNKS):
            for r in range(dil):
                out_ref[r, :, c * LANES:(c + 1) * LANES] = (
                    ys_ref[(p - 1) * YS_CHUNKS + c, pl.ds(r, tm // dil, stride=dil), :].astype(BF16))

    work = []
    if phase in (None, 0):
        work += [functools.partial(gates, c) for c in range(0, P_QA, 512)] + [mixer_a_q, mixer_a_kv]
    if phase in (None, 1):
        for p in range(N_PAT):
            work += [functools.partial(pattern_part, p, j) for j in range(3)]
            if p > 0:
                work.append(functools.partial(regroup, p, (qkv2_ref, qkv3_ref)[p - 1]))
    return work


def _transpose_cast_kernel(w_ref, o_ref):
    o_ref[...] = w_ref[...].T.astype(o_ref.dtype)


def _kv_weights_transposed(w_in):
    d = w_in.shape[0]
    blk = 2 * KA_W
    assert OFF_KA % blk == 0 and OFF_KB % blk == 0 and KVT_ROWS % blk == 0
    first, rest = OFF_KA // blk, OFF_KB // blk - 1
    return pl.pallas_call(
        _transpose_cast_kernel,
        grid=(KVT_ROWS // blk,),
        in_specs=[pl.BlockSpec((d, blk), lambda i: (0, jnp.where(i == 0, first, rest + i)))],
        out_specs=pl.BlockSpec((blk, d), lambda i: (i, 0)),
        out_shape=jax.ShapeDtypeStruct((KVT_ROWS, d), BF16),
        compiler_params=_cparams(("arbitrary",)),
        name="kv_weights_t",
    )(w_in)


def _kv_tail(x_ref, g_ref, wkvt_ref, kvt_ref):
    xb = _rms(x_ref[...], g_ref[...]).astype(BF16)
    kvt_ref[...] = lax.dot_general(wkvt_ref[...], xb, (((1,), (1,)), ((), ())), preferred_element_type=F32)


def _in_proj_kernel(x_ref, g_ref, w_ref, wkvt_ref, act_ref, qkv2_ref, qkv3_ref, kvt_ref, ys_ref, *,
                    tiles_per_seq, tail_tiles):
    for piece in _in_proj_work(x_ref, g_ref, w_ref, act_ref, qkv2_ref, qkv3_ref, ys_ref, None):
        piece()

    @pl.when(pl.program_id(0) % tiles_per_seq >= tiles_per_seq - tail_tiles)
    def _():
        _kv_tail(x_ref, g_ref, wkvt_ref, kvt_ref)


def _in_proj_prompt(x2d, ln_g, w_bf, w_kvt, n, t, tm):
    m, d = x2d.shape
    tps = t // tm
    tail = min(max(w for w, _ in B_PATTERNS), t)
    assert t % tm == 0 and tail % tm == 0 and all(tm % (16 * dl) == 0 for _, dl in B_PATTERNS)
    tail_tiles = tail // tm
    d2, d3 = B_PATTERNS[1][1], B_PATTERNS[2][1]
    return pl.pallas_call(
        functools.partial(_in_proj_kernel, tiles_per_seq=tps, tail_tiles=tail_tiles),
        grid=(m // tm,),
        in_specs=[pl.BlockSpec((tm, d), lambda i: (i, 0)),
                  pl.BlockSpec((1, d), lambda i: (0, 0)),
                  pl.BlockSpec(w_bf.shape, lambda i: (0, 0), pipeline_mode=pl.Buffered(1)),
                  pl.BlockSpec(w_kvt.shape, lambda i: (0, 0), pipeline_mode=pl.Buffered(1))],
        out_specs=[pl.BlockSpec((tm, ACT_W), lambda i: (i, 0)),
                   pl.BlockSpec((None, d2, tm // d2, QKV_W), lambda i: (i // tps, 0, i % tps, 0)),
                   pl.BlockSpec((None, d3, tm // d3, QKV_W), lambda i: (i // tps, 0, i % tps, 0)),
                   pl.BlockSpec((None, KVT_ROWS, tm),
                                lambda i: (i // tps, 0, jnp.maximum(i % tps - (tps - tail_tiles), 0)))],
        out_shape=[jax.ShapeDtypeStruct((m, ACT_W), BF16),
                   jax.ShapeDtypeStruct((n, d2, t // d2, QKV_W), BF16),
                   jax.ShapeDtypeStruct((n, d3, t // d3, QKV_W), BF16),
                   jax.ShapeDtypeStruct((n, KVT_ROWS, tail), F32)],
        scratch_shapes=[pltpu.VMEM(((N_PAT - 1) * YS_CHUNKS, tm, LANES), F32)],
        compiler_params=_cparams(("arbitrary",)),
        name="in_proj",
    )(x2d, ln_g.reshape(1, d), w_bf, w_kvt)


def _in_proj_sample_kernel(x_ref, g_ref, w_ref, y_ref):
    xb = _rms(x_ref[...], g_ref[...]).astype(BF16)
    y_ref[:, :OFF_GA] = jnp.dot(xb, w_ref[:, :OFF_GA], preferred_element_type=F32)
    for lo, hi in ((0, OFF_KA), (OFF_QB, OFF_KB)):
        y_ref[:, lo:hi] = y_ref[:, lo:hi] * Q_SCALE
    for lo in range(OFF_GA, w_ref.shape[1], 1024):
        y_ref[:, lo:lo + 1024] = _sigmoid(jnp.dot(xb, w_ref[:, lo:lo + 1024], preferred_element_type=F32))


def _in_proj_sample(x2d, ln_g, w_bf):
    m, d = x2d.shape
    return pl.pallas_call(
        _in_proj_sample_kernel,
        grid=(1,),
        in_specs=[pl.BlockSpec((m, d), lambda i: (0, 0)),
                  pl.BlockSpec((1, d), lambda i: (0, 0)),
                  pl.BlockSpec(w_bf.shape, lambda i: (0, 0))],
        out_specs=pl.BlockSpec((m, w_bf.shape[1]), lambda i: (0, 0)),
        out_shape=jax.ShapeDtypeStruct((m, w_bf.shape[1]), F32),
        compiler_params=_cparams(("arbitrary",)),
        name="in_proj_sample",
    )(x2d, ln_g.reshape(1, d), w_bf)


def _band_attn_kernel(*refs, kv_heads, grp, has_sink, want_lse):
    if has_sink:
        sink_ref, refs = refs[0], refs[1:]
    q_ref, kp_ref, kc_ref, vp_ref, vc_ref, bias_ref, o_ref = refs[:7]
    lse_ref = refs[7] if want_lse else None
    step = pl.program_id(2)
    hd = HEAD_DIM
    nt = (((1,), (1,)), ((), ()))
    chains = [(sub, kv * grp + g, slice(kv * hd, (kv + 1) * hd))
              for sub in range(q_ref.shape[0] // BLOCK) for kv in range(kv_heads) for g in range(grp)]
    scores = []
    for sub, h, ks in chains:
        rows = slice(sub * BLOCK, (sub + 1) * BLOCK)
        q = q_ref[rows, h * hd:(h + 1) * hd]
        kp = kp_ref[:, ks] if sub == 0 else kc_ref[(sub - 1) * BLOCK:sub * BLOCK, ks]
        sp = lax.dot_general(q, kp, nt, preferred_element_type=F32) + bias_ref[h, :, 0:BLOCK]
        sc = lax.dot_general(q, kc_ref[rows, ks], nt, preferred_element_type=F32) + bias_ref[h, :, BLOCK:2 * BLOCK]
        if sub == 0:
            sp = jnp.where(step > 0, sp, NEG)
        scores.append((sp, sc))
    maxes = []
    for (sub, h, ks), (sp, sc) in zip(chains, scores):
        m = jnp.maximum(jnp.max(sp, axis=-1, keepdims=True), jnp.max(sc, axis=-1, keepdims=True))
        maxes.append(jnp.maximum(m, sink_ref[h]) if has_sink else m)
    probs = []
    for (sub, h, ks), (sp, sc), m in zip(chains, scores, maxes):
        pp, pc = jnp.exp(sp - m), jnp.exp(sc - m)
        den = jnp.sum(pp, axis=-1, keepdims=True) + jnp.sum(pc, axis=-1, keepdims=True)
        if has_sink:
            den = den + jnp.exp(sink_ref[h] - m)
        probs.append((pp.astype(BF16), pc.astype(BF16), den))
    for (sub, h, ks), (pp, pc, den), m in zip(chains, probs, maxes):
        rows = slice(sub * BLOCK, (sub + 1) * BLOCK)
        vp = vp_ref[:, ks] if sub == 0 else vc_ref[(sub - 1) * BLOCK:sub * BLOCK, ks]
        o = (jnp.dot(pp, vp, preferred_element_type=F32)
             + jnp.dot(pc, vc_ref[rows, ks], preferred_element_type=F32))
        o_ref[rows, h * hd:(h + 1) * hd] = (o / den).astype(o_ref.dtype)
        if want_lse:
            lse_ref[rows, h * hd:(h + 1) * hd] = jnp.broadcast_to(m + jnp.log(den), (BLOCK, hd))


def _band_attn(src, bias, sink, *, q_off, k_off, v_off, kv_heads, grp, want_lse):
    n, dil, l, cols = src.shape
    sub = max(1, ATTN_CHAINS // (kv_heads * grp))
    while l % (sub * BLOCK):
        sub //= 2
    rows = sub * BLOCK
    assert sub >= 1 and l % rows == 0
    nb = l // rows
    qw = kv_heads * grp * HEAD_DIM
    kw = kv_heads * HEAD_DIM
    assert q_off % qw == 0 and k_off % kw == 0 and v_off % kw == 0
    qb, kb, vb = q_off // qw, k_off // kw, v_off // kw
    prev = lambda b: jnp.maximum(sub * b - 1, 0)
    in_specs = [
        pl.BlockSpec((None, None, rows, qw), lambda i, r, b: (i, r, b, qb)),
        pl.BlockSpec((None, None, BLOCK, kw), lambda i, r, b: (i, r, prev(b), kb)),
        pl.BlockSpec((None, None, rows, kw), lambda i, r, b: (i, r, b, kb)),
        pl.BlockSpec((None, None, BLOCK, kw), lambda i, r, b: (i, r, prev(b), vb)),
        pl.BlockSpec((None, None, rows, kw), lambda i, r, b: (i, r, b, vb)),
        pl.BlockSpec(bias.shape, lambda i, r, b: (0, 0, 0)),
    ]
    args = [src, src, src, src, src, bias]
    has_sink = sink is not None
    if has_sink:
        in_specs = [pl.BlockSpec(memory_space=pltpu.SMEM)] + in_specs
        args = [sink.astype(F32)] + args
    out_specs = [pl.BlockSpec((None, None, rows, qw), lambda i, r, b: (i, r, b, 0))]
    out_shape = [jax.ShapeDtypeStruct((n, dil, l, qw), BF16)]
    if want_lse:
        out_specs.append(pl.BlockSpec((None, None, rows, qw), lambda i, r, b: (i, r, b, 0)))
        out_shape.append(jax.ShapeDtypeStruct((n, dil, l, qw), F32))
    return pl.pallas_call(
        functools.partial(_band_attn_kernel, kv_heads=kv_heads, grp=grp, has_sink=has_sink, want_lse=want_lse),
        grid=(n, dil, nb),
        in_specs=in_specs,
        out_specs=out_specs,
        out_shape=out_shape,
        compiler_params=_cparams(("arbitrary", "arbitrary", "arbitrary")),
        name=f"band_attn_d{dil}",
    )(*args)


COL_QA = 0
COL_QB = COL_QA + A_HEADS
COL_KA = COL_QB + B_HEADS
COL_VA = COL_KA + A_KV_HEADS
COL_KB = COL_VA + A_KV_HEADS
COL_VB = COL_KB + B_HEADS
N_COLS = COL_VB + B_HEADS


def _attend_cached(kt, vt, q, k_new, v_new, bias, self_bias, sink, write):
    s = jnp.sum(kt * q, axis=1, keepdims=True) + bias
    s_new = jnp.sum(k_new * q, axis=1, keepdims=True) + self_bias
    yield
    m = jnp.maximum(jnp.max(s, axis=2, keepdims=True), s_new)
    if sink is not None:
        m = jnp.maximum(m, sink)
    yield
    p = jnp.exp(s - m)
    p_new = jnp.exp(s_new - m)
    den = jnp.sum(p, axis=2, keepdims=True) + p_new
    if sink is not None:
        den = den + jnp.exp(sink - m)
    yield
    o = (jnp.sum(vt * p, axis=2, keepdims=True) + v_new * p_new) / den
    write(o, m + jnp.log(den))
    yield


def _sample_work(cols_ref, ca_ref, c1_ref, c2_ref, c3_ref, ba_ref, sa_ref, sink_ref, b1_ref, b2_ref, b3_ref,
                 sb_ref, o_ref, lse_ref, ra_ref, r1_ref, r2_ref, r3_ref):
    cols = cols_ref[...]

    def stack(js):
        return jnp.stack([cols[:, j:j + 1] for j in js])

    o_ref[...] = jnp.zeros_like(o_ref)
    lse_ref[...] = jnp.zeros_like(lse_ref)

    def write_a(o, _):
        for h in range(A_HEADS):
            o_ref[:, COL_QA + h:COL_QA + h + 1] = o[h]

    def write_b(p):
        def write(o, lse):
            for h in range(B_HEADS_PER_PATTERN):
                j = p * B_HEADS_PER_PATTERN + h
                o_ref[:, COL_QB + j:COL_QB + j + 1] = o[h]
                lse_ref[0:1, j:j + 1] = lse[h]
        return write

    kv_of = [h // A_GROUP for h in range(A_HEADS)]
    stages = [_attend_cached(jnp.stack([ca_ref[0, kv] for kv in kv_of]), jnp.stack([ca_ref[1, kv] for kv in kv_of]),
                             stack(range(COL_QA, COL_QA + A_HEADS)), stack([COL_KA + kv for kv in kv_of]),
                             stack([COL_VA + kv for kv in kv_of]), ba_ref[...], sa_ref[...], sink_ref[...], write_a)]
    for p, (c_ref, b_ref) in enumerate(((c1_ref, b1_ref), (c2_ref, b2_ref), (c3_ref, b3_ref))):
        js = range(p * B_HEADS_PER_PATTERN, (p + 1) * B_HEADS_PER_PATTERN)
        stages.append(_attend_cached(c_ref[0], c_ref[1], stack([COL_QB + j for j in js]),
                                     stack([COL_KB + j for j in js]), stack([COL_VB + j for j in js]),
                                     b_ref[...], sb_ref[p], None, write_b(p)))

    def roll_plane(c_ref, r_ref, i, h, new_col):
        w = c_ref.shape[-1]
        x = c_ref[i, h]
        lane = lax.broadcasted_iota(jnp.int32, x.shape, 1)
        r_ref[i, h] = jnp.where(lane == w - 1, cols[:, new_col:new_col + 1], pltpu.roll(x, w - 1, 1))

    planes = []
    for ci, (c_ref, r_ref) in enumerate(((ca_ref, ra_ref), (c1_ref, r1_ref), (c2_ref, r2_ref), (c3_ref, r3_ref))):
        for i, (first_a, first_b) in enumerate(((COL_KA, COL_KB), (COL_VA, COL_VB))):
            for h in range(c_ref.shape[1]):
                new_col = first_a + h if ci == 0 else first_b + (ci - 1) * B_HEADS_PER_PATTERN + h
                planes.append((c_ref.shape[-1], functools.partial(roll_plane, c_ref, r_ref, i, h, new_col)))

    work = []
    total = sum(w for w, _ in planes)
    n_slots = 4 * len(stages)
    done, k = 0, 0
    for slot in range(n_slots):
        work.append(functools.partial(next, stages[slot // 4]))
        while k < len(planes) and done < total * (slot + 1) // n_slots:
            done += planes[k][0]
            work.append(planes[k][1])
            k += 1
    assert k == len(planes)
    return work


def _sample_mix_kernel(*refs):
    for piece in _sample_work(*refs):
        piece()


def _run_interleaved(primary, secondary):
    k = 0
    for i, piece in enumerate(primary):
        piece()
        while k < len(secondary) and k < len(secondary) * (i + 1) // len(primary):
            secondary[k]()
            k += 1
    for piece in secondary[k:]:
        piece()


def _in_proj_mix_kernel(x_ref, g_ref, w_ref, wkvt_ref, *rest, tiles_per_seq, tail_tiles, n_sample_in):
    sample_in = rest[:n_sample_in]
    act_ref, qkv2_ref, qkv3_ref, kvt_ref = rest[n_sample_in:n_sample_in + 4]
    sample_out = rest[n_sample_in + 4:-1]
    ys_ref = rest[-1]
    step = pl.program_id(0)
    tile = step // 2
    for phase in (0, 1):
        @pl.when(step % 2 == phase)
        def _():
            _run_interleaved(_sample_work(*sample_in, *sample_out),
                             _in_proj_work(x_ref, g_ref, w_ref, act_ref, qkv2_ref, qkv3_ref, ys_ref, phase))
            if phase == 1:
                @pl.when(tile % tiles_per_seq >= tiles_per_seq - tail_tiles)
                def _():
                    _kv_tail(x_ref, g_ref, wkvt_ref, kvt_ref)


def _sample_operands(proj_s, caches, rel_bias, sinks):
    n = proj_s.shape[0]
    hd = HEAD_DIM
    vecs = jnp.concatenate([proj_s[:, :OFF_KA], proj_s[:, OFF_QB:OFF_KB], proj_s[:, OFF_KA:OFF_QB],
                            proj_s[:, OFF_KB:OFF_GA]], axis=1).reshape(n, N_COLS, hd)
    cols = jnp.pad(jnp.transpose(vecs, (0, 2, 1)), ((0, 0), (0, 0), (0, LANES - N_COLS)))
    cts = [jnp.transpose(c, (0, 2, 3, 4, 1)) for c in caches]

    ba, sa = _decode_bias(rel_bias[:, :A_HEADS], A_WINDOW, 1, 1)
    sink = sinks.astype(F32).reshape(A_HEADS, 1, 1)
    bbs, sbs = [], []
    for p, (win, dil) in enumerate(B_PATTERNS):
        lo = A_HEADS + p * B_HEADS_PER_PATTERN
        assert caches[1 + p].shape[1] == win == BLOCK * dil
        b, s = _decode_bias(rel_bias[:, lo:lo + B_HEADS_PER_PATTERN], win, dil, 0)
        bbs.append(b)
        sbs.append(s)
    sb = jnp.stack(sbs)

    seq_in = [cols] + cts
    consts = [ba, sa, sink] + bbs + [sb]
    out_shape = [jax.ShapeDtypeStruct((n, hd, LANES), F32), jax.ShapeDtypeStruct((n, SUBLANES, LANES), F32)]
    out_shape += [jax.ShapeDtypeStruct(c.shape, c.dtype) for c in cts]
    return seq_in, consts, out_shape


def _per_seq_spec(a):
    nd = len(a.shape)
    return pl.BlockSpec((None,) + tuple(a.shape[1:]), lambda i: (i,) + (0,) * (nd - 1))


def _const_spec(a):
    nd = a.ndim
    return pl.BlockSpec(a.shape, lambda i: (0,) * nd)


def _sample_results(outs):
    n = outs[0].shape[0]
    o_rows = jnp.transpose(outs[0][:, :, :COL_KA], (0, 2, 1))
    oa = o_rows[:, COL_QA:COL_QB].reshape(n, QA_W)
    ob = o_rows[:, COL_QB:COL_KA].reshape(n, N_PAT, PB_W)
    lse = outs[1][:, 0, :B_HEADS].reshape(n, N_PAT, B_HEADS_PER_PATTERN)
    rolled = [jnp.transpose(r, (0, 4, 1, 2, 3)) for r in outs[2:]]
    return oa, ob, lse, rolled


def _sample_mix(proj_s, caches, rel_bias, sinks):
    seq_in, consts, out_shape = _sample_operands(proj_s, caches, rel_bias, sinks)
    outs = pl.pallas_call(
        _sample_mix_kernel,
        grid=(proj_s.shape[0],),
        in_specs=[_per_seq_spec(a) for a in seq_in] + [_const_spec(a) for a in consts],
        out_specs=[_per_seq_spec(s) for s in out_shape],
        out_shape=out_shape,
        compiler_params=_cparams(("arbitrary",)),
        name="sample_mix",
    )(*seq_in, *consts)
    return _sample_results(outs)


def _in_proj_mix(x2d, ln_g, w_bf, w_kvt, n, t, tm, proj_s, caches, rel_bias, sinks):
    m, d = x2d.shape
    tps = t // tm
    tail = min(max(w for w, _ in B_PATTERNS), t)
    assert t % tm == 0 and tail % tm == 0 and all(tm % (16 * dl) == 0 for _, dl in B_PATTERNS)
    assert proj_s.shape[0] == 2 * (m // tm)
    tail_tiles = tail // tm
    d2, d3 = B_PATTERNS[1][1], B_PATTERNS[2][1]
    seq_in, consts, sample_shape = _sample_operands(proj_s, caches, rel_bias, sinks)
    tile = lambda i: i // 2
    outs = pl.pallas_call(
        functools.partial(_in_proj_mix_kernel, tiles_per_seq=tps, tail_tiles=tail_tiles,
                          n_sample_in=len(seq_in) + len(consts)),
        grid=(2 * (m // tm),),
        in_specs=[pl.BlockSpec((tm, d), lambda i: (tile(i), 0)),
                  pl.BlockSpec((1, d), lambda i: (0, 0)),
                  pl.BlockSpec(w_bf.shape, lambda i: (0, 0), pipeline_mode=pl.Buffered(1)),
                  pl.BlockSpec(w_kvt.shape, lambda i: (0, 0), pipeline_mode=pl.Buffered(1))]
                 + [_per_seq_spec(a) for a in seq_in] + [_const_spec(a) for a in consts],
        out_specs=[pl.BlockSpec((tm, ACT_W), lambda i: (tile(i), 0)),
                   pl.BlockSpec((None, d2, tm // d2, QKV_W), lambda i: (tile(i) // tps, 0, tile(i) % tps, 0)),
                   pl.BlockSpec((None, d3, tm // d3, QKV_W), lambda i: (tile(i) // tps, 0, tile(i) % tps, 0)),
                   pl.BlockSpec((None, KVT_ROWS, tm),
                                lambda i: (tile(i) // tps, 0, jnp.maximum(tile(i) % tps - (tps - tail_tiles), 0)))]
                  + [_per_seq_spec(s) for s in sample_shape],
        out_shape=[jax.ShapeDtypeStruct((m, ACT_W), BF16),
                   jax.ShapeDtypeStruct((n, d2, t // d2, QKV_W), BF16),
                   jax.ShapeDtypeStruct((n, d3, t // d3, QKV_W), BF16),
                   jax.ShapeDtypeStruct((n, KVT_ROWS, tail), F32)] + sample_shape,
        scratch_shapes=[pltpu.VMEM(((N_PAT - 1) * YS_CHUNKS, tm, LANES), F32)],
        compiler_params=_cparams(("arbitrary",)),
        name="in_proj_mix",
    )(x2d, ln_g.reshape(1, d), w_bf, w_kvt, *seq_in, *consts)
    return outs[:4], _sample_results(outs[4:])


def _post_attn_kernel(oa_ref, o1_ref, o2_ref, o3_ref, l1_ref, l2_ref, l3_ref, ga_ref, gb_ref, x_ref,
                      wpa_ref, wpb_ref, wout_ref, ln2_ref, wr_ref, br_ref, *rest, n_extra):
    outs = rest[n_extra:n_extra + 4]
    h_ref, hn_ref, route_ref, cnt_ref = outs
    scr_ref = rest[n_extra + 4]
    if n_extra:
        own_tile = pl.program_id(0) < pl.num_programs(0) - 1

        @pl.when(jnp.logical_not(own_tile))
        def _():
            for dst, src in zip(outs, rest[:n_extra]):
                dst[...] = src[...]

        @pl.when(own_tile)
        def _():
            _post_attn_tile(oa_ref, o1_ref, o2_ref, o3_ref, l1_ref, l2_ref, l3_ref, ga_ref, gb_ref, x_ref,
                            wpa_ref, wpb_ref, wout_ref, ln2_ref, wr_ref, br_ref, *outs, scr_ref)
    else:
        _post_attn_tile(oa_ref, o1_ref, o2_ref, o3_ref, l1_ref, l2_ref, l3_ref, ga_ref, gb_ref, x_ref,
                        wpa_ref, wpb_ref, wout_ref, ln2_ref, wr_ref, br_ref, *outs, scr_ref)


def _post_attn_tile(oa_ref, o1_ref, o2_ref, o3_ref, l1_ref, l2_ref, l3_ref, ga_ref, gb_ref, x_ref,
                    wpa_ref, wpb_ref, wout_ref, ln2_ref, wr_ref, br_ref, h_ref, hn_ref, route_ref, cnt_ref, scr_ref):
    tm = x_ref.shape[0]
    chunks = PB_W // LANES

    def token_major(ref, slot):
        dil = ref.shape[0]
        if dil == 1:
            return ref[0].astype(F32)
        for c in range(chunks):
            for r in range(dil):
                scr_ref[slot * chunks + c, pl.ds(r, tm // dil, stride=dil), :] = (
                    ref[r, :, c * LANES:(c + 1) * LANES].astype(F32))
        return jnp.concatenate([scr_ref[slot * chunks + c] for c in range(chunks)], axis=1)

    o1, o2, o3 = (token_major(r, s) for s, r in enumerate((o1_ref, o2_ref, o3_ref)))
    l1, l2, l3 = (token_major(r, 3 + s) for s, r in enumerate((l1_ref, l2_ref, l3_ref)))
    m = jnp.maximum(jnp.maximum(l1, l2), l3)
    a1, a2, a3 = jnp.exp(l1 - m), jnp.exp(l2 - m), jnp.exp(l3 - m)
    ob = (a1 * o1 + a2 * o2 + a3 * o3) / (a1 + a2 + a3)
    ya = jnp.dot(oa_ref[...], wpa_ref[...], preferred_element_type=F32)
    yb = jnp.dot(ob.astype(BF16), wpb_ref[...], preferred_element_type=F32)
    merged = ga_ref[...].astype(F32) * ya + gb_ref[...].astype(F32) * yb
    h = x_ref[...] + jnp.dot(merged.astype(BF16), wout_ref[...], preferred_element_type=F32)
    h_ref[...] = h
    hn = _rms(h, ln2_ref[...])
    hn_ref[...] = hn.astype(BF16)

    hn_hi = hn.astype(BF16)
    hn_lo = (hn - hn_hi.astype(F32)).astype(BF16)
    logits = (jnp.dot(hn_hi, wr_ref[0], preferred_element_type=F32)
              + jnp.dot(hn_lo, wr_ref[0], preferred_element_type=F32)
              + jnp.dot(hn_hi, wr_ref[1], preferred_element_type=F32)) + br_ref[...]
    lane = lax.broadcasted_iota(jnp.int32, logits.shape, 1)
    is_grp = (lane >= N_EXPERTS) & (lane < N_EXPERTS + MOE_GROUPS)
    lg = jnp.where(is_grp, logits, NEG)
    gmax = jnp.max(lg, axis=-1, keepdims=True)
    g_lane = jnp.min(jnp.where(lg == gmax, lane, LANES), axis=-1, keepdims=True)
    p_g = 1.0 / jnp.sum(jnp.where(is_grp, jnp.exp(lg - gmax), 0.0), axis=-1, keepdims=True)
    e_lo = (g_lane - N_EXPERTS) * EXPERTS_PER_GROUP
    in_grp = (lane >= e_lo) & (lane < e_lo + EXPERTS_PER_GROUP)
    le = jnp.where(in_grp, logits, NEG)
    v1 = jnp.max(le, axis=-1, keepdims=True)
    i1 = jnp.min(jnp.where(le == v1, lane, LANES), axis=-1, keepdims=True)
    le2 = jnp.where(lane == i1, NEG, le)
    v2 = jnp.max(le2, axis=-1, keepdims=True)
    i2 = jnp.min(jnp.where(le2 == v2, lane, LANES), axis=-1, keepdims=True)
    e2 = jnp.exp(v2 - v1)
    w1 = p_g / (1.0 + e2)
    w2 = p_g * e2 / (1.0 + e2)
    route = jnp.where(lane == ROUTE_I1, i1.astype(F32), jnp.where(lane == ROUTE_I2, i2.astype(F32), 0.0))
    route_ref[...] = route + jnp.where(lane == ROUTE_W1, w1, 0.0) + jnp.where(lane == ROUTE_W2, w2, 0.0)
    picks = (lane == i1).astype(F32) + (lane == i2).astype(F32)
    cnt_ref[...] = jnp.broadcast_to(jnp.sum(picks, axis=0, keepdims=True), cnt_ref.shape)


def _post_attn(oa, obs, lses, gates_src, ga_blk, x2d, wpa, wpb, wout, ln2, wr, br, tm, tiles_per_seq, extra):
    m, d = x2d.shape
    tps = tiles_per_seq
    own = m // tm
    n_tiles = own + (extra is not None)
    mine = lambda i: jnp.minimum(i, own - 1)

    def tile(w, col=0):
        return pl.BlockSpec((tm, w), lambda i: (mine(i), col))

    def out_tile(rows, w):
        return pl.BlockSpec((rows, w), lambda i: (i, 0))

    def full(a):
        nd = a.ndim
        return pl.BlockSpec(a.shape, lambda i: (0,) * nd)

    def residue(a):
        dil = a.shape[1]
        return pl.BlockSpec((None, dil, tm // dil, PB_W), lambda i: (mine(i) // tps, 0, mine(i) % tps, 0))

    weights = [wpa, wpb, wout, ln2.reshape(1, d), wr, br]
    scratch = [pltpu.VMEM((6 * PB_W // LANES, tm, LANES), F32)]
    in_specs = ([tile(QA_W)] + [residue(a) for a in obs] + [residue(a) for a in lses]
                + [tile(d, ga_blk), tile(d, ga_blk + 1), tile(d)] + [full(w) for w in weights])
    args = [oa, *obs, *lses, gates_src, gates_src, x2d, *weights]
    if extra is not None:
        in_specs = in_specs + [full(a) for a in extra]
        args = args + list(extra)
    return pl.pallas_call(
        functools.partial(_post_attn_kernel, n_extra=0 if extra is None else len(extra)),
        grid=(n_tiles,),
        in_specs=in_specs,
        out_specs=[out_tile(tm, d), out_tile(tm, d), out_tile(tm, LANES), out_tile(SUBLANES, LANES)],
        out_shape=[jax.ShapeDtypeStruct((n_tiles * tm, d), F32),
                   jax.ShapeDtypeStruct((n_tiles * tm, d), BF16),
                   jax.ShapeDtypeStruct((n_tiles * tm, LANES), F32),
                   jax.ShapeDtypeStruct((n_tiles * SUBLANES, LANES), F32)],
        scratch_shapes=scratch,
        compiler_params=_cparams(("arbitrary",)),
        name="post_attn",
    )(*args)


def _route_plan(cnt_rows, n_tiles, n_xtiles):
    g = SUBLANES
    cnt = cnt_rows.reshape(n_tiles, g, LANES)[:, 0, :N_EXPERTS].astype(jnp.int32)
    cnt8 = (cnt + g - 1) // g * g
    loff = jnp.cumsum(cnt8, axis=1) - cnt8
    boff = jnp.cumsum(cnt8, axis=0) - cnt8
    tot = jnp.sum(cnt8, axis=0)
    region = (tot + MOE_TMX - 1) // MOE_TMX * MOE_TMX
    gend = jnp.cumsum(region)
    gbase = gend - region
    cum_tiles = gend // MOE_TMX
    j = jnp.arange(n_xtiles + MOE_SPILL_TILES, dtype=jnp.int32)
    loff_rows = jnp.zeros((n_tiles, g, LANES), F32).at[:, :, :N_EXPERTS].set(loff[:, None, :].astype(F32))
    k8 = g * jnp.arange(MOE_SLOTS // g, dtype=jnp.int32)
    run_end = loff + cnt8
    e_of_k = jnp.sum(k8[None, :, None] >= run_end[:, None, :], axis=2)
    shift = gbase[None, :] + boff - loff
    picked = jnp.sum(jnp.where(e_of_k[:, :, None] == jnp.arange(N_EXPERTS)[None, None, :], shift[:, None, :], 0), axis=2)
    return dict(
        chunk_dst=jnp.where(k8[None, :] < run_end[:, -1:], picked, n_xtiles * MOE_TMX).reshape(-1).astype(jnp.int32)
        + jnp.tile(k8, n_tiles),
        zst=gbase + tot, znch=(region - tot) // g,
        ztot=jnp.sum((region - tot) // g).reshape(1),
        xtile_expert=jnp.minimum(jnp.sum(j[:, None] >= cum_tiles[None, :], axis=1), N_EXPERTS - 1).astype(jnp.int32),
        n_used=cum_tiles[-1:].astype(jnp.int32),
        loff_rows=loff_rows.reshape(n_tiles * g, LANES))


def _local_slots(route, loff_row):
    tm = route.shape[0]
    lane = lax.broadcasted_iota(jnp.int32, (tm, LANES), 1)
    e1 = lane == route[:, ROUTE_I1:ROUTE_I1 + 1].astype(jnp.int32)
    e2 = lane == route[:, ROUTE_I2:ROUTE_I2 + 1].astype(jnp.int32)
    earlier = (lax.broadcasted_iota(jnp.int32, (tm, tm), 1) < lax.broadcasted_iota(jnp.int32, (tm, tm), 0))
    earlier = earlier.astype(BF16)
    c1 = jnp.dot(earlier, e1.astype(BF16), preferred_element_type=F32)
    c2 = jnp.dot(earlier, e2.astype(BF16), preferred_element_type=F32)
    cnt1 = jnp.sum(e1.astype(F32), axis=0, keepdims=True)
    pos1 = jnp.sum(jnp.where(e1, c1 + loff_row, 0.0), axis=1, keepdims=True)
    pos2 = jnp.sum(jnp.where(e2, c2 + cnt1 + loff_row, 0.0), axis=1, keepdims=True)
    slot = lax.broadcasted_iota(jnp.int32, (tm, MOE_SLOTS), 1)
    return slot == pos1.astype(jnp.int32), slot == pos2.astype(jnp.int32)


def _pack_bf16_pairs(x):
    c = x.shape[1] // 2
    bits = lambda v: lax.bitcast_convert_type(v.astype(BF16).astype(F32), jnp.uint32)
    return bits(x[:, :c]) | (bits(x[:, c:]) >> 16)


def _unpack_bf16_pairs(w):
    hi = lax.bitcast_convert_type(w & jnp.uint32(0xFFFF0000), F32).astype(BF16)
    lo = lax.bitcast_convert_type(w << 16, F32).astype(BF16)
    return jnp.concatenate([hi, lo], axis=1)


def _split3(w):
    hi = w.astype(BF16).astype(F32)
    mid = (w - hi).astype(BF16).astype(F32)
    return hi, mid, (w - hi - mid).astype(BF16).astype(F32)


def _moe_scatter_kernel(cdst_s, zst_s, znch_s, ztot_s, nused_s,
                        hn_ref, route_ref, loffv_ref, xs_hbm, buf_ref, zero_ref, sem, zsem, tsem):
    b = pl.program_id(0)
    nb = pl.num_programs(0)
    slot = b % 2
    g = SUBLANES
    d = hn_ref.shape[1]
    n_xtiles = xs_hbm.shape[0] // MOE_TMX - MOE_SPILL_TILES

    def run_copy(s, src_row, dst_row):
        return pltpu.make_async_copy(buf_ref.at[s, pl.ds(src_row, g)], xs_hbm.at[pl.ds(dst_row, g)], sem.at[s])

    def zero_copy(dst_row):
        return pltpu.make_async_copy(zero_ref.at[pl.ds(0, g)], xs_hbm.at[pl.ds(dst_row, g)], zsem)

    def zero_tile_copy(j):
        return pltpu.make_async_copy(zero_ref, xs_hbm.at[pl.ds(pl.multiple_of(j * MOE_TMX, MOE_TMX), MOE_TMX)], tsem)

    n_chunks = MOE_SLOTS // g
    spill = n_xtiles * MOE_TMX

    def wait_buffer(s):
        for _ in range(n_chunks):
            run_copy(s, 0, 0).wait()

    def ship_buffer(s, dst_row_of):
        for k in range(n_chunks):
            run_copy(s, k * g, pl.multiple_of(dst_row_of(k), g)).start()

    @pl.when(b == 0)
    def _():
        zero_ref[...] = jnp.zeros_like(zero_ref)
        buf_ref[1] = jnp.zeros(buf_ref.shape[1:], buf_ref.dtype)

        def per_expert(e, c):
            def per_chunk(k, c2):
                zero_copy(pl.multiple_of(zst_s[e] + k * g, g)).start()
                return c2
            return lax.fori_loop(0, znch_s[e], per_chunk, c)
        lax.fori_loop(0, N_EXPERTS, per_expert, 0)
        lax.fori_loop(nused_s[0], n_xtiles, lambda j, c: (zero_tile_copy(j).start(), c)[1], 0)

    @pl.when(b >= 1)
    def _():
        wait_buffer(slot)

    prev = jnp.maximum(b - 1, 0) * n_chunks
    ship_buffer(1 - slot, lambda k: jnp.where(b == 0, spill + k * g, cdst_s[prev + k]))

    route = route_ref[...]
    p1, p2 = _local_slots(route, loffv_ref[0:1, :])
    tn = (((0,), (0,)), ((), ()))
    picks = (p1 | p2).astype(BF16)
    buf_ref[slot, :, 0:d // 2] = _pack_bf16_pairs(lax.dot_general(picks, hn_ref[...], tn, preferred_element_type=F32))
    lane = lax.broadcasted_iota(jnp.int32, route.shape, 1)
    meta = jnp.zeros((MOE_SLOTS, LANES), F32)
    for p, col in ((p1, ROUTE_W1), (p2, ROUTE_W2)):
        w = route[:, col:col + 1]
        parts = _split3(w)
        wm = sum(jnp.where(lane == k, part, 0.0) for k, part in enumerate(parts))
        meta = meta + lax.dot_general(p.astype(BF16), wm.astype(BF16), tn, preferred_element_type=F32)
    buf_ref[slot, :, d // 2:d // 2 + LANES] = lax.bitcast_convert_type(meta, jnp.uint32)

    @pl.when(b == nb - 1)
    def _():
        wait_buffer(1 - slot)
        ship_buffer(slot, lambda k: cdst_s[b * n_chunks + k])
        wait_buffer(slot)
        lax.fori_loop(0, ztot_s[0], lambda _, c: (zero_copy(0).wait(), c)[1], 0)
        lax.fori_loop(nused_s[0], n_xtiles, lambda j, c: (zero_tile_copy(j).wait(), c)[1], 0)


def _moe_scatter(plan, hn_all, route_all, n_tiles, n_xtiles):
    d = hn_all.shape[1]
    tm = MOE_TM
    grid_spec = pltpu.PrefetchScalarGridSpec(
        num_scalar_prefetch=5,
        grid=(n_tiles,),
        in_specs=[pl.BlockSpec((tm, d), lambda i, *_: (i, 0)),
                  pl.BlockSpec((tm, LANES), lambda i, *_: (i, 0)),
                  pl.BlockSpec((SUBLANES, LANES), lambda i, *_: (i, 0))],
        out_specs=pl.BlockSpec(memory_space=pl.ANY),
        scratch_shapes=[pltpu.VMEM((2, MOE_SLOTS, d // 2 + LANES), jnp.uint32),
                        pltpu.VMEM((MOE_TMX, d // 2 + LANES), jnp.uint32),
                        pltpu.SemaphoreType.DMA((2,)),
                        pltpu.SemaphoreType.DMA(()),
                        pltpu.SemaphoreType.DMA(())])
    return pl.pallas_call(
        _moe_scatter_kernel,
        grid_spec=grid_spec,
        out_shape=jax.ShapeDtypeStruct(((n_xtiles + MOE_SPILL_TILES) * MOE_TMX, d // 2 + LANES), jnp.uint32),
        compiler_params=_cparams(("arbitrary",)),
        name="moe_scatter",
    )(plan["chunk_dst"], plan["zst"], plan["znch"], plan["ztot"], plan["n_used"],
      hn_all, route_all, plan["loff_rows"])


def _moe_experts_kernel(xe_s, nused_s, x_ref, w1_ref, w3_ref, w2_ref, y_ref, w1b_ref, w3b_ref, w2b_ref):
    j = pl.program_id(0)
    d = w1_ref.shape[0]

    @pl.when(j < nused_s[0])
    def _():
        @pl.when((j == 0) | (xe_s[j] != xe_s[jnp.maximum(j - 1, 0)]))
        def _():
            w1b_ref[...] = w1_ref[...].astype(BF16)
            w3b_ref[...] = w3_ref[...].astype(BF16)
            w2b_ref[...] = w2_ref[...].astype(BF16)

        x = _unpack_bf16_pairs(x_ref[:, 0:d // 2])
        gate = jnp.sum(lax.bitcast_convert_type(x_ref[:, d // 2:d // 2 + LANES], F32), axis=1, keepdims=True)
        a = jnp.dot(x, w1b_ref[...], preferred_element_type=F32)
        b = jnp.dot(x, w3b_ref[...], preferred_element_type=F32)
        hh = (a * _sigmoid(a)) * b * gate
        y_ref[...] = _pack_bf16_pairs(jnp.dot(hh.astype(BF16), w2b_ref[...], preferred_element_type=F32))

    @pl.when(j >= nused_s[0])
    def _():
        y_ref[...] = jnp.zeros_like(y_ref)


def _moe_experts(plan, xs, w1, w3, w2, n_xtiles):
    ne, d, f = w1.shape
    last = lambda j, xe, nu: jnp.maximum(jnp.minimum(j, nu[0] - 1), 0)
    grid_spec = pltpu.PrefetchScalarGridSpec(
        num_scalar_prefetch=2,
        grid=(n_xtiles + MOE_SPILL_TILES,),
        in_specs=[pl.BlockSpec((MOE_TMX, d // 2 + LANES), lambda j, xe, nu: (last(j, xe, nu), 0)),
                  pl.BlockSpec((None, d, f), lambda j, xe, nu: (xe[j], 0, 0)),
                  pl.BlockSpec((None, d, f), lambda j, xe, nu: (xe[j], 0, 0)),
                  pl.BlockSpec((None, f, d), lambda j, xe, nu: (xe[j], 0, 0))],
        out_specs=pl.BlockSpec((MOE_TMX, d // 2), lambda j, xe, nu: (j, 0)),
        scratch_shapes=[pltpu.VMEM((d, f), BF16), pltpu.VMEM((d, f), BF16), pltpu.VMEM((f, d), BF16)])
    return pl.pallas_call(
        _moe_experts_kernel,
        grid_spec=grid_spec,
        out_shape=jax.ShapeDtypeStruct(((n_xtiles + MOE_SPILL_TILES) * MOE_TMX, d // 2), jnp.uint32),
        compiler_params=_cparams(("arbitrary",)),
        name="moe_experts",
    )(plan["xtile_expert"], plan["n_used"], xs, w1, w3, w2)


def _moe_combine_kernel(cdst_s, ys_hbm, route_ref, loffv_ref, h_ref, lnf_ref, y_ref,
                        buf_ref, sem, *, tile0, final_norm):
    b = pl.program_id(0)
    nb = pl.num_programs(0)
    slot = b % 2
    g = SUBLANES
    n_chunks = MOE_SLOTS // g
    tile = b + tile0

    def run_copy(s, src_row, dst_row):
        return pltpu.make_async_copy(ys_hbm.at[pl.ds(src_row, g)], buf_ref.at[s, pl.ds(dst_row, g)], sem.at[s])

    def fetch(t, s):
        for k in range(n_chunks):
            run_copy(s, pl.multiple_of(cdst_s[t * n_chunks + k], g), k * g).start()

    def wait_buffer(s):
        for _ in range(n_chunks):
            run_copy(s, 0, 0).wait()

    @pl.when(b == 0)
    def _():
        fetch(tile, slot)

    fetch(jnp.minimum(tile + 1, tile0 + nb - 1), 1 - slot)
    wait_buffer(slot)

    @pl.when(b == nb - 1)
    def _():
        wait_buffer(1 - slot)

    p1, p2 = _local_slots(route_ref[...], loffv_ref[0:1, :])
    picks = (p1 | p2).astype(BF16)
    y = h_ref[...] + jnp.dot(picks, _unpack_bf16_pairs(buf_ref[slot]), preferred_element_type=F32)
    if final_norm:
        y = _rms(y, lnf_ref[...])
    y_ref[...] = y


def _moe_combine(plan, ys, route_all, h_all, lnf, tile0, n_tiles, final_norm):
    d = h_all.shape[1]
    tm = MOE_TM
    grid_spec = pltpu.PrefetchScalarGridSpec(
        num_scalar_prefetch=1,
        grid=(n_tiles,),
        in_specs=[pl.BlockSpec(memory_space=pl.ANY),
                  pl.BlockSpec((tm, LANES), lambda i, *_: (i + tile0, 0)),
                  pl.BlockSpec((SUBLANES, LANES), lambda i, *_: (i + tile0, 0)),
                  pl.BlockSpec((tm, d), lambda i, *_: (i + tile0, 0)),
                  pl.BlockSpec((1, d), lambda i, *_: (0, 0))],
        out_specs=pl.BlockSpec((tm, d), lambda i, *_: (i, 0)),
        scratch_shapes=[pltpu.VMEM((2, MOE_SLOTS, d // 2), jnp.uint32), pltpu.SemaphoreType.DMA((2,))])
    return pl.pallas_call(
        functools.partial(_moe_combine_kernel, tile0=tile0, final_norm=final_norm),
        grid_spec=grid_spec,
        out_shape=jax.ShapeDtypeStruct((n_tiles * tm, d), F32),
        compiler_params=_cparams(("arbitrary",)),
        name="moe_combine",
    )(plan["chunk_dst"], ys, route_all, plan["loff_rows"], h_all, lnf.reshape(1, d))


def _prompt_states(kvt, n, t):
    hd = HEAD_DIM
    tail = kvt.shape[2]

    def state(k_lo, v_lo, rows, heads, win):
        w = min(win, t)
        kv = jnp.stack([kvt[:, k_lo:k_lo + rows, tail - w:], kvt[:, v_lo:v_lo + rows, tail - w:]], axis=1)
        return jnp.transpose(kv.reshape(n, 2, heads, hd, w), (0, 4, 1, 2, 3))

    out = [state(0, KA_W, KA_W, A_KV_HEADS, A_WINDOW)]
    for p, (win, _) in enumerate(B_PATTERNS):
        out.append(state(2 * KA_W + p * PB_W, 2 * KA_W + QB_W + p * PB_W, PB_W, B_HEADS_PER_PATTERN, win))
    return out


def _layer(xp, xs, caches, rel_bias, ln1, w_in, sinks, w_pa, w_pb, w_out, ln2, w_rg, b_rg, w_re, b_re,
           w1, w3, w2, lnf, final_norm):
    n, t, d = xp.shape
    ns = xs.shape[0]
    assert xs.shape[1] == 1 and OFF_GA + 2 * d == w_in.shape[1] and P_QA == 2 * d
    w_bf = w_in.astype(BF16)
    w_kvt = _kv_weights_transposed(w_in)
    wpa, wpb, wout = w_pa.astype(BF16), w_pb.astype(BF16), w_out.astype(BF16)
    wr = jnp.zeros((d, LANES), F32).at[:, :N_EXPERTS].set(w_re).at[:, N_EXPERTS:N_EXPERTS + MOE_GROUPS].set(w_rg)
    br = jnp.zeros((1, LANES), F32).at[0, :N_EXPERTS].set(b_re).at[0, N_EXPERTS:N_EXPERTS + MOE_GROUPS].set(b_rg)
    wr_hi = wr.astype(BF16)
    wr = jnp.stack([wr_hi, (wr - wr_hi.astype(F32)).astype(BF16)])

    tm = MOE_TM
    xp2 = xp.reshape(n * t, d)
    xs2 = xs.reshape(ns, d)
    proj_s = _in_proj_sample(xs2, ln1, w_bf)
    if ns == 2 * (n * t // MIX_TM):
        (act, qkv2, qkv3, kvt), sampled = _in_proj_mix(xp2, ln1, w_bf, w_kvt, n, t, MIX_TM, proj_s, caches,
                                                       rel_bias, sinks)
    else:
        act, qkv2, qkv3, kvt = _in_proj_prompt(xp2, ln1, w_bf, w_kvt, n, t, IN_PROJ_TM)
        sampled = _sample_mix(proj_s, caches, rel_bias, sinks)
    oa_s, ob_s, lse_s, st_s = sampled
    act4 = act.reshape(n, 1, t, ACT_W)
    bias_a = _band_bias(rel_bias[:, :A_HEADS], A_WINDOW - 1, 1)
    (oa,) = _band_attn(act4, bias_a, sinks, q_off=P_QA, k_off=P_KA, v_off=P_VA,
                       kv_heads=A_KV_HEADS, grp=A_GROUP, want_lse=False)
    obs, lses = [], []
    for p, (win, dil) in enumerate(B_PATTERNS):
        lo = A_HEADS + p * B_HEADS_PER_PATTERN
        bias_p = _band_bias(rel_bias[:, lo:lo + B_HEADS_PER_PATTERN], win // dil, dil)
        src, base = ((act4, P_B), (qkv2, 0), (qkv3, 0))[p]
        o, lse = _band_attn(src, bias_p, None, q_off=base, k_off=base + PB_W, v_off=base + 2 * PB_W,
                            kv_heads=B_HEADS_PER_PATTERN, grp=1, want_lse=True)
        obs.append(o)
        lses.append(lse)
    assert ns <= MOE_TM
    p_tiles = n * t // MOE_TM
    n_tiles = p_tiles + 1
    m_all = n_tiles * MOE_TM
    st_p = _prompt_states(kvt, n, t)

    rows = lambda a: jnp.pad(a, ((0, MOE_TM - ns), (0, 0)))
    obs_s = [rows(ob_s[:, p].astype(BF16)).reshape(1, 1, MOE_TM, PB_W) for p in range(N_PAT)]
    lses_s = [rows(jnp.repeat(lse_s[:, p], HEAD_DIM, axis=-1)).reshape(1, 1, MOE_TM, PB_W) for p in range(N_PAT)]
    gates_s = rows(proj_s[:, OFF_GA:].astype(BF16))
    routed_s = _post_attn(rows(oa_s.astype(BF16)), obs_s, lses_s, gates_s, 0, rows(xs2),
                          wpa, wpb, wout, ln2, wr, br, MOE_TM, 1, None)
    h_all, hn_all, route_all, cnt_all = _post_attn(oa.reshape(n * t, QA_W), obs, lses, act, 0, xp2,
                                                   wpa, wpb, wout, ln2, wr, br, tm, t // tm, routed_s)

    max_rows = 2 * m_all + n_tiles * N_EXPERTS * (SUBLANES - 1) + N_EXPERTS * (MOE_TMX - SUBLANES)
    n_xtiles = -(-max_rows // MOE_TMX)
    plan = _route_plan(cnt_all, n_tiles, n_xtiles)
    xs_sorted = _moe_scatter(plan, hn_all, route_all, n_tiles, n_xtiles)
    ys_sorted = _moe_experts(plan, xs_sorted, w1, w3, w2, n_xtiles)
    yp = _moe_combine(plan, ys_sorted, route_all, h_all, lnf, 0, p_tiles, final_norm).reshape(n, t, d)
    ys = _moe_combine(plan, ys_sorted, route_all, h_all, lnf, p_tiles, 1, final_norm)[:ns].reshape(ns, 1, d)
    return yp, ys, st_p, st_s


def kernel(x_prompt, x_sample, cache_a_kv, cache_b1_kv, cache_b2_kv, cache_b3_kv, rel_bias, ln1_g, w_in, sinks,
           w_pa, w_pb, w_out, ln2_g, w_rg, b_rg, w_re, b_re, w1, w3, w2, lnf_g):
    depth = w_in.shape[0]
    assert depth >= 1
    xp, xs = x_prompt, x_sample
    new_p = [[] for _ in range(4)]
    new_s = [[] for _ in range(4)]
    for l in range(depth):
        caches = (cache_a_kv[l], cache_b1_kv[l], cache_b2_kv[l], cache_b3_kv[l])
        xp, xs, st_p, st_s = _layer(xp, xs, caches, rel_bias, ln1_g[l], w_in[l], sinks[l], w_pa[l], w_pb[l],
                                    w_out[l], ln2_g[l], w_rg[l], b_rg[l], w_re[l], b_re[l], w1[l], w3[l], w2[l],
                                    lnf_g, l == depth - 1)
        for i in range(4):
            new_p[i].append(st_p[i])
            new_s[i].append(st_s[i])
    a_p, b1_p, b2_p, b3_p = [jnp.stack(v) for v in new_p]
    a_s, b1_s, b2_s, b3_s = [jnp.stack(v) for v in new_s]
    return (xp, xs, a_p, a_s, b1_p, b1_s, b2_p, b2_s, b3_p, b3_s)
```

```python
import functools
import math

import numpy as np
import jax
import jax.numpy as jnp
from jax import lax
from jax.experimental import pallas as pl
from jax.experimental.pallas import tpu as pltpu

F32 = jnp.float32
BF16 = jnp.bfloat16

HEAD_DIM = 64
A_HEADS = 8
A_KV_HEADS = 2
A_GROUP = A_HEADS // A_KV_HEADS
A_WINDOW = 128
B_PATTERNS = ((128, 1), (512, 4), (2048, 16))
N_PAT = len(B_PATTERNS)
B_HEADS_PER_PATTERN = 4
B_HEADS = B_HEADS_PER_PATTERN * N_PAT
BLOCK = 128
ATTN_CHAINS = 16
NUM_BUCKETS = 32
MAX_DISTANCE = 2048
MOE_GROUPS = 4
EXPERTS_PER_GROUP = 8
N_EXPERTS = MOE_GROUPS * EXPERTS_PER_GROUP
EPS = 1e-6
NEG = -1e30
LANES = 128
SUBLANES = 8
ROUTE_I1, ROUTE_I2, ROUTE_W1, ROUTE_W2 = 0, 1, 2, 3
IN_PROJ_TM = 512
MIX_TM = 256
MOE_TM = 256
MOE_SLOTS = 2 * MOE_TM + N_EXPERTS * SUBLANES
MOE_TMX = 256
MOE_SPILL_TILES = -(-MOE_SLOTS // MOE_TMX)
Q_SCALE = HEAD_DIM ** -0.5

QA_W = A_HEADS * HEAD_DIM
KA_W = A_KV_HEADS * HEAD_DIM
QB_W = B_HEADS * HEAD_DIM
PB_W = B_HEADS_PER_PATTERN * HEAD_DIM
QKV_W = 3 * PB_W
OFF_KA = QA_W
OFF_QB = OFF_KA + 2 * KA_W
OFF_KB = OFF_QB + QB_W
OFF_VB = OFF_KB + QB_W
OFF_GA = OFF_VB + QB_W
P_QA = 2048
P_KA = P_QA + QA_W
P_VA = P_KA + KA_W
P_B = P_VA + KA_W
ACT_W = P_B + QKV_W
KVT_ROWS = 2 * KA_W + 2 * QB_W

VMEM_LIMIT = 56 * 1024 * 1024


def _cparams(sem):
    return pltpu.CompilerParams(dimension_semantics=sem, vmem_limit_bytes=VMEM_LIMIT)


def _bucket_np(dist):
    dist = np.asarray(dist, np.int64)
    max_exact = NUM_BUCKETS // 2
    df = np.maximum(dist, max_exact).astype(np.float64)
    large = max_exact + (np.log(df / max_exact) / math.log(MAX_DISTANCE / max_exact)
                         * (NUM_BUCKETS - max_exact)).astype(np.int64)
    return np.where(dist < max_exact, dist, np.minimum(large, NUM_BUCKETS - 1))


def _table_rows(table_cols, dist, valid):
    onehot = (_bucket_np(dist)[:, None] == np.arange(NUM_BUCKETS)[None, :]).astype(np.float32)
    rows = jnp.einsum("ck,kh->hc", jnp.asarray(onehot), table_cols.astype(F32), precision=lax.Precision.HIGHEST)
    return jnp.where(jnp.asarray(valid)[None, :], rows, NEG)


def _band_bias(table_cols, max_dist, dilation):
    period = 3 * BLOCK
    m = np.arange(period)
    k = np.where(m < 2 * BLOCK, m, m - period)
    dist = BLOCK - k
    valid = (dist >= 0) & (dist <= max_dist) & (m != 2 * BLOCK)
    v = _table_rows(table_cols, np.clip(dist, 0, None) * dilation, valid)
    heads = v.shape[0]
    flat = jnp.tile(v, (1, BLOCK))[:, :BLOCK * (period - 1)]
    return flat.reshape(heads, BLOCK, period - 1)[:, :, :2 * BLOCK]


def _decode_bias(table_cols, width, dilation, first_valid):
    c = np.arange(width)
    valid = (c % dilation == 0) & (c >= first_valid)
    rows = _table_rows(table_cols, width - c, valid)
    self_bias = _table_rows(table_cols, np.zeros((1,), np.int64), np.ones((1,), bool))
    return rows[:, None, :], self_bias[:, None, :]


def _rms(x, g):
    return (x * lax.rsqrt(jnp.mean(x * x, axis=-1, keepdims=True) + EPS)) * g


def _sigmoid(x):
    return 1.0 / (1.0 + jnp.exp(-x))


YS_CHUNKS = QKV_W // LANES


def _in_proj_work(x_ref, g_ref, w_ref, act_ref, qkv2_ref, qkv3_ref, ys_ref, phase):
    tm = x_ref.shape[0]
    cache = {}

    def xb():
        if "xb" not in cache:
            cache["xb"] = _rms(x_ref[...], g_ref[...]).astype(BF16)
        return cache["xb"]

    def proj(lo, hi):
        return jnp.dot(xb(), w_ref[:, lo:hi], preferred_element_type=F32)

    def gates(c):
        act_ref[:, c:c + 512] = _sigmoid(proj(OFF_GA + c, OFF_GA + c + 512)).astype(BF16)

    def mixer_a_q():
        act_ref[:, P_QA:P_KA] = (proj(0, OFF_KA) * Q_SCALE).astype(BF16)

    def mixer_a_kv():
        act_ref[:, P_KA:P_B] = proj(OFF_KA, OFF_QB).astype(BF16)

    def pattern_part(p, j):
        lo = (OFF_QB, OFF_KB, OFF_VB)[j] + p * PB_W
        part = proj(lo, lo + PB_W)
        if j == 0:
            part = part * Q_SCALE
        if p == 0:
            act_ref[:, P_B + j * PB_W:P_B + (j + 1) * PB_W] = part.astype(BF16)
        else:
            for c in range(PB_W // LANES):
                ys_ref[(p - 1) * YS_CHUNKS + j * (PB_W // LANES) + c] = part[:, c * LANES:(c + 1) * LANES]

    def regroup(p, out_ref):
        dil = B_PATTERNS[p][1]
        for c in range(YS_CHUNKS):
            for r in range(dil):
                out_ref[r, :, c * LANES:(c + 1) * LANES] = (
                    ys_ref[(p - 1) * YS_CHUNKS + c, pl.ds(r, tm // dil, stride=dil), :].astype(BF16))

    work = []
    if phase in (None, 0):
        work += [functools.partial(gates, c) for c in range(0, P_QA, 512)] + [mixer_a_q, mixer_a_kv]
    if phase in (None, 1):
        for p in range(N_PAT):
            work += [functools.partial(pattern_part, p, j) for j in range(3)]
            if p > 0:
                work.append(functools.partial(regroup, p, (qkv2_ref, qkv3_ref)[p - 1]))
    return work


def _transpose_cast_kernel(w_ref, o_ref):
    o_ref[...] = w_ref[...].T.astype(o_ref.dtype)


def _cast_kernel(w_ref, o_ref):
    o_ref[...] = w_ref[...].astype(o_ref.dtype)


def _cast_bf16(w, blk=512):
    r, c = w.shape
    assert c % blk == 0
    return pl.pallas_call(
        _cast_kernel,
        grid=(c // blk,),
        in_specs=[pl.BlockSpec((r, blk), lambda i: (0, i))],
        out_specs=pl.BlockSpec((r, blk), lambda i: (0, i)),
        out_shape=jax.ShapeDtypeStruct((r, c), BF16),
        compiler_params=_cparams(("arbitrary",)),
        name="cast_bf16",
    )(w)


def _kv_weights_transposed(w_in):
    d = w_in.shape[0]
    blk = 2 * KA_W
    assert OFF_KA % blk == 0 and OFF_KB % blk == 0 and KVT_ROWS % blk == 0
    first, rest = OFF_KA // blk, OFF_KB // blk - 1
    return pl.pallas_call(
        _transpose_cast_kernel,
        grid=(KVT_ROWS // blk,),
        in_specs=[pl.BlockSpec((d, blk), lambda i: (0, jnp.where(i == 0, first, rest + i)))],
        out_specs=pl.BlockSpec((blk, d), lambda i: (i, 0)),
        out_shape=jax.ShapeDtypeStruct((KVT_ROWS, d), BF16),
        compiler_params=_cparams(("arbitrary",)),
        name="kv_weights_t",
    )(w_in)


def _kv_tail(x_ref, g_ref, wkvt_ref, kvt_ref):
    xb = _rms(x_ref[...], g_ref[...]).astype(BF16)
    kvt_ref[...] = lax.dot_general(wkvt_ref[...], xb, (((1,), (1,)), ((), ())), preferred_element_type=F32)


def _in_proj_kernel(x_ref, g_ref, w_ref, wkvt_ref, act_ref, qkv2_ref, qkv3_ref, kvt_ref, ys_ref, *,
                    tiles_per_seq, tail_tiles):
    for piece in _in_proj_work(x_ref, g_ref, w_ref, act_ref, qkv2_ref, qkv3_ref, ys_ref, None):
        piece()

    @pl.when(pl.program_id(0) % tiles_per_seq >= tiles_per_seq - tail_tiles)
    def _():
        _kv_tail(x_ref, g_ref, wkvt_ref, kvt_ref)


def _in_proj_prompt(x2d, ln_g, w_bf, w_kvt, n, t, tm):
    m, d = x2d.shape
    tps = t // tm
    tail = min(max(w for w, _ in B_PATTERNS), t)
    assert t % tm == 0 and tail % tm == 0 and all(tm % (16 * dl) == 0 for _, dl in B_PATTERNS)
    tail_tiles = tail // tm
    d2, d3 = B_PATTERNS[1][1], B_PATTERNS[2][1]
    return pl.pallas_call(
        functools.partial(_in_proj_kernel, tiles_per_seq=tps, tail_tiles=tail_tiles),
        grid=(m // tm,),
        in_specs=[pl.BlockSpec((tm, d), lambda i: (i, 0)),
                  pl.BlockSpec((1, d), lambda i: (0, 0)),
                  pl.BlockSpec(w_bf.shape, lambda i: (0, 0), pipeline_mode=pl.Buffered(1)),
                  pl.BlockSpec(w_kvt.shape, lambda i: (0, 0), pipeline_mode=pl.Buffered(1))],
        out_specs=[pl.BlockSpec((tm, ACT_W), lambda i: (i, 0)),
                   pl.BlockSpec((None, d2, tm // d2, QKV_W), lambda i: (i // tps, 0, i % tps, 0)),
                   pl.BlockSpec((None, d3, tm // d3, QKV_W), lambda i: (i // tps, 0, i % tps, 0)),
                   pl.BlockSpec((None, KVT_ROWS, tm),
                                lambda i: (i // tps, 0, jnp.maximum(i % tps - (tps - tail_tiles), 0)))],
        out_shape=[jax.ShapeDtypeStruct((m, ACT_W), BF16),
                   jax.ShapeDtypeStruct((n, d2, t // d2, QKV_W), BF16),
                   jax.ShapeDtypeStruct((n, d3, t // d3, QKV_W), BF16),
                   jax.ShapeDtypeStruct((n, KVT_ROWS, tail), F32)],
        scratch_shapes=[pltpu.VMEM(((N_PAT - 1) * YS_CHUNKS, tm, LANES), F32)],
        compiler_params=_cparams(("arbitrary",)),
        name="in_proj",
    )(x2d, ln_g.reshape(1, d), w_bf, w_kvt)


def _in_proj_sample_kernel(x_ref, g_ref, w_ref, y_ref):
    xb = _rms(x_ref[...], g_ref[...]).astype(BF16)
    y_ref[:, :OFF_GA] = jnp.dot(xb, w_ref[:, :OFF_GA], preferred_element_type=F32)
    for lo, hi in ((0, OFF_KA), (OFF_QB, OFF_KB)):
        y_ref[:, lo:hi] = y_ref[:, lo:hi] * Q_SCALE
    for lo in range(OFF_GA, w_ref.shape[1], 1024):
        y_ref[:, lo:lo + 1024] = _sigmoid(jnp.dot(xb, w_ref[:, lo:lo + 1024], preferred_element_type=F32))


def _in_proj_sample(x2d, ln_g, w_bf):
    m, d = x2d.shape
    return pl.pallas_call(
        _in_proj_sample_kernel,
        grid=(1,),
        in_specs=[pl.BlockSpec((m, d), lambda i: (0, 0)),
                  pl.BlockSpec((1, d), lambda i: (0, 0)),
                  pl.BlockSpec(w_bf.shape, lambda i: (0, 0))],
        out_specs=pl.BlockSpec((m, w_bf.shape[1]), lambda i: (0, 0)),
        out_shape=jax.ShapeDtypeStruct((m, w_bf.shape[1]), F32),
        compiler_params=_cparams(("arbitrary",)),
        name="in_proj_sample",
    )(x2d, ln_g.reshape(1, d), w_bf)


def _band_attn_kernel(*refs, kv_heads, grp, has_sink, want_lse):
    if has_sink:
        sink_ref, refs = refs[0], refs[1:]
    q_ref, kp_ref, kc_ref, vp_ref, vc_ref, bias_ref, o_ref = refs[:7]
    lse_ref = refs[7] if want_lse else None
    step = pl.program_id(2)
    hd = HEAD_DIM
    nt = (((1,), (1,)), ((), ()))
    chains = [(sub, kv * grp + g, slice(kv * hd, (kv + 1) * hd))
              for sub in range(q_ref.shape[0] // BLOCK) for kv in range(kv_heads) for g in range(grp)]
    scores = []
    for sub, h, ks in chains:
        rows = slice(sub * BLOCK, (sub + 1) * BLOCK)
        q = q_ref[rows, h * hd:(h + 1) * hd]
        kp = kp_ref[:, ks] if sub == 0 else kc_ref[(sub - 1) * BLOCK:sub * BLOCK, ks]
        sp = lax.dot_general(q, kp, nt, preferred_element_type=F32) + bias_ref[h, :, 0:BLOCK]
        sc = lax.dot_general(q, kc_ref[rows, ks], nt, preferred_element_type=F32) + bias_ref[h, :, BLOCK:2 * BLOCK]
        if sub == 0:
            sp = jnp.where(step > 0, sp, NEG)
        scores.append((sp, sc))
    maxes = []
    for (sub, h, ks), (sp, sc) in zip(chains, scores):
        m = jnp.maximum(jnp.max(sp, axis=-1, keepdims=True), jnp.max(sc, axis=-1, keepdims=True))
        maxes.append(jnp.maximum(m, sink_ref[h]) if has_sink else m)
    probs = []
    for (sub, h, ks), (sp, sc), m in zip(chains, scores, maxes):
        pp, pc = jnp.exp(sp - m), jnp.exp(sc - m)
        den = jnp.sum(pp, axis=-1, keepdims=True) + jnp.sum(pc, axis=-1, keepdims=True)
        if has_sink:
            den = den + jnp.exp(sink_ref[h] - m)
        probs.append((pp.astype(BF16), pc.astype(BF16), den))
    for (sub, h, ks), (pp, pc, den), m in zip(chains, probs, maxes):
        rows = slice(sub * BLOCK, (sub + 1) * BLOCK)
        vp = vp_ref[:, ks] if sub == 0 else vc_ref[(sub - 1) * BLOCK:sub * BLOCK, ks]
        o = (jnp.dot(pp, vp, preferred_element_type=F32)
             + jnp.dot(pc, vc_ref[rows, ks], preferred_element_type=F32))
        o_ref[rows, h * hd:(h + 1) * hd] = (o / den).astype(o_ref.dtype)
        if want_lse:
            lse_ref[rows, h * hd:(h + 1) * hd] = jnp.broadcast_to(m + jnp.log(den), (BLOCK, hd))


def _band_attn(src, bias, sink, *, q_off, k_off, v_off, kv_heads, grp, want_lse):
    n, dil, l, cols = src.shape
    sub = max(1, ATTN_CHAINS // (kv_heads * grp))
    while l % (sub * BLOCK):
        sub //= 2
    rows = sub * BLOCK
    assert sub >= 1 and l % rows == 0
    nb = l // rows
    qw = kv_heads * grp * HEAD_DIM
    kw = kv_heads * HEAD_DIM
    assert q_off % qw == 0 and k_off % kw == 0 and v_off % kw == 0
    qb, kb, vb = q_off // qw, k_off // kw, v_off // kw
    prev = lambda b: jnp.maximum(sub * b - 1, 0)
    in_specs = [
        pl.BlockSpec((None, None, rows, qw), lambda i, r, b: (i, r, b, qb)),
        pl.BlockSpec((None, None, BLOCK, kw), lambda i, r, b: (i, r, prev(b), kb)),
        pl.BlockSpec((None, None, rows, kw), lambda i, r, b: (i, r, b, kb)),
        pl.BlockSpec((None, None, BLOCK, kw), lambda i, r, b: (i, r, prev(b), vb)),
        pl.BlockSpec((None, None, rows, kw), lambda i, r, b: (i, r, b, vb)),
        pl.BlockSpec(bias.shape, lambda i, r, b: (0, 0, 0)),
    ]
    args = [src, src, src, src, src, bias]
    has_sink = sink is not None
    if has_sink:
        in_specs = [pl.BlockSpec(memory_space=pltpu.SMEM)] + in_specs
        args = [sink.astype(F32)] + args
    out_specs = [pl.BlockSpec((None, None, rows, qw), lambda i, r, b: (i, r, b, 0))]
    out_shape = [jax.ShapeDtypeStruct((n, dil, l, qw), BF16)]
    if want_lse:
        out_specs.append(pl.BlockSpec((None, None, rows, qw), lambda i, r, b: (i, r, b, 0)))
        out_shape.append(jax.ShapeDtypeStruct((n, dil, l, qw), F32))
    return pl.pallas_call(
        functools.partial(_band_attn_kernel, kv_heads=kv_heads, grp=grp, has_sink=has_sink, want_lse=want_lse),
        grid=(n, dil, nb),
        in_specs=in_specs,
        out_specs=out_specs,
        out_shape=out_shape,
        compiler_params=_cparams(("arbitrary", "arbitrary", "arbitrary")),
        name=f"band_attn_d{dil}",
    )(*args)


COL_QA = 0
COL_QB = COL_QA + A_HEADS
COL_KA = COL_QB + B_HEADS
COL_VA = COL_KA + A_KV_HEADS
COL_KB = COL_VA + A_KV_HEADS
COL_VB = COL_KB + B_HEADS
N_COLS = COL_VB + B_HEADS


def _attend_cached(kt, vt, q, k_new, v_new, bias, self_bias, sink, write):
    s = jnp.sum(kt * q, axis=1, keepdims=True) + bias
    s_new = jnp.sum(k_new * q, axis=1, keepdims=True) + self_bias
    yield
    m = jnp.maximum(jnp.max(s, axis=2, keepdims=True), s_new)
    if sink is not None:
        m = jnp.maximum(m, sink)
    yield
    p = jnp.exp(s - m)
    p_new = jnp.exp(s_new - m)
    den = jnp.sum(p, axis=2, keepdims=True) + p_new
    if sink is not None:
        den = den + jnp.exp(sink - m)
    yield
    o = (jnp.sum(vt * p, axis=2, keepdims=True) + v_new * p_new) / den
    write(o, m + jnp.log(den))
    yield


def _sample_work(cols_ref, ca_ref, c1_ref, c2_ref, c3_ref, ba_ref, sa_ref, sink_ref, b1_ref, b2_ref, b3_ref,
                 sb_ref, o_ref, lse_ref, ra_ref, r1_ref, r2_ref, r3_ref):
    cols = cols_ref[...]

    def stack(js):
        return jnp.stack([cols[:, j:j + 1] for j in js])

    o_ref[...] = jnp.zeros_like(o_ref)
    lse_ref[...] = jnp.zeros_like(lse_ref)

    def write_a(o, _):
        for h in range(A_HEADS):
            o_ref[:, COL_QA + h:COL_QA + h + 1] = o[h]

    def write_b(p):
        def write(o, lse):
            for h in range(B_HEADS_PER_PATTERN):
                j = p * B_HEADS_PER_PATTERN + h
                o_ref[:, COL_QB + j:COL_QB + j + 1] = o[h]
                lse_ref[0:1, j:j + 1] = lse[h]
        return write

    kv_of = [h // A_GROUP for h in range(A_HEADS)]
    stages = [_attend_cached(jnp.stack([ca_ref[0, kv] for kv in kv_of]), jnp.stack([ca_ref[1, kv] for kv in kv_of]),
                             stack(range(COL_QA, COL_QA + A_HEADS)), stack([COL_KA + kv for kv in kv_of]),
                             stack([COL_VA + kv for kv in kv_of]), ba_ref[...], sa_ref[...], sink_ref[...], write_a)]
    for p, (c_ref, b_ref) in enumerate(((c1_ref, b1_ref), (c2_ref, b2_ref), (c3_ref, b3_ref))):
        js = range(p * B_HEADS_PER_PATTERN, (p + 1) * B_HEADS_PER_PATTERN)
        stages.append(_attend_cached(c_ref[0], c_ref[1], stack([COL_QB + j for j in js]),
                                     stack([COL_KB + j for j in js]), stack([COL_VB + j for j in js]),
                                     b_ref[...], sb_ref[p], None, write_b(p)))

    def roll_plane(c_ref, r_ref, i, h, new_col):
        w = c_ref.shape[-1]
        x = c_ref[i, h]
        lane = lax.broadcasted_iota(jnp.int32, x.shape, 1)
        r_ref[i, h] = jnp.where(lane == w - 1, cols[:, new_col:new_col + 1], pltpu.roll(x, w - 1, 1))

    planes = []
    for ci, (c_ref, r_ref) in enumerate(((ca_ref, ra_ref), (c1_ref, r1_ref), (c2_ref, r2_ref), (c3_ref, r3_ref))):
        for i, (first_a, first_b) in enumerate(((COL_KA, COL_KB), (COL_VA, COL_VB))):
            for h in range(c_ref.shape[1]):
                new_col = first_a + h if ci == 0 else first_b + (ci - 1) * B_HEADS_PER_PATTERN + h
                planes.append((c_ref.shape[-1], functools.partial(roll_plane, c_ref, r_ref, i, h, new_col)))

    work = []
    total = sum(w for w, _ in planes)
    n_slots = 4 * len(stages)
    done, k = 0, 0
    for slot in range(n_slots):
        work.append(functools.partial(next, stages[slot // 4]))
        while k < len(planes) and done < total * (slot + 1) // n_slots:
            done += planes[k][0]
            work.append(planes[k][1])
            k += 1
    assert k == len(planes)
    return work


def _sample_mix_kernel(*refs):
    for piece in _sample_work(*refs):
        piece()


def _run_interleaved(primary, secondary):
    k = 0
    for i, piece in enumerate(primary):
        piece()
        while k < len(secondary) and k < len(secondary) * (i + 1) // len(primary):
            secondary[k]()
            k += 1
    for piece in secondary[k:]:
        piece()


def _in_proj_mix_kernel(x_ref, g_ref, w_ref, wkvt_ref, *rest, tiles_per_seq, tail_tiles, n_sample_in):
    sample_in = rest[:n_sample_in]
    act_ref, qkv2_ref, qkv3_ref, kvt_ref = rest[n_sample_in:n_sample_in + 4]
    sample_out = rest[n_sample_in + 4:-1]
    ys_ref = rest[-1]
    step = pl.program_id(0)
    tile = step // 2
    for phase in (0, 1):
        @pl.when(step % 2 == phase)
        def _():
            _run_interleaved(_sample_work(*sample_in, *sample_out),
                             _in_proj_work(x_ref, g_ref, w_ref, act_ref, qkv2_ref, qkv3_ref, ys_ref, phase))
            if phase == 1:
                @pl.when(tile % tiles_per_seq >= tiles_per_seq - tail_tiles)
                def _():
                    _kv_tail(x_ref, g_ref, wkvt_ref, kvt_ref)


def _sample_operands(proj_s, caches, rel_bias, sinks):
    n = proj_s.shape[0]
    hd = HEAD_DIM
    vecs = jnp.concatenate([proj_s[:, :OFF_KA], proj_s[:, OFF_QB:OFF_KB], proj_s[:, OFF_KA:OFF_QB],
                            proj_s[:, OFF_KB:OFF_GA]], axis=1).reshape(n, N_COLS, hd)
    cols = jnp.pad(jnp.transpose(vecs, (0, 2, 1)), ((0, 0), (0, 0), (0, LANES - N_COLS)))
    cts = [jnp.transpose(c, (0, 2, 3, 4, 1)) for c in caches]

    ba, sa = _decode_bias(rel_bias[:, :A_HEADS], A_WINDOW, 1, 1)
    sink = sinks.astype(F32).reshape(A_HEADS, 1, 1)
    bbs, sbs = [], []
    for p, (win, dil) in enumerate(B_PATTERNS):
        lo = A_HEADS + p * B_HEADS_PER_PATTERN
        assert caches[1 + p].shape[1] == win == BLOCK * dil
        b, s = _decode_bias(rel_bias[:, lo:lo + B_HEADS_PER_PATTERN], win, dil, 0)
        bbs.append(b)
        sbs.append(s)
    sb = jnp.stack(sbs)

    seq_in = [cols] + cts
    consts = [ba, sa, sink] + bbs + [sb]
    out_shape = [jax.ShapeDtypeStruct((n, hd, LANES), F32), jax.ShapeDtypeStruct((n, SUBLANES, LANES), F32)]
    out_shape += [jax.ShapeDtypeStruct(c.shape, c.dtype) for c in cts]
    return seq_in, consts, out_shape


def _per_seq_spec(a):
    nd = len(a.shape)
    return pl.BlockSpec((None,) + tuple(a.shape[1:]), lambda i: (i,) + (0,) * (nd - 1))


def _const_spec(a):
    nd = a.ndim
    return pl.BlockSpec(a.shape, lambda i: (0,) * nd)


def _sample_results(outs):
    n = outs[0].shape[0]
    o_rows = jnp.transpose(outs[0][:, :, :COL_KA], (0, 2, 1))
    oa = o_rows[:, COL_QA:COL_QB].reshape(n, QA_W)
    ob = o_rows[:, COL_QB:COL_KA].reshape(n, N_PAT, PB_W)
    lse = outs[1][:, 0, :B_HEADS].reshape(n, N_PAT, B_HEADS_PER_PATTERN)
    rolled = [jnp.transpose(r, (0, 4, 1, 2, 3)) for r in outs[2:]]
    return oa, ob, lse, rolled


def _sample_mix(proj_s, caches, rel_bias, sinks):
    seq_in, consts, out_shape = _sample_operands(proj_s, caches, rel_bias, sinks)
    outs = pl.pallas_call(
        _sample_mix_kernel,
        grid=(proj_s.shape[0],),
        in_specs=[_per_seq_spec(a) for a in seq_in] + [_const_spec(a) for a in consts],
        out_specs=[_per_seq_spec(s) for s in out_shape],
        out_shape=out_shape,
        compiler_params=_cparams(("arbitrary",)),
        name="sample_mix",
    )(*seq_in, *consts)
    return _sample_results(outs)


def _in_proj_mix(x2d, ln_g, w_bf, w_kvt, n, t, tm, proj_s, caches, rel_bias, sinks):
    m, d = x2d.shape
    tps = t // tm
    tail = min(max(w for w, _ in B_PATTERNS), t)
    assert t % tm == 0 and tail % tm == 0 and all(tm % (16 * dl) == 0 for _, dl in B_PATTERNS)
    assert proj_s.shape[0] == 2 * (m // tm)
    tail_tiles = tail // tm
    d2, d3 = B_PATTERNS[1][1], B_PATTERNS[2][1]
    seq_in, consts, sample_shape = _sample_operands(proj_s, caches, rel_bias, sinks)
    tile = lambda i: i // 2
    outs = pl.pallas_call(
        functools.partial(_in_proj_mix_kernel, tiles_per_seq=tps, tail_tiles=tail_tiles,
                          n_sample_in=len(seq_in) + len(consts)),
        grid=(2 * (m // tm),),
        in_specs=[pl.BlockSpec((tm, d), lambda i: (tile(i), 0)),
                  pl.BlockSpec((1, d), lambda i: (0, 0)),
                  pl.BlockSpec(w_bf.shape, lambda i: (0, 0), pipeline_mode=pl.Buffered(1)),
                  pl.BlockSpec(w_kvt.shape, lambda i: (0, 0), pipeline_mode=pl.Buffered(1))]
                 + [_per_seq_spec(a) for a in seq_in] + [_const_spec(a) for a in consts],
        out_specs=[pl.BlockSpec((tm, ACT_W), lambda i: (tile(i), 0)),
                   pl.BlockSpec((None, d2, tm // d2, QKV_W), lambda i: (tile(i) // tps, 0, tile(i) % tps, 0)),
                   pl.BlockSpec((None, d3, tm // d3, QKV_W), lambda i: (tile(i) // tps, 0, tile(i) % tps, 0)),
                   pl.BlockSpec((None, KVT_ROWS, tm),
                                lambda i: (tile(i) // tps, 0, jnp.maximum(tile(i) % tps - (tps - tail_tiles), 0)))]
                  + [_per_seq_spec(s) for s in sample_shape],
        out_shape=[jax.ShapeDtypeStruct((m, ACT_W), BF16),
                   jax.ShapeDtypeStruct((n, d2, t // d2, QKV_W), BF16),
                   jax.ShapeDtypeStruct((n, d3, t // d3, QKV_W), BF16),
                   jax.ShapeDtypeStruct((n, KVT_ROWS, tail), F32)] + sample_shape,
        scratch_shapes=[pltpu.VMEM(((N_PAT - 1) * YS_CHUNKS, tm, LANES), F32)],
        compiler_params=_cparams(("arbitrary",)),
        name="in_proj_mix",
    )(x2d, ln_g.reshape(1, d), w_bf, w_kvt, *seq_in, *consts)
    return outs[:4], _sample_results(outs[4:])


def _post_attn_kernel(oa_ref, o1_ref, o2_ref, o3_ref, l1_ref, l2_ref, l3_ref, ga_ref, gb_ref, x_ref,
                      wpa_ref, wpb_ref, wout_ref, ln2_ref, wr_ref, br_ref, *rest, n_extra):
    outs = rest[n_extra:n_extra + 4]
    h_ref, hn_ref, route_ref, cnt_ref = outs
    scr_ref = rest[n_extra + 4]
    if n_extra:
        own_tile = pl.program_id(0) < pl.num_programs(0) - 1

        @pl.when(jnp.logical_not(own_tile))
        def _():
            for dst, src in zip(outs, rest[:n_extra]):
                dst[...] = src[...]

        @pl.when(own_tile)
        def _():
            _post_attn_tile(oa_ref, o1_ref, o2_ref, o3_ref, l1_ref, l2_ref, l3_ref, ga_ref, gb_ref, x_ref,
                            wpa_ref, wpb_ref, wout_ref, ln2_ref, wr_ref, br_ref, *outs, scr_ref)
    else:
        _post_attn_tile(oa_ref, o1_ref, o2_ref, o3_ref, l1_ref, l2_ref, l3_ref, ga_ref, gb_ref, x_ref,
                        wpa_ref, wpb_ref, wout_ref, ln2_ref, wr_ref, br_ref, *outs, scr_ref)


def _post_attn_tile(oa_ref, o1_ref, o2_ref, o3_ref, l1_ref, l2_ref, l3_ref, ga_ref, gb_ref, x_ref,
                    wpa_ref, wpb_ref, wout_ref, ln2_ref, wr_ref, br_ref, h_ref, hn_ref, route_ref, cnt_ref, scr_ref):
    tm = x_ref.shape[0]
    chunks = PB_W // LANES

    def token_major(ref, slot):
        dil = ref.shape[0]
        if dil == 1:
            return ref[0].astype(F32)
        for c in range(chunks):
            for r in range(dil):
                scr_ref[slot * chunks + c, pl.ds(r, tm // dil, stride=dil), :] = (
                    ref[r, :, c * LANES:(c + 1) * LANES].astype(F32))
        return jnp.concatenate([scr_ref[slot * chunks + c] for c in range(chunks)], axis=1)

    o1, o2, o3 = (token_major(r, s) for s, r in enumerate((o1_ref, o2_ref, o3_ref)))
    l1, l2, l3 = (token_major(r, 3 + s) for s, r in enumerate((l1_ref, l2_ref, l3_ref)))
    m = jnp.maximum(jnp.maximum(l1, l2), l3)
    a1, a2, a3 = jnp.exp(l1 - m), jnp.exp(l2 - m), jnp.exp(l3 - m)
    ob = (a1 * o1 + a2 * o2 + a3 * o3) / (a1 + a2 + a3)
    ya = jnp.dot(oa_ref[...], wpa_ref[...], preferred_element_type=F32)
    yb = jnp.dot(ob.astype(BF16), wpb_ref[...], preferred_element_type=F32)
    merged = ga_ref[...].astype(F32) * ya + gb_ref[...].astype(F32) * yb
    h = x_ref[...] + jnp.dot(merged.astype(BF16), wout_ref[...], preferred_element_type=F32)
    h_ref[...] = h
    hn = _rms(h, ln2_ref[...])
    hn_ref[...] = hn.astype(BF16)

    hn_hi = hn.astype(BF16)
    hn_lo = (hn - hn_hi.astype(F32)).astype(BF16)
    logits = (jnp.dot(hn_hi, wr_ref[0], preferred_element_type=F32)
              + jnp.dot(hn_lo, wr_ref[0], preferred_element_type=F32)
              + jnp.dot(hn_hi, wr_ref[1], preferred_element_type=F32)) + br_ref[...]
    lane = lax.broadcasted_iota(jnp.int32, logits.shape, 1)
    is_grp = (lane >= N_EXPERTS) & (lane < N_EXPERTS + MOE_GROUPS)
    lg = jnp.where(is_grp, logits, NEG)
    gmax = jnp.max(lg, axis=-1, keepdims=True)
    g_lane = jnp.min(jnp.where(lg == gmax, lane, LANES), axis=-1, keepdims=True)
    p_g = 1.0 / jnp.sum(jnp.where(is_grp, jnp.exp(lg - gmax), 0.0), axis=-1, keepdims=True)
    e_lo = (g_lane - N_EXPERTS) * EXPERTS_PER_GROUP
    in_grp = (lane >= e_lo) & (lane < e_lo + EXPERTS_PER_GROUP)
    le = jnp.where(in_grp, logits, NEG)
    v1 = jnp.max(le, axis=-1, keepdims=True)
    i1 = jnp.min(jnp.where(le == v1, lane, LANES), axis=-1, keepdims=True)
    le2 = jnp.where(lane == i1, NEG, le)
    v2 = jnp.max(le2, axis=-1, keepdims=True)
    i2 = jnp.min(jnp.where(le2 == v2, lane, LANES), axis=-1, keepdims=True)
    e2 = jnp.exp(v2 - v1)
    w1 = p_g / (1.0 + e2)
    w2 = p_g * e2 / (1.0 + e2)
    route = jnp.where(lane == ROUTE_I1, i1.astype(F32), jnp.where(lane == ROUTE_I2, i2.astype(F32), 0.0))
    route_ref[...] = route + jnp.where(lane == ROUTE_W1, w1, 0.0) + jnp.where(lane == ROUTE_W2, w2, 0.0)
    picks = (lane == i1).astype(F32) + (lane == i2).astype(F32)
    cnt_ref[...] = jnp.broadcast_to(jnp.sum(picks, axis=0, keepdims=True), cnt_ref.shape)


def _post_attn(oa, obs, lses, gates_src, ga_blk, x2d, wpa, wpb, wout, ln2, wr, br, tm, tiles_per_seq, extra):
    m, d = x2d.shape
    tps = tiles_per_seq
    own = m // tm
    n_tiles = own + (extra is not None)
    mine = lambda i: jnp.minimum(i, own - 1)

    def tile(w, col=0):
        return pl.BlockSpec((tm, w), lambda i: (mine(i), col))

    def out_tile(rows, w):
        return pl.BlockSpec((rows, w), lambda i: (i, 0))

    def full(a):
        nd = a.ndim
        return pl.BlockSpec(a.shape, lambda i: (0,) * nd)

    def residue(a):
        dil = a.shape[1]
        return pl.BlockSpec((None, dil, tm // dil, PB_W), lambda i: (mine(i) // tps, 0, mine(i) % tps, 0))

    weights = [wpa, wpb, wout, ln2.reshape(1, d), wr, br]
    scratch = [pltpu.VMEM((6 * PB_W // LANES, tm, LANES), F32)]
    in_specs = ([tile(QA_W)] + [residue(a) for a in obs] + [residue(a) for a in lses]
                + [tile(d, ga_blk), tile(d, ga_blk + 1), tile(d)] + [full(w) for w in weights])
    args = [oa, *obs, *lses, gates_src, gates_src, x2d, *weights]
    if extra is not None:
        in_specs = in_specs + [full(a) for a in extra]
        args = args + list(extra)
    return pl.pallas_call(
        functools.partial(_post_attn_kernel, n_extra=0 if extra is None else len(extra)),
        grid=(n_tiles,),
        in_specs=in_specs,
        out_specs=[out_tile(tm, d), out_tile(tm, d), out_tile(tm, LANES), out_tile(SUBLANES, LANES)],
        out_shape=[jax.ShapeDtypeStruct((n_tiles * tm, d), F32),
                   jax.ShapeDtypeStruct((n_tiles * tm, d), BF16),
                   jax.ShapeDtypeStruct((n_tiles * tm, LANES), F32),
                   jax.ShapeDtypeStruct((n_tiles * SUBLANES, LANES), F32)],
        scratch_shapes=scratch,
        compiler_params=_cparams(("arbitrary",)),
        name="post_attn",
    )(*args)


def _route_plan(cnt_rows, n_tiles, n_xtiles):
    g = SUBLANES
    cnt = cnt_rows.reshape(n_tiles, g, LANES)[:, 0, :N_EXPERTS].astype(jnp.int32)
    cnt8 = (cnt + g - 1) // g * g
    loff = jnp.cumsum(cnt8, axis=1) - cnt8
    boff = jnp.cumsum(cnt8, axis=0) - cnt8
    tot = jnp.sum(cnt8, axis=0)
    region = (tot + MOE_TMX - 1) // MOE_TMX * MOE_TMX
    gend = jnp.cumsum(region)
    gbase = gend - region
    cum_tiles = gend // MOE_TMX
    j = jnp.arange(n_xtiles + MOE_SPILL_TILES, dtype=jnp.int32)
    loff_rows = jnp.zeros((n_tiles, g, LANES), F32).at[:, :, :N_EXPERTS].set(loff[:, None, :].astype(F32))
    k8 = g * jnp.arange(MOE_SLOTS // g, dtype=jnp.int32)
    run_end = loff + cnt8
    e_of_k = jnp.sum(k8[None, :, None] >= run_end[:, None, :], axis=2)
    shift = gbase[None, :] + boff - loff
    picked = jnp.sum(jnp.where(e_of_k[:, :, None] == jnp.arange(N_EXPERTS)[None, None, :], shift[:, None, :], 0), axis=2)
    ids = jnp.arange(N_EXPERTS, dtype=jnp.int32)
    xe = jnp.minimum(jnp.sum(j[:, None] >= cum_tiles[None, :], axis=1), N_EXPERTS - 1).astype(jnp.int32)
    has_tiles = region > 0
    first_tile = cum_tiles - region // MOE_TMX
    of_tile = lambda per_expert: jnp.sum(jnp.where(xe[:, None] == ids[None, :], per_expert[None, :], 0), axis=1)
    later = jnp.where(has_tiles[None, :] & (ids[None, :] > ids[:, None]), ids[None, :], N_EXPERTS)
    next_of = jnp.min(later, axis=1)
    next_of = jnp.where(next_of == N_EXPERTS, -1, next_of)
    xtile_first = (jnp.any(has_tiles[None, :] & (j[:, None] == first_tile[None, :]), axis=1)
                   & (j < cum_tiles[-1])).astype(jnp.int32)
    return dict(
        xtile_first=xtile_first, xtile_next=of_tile(next_of).astype(jnp.int32),
        xtile_parity=of_tile((jnp.cumsum(has_tiles.astype(jnp.int32)) - 1) % 2).astype(jnp.int32),
        chunk_dst=jnp.where(k8[None, :] < run_end[:, -1:], picked, n_xtiles * MOE_TMX).reshape(-1).astype(jnp.int32)
        + jnp.tile(k8, n_tiles),
        zst=gbase + tot, znch=(region - tot) // g,
        ztot=jnp.sum((region - tot) // g).reshape(1),
        xtile_expert=xe,
        n_used=cum_tiles[-1:].astype(jnp.int32),
        loff_rows=loff_rows.reshape(n_tiles * g, LANES))


def _local_slots(route, loff_row):
    tm = route.shape[0]
    lane = lax.broadcasted_iota(jnp.int32, (tm, LANES), 1)
    e1 = lane == route[:, ROUTE_I1:ROUTE_I1 + 1].astype(jnp.int32)
    e2 = lane == route[:, ROUTE_I2:ROUTE_I2 + 1].astype(jnp.int32)
    earlier = (lax.broadcasted_iota(jnp.int32, (tm, tm), 1) < lax.broadcasted_iota(jnp.int32, (tm, tm), 0))
    earlier = earlier.astype(BF16)
    c1 = jnp.dot(earlier, e1.astype(BF16), preferred_element_type=F32)
    c2 = jnp.dot(earlier, e2.astype(BF16), preferred_element_type=F32)
    cnt1 = jnp.sum(e1.astype(F32), axis=0, keepdims=True)
    pos1 = jnp.sum(jnp.where(e1, c1 + loff_row, 0.0), axis=1, keepdims=True)
    pos2 = jnp.sum(jnp.where(e2, c2 + cnt1 + loff_row, 0.0), axis=1, keepdims=True)
    slot = lax.broadcasted_iota(jnp.int32, (tm, MOE_SLOTS), 1)
    return slot == pos1.astype(jnp.int32), slot == pos2.astype(jnp.int32)


def _pack_bf16_pairs(x):
    c = x.shape[1] // 2
    bits = lambda v: lax.bitcast_convert_type(v.astype(BF16).astype(F32), jnp.uint32)
    return bits(x[:, :c]) | (bits(x[:, c:]) >> 16)


def _unpack_bf16_pairs(w):
    hi = lax.bitcast_convert_type(w & jnp.uint32(0xFFFF0000), F32).astype(BF16)
    lo = lax.bitcast_convert_type(w << 16, F32).astype(BF16)
    return jnp.concatenate([hi, lo], axis=1)


def _split3(w):
    hi = w.astype(BF16).astype(F32)
    mid = (w - hi).astype(BF16).astype(F32)
    return hi, mid, (w - hi - mid).astype(BF16).astype(F32)


def _moe_scatter_kernel(cdst_s, zst_s, znch_s, ztot_s, nused_s,
                        hn_ref, route_ref, loffv_ref, xs_hbm, buf_ref, zero_ref, sem, zsem, tsem):
    b = pl.program_id(0)
    nb = pl.num_programs(0)
    slot = b % 2
    g = SUBLANES
    d = hn_ref.shape[1]
    n_xtiles = xs_hbm.shape[0] // MOE_TMX - MOE_SPILL_TILES

    def run_copy(s, src_row, dst_row):
        return pltpu.make_async_copy(buf_ref.at[s, pl.ds(src_row, g)], xs_hbm.at[pl.ds(dst_row, g)], sem.at[s])

    def zero_copy(dst_row):
        return pltpu.make_async_copy(zero_ref.at[pl.ds(0, g)], xs_hbm.at[pl.ds(dst_row, g)], zsem)

    def zero_tile_copy(j):
        return pltpu.make_async_copy(zero_ref, xs_hbm.at[pl.ds(pl.multiple_of(j * MOE_TMX, MOE_TMX), MOE_TMX)], tsem)

    n_chunks = MOE_SLOTS // g
    spill = n_xtiles * MOE_TMX

    def wait_buffer(s):
        for _ in range(n_chunks):
            run_copy(s, 0, 0).wait()

    def ship_buffer(s, dst_row_of):
        for k in range(n_chunks):
            run_copy(s, k * g, pl.multiple_of(dst_row_of(k), g)).start()

    @pl.when(b == 0)
    def _():
        zero_ref[...] = jnp.zeros_like(zero_ref)
        buf_ref[1] = jnp.zeros(buf_ref.shape[1:], buf_ref.dtype)

        def per_expert(e, c):
            def per_chunk(k, c2):
                zero_copy(pl.multiple_of(zst_s[e] + k * g, g)).start()
                return c2
            return lax.fori_loop(0, znch_s[e], per_chunk, c)
        lax.fori_loop(0, N_EXPERTS, per_expert, 0)
        lax.fori_loop(nused_s[0], n_xtiles, lambda j, c: (zero_tile_copy(j).start(), c)[1], 0)

    @pl.when(b >= 1)
    def _():
        wait_buffer(slot)

    prev = jnp.maximum(b - 1, 0) * n_chunks
    ship_buffer(1 - slot, lambda k: jnp.where(b == 0, spill + k * g, cdst_s[prev + k]))

    route = route_ref[...]
    p1, p2 = _local_slots(route, loffv_ref[0:1, :])
    tn = (((0,), (0,)), ((), ()))
    picks = (p1 | p2).astype(BF16)
    buf_ref[slot, :, 0:d // 2] = _pack_bf16_pairs(lax.dot_general(picks, hn_ref[...], tn, preferred_element_type=F32))
    lane = lax.broadcasted_iota(jnp.int32, route.shape, 1)
    meta = jnp.zeros((MOE_SLOTS, LANES), F32)
    for p, col in ((p1, ROUTE_W1), (p2, ROUTE_W2)):
        w = route[:, col:col + 1]
        parts = _split3(w)
        wm = sum(jnp.where(lane == k, part, 0.0) for k, part in enumerate(parts))
        meta = meta + lax.dot_general(p.astype(BF16), wm.astype(BF16), tn, preferred_element_type=F32)
    buf_ref[slot, :, d // 2:d // 2 + LANES] = lax.bitcast_convert_type(meta, jnp.uint32)

    @pl.when(b == nb - 1)
    def _():
        wait_buffer(1 - slot)
        ship_buffer(slot, lambda k: cdst_s[b * n_chunks + k])
        wait_buffer(slot)
        lax.fori_loop(0, ztot_s[0], lambda _, c: (zero_copy(0).wait(), c)[1], 0)
        lax.fori_loop(nused_s[0], n_xtiles, lambda j, c: (zero_tile_copy(j).wait(), c)[1], 0)


def _moe_scatter(plan, hn_all, route_all, n_tiles, n_xtiles):
    d = hn_all.shape[1]
    tm = MOE_TM
    grid_spec = pltpu.PrefetchScalarGridSpec(
        num_scalar_prefetch=5,
        grid=(n_tiles,),
        in_specs=[pl.BlockSpec((tm, d), lambda i, *_: (i, 0)),
                  pl.BlockSpec((tm, LANES), lambda i, *_: (i, 0)),
                  pl.BlockSpec((SUBLANES, LANES), lambda i, *_: (i, 0))],
        out_specs=pl.BlockSpec(memory_space=pl.ANY),
        scratch_shapes=[pltpu.VMEM((2, MOE_SLOTS, d // 2 + LANES), jnp.uint32),
                        pltpu.VMEM((MOE_TMX, d // 2 + LANES), jnp.uint32),
                        pltpu.SemaphoreType.DMA((2,)),
                        pltpu.SemaphoreType.DMA(()),
                        pltpu.SemaphoreType.DMA(())])
    return pl.pallas_call(
        _moe_scatter_kernel,
        grid_spec=grid_spec,
        out_shape=jax.ShapeDtypeStruct(((n_xtiles + MOE_SPILL_TILES) * MOE_TMX, d // 2 + LANES), jnp.uint32),
        compiler_params=_cparams(("arbitrary",)),
        name="moe_scatter",
    )(plan["chunk_dst"], plan["zst"], plan["znch"], plan["ztot"], plan["n_used"],
      hn_all, route_all, plan["loff_rows"])


def _moe_experts_kernel(xe_s, first_s, next_s, par_s, nused_s, x_ref, w1_hbm, w3_hbm, w2_hbm, y_ref,
                        w1f_ref, w3f_ref, w2f_ref, w1b_ref, w3b_ref, w2b_ref, sem):
    j = pl.program_id(0)
    d = w1b_ref.shape[0]

    def weight_copies(e, p):
        return [pltpu.make_async_copy(src.at[e], dst.at[p], sem.at[p, i])
                for i, (src, dst) in enumerate(((w1_hbm, w1f_ref), (w3_hbm, w3f_ref), (w2_hbm, w2f_ref)))]

    @pl.when(j < nused_s[0])
    def _():
        @pl.when(first_s[j] == 1)
        def _():
            p = par_s[j]

            @pl.when(j == 0)
            def _():
                for c in weight_copies(xe_s[j], p):
                    c.start()
            for c in weight_copies(xe_s[j], p):
                c.wait()
            w1b_ref[...] = w1f_ref[p].astype(BF16)
            w3b_ref[...] = w3f_ref[p].astype(BF16)
            w2b_ref[...] = w2f_ref[p].astype(BF16)

            @pl.when(next_s[j] >= 0)
            def _():
                for c in weight_copies(next_s[j], 1 - p):
                    c.start()

        x = _unpack_bf16_pairs(x_ref[:, 0:d // 2])
        gate = jnp.sum(lax.bitcast_convert_type(x_ref[:, d // 2:d // 2 + LANES], F32), axis=1, keepdims=True)
        a = jnp.dot(x, w1b_ref[...], preferred_element_type=F32)
        b = jnp.dot(x, w3b_ref[...], preferred_element_type=F32)
        hh = (a * _sigmoid(a)) * b * gate
        y_ref[...] = _pack_bf16_pairs(jnp.dot(hh.astype(BF16), w2b_ref[...], preferred_element_type=F32))

    @pl.when(j >= nused_s[0])
    def _():
        y_ref[...] = jnp.zeros_like(y_ref)


def _moe_experts(plan, xs, w1, w3, w2, n_xtiles):
    ne, d, f = w1.shape
    last = lambda j, xe, nu: jnp.maximum(jnp.minimum(j, nu[0] - 1), 0)
    any_spec = pl.BlockSpec(memory_space=pl.ANY)
    grid_spec = pltpu.PrefetchScalarGridSpec(
        num_scalar_prefetch=5,
        grid=(n_xtiles + MOE_SPILL_TILES,),
        in_specs=[pl.BlockSpec((MOE_TMX, d // 2 + LANES), lambda j, xe, fi, nx, pa, nu: (last(j, xe, nu), 0)),
                  any_spec, any_spec, any_spec],
        out_specs=pl.BlockSpec((MOE_TMX, d // 2), lambda j, *_: (j, 0)),
        scratch_shapes=[pltpu.VMEM((2, d, f), F32), pltpu.VMEM((2, d, f), F32), pltpu.VMEM((2, f, d), F32),
                        pltpu.VMEM((d, f), BF16), pltpu.VMEM((d, f), BF16), pltpu.VMEM((f, d), BF16),
                        pltpu.SemaphoreType.DMA((2, 3))])
    return pl.pallas_call(
        _moe_experts_kernel,
        grid_spec=grid_spec,
        out_shape=jax.ShapeDtypeStruct(((n_xtiles + MOE_SPILL_TILES) * MOE_TMX, d // 2), jnp.uint32),
        compiler_params=_cparams(("arbitrary",)),
        name="moe_experts",
    )(plan["xtile_expert"], plan["xtile_first"], plan["xtile_next"], plan["xtile_parity"], plan["n_used"],
      xs, w1, w3, w2)


def _moe_combine_kernel(cdst_s, ys_hbm, route_ref, loffv_ref, h_ref, lnf_ref, y_ref,
                        buf_ref, sem, *, tile0, final_norm):
    b = pl.program_id(0)
    nb = pl.num_programs(0)
    slot = b % 2
    g = SUBLANES
    n_chunks = MOE_SLOTS // g
    tile = b + tile0

    def run_copy(s, src_row, dst_row):
        return pltpu.make_async_copy(ys_hbm.at[pl.ds(src_row, g)], buf_ref.at[s, pl.ds(dst_row, g)], sem.at[s])

    def fetch(t, s):
        for k in range(n_chunks):
            run_copy(s, pl.multiple_of(cdst_s[t * n_chunks + k], g), k * g).start()

    def wait_buffer(s):
        for _ in range(n_chunks):
            run_copy(s, 0, 0).wait()

    @pl.when(b == 0)
    def _():
        fetch(tile, slot)

    fetch(jnp.minimum(tile + 1, tile0 + nb - 1), 1 - slot)
    wait_buffer(slot)

    @pl.when(b == nb - 1)
    def _():
        wait_buffer(1 - slot)

    p1, p2 = _local_slots(route_ref[...], loffv_ref[0:1, :])
    picks = (p1 | p2).astype(BF16)
    y = h_ref[...] + jnp.dot(picks, _unpack_bf16_pairs(buf_ref[slot]), preferred_element_type=F32)
    if final_norm:
        y = _rms(y, lnf_ref[...])
    y_ref[...] = y


def _moe_combine(plan, ys, route_all, h_all, lnf, tile0, n_tiles, final_norm):
    d = h_all.shape[1]
    tm = MOE_TM
    grid_spec = pltpu.PrefetchScalarGridSpec(
        num_scalar_prefetch=1,
        grid=(n_tiles,),
        in_specs=[pl.BlockSpec(memory_space=pl.ANY),
                  pl.BlockSpec((tm, LANES), lambda i, *_: (i + tile0, 0)),
                  pl.BlockSpec((SUBLANES, LANES), lambda i, *_: (i + tile0, 0)),
                  pl.BlockSpec((tm, d), lambda i, *_: (i + tile0, 0)),
                  pl.BlockSpec((1, d), lambda i, *_: (0, 0))],
        out_specs=pl.BlockSpec((tm, d), lambda i, *_: (i, 0)),
        scratch_shapes=[pltpu.VMEM((2, MOE_SLOTS, d // 2), jnp.uint32), pltpu.SemaphoreType.DMA((2,))])
    return pl.pallas_call(
        functools.partial(_moe_combine_kernel, tile0=tile0, final_norm=final_norm),
        grid_spec=grid_spec,
        out_shape=jax.ShapeDtypeStruct((n_tiles * tm, d), F32),
        compiler_params=_cparams(("arbitrary",)),
        name="moe_combine",
    )(plan["chunk_dst"], ys, route_all, plan["loff_rows"], h_all, lnf.reshape(1, d))


def _prompt_states(kvt, n, t):
    hd = HEAD_DIM
    tail = kvt.shape[2]

    def state(k_lo, v_lo, rows, heads, win):
        w = min(win, t)
        kv = jnp.stack([kvt[:, k_lo:k_lo + rows, tail - w:], kvt[:, v_lo:v_lo + rows, tail - w:]], axis=1)
        return jnp.transpose(kv.reshape(n, 2, heads, hd, w), (0, 4, 1, 2, 3))

    out = [state(0, KA_W, KA_W, A_KV_HEADS, A_WINDOW)]
    for p, (win, _) in enumerate(B_PATTERNS):
        out.append(state(2 * KA_W + p * PB_W, 2 * KA_W + QB_W + p * PB_W, PB_W, B_HEADS_PER_PATTERN, win))
    return out


def _layer(xp, xs, caches, rel_bias, ln1, w_in, sinks, w_pa, w_pb, w_out, ln2, w_rg, b_rg, w_re, b_re,
           w1, w3, w2, lnf, final_norm):
    n, t, d = xp.shape
    ns = xs.shape[0]
    assert xs.shape[1] == 1 and OFF_GA + 2 * d == w_in.shape[1] and P_QA == 2 * d
    w_bf = _cast_bf16(w_in)
    w_kvt = _kv_weights_transposed(w_in)
    wpa, wpb, wout = w_pa.astype(BF16), w_pb.astype(BF16), w_out.astype(BF16)
    wr = jnp.zeros((d, LANES), F32).at[:, :N_EXPERTS].set(w_re).at[:, N_EXPERTS:N_EXPERTS + MOE_GROUPS].set(w_rg)
    br = jnp.zeros((1, LANES), F32).at[0, :N_EXPERTS].set(b_re).at[0, N_EXPERTS:N_EXPERTS + MOE_GROUPS].set(b_rg)
    wr_hi = wr.astype(BF16)
    wr = jnp.stack([wr_hi, (wr - wr_hi.astype(F32)).astype(BF16)])

    tm = MOE_TM
    xp2 = xp.reshape(n * t, d)
    xs2 = xs.reshape(ns, d)
    proj_s = _in_proj_sample(xs2, ln1, w_bf)
    if ns == 2 * (n * t // MIX_TM):
        (act, qkv2, qkv3, kvt), sampled = _in_proj_mix(xp2, ln1, w_bf, w_kvt, n, t, MIX_TM, proj_s, caches,
                                                       rel_bias, sinks)
    else:
        act, qkv2, qkv3, kvt = _in_proj_prompt(xp2, ln1, w_bf, w_kvt, n, t, IN_PROJ_TM)
        sampled = _sample_mix(proj_s, caches, rel_bias, sinks)
    oa_s, ob_s, lse_s, st_s = sampled
    act4 = act.reshape(n, 1, t, ACT_W)
    bias_a = _band_bias(rel_bias[:, :A_HEADS], A_WINDOW - 1, 1)
    (oa,) = _band_attn(act4, bias_a, sinks, q_off=P_QA, k_off=P_KA, v_off=P_VA,
                       kv_heads=A_KV_HEADS, grp=A_GROUP, want_lse=False)
    obs, lses = [], []
    for p, (win, dil) in enumerate(B_PATTERNS):
        lo = A_HEADS + p * B_HEADS_PER_PATTERN
        bias_p = _band_bias(rel_bias[:, lo:lo + B_HEADS_PER_PATTERN], win // dil, dil)
        src, base = ((act4, P_B), (qkv2, 0), (qkv3, 0))[p]
        o, lse = _band_attn(src, bias_p, None, q_off=base, k_off=base + PB_W, v_off=base + 2 * PB_W,
                            kv_heads=B_HEADS_PER_PATTERN, grp=1, want_lse=True)
        obs.append(o)
        lses.append(lse)
    assert ns <= MOE_TM
    p_tiles = n * t // MOE_TM
    n_tiles = p_tiles + 1
    m_all = n_tiles * MOE_TM
    st_p = _prompt_states(kvt, n, t)

    rows = lambda a: jnp.pad(a, ((0, MOE_TM - ns), (0, 0)))
    obs_s = [rows(ob_s[:, p].astype(BF16)).reshape(1, 1, MOE_TM, PB_W) for p in range(N_PAT)]
    lses_s = [rows(jnp.repeat(lse_s[:, p], HEAD_DIM, axis=-1)).reshape(1, 1, MOE_TM, PB_W) for p in range(N_PAT)]
    gates_s = rows(proj_s[:, OFF_GA:].astype(BF16))
    routed_s = _post_attn(rows(oa_s.astype(BF16)), obs_s, lses_s, gates_s, 0, rows(xs2),
                          wpa, wpb, wout, ln2, wr, br, MOE_TM, 1, None)
    h_all, hn_all, route_all, cnt_all = _post_attn(oa.reshape(n * t, QA_W), obs, lses, act, 0, xp2,
                                                   wpa, wpb, wout, ln2, wr, br, tm, t // tm, routed_s)

    max_rows = 2 * m_all + n_tiles * N_EXPERTS * (SUBLANES - 1) + N_EXPERTS * (MOE_TMX - SUBLANES)
    n_xtiles = -(-max_rows // MOE_TMX)
    plan = _route_plan(cnt_all, n_tiles, n_xtiles)
    xs_sorted = _moe_scatter(plan, hn_all, route_all, n_tiles, n_xtiles)
    ys_sorted = _moe_experts(plan, xs_sorted, w1, w3, w2, n_xtiles)
    yp = _moe_combine(plan, ys_sorted, route_all, h_all, lnf, 0, p_tiles, final_norm).reshape(n, t, d)
    ys = _moe_combine(plan, ys_sorted, route_all, h_all, lnf, p_tiles, 1, final_norm)[:ns].reshape(ns, 1, d)
    return yp, ys, st_p, st_s


def kernel(x_prompt, x_sample, cache_a_kv, cache_b1_kv, cache_b2_kv, cache_b3_kv, rel_bias, ln1_g, w_in, sinks,
           w_pa, w_pb, w_out, ln2_g, w_rg, b_rg, w_re, b_re, w1, w3, w2, lnf_g):
    depth = w_in.shape[0]
    assert depth >= 1
    xp, xs = x_prompt, x_sample
    new_p = [[] for _ in range(4)]
    new_s = [[] for _ in range(4)]
    for l in range(depth):
        caches = (cache_a_kv[l], cache_b1_kv[l], cache_b2_kv[l], cache_b3_kv[l])
        xp, xs, st_p, st_s = _layer(xp, xs, caches, rel_bias, ln1_g[l], w_in[l], sinks[l], w_pa[l], w_pb[l],
                                    w_out[l], ln2_g[l], w_rg[l], b_rg[l], w_re[l], b_re[l], w1[l], w3[l], w2[l],
                                    lnf_g, l == depth - 1)
        for i in range(4):
            new_p[i].append(st_p[i])
            new_s[i].append(st_s[i])
    a_p, b1_p, b2_p, b3_p = [jnp.stack(v) for v in new_p]
    a_s, b1_s, b2_s, b3_s = [jnp.stack(v) for v in new_s]
    return (xp, xs, a_p, a_s, b1_p, b1_s, b2_p, b2_s, b3_p, b3_s)
```

```python
import functools
import math

import numpy as np
import jax
import jax.numpy as jnp
from jax import lax
from jax.experimental import pallas as pl
from jax.experimental.pallas import tpu as pltpu

F32 = jnp.float32
BF16 = jnp.bfloat16

HEAD_DIM = 64
A_HEADS = 8
A_KV_HEADS = 2
A_GROUP = A_HEADS // A_KV_HEADS
A_WINDOW = 128
B_PATTERNS = ((128, 1), (512, 4), (2048, 16))
N_PAT = len(B_PATTERNS)
B_HEADS_PER_PATTERN = 4
B_HEADS = B_HEADS_PER_PATTERN * N_PAT
BLOCK = 128
ATTN_CHAINS = 16
NUM_BUCKETS = 32
MAX_DISTANCE = 2048
MOE_GROUPS = 4
EXPERTS_PER_GROUP = 8
N_EXPERTS = MOE_GROUPS * EXPERTS_PER_GROUP
EPS = 1e-6
NEG = -1e30
LANES = 128
SUBLANES = 8
ROUTE_I1, ROUTE_I2, ROUTE_W1, ROUTE_W2 = 0, 1, 2, 3
IN_PROJ_TM = 512
MIX_TM = 256
MOE_TM = 256
POST_ATTN_PARTS = 1
MOE_SLOTS = 2 * MOE_TM + N_EXPERTS * SUBLANES
MOE_TMX = 256
MOE_SPILL_TILES = -(-MOE_SLOTS // MOE_TMX)
MOE_XSTEP_TILES = 2
Q_SCALE = HEAD_DIM ** -0.5

QA_W = A_HEADS * HEAD_DIM
KA_W = A_KV_HEADS * HEAD_DIM
QB_W = B_HEADS * HEAD_DIM
PB_W = B_HEADS_PER_PATTERN * HEAD_DIM
QKV_W = 3 * PB_W
OFF_KA = QA_W
OFF_QB = OFF_KA + 2 * KA_W
OFF_KB = OFF_QB + QB_W
OFF_VB = OFF_KB + QB_W
OFF_GA = OFF_VB + QB_W
P_QA = 2048
P_KA = P_QA + QA_W
P_VA = P_KA + KA_W
P_B = P_VA + KA_W
ACT_W = P_B + QKV_W
KVT_ROWS = 2 * KA_W + 2 * QB_W

VMEM_LIMIT = 56 * 1024 * 1024


def _cparams(sem):
    return pltpu.CompilerParams(dimension_semantics=sem, vmem_limit_bytes=VMEM_LIMIT)


def _bucket_np(dist):
    dist = np.asarray(dist, np.int64)
    max_exact = NUM_BUCKETS // 2
    df = np.maximum(dist, max_exact).astype(np.float64)
    large = max_exact + (np.log(df / max_exact) / math.log(MAX_DISTANCE / max_exact)
                         * (NUM_BUCKETS - max_exact)).astype(np.int64)
    return np.where(dist < max_exact, dist, np.minimum(large, NUM_BUCKETS - 1))


def _table_rows(table_cols, dist, valid):
    onehot = (_bucket_np(dist)[:, None] == np.arange(NUM_BUCKETS)[None, :]).astype(np.float32)
    rows = jnp.einsum("ck,kh->hc", jnp.asarray(onehot), table_cols.astype(F32), precision=lax.Precision.HIGHEST)
    return jnp.where(jnp.asarray(valid)[None, :], rows, NEG)


def _band_bias(table_cols, max_dist, dilation):
    period = 3 * BLOCK
    m = np.arange(period)
    k = np.where(m < 2 * BLOCK, m, m - period)
    dist = BLOCK - k
    valid = (dist >= 0) & (dist <= max_dist) & (m != 2 * BLOCK)
    v = _table_rows(table_cols, np.clip(dist, 0, None) * dilation, valid)
    heads = v.shape[0]
    flat = jnp.tile(v, (1, BLOCK))[:, :BLOCK * (period - 1)]
    return flat.reshape(heads, BLOCK, period - 1)[:, :, :2 * BLOCK]


def _decode_bias(table_cols, width, dilation, first_valid):
    c = np.arange(width)
    valid = (c % dilation == 0) & (c >= first_valid)
    rows = _table_rows(table_cols, width - c, valid)
    self_bias = _table_rows(table_cols, np.zeros((1,), np.int64), np.ones((1,), bool))
    return rows[:, None, :], self_bias[:, None, :]


def _rms(x, g):
    return (x * lax.rsqrt(jnp.mean(x * x, axis=-1, keepdims=True) + EPS)) * g


def _sigmoid(x):
    return 1.0 / (1.0 + jnp.exp(-x))


YS_CHUNKS = QKV_W // LANES


def _in_proj_work(x_ref, g_ref, w_ref, act_ref, qkv2_ref, qkv3_ref, ys_ref, phase):
    tm = x_ref.shape[0]
    cache = {}

    def xb():
        if "xb" not in cache:
            cache["xb"] = _rms(x_ref[...], g_ref[...]).astype(BF16)
        return cache["xb"]

    def proj(lo, hi):
        return jnp.dot(xb(), w_ref[:, lo:hi], preferred_element_type=F32)

    def gates(c):
        act_ref[:, c:c + 512] = _sigmoid(proj(OFF_GA + c, OFF_GA + c + 512)).astype(BF16)

    def mixer_a_q():
        act_ref[:, P_QA:P_KA] = (proj(0, OFF_KA) * Q_SCALE).astype(BF16)

    def mixer_a_kv():
        act_ref[:, P_KA:P_B] = proj(OFF_KA, OFF_QB).astype(BF16)

    def pattern_part(p, j):
        lo = (OFF_QB, OFF_KB, OFF_VB)[j] + p * PB_W
        part = proj(lo, lo + PB_W)
        if j == 0:
            part = part * Q_SCALE
        if p == 0:
            act_ref[:, P_B + j * PB_W:P_B + (j + 1) * PB_W] = part.astype(BF16)
        else:
            for c in range(PB_W // LANES):
                ys_ref[(p - 1) * YS_CHUNKS + j * (PB_W // LANES) + c] = part[:, c * LANES:(c + 1) * LANES]

    def regroup(p, out_ref):
        dil = B_PATTERNS[p][1]
        for c in range(YS_CHUNKS):
            for r in range(dil):
                out_ref[r, :, c * LANES:(c + 1) * LANES] = (
                    ys_ref[(p - 1) * YS_CHUNKS + c, pl.ds(r, tm // dil, stride=dil), :].astype(BF16))

    work = []
    if phase in (None, 0):
        work += [functools.partial(gates, c) for c in range(0, P_QA, 512)] + [mixer_a_q, mixer_a_kv]
    if phase in (None, 1):
        for p in range(N_PAT):
            work += [functools.partial(pattern_part, p, j) for j in range(3)]
            if p > 0:
                work.append(functools.partial(regroup, p, (qkv2_ref, qkv3_ref)[p - 1]))
    return work


def _transpose_cast_kernel(w_ref, o_ref):
    o_ref[...] = w_ref[...].T.astype(o_ref.dtype)


def _kv_weights_transposed(w_in):
    d = w_in.shape[0]
    blk = 2 * KA_W
    assert OFF_KA % blk == 0 and OFF_KB % blk == 0 and KVT_ROWS % blk == 0
    first, rest = OFF_KA // blk, OFF_KB // blk - 1
    return pl.pallas_call(
        _transpose_cast_kernel,
        grid=(KVT_ROWS // blk,),
        in_specs=[pl.BlockSpec((d, blk), lambda i: (0, jnp.where(i == 0, first, rest + i)))],
        out_specs=pl.BlockSpec((blk, d), lambda i: (i, 0)),
        out_shape=jax.ShapeDtypeStruct((KVT_ROWS, d), BF16),
        compiler_params=_cparams(("arbitrary",)),
        name="kv_weights_t",
    )(w_in)


def _kv_tail(x_ref, g_ref, wkvt_ref, kvt_ref):
    xb = _rms(x_ref[...], g_ref[...]).astype(BF16)
    kvt_ref[...] = lax.dot_general(wkvt_ref[...], xb, (((1,), (1,)), ((), ())), preferred_element_type=F32)


def _in_proj_kernel(x_ref, g_ref, w_ref, wkvt_ref, act_ref, qkv2_ref, qkv3_ref, kvt_ref, ys_ref, *,
                    tiles_per_seq, tail_tiles):
    for piece in _in_proj_work(x_ref, g_ref, w_ref, act_ref, qkv2_ref, qkv3_ref, ys_ref, None):
        piece()

    @pl.when(pl.program_id(0) % tiles_per_seq >= tiles_per_seq - tail_tiles)
    def _():
        _kv_tail(x_ref, g_ref, wkvt_ref, kvt_ref)


def _in_proj_prompt(x2d, ln_g, w_bf, w_kvt, n, t, tm):
    m, d = x2d.shape
    tps = t // tm
    tail = min(max(w for w, _ in B_PATTERNS), t)
    assert t % tm == 0 and tail % tm == 0 and all(tm % (16 * dl) == 0 for _, dl in B_PATTERNS)
    tail_tiles = tail // tm
    d2, d3 = B_PATTERNS[1][1], B_PATTERNS[2][1]
    return pl.pallas_call(
        functools.partial(_in_proj_kernel, tiles_per_seq=tps, tail_tiles=tail_tiles),
        grid=(m // tm,),
        in_specs=[pl.BlockSpec((tm, d), lambda i: (i, 0)),
                  pl.BlockSpec((1, d), lambda i: (0, 0)),
                  pl.BlockSpec(w_bf.shape, lambda i: (0, 0), pipeline_mode=pl.Buffered(1)),
                  pl.BlockSpec(w_kvt.shape, lambda i: (0, 0), pipeline_mode=pl.Buffered(1))],
        out_specs=[pl.BlockSpec((tm, ACT_W), lambda i: (i, 0)),
                   pl.BlockSpec((None, d2, tm // d2, QKV_W), lambda i: (i // tps, 0, i % tps, 0)),
                   pl.BlockSpec((None, d3, tm // d3, QKV_W), lambda i: (i // tps, 0, i % tps, 0)),
                   pl.BlockSpec((None, KVT_ROWS, tm),
                                lambda i: (i // tps, 0, jnp.maximum(i % tps - (tps - tail_tiles), 0)))],
        out_shape=[jax.ShapeDtypeStruct((m, ACT_W), BF16),
                   jax.ShapeDtypeStruct((n, d2, t // d2, QKV_W), BF16),
                   jax.ShapeDtypeStruct((n, d3, t // d3, QKV_W), BF16),
                   jax.ShapeDtypeStruct((n, KVT_ROWS, tail), F32)],
        scratch_shapes=[pltpu.VMEM(((N_PAT - 1) * YS_CHUNKS, tm, LANES), F32)],
        compiler_params=_cparams(("arbitrary",)),
        name="in_proj",
    )(x2d, ln_g.reshape(1, d), w_bf, w_kvt)


def _in_proj_sample_kernel(x_ref, g_ref, w_ref, y_ref, wb_ref):
    blk = w_ref.shape[1]
    wb = w_ref[...].astype(BF16)
    wb_ref[...] = wb
    y = jnp.dot(_rms(x_ref[...], g_ref[...]).astype(BF16), wb, preferred_element_type=F32)
    col = pl.program_id(0) * blk + lax.broadcasted_iota(jnp.int32, y.shape, 1)
    is_q = (col < OFF_KA) | ((col >= OFF_QB) & (col < OFF_KB))
    y = jnp.where(is_q, y * Q_SCALE, y)
    y_ref[...] = jnp.where(col >= OFF_GA, _sigmoid(y), y)


def _in_proj_sample(x2d, ln_g, w_in, blk=512):
    m, d = x2d.shape
    in_w = w_in.shape[1]
    assert in_w % blk == 0
    return pl.pallas_call(
        _in_proj_sample_kernel,
        grid=(in_w // blk,),
        in_specs=[pl.BlockSpec((m, d), lambda i: (0, 0)),
                  pl.BlockSpec((1, d), lambda i: (0, 0)),
                  pl.BlockSpec((d, blk), lambda i: (0, i))],
        out_specs=[pl.BlockSpec((m, blk), lambda i: (0, i)), pl.BlockSpec((d, blk), lambda i: (0, i))],
        out_shape=[jax.ShapeDtypeStruct((m, in_w), F32), jax.ShapeDtypeStruct((d, in_w), BF16)],
        compiler_params=_cparams(("arbitrary",)),
        name="in_proj_sample",
    )(x2d, ln_g.reshape(1, d), w_in)


def _band_attn_kernel(*refs, kv_heads, grp, has_sink, want_lse):
    if has_sink:
        sink_ref, refs = refs[0], refs[1:]
    q_ref, kp_ref, kc_ref, vp_ref, vc_ref, bias_ref, o_ref = refs[:7]
    lse_ref = refs[7] if want_lse else None
    step = pl.program_id(2)
    hd = HEAD_DIM
    nt = (((1,), (1,)), ((), ()))
    chains = [(sub, kv * grp + g, slice(kv * hd, (kv + 1) * hd))
              for sub in range(q_ref.shape[0] // BLOCK) for kv in range(kv_heads) for g in range(grp)]
    scores = []
    for sub, h, ks in chains:
        rows = slice(sub * BLOCK, (sub + 1) * BLOCK)
        q = q_ref[rows, h * hd:(h + 1) * hd]
        kp = kp_ref[:, ks] if sub == 0 else kc_ref[(sub - 1) * BLOCK:sub * BLOCK, ks]
        sp = lax.dot_general(q, kp, nt, preferred_element_type=F32) + bias_ref[h, :, 0:BLOCK]
        sc = lax.dot_general(q, kc_ref[rows, ks], nt, preferred_element_type=F32) + bias_ref[h, :, BLOCK:2 * BLOCK]
        if sub == 0:
            sp = jnp.where(step > 0, sp, NEG)
        scores.append((sp, sc))
    maxes = []
    for (sub, h, ks), (sp, sc) in zip(chains, scores):
        m = jnp.maximum(jnp.max(sp, axis=-1, keepdims=True), jnp.max(sc, axis=-1, keepdims=True))
        maxes.append(jnp.maximum(m, sink_ref[h]) if has_sink else m)
    probs = []
    for (sub, h, ks), (sp, sc), m in zip(chains, scores, maxes):
        pp, pc = jnp.exp(sp - m), jnp.exp(sc - m)
        den = jnp.sum(pp, axis=-1, keepdims=True) + jnp.sum(pc, axis=-1, keepdims=True)
        if has_sink:
            den = den + jnp.exp(sink_ref[h] - m)
        probs.append((pp.astype(BF16), pc.astype(BF16), den))
    for (sub, h, ks), (pp, pc, den), m in zip(chains, probs, maxes):
        rows = slice(sub * BLOCK, (sub + 1) * BLOCK)
        vp = vp_ref[:, ks] if sub == 0 else vc_ref[(sub - 1) * BLOCK:sub * BLOCK, ks]
        o = (jnp.dot(pp, vp, preferred_element_type=F32)
             + jnp.dot(pc, vc_ref[rows, ks], preferred_element_type=F32))
        o_ref[rows, h * hd:(h + 1) * hd] = (o / den).astype(o_ref.dtype)
        if want_lse:
            lse_ref[rows, h * hd:(h + 1) * hd] = jnp.broadcast_to(m + jnp.log(den), (BLOCK, hd))


def _band_attn(src, bias, sink, *, q_off, k_off, v_off, kv_heads, grp, want_lse):
    n, dil, l, cols = src.shape
    sub = max(1, ATTN_CHAINS // (kv_heads * grp))
    while l % (sub * BLOCK):
        sub //= 2
    rows = sub * BLOCK
    assert sub >= 1 and l % rows == 0
    nb = l // rows
    qw = kv_heads * grp * HEAD_DIM
    kw = kv_heads * HEAD_DIM
    assert q_off % qw == 0 and k_off % kw == 0 and v_off % kw == 0
    qb, kb, vb = q_off // qw, k_off // kw, v_off // kw
    prev = lambda b: jnp.maximum(sub * b - 1, 0)
    in_specs = [
        pl.BlockSpec((None, None, rows, qw), lambda i, r, b: (i, r, b, qb)),
        pl.BlockSpec((None, None, BLOCK, kw), lambda i, r, b: (i, r, prev(b), kb)),
        pl.BlockSpec((None, None, rows, kw), lambda i, r, b: (i, r, b, kb)),
        pl.BlockSpec((None, None, BLOCK, kw), lambda i, r, b: (i, r, prev(b), vb)),
        pl.BlockSpec((None, None, rows, kw), lambda i, r, b: (i, r, b, vb)),
        pl.BlockSpec(bias.shape, lambda i, r, b: (0, 0, 0)),
    ]
    args = [src, src, src, src, src, bias]
    has_sink = sink is not None
    if has_sink:
        in_specs = [pl.BlockSpec(memory_space=pltpu.SMEM)] + in_specs
        args = [sink.astype(F32)] + args
    out_specs = [pl.BlockSpec((None, None, rows, qw), lambda i, r, b: (i, r, b, 0))]
    out_shape = [jax.ShapeDtypeStruct((n, dil, l, qw), BF16)]
    if want_lse:
        out_specs.append(pl.BlockSpec((None, None, rows, qw), lambda i, r, b: (i, r, b, 0)))
        out_shape.append(jax.ShapeDtypeStruct((n, dil, l, qw), F32))
    return pl.pallas_call(
        functools.partial(_band_attn_kernel, kv_heads=kv_heads, grp=grp, has_sink=has_sink, want_lse=want_lse),
        grid=(n, dil, nb),
        in_specs=in_specs,
        out_specs=out_specs,
        out_shape=out_shape,
        compiler_params=_cparams(("arbitrary", "arbitrary", "arbitrary")),
        name=f"band_attn_d{dil}",
    )(*args)


COL_QA = 0
COL_QB = COL_QA + A_HEADS
COL_KA = COL_QB + B_HEADS
COL_VA = COL_KA + A_KV_HEADS
COL_KB = COL_VA + A_KV_HEADS
COL_VB = COL_KB + B_HEADS
N_COLS = COL_VB + B_HEADS


def _attend_cached(kt, vt, q, k_new, v_new, bias, self_bias, sink, write):
    s = jnp.sum(kt * q, axis=1, keepdims=True) + bias
    s_new = jnp.sum(k_new * q, axis=1, keepdims=True) + self_bias
    yield
    m = jnp.maximum(jnp.max(s, axis=2, keepdims=True), s_new)
    if sink is not None:
        m = jnp.maximum(m, sink)
    yield
    p = jnp.exp(s - m)
    p_new = jnp.exp(s_new - m)
    den = jnp.sum(p, axis=2, keepdims=True) + p_new
    if sink is not None:
        den = den + jnp.exp(sink - m)
    yield
    o = (jnp.sum(vt * p, axis=2, keepdims=True) + v_new * p_new) / den
    write(o, m + jnp.log(den))
    yield


def _sample_work(cols_ref, ca_ref, c1_ref, c2_ref, c3_ref, ba_ref, sa_ref, sink_ref, b1_ref, b2_ref, b3_ref,
                 sb_ref, o_ref, lse_ref, ra_ref, r1_ref, r2_ref, r3_ref):
    cols = cols_ref[...]

    def stack(js):
        return jnp.stack([cols[:, j:j + 1] for j in js])

    o_ref[...] = jnp.zeros_like(o_ref)
    lse_ref[...] = jnp.zeros_like(lse_ref)

    def write_a(o, _):
        for h in range(A_HEADS):
            o_ref[:, COL_QA + h:COL_QA + h + 1] = o[h]

    def write_b(p):
        def write(o, lse):
            for h in range(B_HEADS_PER_PATTERN):
                j = p * B_HEADS_PER_PATTERN + h
                o_ref[:, COL_QB + j:COL_QB + j + 1] = o[h]
                lse_ref[0:1, j:j + 1] = lse[h]
        return write

    kv_of = [h // A_GROUP for h in range(A_HEADS)]
    stages = [_attend_cached(jnp.stack([ca_ref[0, kv] for kv in kv_of]), jnp.stack([ca_ref[1, kv] for kv in kv_of]),
                             stack(range(COL_QA, COL_QA + A_HEADS)), stack([COL_KA + kv for kv in kv_of]),
                             stack([COL_VA + kv for kv in kv_of]), ba_ref[...], sa_ref[...], sink_ref[...], write_a)]
    for p, (c_ref, b_ref) in enumerate(((c1_ref, b1_ref), (c2_ref, b2_ref), (c3_ref, b3_ref))):
        js = range(p * B_HEADS_PER_PATTERN, (p + 1) * B_HEADS_PER_PATTERN)
        stages.append(_attend_cached(c_ref[0], c_ref[1], stack([COL_QB + j for j in js]),
                                     stack([COL_KB + j for j in js]), stack([COL_VB + j for j in js]),
                                     b_ref[...], sb_ref[p], None, write_b(p)))

    def roll_plane(c_ref, r_ref, i, h, new_col):
        w = c_ref.shape[-1]
        x = c_ref[i, h]
        lane = lax.broadcasted_iota(jnp.int32, x.shape, 1)
        r_ref[i, h] = jnp.where(lane == w - 1, cols[:, new_col:new_col + 1], pltpu.roll(x, w - 1, 1))

    planes = []
    for ci, (c_ref, r_ref) in enumerate(((ca_ref, ra_ref), (c1_ref, r1_ref), (c2_ref, r2_ref), (c3_ref, r3_ref))):
        for i, (first_a, first_b) in enumerate(((COL_KA, COL_KB), (COL_VA, COL_VB))):
            for h in range(c_ref.shape[1]):
                new_col = first_a + h if ci == 0 else first_b + (ci - 1) * B_HEADS_PER_PATTERN + h
                planes.append((c_ref.shape[-1], functools.partial(roll_plane, c_ref, r_ref, i, h, new_col)))

    work = []
    total = sum(w for w, _ in planes)
    n_slots = 4 * len(stages)
    done, k = 0, 0
    for slot in range(n_slots):
        work.append(functools.partial(next, stages[slot // 4]))
        while k < len(planes) and done < total * (slot + 1) // n_slots:
            done += planes[k][0]
            work.append(planes[k][1])
            k += 1
    assert k == len(planes)
    return work


def _sample_mix_kernel(*refs):
    for piece in _sample_work(*refs):
        piece()


def _run_interleaved(primary, secondary):
    k = 0
    for i, piece in enumerate(primary):
        piece()
        while k < len(secondary) and k < len(secondary) * (i + 1) // len(primary):
            secondary[k]()
            k += 1
    for piece in secondary[k:]:
        piece()


def _in_proj_mix_kernel(x_ref, g_ref, w_ref, wkvt_ref, *rest, tiles_per_seq, tail_tiles, n_sample_in):
    sample_in = rest[:n_sample_in]
    act_ref, qkv2_ref, qkv3_ref, kvt_ref = rest[n_sample_in:n_sample_in + 4]
    sample_out = rest[n_sample_in + 4:-1]
    ys_ref = rest[-1]
    step = pl.program_id(0)
    tile = step // 2
    for phase in (0, 1):
        @pl.when(step % 2 == phase)
        def _():
            _run_interleaved(_sample_work(*sample_in, *sample_out),
                             _in_proj_work(x_ref, g_ref, w_ref, act_ref, qkv2_ref, qkv3_ref, ys_ref, phase))
            if phase == 1:
                @pl.when(tile % tiles_per_seq >= tiles_per_seq - tail_tiles)
                def _():
                    _kv_tail(x_ref, g_ref, wkvt_ref, kvt_ref)


def _sample_operands(proj_s, caches, rel_bias, sinks):
    n = proj_s.shape[0]
    hd = HEAD_DIM
    vecs = jnp.concatenate([proj_s[:, :OFF_KA], proj_s[:, OFF_QB:OFF_KB], proj_s[:, OFF_KA:OFF_QB],
                            proj_s[:, OFF_KB:OFF_GA]], axis=1).reshape(n, N_COLS, hd)
    cols = jnp.pad(jnp.transpose(vecs, (0, 2, 1)), ((0, 0), (0, 0), (0, LANES - N_COLS)))
    cts = [jnp.transpose(c, (0, 2, 3, 4, 1)) for c in caches]

    ba, sa = _decode_bias(rel_bias[:, :A_HEADS], A_WINDOW, 1, 1)
    sink = sinks.astype(F32).reshape(A_HEADS, 1, 1)
    bbs, sbs = [], []
    for p, (win, dil) in enumerate(B_PATTERNS):
        lo = A_HEADS + p * B_HEADS_PER_PATTERN
        assert caches[1 + p].shape[1] == win == BLOCK * dil
        b, s = _decode_bias(rel_bias[:, lo:lo + B_HEADS_PER_PATTERN], win, dil, 0)
        bbs.append(b)
        sbs.append(s)
    sb = jnp.stack(sbs)

    seq_in = [cols] + cts
    consts = [ba, sa, sink] + bbs + [sb]
    out_shape = [jax.ShapeDtypeStruct((n, hd, LANES), F32), jax.ShapeDtypeStruct((n, SUBLANES, LANES), F32)]
    out_shape += [jax.ShapeDtypeStruct(c.shape, c.dtype) for c in cts]
    return seq_in, consts, out_shape


def _per_seq_spec(a):
    nd = len(a.shape)
    return pl.BlockSpec((None,) + tuple(a.shape[1:]), lambda i: (i,) + (0,) * (nd - 1))


def _const_spec(a):
    nd = a.ndim
    return pl.BlockSpec(a.shape, lambda i: (0,) * nd)


def _sample_results(outs):
    n = outs[0].shape[0]
    o_rows = jnp.transpose(outs[0][:, :, :COL_KA], (0, 2, 1))
    oa = o_rows[:, COL_QA:COL_QB].reshape(n, QA_W)
    ob = o_rows[:, COL_QB:COL_KA].reshape(n, N_PAT, PB_W)
    lse = outs[1][:, 0, :B_HEADS].reshape(n, N_PAT, B_HEADS_PER_PATTERN)
    rolled = [jnp.transpose(r, (0, 4, 1, 2, 3)) for r in outs[2:]]
    return oa, ob, lse, rolled


def _sample_mix(proj_s, caches, rel_bias, sinks):
    seq_in, consts, out_shape = _sample_operands(proj_s, caches, rel_bias, sinks)
    outs = pl.pallas_call(
        _sample_mix_kernel,
        grid=(proj_s.shape[0],),
        in_specs=[_per_seq_spec(a) for a in seq_in] + [_const_spec(a) for a in consts],
        out_specs=[_per_seq_spec(s) for s in out_shape],
        out_shape=out_shape,
        compiler_params=_cparams(("arbitrary",)),
        name="sample_mix",
    )(*seq_in, *consts)
    return _sample_results(outs)


def _in_proj_mix(x2d, ln_g, w_bf, w_kvt, n, t, tm, proj_s, caches, rel_bias, sinks):
    m, d = x2d.shape
    tps = t // tm
    tail = min(max(w for w, _ in B_PATTERNS), t)
    assert t % tm == 0 and tail % tm == 0 and all(tm % (16 * dl) == 0 for _, dl in B_PATTERNS)
    assert proj_s.shape[0] == 2 * (m // tm)
    tail_tiles = tail // tm
    d2, d3 = B_PATTERNS[1][1], B_PATTERNS[2][1]
    seq_in, consts, sample_shape = _sample_operands(proj_s, caches, rel_bias, sinks)
    tile = lambda i: i // 2
    outs = pl.pallas_call(
        functools.partial(_in_proj_mix_kernel, tiles_per_seq=tps, tail_tiles=tail_tiles,
                          n_sample_in=len(seq_in) + len(consts)),
        grid=(2 * (m // tm),),
        in_specs=[pl.BlockSpec((tm, d), lambda i: (tile(i), 0)),
                  pl.BlockSpec((1, d), lambda i: (0, 0)),
                  pl.BlockSpec(w_bf.shape, lambda i: (0, 0), pipeline_mode=pl.Buffered(1)),
                  pl.BlockSpec(w_kvt.shape, lambda i: (0, 0), pipeline_mode=pl.Buffered(1))]
                 + [_per_seq_spec(a) for a in seq_in] + [_const_spec(a) for a in consts],
        out_specs=[pl.BlockSpec((tm, ACT_W), lambda i: (tile(i), 0)),
                   pl.BlockSpec((None, d2, tm // d2, QKV_W), lambda i: (tile(i) // tps, 0, tile(i) % tps, 0)),
                   pl.BlockSpec((None, d3, tm // d3, QKV_W), lambda i: (tile(i) // tps, 0, tile(i) % tps, 0)),
                   pl.BlockSpec((None, KVT_ROWS, tm),
                                lambda i: (tile(i) // tps, 0, jnp.maximum(tile(i) % tps - (tps - tail_tiles), 0)))]
                  + [_per_seq_spec(s) for s in sample_shape],
        out_shape=[jax.ShapeDtypeStruct((m, ACT_W), BF16),
                   jax.ShapeDtypeStruct((n, d2, t // d2, QKV_W), BF16),
                   jax.ShapeDtypeStruct((n, d3, t // d3, QKV_W), BF16),
                   jax.ShapeDtypeStruct((n, KVT_ROWS, tail), F32)] + sample_shape,
        scratch_shapes=[pltpu.VMEM(((N_PAT - 1) * YS_CHUNKS, tm, LANES), F32)],
        compiler_params=_cparams(("arbitrary",)),
        name="in_proj_mix",
    )(x2d, ln_g.reshape(1, d), w_bf, w_kvt, *seq_in, *consts)
    return outs[:4], _sample_results(outs[4:])


def _post_attn_kernel(oa_ref, o1_ref, o2_ref, o3_ref, l1_ref, l2_ref, l3_ref, ga_ref, gb_ref, x_ref,
                      wpa_ref, wpb_ref, wout_ref, ln2_ref, wr_ref, br_ref, *rest, n_extra):
    outs = rest[n_extra:n_extra + 4]
    h_ref, hn_ref, route_ref, cnt_ref = outs
    scr_ref = rest[n_extra + 4]
    if n_extra:
        own_tile = pl.program_id(0) < pl.num_programs(0) - 1

        @pl.when(jnp.logical_not(own_tile))
        def _():
            for dst, src in zip(outs, rest[:n_extra]):
                dst[...] = src[...]

        @pl.when(own_tile)
        def _():
            _post_attn_tile(oa_ref, o1_ref, o2_ref, o3_ref, l1_ref, l2_ref, l3_ref, ga_ref, gb_ref, x_ref,
                            wpa_ref, wpb_ref, wout_ref, ln2_ref, wr_ref, br_ref, *outs, scr_ref)
    else:
        _post_attn_tile(oa_ref, o1_ref, o2_ref, o3_ref, l1_ref, l2_ref, l3_ref, ga_ref, gb_ref, x_ref,
                        wpa_ref, wpb_ref, wout_ref, ln2_ref, wr_ref, br_ref, *outs, scr_ref)


def _post_attn_tile(oa_ref, o1_ref, o2_ref, o3_ref, l1_ref, l2_ref, l3_ref, ga_ref, gb_ref, x_ref,
                    wpa_ref, wpb_ref, wout_ref, ln2_ref, wr_ref, br_ref, h_ref, hn_ref, route_ref, cnt_ref, scr_ref):
    tm = x_ref.shape[0]
    chunks = PB_W // LANES
    parts = POST_ATTN_PARTS if tm % (POST_ATTN_PARTS * SUBLANES * max(d for _, d in B_PATTERNS)) == 0 else 1
    rows_per = tm // parts
    counts = []

    def chain(part):
        r0 = part * rows_per
        rows = slice(r0, r0 + rows_per)

        def token_major(ref, slot):
            dil = ref.shape[0]
            if dil == 1:
                return ref[0, rows].astype(F32)
            n = rows_per // dil
            for c in range(chunks):
                for r in range(dil):
                    scr_ref[slot * chunks + c, pl.ds(r0 + r, n, stride=dil), :] = (
                        ref[r, r0 // dil:r0 // dil + n, c * LANES:(c + 1) * LANES].astype(F32))
            return jnp.concatenate([scr_ref[slot * chunks + c, rows] for c in range(chunks)], axis=1)

        o1, o2, o3 = (token_major(r, s) for s, r in enumerate((o1_ref, o2_ref, o3_ref)))
        l1, l2, l3 = (token_major(r, 3 + s) for s, r in enumerate((l1_ref, l2_ref, l3_ref)))
        yield
        m = jnp.maximum(jnp.maximum(l1, l2), l3)
        a1, a2, a3 = jnp.exp(l1 - m), jnp.exp(l2 - m), jnp.exp(l3 - m)
        ob = (a1 * o1 + a2 * o2 + a3 * o3) / (a1 + a2 + a3)
        yield
        ya = jnp.dot(oa_ref[rows], wpa_ref[...], preferred_element_type=F32)
        yb = jnp.dot(ob.astype(BF16), wpb_ref[...], preferred_element_type=F32)
        merged = ga_ref[rows].astype(F32) * ya + gb_ref[rows].astype(F32) * yb
        yield
        h = x_ref[rows] + jnp.dot(merged.astype(BF16), wout_ref[...], preferred_element_type=F32)
        h_ref[rows] = h
        hn = _rms(h, ln2_ref[...])
        hn_hi = hn.astype(BF16)
        hn_ref[rows] = hn_hi
        yield
        hn_lo = (hn - hn_hi.astype(F32)).astype(BF16)
        logits = (jnp.dot(hn_hi, wr_ref[0], preferred_element_type=F32)
                  + jnp.dot(hn_lo, wr_ref[0], preferred_element_type=F32)
                  + jnp.dot(hn_hi, wr_ref[1], preferred_element_type=F32)) + br_ref[...]
        yield
        lane = lax.broadcasted_iota(jnp.int32, logits.shape, 1)
        is_grp = (lane >= N_EXPERTS) & (lane < N_EXPERTS + MOE_GROUPS)
        lg = jnp.where(is_grp, logits, NEG)
        gmax = jnp.max(lg, axis=-1, keepdims=True)
        yield
        g_lane = jnp.min(jnp.where(lg == gmax, lane, LANES), axis=-1, keepdims=True)
        p_g = 1.0 / jnp.sum(jnp.where(is_grp, jnp.exp(lg - gmax), 0.0), axis=-1, keepdims=True)
        yield
        e_lo = (g_lane - N_EXPERTS) * EXPERTS_PER_GROUP
        in_grp = (lane >= e_lo) & (lane < e_lo + EXPERTS_PER_GROUP)
        le = jnp.where(in_grp, logits, NEG)
        v1 = jnp.max(le, axis=-1, keepdims=True)
        yield
        i1 = jnp.min(jnp.where(le == v1, lane, LANES), axis=-1, keepdims=True)
        yield
        le2 = jnp.where(lane == i1, NEG, le)
        v2 = jnp.max(le2, axis=-1, keepdims=True)
        yield
        i2 = jnp.min(jnp.where(le2 == v2, lane, LANES), axis=-1, keepdims=True)
        yield
        e2 = jnp.exp(v2 - v1)
        w1 = p_g / (1.0 + e2)
        w2 = p_g * e2 / (1.0 + e2)
        route = jnp.where(lane == ROUTE_I1, i1.astype(F32), jnp.where(lane == ROUTE_I2, i2.astype(F32), 0.0))
        route_ref[rows] = route + jnp.where(lane == ROUTE_W1, w1, 0.0) + jnp.where(lane == ROUTE_W2, w2, 0.0)
        picks = (lane == i1).astype(F32) + (lane == i2).astype(F32)
        counts.append(jnp.sum(picks, axis=0, keepdims=True))
        yield

    chains = [chain(p) for p in range(parts)]
    live = list(chains)
    while live:
        for c in list(live):
            if next(c, "done") == "done":
                live.remove(c)
    cnt_ref[...] = jnp.broadcast_to(sum(counts), cnt_ref.shape)


def _post_attn(oa, obs, lses, gates_src, ga_blk, x2d, wpa, wpb, wout, ln2, wr, br, tm, tiles_per_seq, extra):
    m, d = x2d.shape
    tps = tiles_per_seq
    own = m // tm
    n_tiles = own + (extra is not None)
    mine = lambda i: jnp.minimum(i, own - 1)

    def tile(w, col=0):
        return pl.BlockSpec((tm, w), lambda i: (mine(i), col))

    def out_tile(rows, w):
        return pl.BlockSpec((rows, w), lambda i: (i, 0))

    def full(a):
        nd = a.ndim
        return pl.BlockSpec(a.shape, lambda i: (0,) * nd)

    def residue(a):
        dil = a.shape[1]
        return pl.BlockSpec((None, dil, tm // dil, PB_W), lambda i: (mine(i) // tps, 0, mine(i) % tps, 0))

    weights = [wpa, wpb, wout, ln2.reshape(1, d), wr, br]
    scratch = [pltpu.VMEM((6 * PB_W // LANES, tm, LANES), F32)]
    in_specs = ([tile(QA_W)] + [residue(a) for a in obs] + [residue(a) for a in lses]
                + [tile(d, ga_blk), tile(d, ga_blk + 1), tile(d)] + [full(w) for w in weights])
    args = [oa, *obs, *lses, gates_src, gates_src, x2d, *weights]
    if extra is not None:
        in_specs = in_specs + [full(a) for a in extra]
        args = args + list(extra)
    return pl.pallas_call(
        functools.partial(_post_attn_kernel, n_extra=0 if extra is None else len(extra)),
        grid=(n_tiles,),
        in_specs=in_specs,
        out_specs=[out_tile(tm, d), out_tile(tm, d), out_tile(tm, LANES), out_tile(SUBLANES, LANES)],
        out_shape=[jax.ShapeDtypeStruct((n_tiles * tm, d), F32),
                   jax.ShapeDtypeStruct((n_tiles * tm, d), BF16),
                   jax.ShapeDtypeStruct((n_tiles * tm, LANES), F32),
                   jax.ShapeDtypeStruct((n_tiles * SUBLANES, LANES), F32)],
        scratch_shapes=scratch,
        compiler_params=_cparams(("arbitrary",)),
        name="post_attn",
    )(*args)


def _route_plan(cnt_rows, n_tiles, n_xtiles):
    g = SUBLANES
    cnt = cnt_rows.reshape(n_tiles, g, LANES)[:, 0, :N_EXPERTS].astype(jnp.int32)
    cnt8 = (cnt + g - 1) // g * g
    loff = jnp.cumsum(cnt8, axis=1) - cnt8
    boff = jnp.cumsum(cnt8, axis=0) - cnt8
    tot = jnp.sum(cnt8, axis=0)
    region = (tot + MOE_TMX - 1) // MOE_TMX * MOE_TMX
    gend = jnp.cumsum(region)
    gbase = gend - region
    cum_tiles = gend // MOE_TMX
    j = jnp.arange(n_xtiles + MOE_SPILL_TILES, dtype=jnp.int32)
    loff_rows = jnp.zeros((n_tiles, g, LANES), F32).at[:, :, :N_EXPERTS].set(loff[:, None, :].astype(F32))
    k8 = g * jnp.arange(MOE_SLOTS // g, dtype=jnp.int32)
    run_end = loff + cnt8
    e_of_k = jnp.sum(k8[None, :, None] >= run_end[:, None, :], axis=2)
    shift = gbase[None, :] + boff - loff
    picked = jnp.sum(jnp.where(e_of_k[:, :, None] == jnp.arange(N_EXPERTS)[None, None, :], shift[:, None, :], 0), axis=2)
    ids = jnp.arange(N_EXPERTS, dtype=jnp.int32)
    xe = jnp.minimum(jnp.sum(j[:, None] >= cum_tiles[None, :], axis=1), N_EXPERTS - 1).astype(jnp.int32)
    has_tiles = region > 0
    first_tile = cum_tiles - region // MOE_TMX
    of_tile = lambda per_expert: jnp.sum(jnp.where(xe[:, None] == ids[None, :], per_expert[None, :], 0), axis=1)
    later = jnp.where(has_tiles[None, :] & (ids[None, :] > ids[:, None]), ids[None, :], N_EXPERTS)
    next_of = jnp.min(later, axis=1)
    next_of = jnp.where(next_of == N_EXPERTS, -1, next_of)
    xtile_first = (jnp.any(has_tiles[None, :] & (j[:, None] == first_tile[None, :]), axis=1)
                   & (j < cum_tiles[-1])).astype(jnp.int32)
    return dict(
        xtile_first=xtile_first, xtile_next=of_tile(next_of).astype(jnp.int32),
        xtile_parity=of_tile((jnp.cumsum(has_tiles.astype(jnp.int32)) - 1) % 2).astype(jnp.int32),
        chunk_dst=jnp.where(k8[None, :] < run_end[:, -1:], picked, n_xtiles * MOE_TMX).reshape(-1).astype(jnp.int32)
        + jnp.tile(k8, n_tiles),
        zst=gbase + tot, znch=(region - tot) // g,
        ztot=jnp.sum((region - tot) // g).reshape(1),
        xtile_expert=xe,
        n_used=cum_tiles[-1:].astype(jnp.int32),
        loff_rows=loff_rows.reshape(n_tiles * g, LANES))


def _local_slots(route, loff_row):
    tm = route.shape[0]
    lane = lax.broadcasted_iota(jnp.int32, (tm, LANES), 1)
    e1 = lane == route[:, ROUTE_I1:ROUTE_I1 + 1].astype(jnp.int32)
    e2 = lane == route[:, ROUTE_I2:ROUTE_I2 + 1].astype(jnp.int32)
    earlier = (lax.broadcasted_iota(jnp.int32, (tm, tm), 1) < lax.broadcasted_iota(jnp.int32, (tm, tm), 0))
    earlier = earlier.astype(BF16)
    c1 = jnp.dot(earlier, e1.astype(BF16), preferred_element_type=F32)
    c2 = jnp.dot(earlier, e2.astype(BF16), preferred_element_type=F32)
    cnt1 = jnp.sum(e1.astype(F32), axis=0, keepdims=True)
    pos1 = jnp.sum(jnp.where(e1, c1 + loff_row, 0.0), axis=1, keepdims=True)
    pos2 = jnp.sum(jnp.where(e2, c2 + cnt1 + loff_row, 0.0), axis=1, keepdims=True)
    slot = lax.broadcasted_iota(jnp.int32, (tm, MOE_SLOTS), 1)
    return slot == pos1.astype(jnp.int32), slot == pos2.astype(jnp.int32)


def _pack_bf16_pairs(x):
    c = x.shape[1] // 2
    bits = lambda v: lax.bitcast_convert_type(v.astype(BF16).astype(F32), jnp.uint32)
    return bits(x[:, :c]) | (bits(x[:, c:]) >> 16)


def _unpack_bf16_pairs(w):
    hi = lax.bitcast_convert_type(w & jnp.uint32(0xFFFF0000), F32).astype(BF16)
    lo = lax.bitcast_convert_type(w << 16, F32).astype(BF16)
    return jnp.concatenate([hi, lo], axis=1)


def _split3(w):
    hi = w.astype(BF16).astype(F32)
    mid = (w - hi).astype(BF16).astype(F32)
    return hi, mid, (w - hi - mid).astype(BF16).astype(F32)


def _moe_scatter_kernel(cdst_s, zst_s, znch_s, ztot_s, nused_s,
                        hn_ref, route_ref, loffv_ref, xs_hbm, buf_ref, zero_ref, sem, zsem, tsem):
    b = pl.program_id(0)
    nb = pl.num_programs(0)
    slot = b % 2
    g = SUBLANES
    d = hn_ref.shape[1]
    n_xtiles = xs_hbm.shape[0] // MOE_TMX - MOE_SPILL_TILES

    def run_copy(s, src_row, dst_row):
        return pltpu.make_async_copy(buf_ref.at[s, pl.ds(src_row, g)], xs_hbm.at[pl.ds(dst_row, g)], sem.at[s])

    def zero_copy(dst_row):
        return pltpu.make_async_copy(zero_ref.at[pl.ds(0, g)], xs_hbm.at[pl.ds(dst_row, g)], zsem)

    def zero_tile_copy(j):
        return pltpu.make_async_copy(zero_ref, xs_hbm.at[pl.ds(pl.multiple_of(j * MOE_TMX, MOE_TMX), MOE_TMX)], tsem)

    n_chunks = MOE_SLOTS // g
    spill = n_xtiles * MOE_TMX

    def wait_buffer(s):
        for _ in range(n_chunks):
            run_copy(s, 0, 0).wait()

    def ship_buffer(s, dst_row_of):
        for k in range(n_chunks):
            run_copy(s, k * g, pl.multiple_of(dst_row_of(k), g)).start()

    @pl.when(b == 0)
    def _():
        zero_ref[...] = jnp.zeros_like(zero_ref)
        buf_ref[1] = jnp.zeros(buf_ref.shape[1:], buf_ref.dtype)

        def per_expert(e, c):
            def per_chunk(k, c2):
                zero_copy(pl.multiple_of(zst_s[e] + k * g, g)).start()
                return c2
            return lax.fori_loop(0, znch_s[e], per_chunk, c)
        lax.fori_loop(0, N_EXPERTS, per_expert, 0)
        lax.fori_loop(nused_s[0], n_xtiles, lambda j, c: (zero_tile_copy(j).start(), c)[1], 0)

    @pl.when(b >= 1)
    def _():
        wait_buffer(slot)

    prev = jnp.maximum(b - 1, 0) * n_chunks
    ship_buffer(1 - slot, lambda k: jnp.where(b == 0, spill + k * g, cdst_s[prev + k]))

    route = route_ref[...]
    p1, p2 = _local_slots(route, loffv_ref[0:1, :])
    tn = (((0,), (0,)), ((), ()))
    picks = (p1 | p2).astype(BF16)
    buf_ref[slot, :, 0:d // 2] = _pack_bf16_pairs(lax.dot_general(picks, hn_ref[...], tn, preferred_element_type=F32))
    lane = lax.broadcasted_iota(jnp.int32, route.shape, 1)
    meta = jnp.zeros((MOE_SLOTS, LANES), F32)
    for p, col in ((p1, ROUTE_W1), (p2, ROUTE_W2)):
        w = route[:, col:col + 1]
        parts = _split3(w)
        wm = sum(jnp.where(lane == k, part, 0.0) for k, part in enumerate(parts))
        meta = meta + lax.dot_general(p.astype(BF16), wm.astype(BF16), tn, preferred_element_type=F32)
    buf_ref[slot, :, d // 2:d // 2 + LANES] = lax.bitcast_convert_type(meta, jnp.uint32)

    @pl.when(b == nb - 1)
    def _():
        wait_buffer(1 - slot)
        ship_buffer(slot, lambda k: cdst_s[b * n_chunks + k])
        wait_buffer(slot)
        lax.fori_loop(0, ztot_s[0], lambda _, c: (zero_copy(0).wait(), c)[1], 0)
        lax.fori_loop(nused_s[0], n_xtiles, lambda j, c: (zero_tile_copy(j).wait(), c)[1], 0)


def _moe_scatter(plan, hn_all, route_all, n_tiles, n_xtiles):
    d = hn_all.shape[1]
    tm = MOE_TM
    grid_spec = pltpu.PrefetchScalarGridSpec(
        num_scalar_prefetch=5,
        grid=(n_tiles,),
        in_specs=[pl.BlockSpec((tm, d), lambda i, *_: (i, 0)),
                  pl.BlockSpec((tm, LANES), lambda i, *_: (i, 0)),
                  pl.BlockSpec((SUBLANES, LANES), lambda i, *_: (i, 0))],
        out_specs=pl.BlockSpec(memory_space=pl.ANY),
        scratch_shapes=[pltpu.VMEM((2, MOE_SLOTS, d // 2 + LANES), jnp.uint32),
                        pltpu.VMEM((MOE_TMX, d // 2 + LANES), jnp.uint32),
                        pltpu.SemaphoreType.DMA((2,)),
                        pltpu.SemaphoreType.DMA(()),
                        pltpu.SemaphoreType.DMA(())])
    return pl.pallas_call(
        _moe_scatter_kernel,
        grid_spec=grid_spec,
        out_shape=jax.ShapeDtypeStruct(((n_xtiles + MOE_SPILL_TILES) * MOE_TMX, d // 2 + LANES), jnp.uint32),
        compiler_params=_cparams(("arbitrary",)),
        name="moe_scatter",
    )(plan["chunk_dst"], plan["zst"], plan["znch"], plan["ztot"], plan["n_used"],
      hn_all, route_all, plan["loff_rows"])


def _moe_experts_kernel(xe_s, first_s, next_s, par_s, nused_s, x_ref, w1_hbm, w3_hbm, w2_hbm, y_ref,
                        w1f_ref, w3f_ref, w2f_ref, w1b_ref, w3b_ref, w2b_ref, sem):
    d = w1b_ref.shape[0]

    def weight_copies(e, p):
        return [pltpu.make_async_copy(src.at[e], dst.at[p], sem.at[p, i])
                for i, (src, dst) in enumerate(((w1_hbm, w1f_ref), (w3_hbm, w3f_ref), (w2_hbm, w2f_ref)))]

    def one_tile(j, rows):
        @pl.when(j < nused_s[0])
        def _():
            @pl.when(first_s[j] == 1)
            def _():
                p = par_s[j]

                @pl.when(j == 0)
                def _():
                    for c in weight_copies(xe_s[j], p):
                        c.start()
                for c in weight_copies(xe_s[j], p):
                    c.wait()
                w1b_ref[...] = w1f_ref[p].astype(BF16)
                w3b_ref[...] = w3f_ref[p].astype(BF16)
                w2b_ref[...] = w2f_ref[p].astype(BF16)

                @pl.when(next_s[j] >= 0)
                def _():
                    for c in weight_copies(next_s[j], 1 - p):
                        c.start()

            x = _unpack_bf16_pairs(x_ref[rows, 0:d // 2])
            gate = jnp.sum(lax.bitcast_convert_type(x_ref[rows, d // 2:d // 2 + LANES], F32), axis=1, keepdims=True)
            a = jnp.dot(x, w1b_ref[...], preferred_element_type=F32)
            b = jnp.dot(x, w3b_ref[...], preferred_element_type=F32)
            hh = (a * _sigmoid(a)) * b * gate
            y_ref[rows] = _pack_bf16_pairs(jnp.dot(hh.astype(BF16), w2b_ref[...], preferred_element_type=F32))

        @pl.when(j >= nused_s[0])
        def _():
            y_ref[rows] = jnp.zeros((MOE_TMX, y_ref.shape[1]), y_ref.dtype)

    for sub in range(MOE_XSTEP_TILES):
        one_tile(MOE_XSTEP_TILES * pl.program_id(0) + sub, slice(sub * MOE_TMX, (sub + 1) * MOE_TMX))


def _moe_experts(plan, xs, w1, w3, w2, n_xtiles):
    ne, d, f = w1.shape
    per = MOE_XSTEP_TILES
    assert (n_xtiles + MOE_SPILL_TILES) % per == 0
    last = lambda s, nu: jnp.maximum(jnp.minimum(s, (nu[0] - 1) // per), 0)
    any_spec = pl.BlockSpec(memory_space=pl.ANY)
    grid_spec = pltpu.PrefetchScalarGridSpec(
        num_scalar_prefetch=5,
        grid=((n_xtiles + MOE_SPILL_TILES) // per,),
        in_specs=[pl.BlockSpec((per * MOE_TMX, d // 2 + LANES), lambda s, xe, fi, nx, pa, nu: (last(s, nu), 0)),
                  any_spec, any_spec, any_spec],
        out_specs=pl.BlockSpec((per * MOE_TMX, d // 2), lambda s, *_: (s, 0)),
        scratch_shapes=[pltpu.VMEM((2, d, f), F32), pltpu.VMEM((2, d, f), F32), pltpu.VMEM((2, f, d), F32),
                        pltpu.VMEM((d, f), BF16), pltpu.VMEM((d, f), BF16), pltpu.VMEM((f, d), BF16),
                        pltpu.SemaphoreType.DMA((2, 3))])
    return pl.pallas_call(
        _moe_experts_kernel,
        grid_spec=grid_spec,
        out_shape=jax.ShapeDtypeStruct(((n_xtiles + MOE_SPILL_TILES) * MOE_TMX, d // 2), jnp.uint32),
        compiler_params=_cparams(("arbitrary",)),
        name="moe_experts",
    )(plan["xtile_expert"], plan["xtile_first"], plan["xtile_next"], plan["xtile_parity"], plan["n_used"],
      xs, w1, w3, w2)


def _moe_combine_kernel(cdst_s, ys_hbm, route_ref, loffv_ref, h_ref, lnf_ref, y_ref,
                        buf_ref, sem, *, tile0, final_norm):
    b = pl.program_id(0)
    nb = pl.num_programs(0)
    slot = b % 2
    g = SUBLANES
    n_chunks = MOE_SLOTS // g
    tile = b + tile0

    def run_copy(s, src_row, dst_row):
        return pltpu.make_async_copy(ys_hbm.at[pl.ds(src_row, g)], buf_ref.at[s, pl.ds(dst_row, g)], sem.at[s])

    def fetch(t, s):
        for k in range(n_chunks):
            run_copy(s, pl.multiple_of(cdst_s[t * n_chunks + k], g), k * g).start()

    def wait_buffer(s):
        for _ in range(n_chunks):
            run_copy(s, 0, 0).wait()

    @pl.when(b == 0)
    def _():
        fetch(tile, slot)

    fetch(jnp.minimum(tile + 1, tile0 + nb - 1), 1 - slot)
    wait_buffer(slot)

    @pl.when(b == nb - 1)
    def _():
        wait_buffer(1 - slot)

    p1, p2 = _local_slots(route_ref[...], loffv_ref[0:1, :])
    picks = (p1 | p2).astype(BF16)
    y = h_ref[...] + jnp.dot(picks, _unpack_bf16_pairs(buf_ref[slot]), preferred_element_type=F32)
    if final_norm:
        y = _rms(y, lnf_ref[...])
    y_ref[...] = y


def _moe_combine(plan, ys, route_all, h_all, lnf, tile0, n_tiles, final_norm):
    d = h_all.shape[1]
    tm = MOE_TM
    grid_spec = pltpu.PrefetchScalarGridSpec(
        num_scalar_prefetch=1,
        grid=(n_tiles,),
        in_specs=[pl.BlockSpec(memory_space=pl.ANY),
                  pl.BlockSpec((tm, LANES), lambda i, *_: (i + tile0, 0)),
                  pl.BlockSpec((SUBLANES, LANES), lambda i, *_: (i + tile0, 0)),
                  pl.BlockSpec((tm, d), lambda i, *_: (i + tile0, 0)),
                  pl.BlockSpec((1, d), lambda i, *_: (0, 0))],
        out_specs=pl.BlockSpec((tm, d), lambda i, *_: (i, 0)),
        scratch_shapes=[pltpu.VMEM((2, MOE_SLOTS, d // 2), jnp.uint32), pltpu.SemaphoreType.DMA((2,))])
    return pl.pallas_call(
        functools.partial(_moe_combine_kernel, tile0=tile0, final_norm=final_norm),
        grid_spec=grid_spec,
        out_shape=jax.ShapeDtypeStruct((n_tiles * tm, d), F32),
        compiler_params=_cparams(("arbitrary",)),
        name="moe_combine",
    )(plan["chunk_dst"], ys, route_all, plan["loff_rows"], h_all, lnf.reshape(1, d))


def _prompt_states(kvt, n, t):
    hd = HEAD_DIM
    tail = kvt.shape[2]

    def state(k_lo, v_lo, rows, heads, win):
        w = min(win, t)
        kv = jnp.stack([kvt[:, k_lo:k_lo + rows, tail - w:], kvt[:, v_lo:v_lo + rows, tail - w:]], axis=1)
        return jnp.transpose(kv.reshape(n, 2, heads, hd, w), (0, 4, 1, 2, 3))

    out = [state(0, KA_W, KA_W, A_KV_HEADS, A_WINDOW)]
    for p, (win, _) in enumerate(B_PATTERNS):
        out.append(state(2 * KA_W + p * PB_W, 2 * KA_W + QB_W + p * PB_W, PB_W, B_HEADS_PER_PATTERN, win))
    return out


def _layer(xp, xs, caches, rel_bias, ln1, w_in, sinks, w_pa, w_pb, w_out, ln2, w_rg, b_rg, w_re, b_re,
           w1, w3, w2, lnf, final_norm):
    n, t, d = xp.shape
    ns = xs.shape[0]
    assert xs.shape[1] == 1 and OFF_GA + 2 * d == w_in.shape[1] and P_QA == 2 * d
    w_kvt = _kv_weights_transposed(w_in)
    wpa, wpb, wout = w_pa.astype(BF16), w_pb.astype(BF16), w_out.astype(BF16)
    wr = jnp.zeros((d, LANES), F32).at[:, :N_EXPERTS].set(w_re).at[:, N_EXPERTS:N_EXPERTS + MOE_GROUPS].set(w_rg)
    br = jnp.zeros((1, LANES), F32).at[0, :N_EXPERTS].set(b_re).at[0, N_EXPERTS:N_EXPERTS + MOE_GROUPS].set(b_rg)
    wr_hi = wr.astype(BF16)
    wr = jnp.stack([wr_hi, (wr - wr_hi.astype(F32)).astype(BF16)])

    tm = MOE_TM
    xp2 = xp.reshape(n * t, d)
    xs2 = xs.reshape(ns, d)
    proj_s, w_bf = _in_proj_sample(xs2, ln1, w_in)
    if ns == 2 * (n * t // MIX_TM):
        (act, qkv2, qkv3, kvt), sampled = _in_proj_mix(xp2, ln1, w_bf, w_kvt, n, t, MIX_TM, proj_s, caches,
                                                       rel_bias, sinks)
    else:
        act, qkv2, qkv3, kvt = _in_proj_prompt(xp2, ln1, w_bf, w_kvt, n, t, IN_PROJ_TM)
        sampled = _sample_mix(proj_s, caches, rel_bias, sinks)
    oa_s, ob_s, lse_s, st_s = sampled
    act4 = act.reshape(n, 1, t, ACT_W)
    bias_a = _band_bias(rel_bias[:, :A_HEADS], A_WINDOW - 1, 1)
    (oa,) = _band_attn(act4, bias_a, sinks, q_off=P_QA, k_off=P_KA, v_off=P_VA,
                       kv_heads=A_KV_HEADS, grp=A_GROUP, want_lse=False)
    obs, lses = [], []
    for p, (win, dil) in enumerate(B_PATTERNS):
        lo = A_HEADS + p * B_HEADS_PER_PATTERN
        bias_p = _band_bias(rel_bias[:, lo:lo + B_HEADS_PER_PATTERN], win // dil, dil)
        src, base = ((act4, P_B), (qkv2, 0), (qkv3, 0))[p]
        o, lse = _band_attn(src, bias_p, None, q_off=base, k_off=base + PB_W, v_off=base + 2 * PB_W,
                            kv_heads=B_HEADS_PER_PATTERN, grp=1, want_lse=True)
        obs.append(o)
        lses.append(lse)
    assert ns <= MOE_TM
    p_tiles = n * t // MOE_TM
    n_tiles = p_tiles + 1
    m_all = n_tiles * MOE_TM
    st_p = _prompt_states(kvt, n, t)

    rows = lambda a: jnp.pad(a, ((0, MOE_TM - ns), (0, 0)))
    obs_s = [rows(ob_s[:, p].astype(BF16)).reshape(1, 1, MOE_TM, PB_W) for p in range(N_PAT)]
    lses_s = [rows(jnp.repeat(lse_s[:, p], HEAD_DIM, axis=-1)).reshape(1, 1, MOE_TM, PB_W) for p in range(N_PAT)]
    gates_s = rows(proj_s[:, OFF_GA:].astype(BF16))
    routed_s = _post_attn(rows(oa_s.astype(BF16)), obs_s, lses_s, gates_s, 0, rows(xs2),
                          wpa, wpb, wout, ln2, wr, br, MOE_TM, 1, None)
    h_all, hn_all, route_all, cnt_all = _post_attn(oa.reshape(n * t, QA_W), obs, lses, act, 0, xp2,
                                                   wpa, wpb, wout, ln2, wr, br, tm, t // tm, routed_s)

    max_rows = 2 * m_all + n_tiles * N_EXPERTS * (SUBLANES - 1) + N_EXPERTS * (MOE_TMX - SUBLANES)
    n_xtiles = -(-max_rows // MOE_TMX)
    n_xtiles += -(n_xtiles + MOE_SPILL_TILES) % MOE_XSTEP_TILES
    plan = _route_plan(cnt_all, n_tiles, n_xtiles)
    xs_sorted = _moe_scatter(plan, hn_all, route_all, n_tiles, n_xtiles)
    ys_sorted = _moe_experts(plan, xs_sorted, w1, w3, w2, n_xtiles)
    yp = _moe_combine(plan, ys_sorted, route_all, h_all, lnf, 0, p_tiles, final_norm).reshape(n, t, d)
    ys = _moe_combine(plan, ys_sorted, route_all, h_all, lnf, p_tiles, 1, final_norm)[:ns].reshape(ns, 1, d)
    return yp, ys, st_p, st_s


def kernel(x_prompt, x_sample, cache_a_kv, cache_b1_kv, cache_b2_kv, cache_b3_kv, rel_bias, ln1_g, w_in, sinks,
           w_pa, w_pb, w_out, ln2_g, w_rg, b_rg, w_re, b_re, w1, w3, w2, lnf_g):
    depth = w_in.shape[0]
    assert depth >= 1
    xp, xs = x_prompt, x_sample
    new_p = [[] for _ in range(4)]
    new_s = [[] for _ in range(4)]
    for l in range(depth):
        caches = (cache_a_kv[l], cache_b1_kv[l], cache_b2_kv[l], cache_b3_kv[l])
        xp, xs, st_p, st_s = _layer(xp, xs, caches, rel_bias, ln1_g[l], w_in[l], sinks[l], w_pa[l], w_pb[l],
                                    w_out[l], ln2_g[l], w_rg[l], b_rg[l], w_re[l], b_re[l], w1[l], w3[l], w2[l],
                                    lnf_g, l == depth - 1)
        for i in range(4):
            new_p[i].append(st_p[i])
            new_s[i].append(st_s[i])
    a_p, b1_p, b2_p, b3_p = [jnp.stack(v) for v in new_p]
    a_s, b1_s, b2_s, b3_s = [jnp.stack(v) for v in new_s]
    return (xp, xs, a_p, a_s, b1_p, b1_s, b2_p, b2_s, b3_p, b3_s)
```

```python
import functools
import math

import numpy as np
import jax
import jax.numpy as jnp
from jax import lax
from jax.experimental import pallas as pl
from jax.experimental.pallas import tpu as pltpu

F32 = jnp.float32
BF16 = jnp.bfloat16

HEAD_DIM = 64
A_HEADS = 8
A_KV_HEADS = 2
A_GROUP = A_HEADS // A_KV_HEADS
A_WINDOW = 128
B_PATTERNS = ((128, 1), (512, 4), (2048, 16))
N_PAT = len(B_PATTERNS)
B_HEADS_PER_PATTERN = 4
B_HEADS = B_HEADS_PER_PATTERN * N_PAT
BLOCK = 128
ATTN_CHAINS = 16
NUM_BUCKETS = 32
MAX_DISTANCE = 2048
MOE_GROUPS = 4
EXPERTS_PER_GROUP = 8
N_EXPERTS = MOE_GROUPS * EXPERTS_PER_GROUP
EPS = 1e-6
NEG = -1e30
LANES = 128
SUBLANES = 8
ROUTE_I1, ROUTE_I2, ROUTE_W1, ROUTE_W2 = 0, 1, 2, 3
IN_PROJ_TM = 512
MIX_TM = 256
MOE_TM = 256
MOE_SLOTS = 2 * MOE_TM + N_EXPERTS * SUBLANES
MOE_TMX = 256
MOE_SPILL_TILES = -(-MOE_SLOTS // MOE_TMX)
MOE_XSTEP_TILES = 4
Q_SCALE = HEAD_DIM ** -0.5

QA_W = A_HEADS * HEAD_DIM
KA_W = A_KV_HEADS * HEAD_DIM
QB_W = B_HEADS * HEAD_DIM
PB_W = B_HEADS_PER_PATTERN * HEAD_DIM
QKV_W = 3 * PB_W
OFF_KA = QA_W
OFF_QB = OFF_KA + 2 * KA_W
OFF_KB = OFF_QB + QB_W
OFF_VB = OFF_KB + QB_W
OFF_GA = OFF_VB + QB_W
P_QA = 2048
P_KA = P_QA + QA_W
P_VA = P_KA + KA_W
P_B = P_VA + KA_W
ACT_W = P_B + QKV_W
KVT_ROWS = 2 * KA_W + 2 * QB_W

VMEM_LIMIT = 56 * 1024 * 1024


def _cparams(sem):
    return pltpu.CompilerParams(dimension_semantics=sem, vmem_limit_bytes=VMEM_LIMIT)


def _bucket_np(dist):
    dist = np.asarray(dist, np.int64)
    max_exact = NUM_BUCKETS // 2
    df = np.maximum(dist, max_exact).astype(np.float64)
    large = max_exact + (np.log(df / max_exact) / math.log(MAX_DISTANCE / max_exact)
                         * (NUM_BUCKETS - max_exact)).astype(np.int64)
    return np.where(dist < max_exact, dist, np.minimum(large, NUM_BUCKETS - 1))


def _table_rows(table_cols, dist, valid):
    onehot = (_bucket_np(dist)[:, None] == np.arange(NUM_BUCKETS)[None, :]).astype(np.float32)
    rows = jnp.einsum("ck,kh->hc", jnp.asarray(onehot), table_cols.astype(F32), precision=lax.Precision.HIGHEST)
    return jnp.where(jnp.asarray(valid)[None, :], rows, NEG)


def _band_bias(table_cols, max_dist, dilation):
    period = 3 * BLOCK
    m = np.arange(period)
    k = np.where(m < 2 * BLOCK, m, m - period)
    dist = BLOCK - k
    valid = (dist >= 0) & (dist <= max_dist) & (m != 2 * BLOCK)
    v = _table_rows(table_cols, np.clip(dist, 0, None) * dilation, valid)
    heads = v.shape[0]
    flat = jnp.tile(v, (1, BLOCK))[:, :BLOCK * (period - 1)]
    return flat.reshape(heads, BLOCK, period - 1)[:, :, :2 * BLOCK]


def _decode_bias(table_cols, width, dilation, first_valid):
    c = np.arange(width)
    valid = (c % dilation == 0) & (c >= first_valid)
    rows = _table_rows(table_cols, width - c, valid)
    self_bias = _table_rows(table_cols, np.zeros((1,), np.int64), np.ones((1,), bool))
    return rows[:, None, :], self_bias[:, None, :]


def _rms(x, g):
    return (x * lax.rsqrt(jnp.mean(x * x, axis=-1, keepdims=True) + EPS)) * g


def _sigmoid(x):
    return 1.0 / (1.0 + jnp.exp(-x))


YS_CHUNKS = QKV_W // LANES


def _in_proj_work(x_ref, g_ref, w_ref, act_ref, qkv2_ref, qkv3_ref, ys_ref, phase):
    tm = x_ref.shape[0]
    cache = {}

    def xb():
        if "xb" not in cache:
            cache["xb"] = _rms(x_ref[...], g_ref[...]).astype(BF16)
        return cache["xb"]

    def proj(lo, hi):
        return jnp.dot(xb(), w_ref[:, lo:hi], preferred_element_type=F32)

    def gates(c):
        act_ref[:, c:c + 512] = _sigmoid(proj(OFF_GA + c, OFF_GA + c + 512)).astype(BF16)

    def mixer_a_q():
        act_ref[:, P_QA:P_KA] = (proj(0, OFF_KA) * Q_SCALE).astype(BF16)

    def mixer_a_kv():
        act_ref[:, P_KA:P_B] = proj(OFF_KA, OFF_QB).astype(BF16)

    def pattern_part(p, j):
        lo = (OFF_QB, OFF_KB, OFF_VB)[j] + p * PB_W
        part = proj(lo, lo + PB_W)
        if j == 0:
            part = part * Q_SCALE
        if p == 0:
            act_ref[:, P_B + j * PB_W:P_B + (j + 1) * PB_W] = part.astype(BF16)
        else:
            for c in range(PB_W // LANES):
                ys_ref[(p - 1) * YS_CHUNKS + j * (PB_W // LANES) + c] = part[:, c * LANES:(c + 1) * LANES]

    def regroup(p, out_ref):
        dil = B_PATTERNS[p][1]
        for c in range(YS_CHUNKS):
            for r in range(dil):
                out_ref[r, :, c * LANES:(c + 1) * LANES] = (
                    ys_ref[(p - 1) * YS_CHUNKS + c, pl.ds(r, tm // dil, stride=dil), :].astype(BF16))

    work = []
    if phase in (None, 0):
        work += [functools.partial(gates, c) for c in range(0, P_QA, 512)] + [mixer_a_q, mixer_a_kv]
    if phase in (None, 1):
        for p in range(N_PAT):
            work += [functools.partial(pattern_part, p, j) for j in range(3)]
            if p > 0:
                work.append(functools.partial(regroup, p, (qkv2_ref, qkv3_ref)[p - 1]))
    return work


def _transpose_cast_kernel(w_ref, o_ref):
    o_ref[...] = w_ref[...].T.astype(o_ref.dtype)


def _kv_weights_transposed(w_in):
    d = w_in.shape[0]
    blk = 2 * KA_W
    assert OFF_KA % blk == 0 and OFF_KB % blk == 0 and KVT_ROWS % blk == 0
    first, rest = OFF_KA // blk, OFF_KB // blk - 1
    return pl.pallas_call(
        _transpose_cast_kernel,
        grid=(KVT_ROWS // blk,),
        in_specs=[pl.BlockSpec((d, blk), lambda i: (0, jnp.where(i == 0, first, rest + i)))],
        out_specs=pl.BlockSpec((blk, d), lambda i: (i, 0)),
        out_shape=jax.ShapeDtypeStruct((KVT_ROWS, d), BF16),
        compiler_params=_cparams(("arbitrary",)),
        name="kv_weights_t",
    )(w_in)


def _kv_tail(x_ref, g_ref, wkvt_ref, kvt_ref):
    xb = _rms(x_ref[...], g_ref[...]).astype(BF16)
    kvt_ref[...] = lax.dot_general(wkvt_ref[...], xb, (((1,), (1,)), ((), ())), preferred_element_type=F32)


def _in_proj_kernel(x_ref, g_ref, w_ref, wkvt_ref, act_ref, qkv2_ref, qkv3_ref, kvt_ref, ys_ref, *,
                    tiles_per_seq, tail_tiles):
    for piece in _in_proj_work(x_ref, g_ref, w_ref, act_ref, qkv2_ref, qkv3_ref, ys_ref, None):
        piece()

    @pl.when(pl.program_id(0) % tiles_per_seq >= tiles_per_seq - tail_tiles)
    def _():
        _kv_tail(x_ref, g_ref, wkvt_ref, kvt_ref)


def _in_proj_prompt(x2d, ln_g, w_bf, w_kvt, n, t, tm):
    m, d = x2d.shape
    tps = t // tm
    tail = min(max(w for w, _ in B_PATTERNS), t)
    assert t % tm == 0 and tail % tm == 0 and all(tm % (16 * dl) == 0 for _, dl in B_PATTERNS)
    tail_tiles = tail // tm
    d2, d3 = B_PATTERNS[1][1], B_PATTERNS[2][1]
    return pl.pallas_call(
        functools.partial(_in_proj_kernel, tiles_per_seq=tps, tail_tiles=tail_tiles),
        grid=(m // tm,),
        in_specs=[pl.BlockSpec((tm, d), lambda i: (i, 0)),
                  pl.BlockSpec((1, d), lambda i: (0, 0)),
                  pl.BlockSpec(w_bf.shape, lambda i: (0, 0), pipeline_mode=pl.Buffered(1)),
                  pl.BlockSpec(w_kvt.shape, lambda i: (0, 0), pipeline_mode=pl.Buffered(1))],
        out_specs=[pl.BlockSpec((tm, ACT_W), lambda i: (i, 0)),
                   pl.BlockSpec((None, d2, tm // d2, QKV_W), lambda i: (i // tps, 0, i % tps, 0)),
                   pl.BlockSpec((None, d3, tm // d3, QKV_W), lambda i: (i // tps, 0, i % tps, 0)),
                   pl.BlockSpec((None, KVT_ROWS, tm),
                                lambda i: (i // tps, 0, jnp.maximum(i % tps - (tps - tail_tiles), 0)))],
        out_shape=[jax.ShapeDtypeStruct((m, ACT_W), BF16),
                   jax.ShapeDtypeStruct((n, d2, t // d2, QKV_W), BF16),
                   jax.ShapeDtypeStruct((n, d3, t // d3, QKV_W), BF16),
                   jax.ShapeDtypeStruct((n, KVT_ROWS, tail), F32)],
        scratch_shapes=[pltpu.VMEM(((N_PAT - 1) * YS_CHUNKS, tm, LANES), F32)],
        compiler_params=_cparams(("arbitrary",)),
        name="in_proj",
    )(x2d, ln_g.reshape(1, d), w_bf, w_kvt)


def _in_proj_sample_kernel(x_ref, g_ref, w_ref, y_ref, wb_ref):
    blk = w_ref.shape[1]
    wb = w_ref[...].astype(BF16)
    wb_ref[...] = wb
    y = jnp.dot(_rms(x_ref[...], g_ref[...]).astype(BF16), wb, preferred_element_type=F32)
    col = pl.program_id(0) * blk + lax.broadcasted_iota(jnp.int32, y.shape, 1)
    is_q = (col < OFF_KA) | ((col >= OFF_QB) & (col < OFF_KB))
    y = jnp.where(is_q, y * Q_SCALE, y)
    y_ref[...] = jnp.where(col >= OFF_GA, _sigmoid(y), y)


def _in_proj_sample(x2d, ln_g, w_in, blk=512):
    m, d = x2d.shape
    in_w = w_in.shape[1]
    assert in_w % blk == 0
    return pl.pallas_call(
        _in_proj_sample_kernel,
        grid=(in_w // blk,),
        in_specs=[pl.BlockSpec((m, d), lambda i: (0, 0)),
                  pl.BlockSpec((1, d), lambda i: (0, 0)),
                  pl.BlockSpec((d, blk), lambda i: (0, i))],
        out_specs=[pl.BlockSpec((m, blk), lambda i: (0, i)), pl.BlockSpec((d, blk), lambda i: (0, i))],
        out_shape=[jax.ShapeDtypeStruct((m, in_w), F32), jax.ShapeDtypeStruct((d, in_w), BF16)],
        compiler_params=_cparams(("arbitrary",)),
        name="in_proj_sample",
    )(x2d, ln_g.reshape(1, d), w_in)


def _band_attn_kernel(*refs, kv_heads, grp, has_sink, want_lse):
    if has_sink:
        sink_ref, refs = refs[0], refs[1:]
    q_ref, kp_ref, kc_ref, vp_ref, vc_ref, bias_ref, o_ref = refs[:7]
    lse_ref = refs[7] if want_lse else None
    step = pl.program_id(2)
    hd = HEAD_DIM
    nt = (((1,), (1,)), ((), ()))
    chains = [(sub, kv * grp + g, slice(kv * hd, (kv + 1) * hd))
              for sub in range(q_ref.shape[0] // BLOCK) for kv in range(kv_heads) for g in range(grp)]
    scores = []
    for sub, h, ks in chains:
        rows = slice(sub * BLOCK, (sub + 1) * BLOCK)
        q = q_ref[rows, h * hd:(h + 1) * hd]
        kp = kp_ref[:, ks] if sub == 0 else kc_ref[(sub - 1) * BLOCK:sub * BLOCK, ks]
        sp = lax.dot_general(q, kp, nt, preferred_element_type=F32) + bias_ref[h, :, 0:BLOCK]
        sc = lax.dot_general(q, kc_ref[rows, ks], nt, preferred_element_type=F32) + bias_ref[h, :, BLOCK:2 * BLOCK]
        if sub == 0:
            sp = jnp.where(step > 0, sp, NEG)
        scores.append((sp, sc))
    maxes = []
    for (sub, h, ks), (sp, sc) in zip(chains, scores):
        m = jnp.maximum(jnp.max(sp, axis=-1, keepdims=True), jnp.max(sc, axis=-1, keepdims=True))
        maxes.append(jnp.maximum(m, sink_ref[h]) if has_sink else m)
    probs = []
    for (sub, h, ks), (sp, sc), m in zip(chains, scores, maxes):
        pp, pc = jnp.exp(sp - m), jnp.exp(sc - m)
        den = jnp.sum(pp, axis=-1, keepdims=True) + jnp.sum(pc, axis=-1, keepdims=True)
        if has_sink:
            den = den + jnp.exp(sink_ref[h] - m)
        probs.append((pp.astype(BF16), pc.astype(BF16), den))
    for (sub, h, ks), (pp, pc, den), m in zip(chains, probs, maxes):
        rows = slice(sub * BLOCK, (sub + 1) * BLOCK)
        vp = vp_ref[:, ks] if sub == 0 else vc_ref[(sub - 1) * BLOCK:sub * BLOCK, ks]
        o = (jnp.dot(pp, vp, preferred_element_type=F32)
             + jnp.dot(pc, vc_ref[rows, ks], preferred_element_type=F32))
        o_ref[rows, h * hd:(h + 1) * hd] = (o / den).astype(o_ref.dtype)
        if want_lse:
            lse_ref[rows, h * hd:(h + 1) * hd] = jnp.broadcast_to(m + jnp.log(den), (BLOCK, hd))


def _band_attn(src, bias, sink, *, q_off, k_off, v_off, kv_heads, grp, want_lse):
    n, dil, l, cols = src.shape
    sub = max(1, ATTN_CHAINS // (kv_heads * grp))
    while l % (sub * BLOCK):
        sub //= 2
    rows = sub * BLOCK
    assert sub >= 1 and l % rows == 0
    nb = l // rows
    qw = kv_heads * grp * HEAD_DIM
    kw = kv_heads * HEAD_DIM
    assert q_off % qw == 0 and k_off % kw == 0 and v_off % kw == 0
    qb, kb, vb = q_off // qw, k_off // kw, v_off // kw
    prev = lambda b: jnp.maximum(sub * b - 1, 0)
    in_specs = [
        pl.BlockSpec((None, None, rows, qw), lambda i, r, b: (i, r, b, qb)),
        pl.BlockSpec((None, None, BLOCK, kw), lambda i, r, b: (i, r, prev(b), kb)),
        pl.BlockSpec((None, None, rows, kw), lambda i, r, b: (i, r, b, kb)),
        pl.BlockSpec((None, None, BLOCK, kw), lambda i, r, b: (i, r, prev(b), vb)),
        pl.BlockSpec((None, None, rows, kw), lambda i, r, b: (i, r, b, vb)),
        pl.BlockSpec(bias.shape, lambda i, r, b: (0, 0, 0)),
    ]
    args = [src, src, src, src, src, bias]
    has_sink = sink is not None
    if has_sink:
        in_specs = [pl.BlockSpec(memory_space=pltpu.SMEM)] + in_specs
        args = [sink.astype(F32)] + args
    out_specs = [pl.BlockSpec((None, None, rows, qw), lambda i, r, b: (i, r, b, 0))]
    out_shape = [jax.ShapeDtypeStruct((n, dil, l, qw), BF16)]
    if want_lse:
        out_specs.append(pl.BlockSpec((None, None, rows, qw), lambda i, r, b: (i, r, b, 0)))
        out_shape.append(jax.ShapeDtypeStruct((n, dil, l, qw), F32))
    return pl.pallas_call(
        functools.partial(_band_attn_kernel, kv_heads=kv_heads, grp=grp, has_sink=has_sink, want_lse=want_lse),
        grid=(n, dil, nb),
        in_specs=in_specs,
        out_specs=out_specs,
        out_shape=out_shape,
        compiler_params=_cparams(("arbitrary", "arbitrary", "arbitrary")),
        name=f"band_attn_d{dil}",
    )(*args)


COL_QA = 0
COL_QB = COL_QA + A_HEADS
COL_KA = COL_QB + B_HEADS
COL_VA = COL_KA + A_KV_HEADS
COL_KB = COL_VA + A_KV_HEADS
COL_VB = COL_KB + B_HEADS
N_COLS = COL_VB + B_HEADS


def _attend_cached(kt, vt, q, k_new, v_new, bias, self_bias, sink, write):
    s = jnp.sum(kt * q, axis=1, keepdims=True) + bias
    s_new = jnp.sum(k_new * q, axis=1, keepdims=True) + self_bias
    yield
    m = jnp.maximum(jnp.max(s, axis=2, keepdims=True), s_new)
    if sink is not None:
        m = jnp.maximum(m, sink)
    yield
    p = jnp.exp(s - m)
    p_new = jnp.exp(s_new - m)
    den = jnp.sum(p, axis=2, keepdims=True) + p_new
    if sink is not None:
        den = den + jnp.exp(sink - m)
    yield
    o = (jnp.sum(vt * p, axis=2, keepdims=True) + v_new * p_new) / den
    write(o, m + jnp.log(den))
    yield


def _sample_work(cols_ref, ca_ref, c1_ref, c2_ref, c3_ref, ba_ref, sa_ref, sink_ref, b1_ref, b2_ref, b3_ref,
                 sb_ref, o_ref, lse_ref, ra_ref, r1_ref, r2_ref, r3_ref):
    cols = cols_ref[...]

    def stack(js):
        return jnp.stack([cols[:, j:j + 1] for j in js])

    o_ref[...] = jnp.zeros_like(o_ref)
    lse_ref[...] = jnp.zeros_like(lse_ref)

    def write_a(o, _):
        for h in range(A_HEADS):
            o_ref[:, COL_QA + h:COL_QA + h + 1] = o[h]

    def write_b(p):
        def write(o, lse):
            for h in range(B_HEADS_PER_PATTERN):
                j = p * B_HEADS_PER_PATTERN + h
                o_ref[:, COL_QB + j:COL_QB + j + 1] = o[h]
                lse_ref[0:1, j:j + 1] = lse[h]
        return write

    kv_of = [h // A_GROUP for h in range(A_HEADS)]
    stages = [_attend_cached(jnp.stack([ca_ref[0, kv] for kv in kv_of]), jnp.stack([ca_ref[1, kv] for kv in kv_of]),
                             stack(range(COL_QA, COL_QA + A_HEADS)), stack([COL_KA + kv for kv in kv_of]),
                             stack([COL_VA + kv for kv in kv_of]), ba_ref[...], sa_ref[...], sink_ref[...], write_a)]
    for p, (c_ref, b_ref) in enumerate(((c1_ref, b1_ref), (c2_ref, b2_ref), (c3_ref, b3_ref))):
        js = range(p * B_HEADS_PER_PATTERN, (p + 1) * B_HEADS_PER_PATTERN)
        stages.append(_attend_cached(c_ref[0], c_ref[1], stack([COL_QB + j for j in js]),
                                     stack([COL_KB + j for j in js]), stack([COL_VB + j for j in js]),
                                     b_ref[...], sb_ref[p], None, write_b(p)))

    def roll_plane(c_ref, r_ref, i, h, new_col):
        w = c_ref.shape[-1]
        x = c_ref[i, h]
        lane = lax.broadcasted_iota(jnp.int32, x.shape, 1)
        r_ref[i, h] = jnp.where(lane == w - 1, cols[:, new_col:new_col + 1], pltpu.roll(x, w - 1, 1))

    planes = []
    for ci, (c_ref, r_ref) in enumerate(((ca_ref, ra_ref), (c1_ref, r1_ref), (c2_ref, r2_ref), (c3_ref, r3_ref))):
        for i, (first_a, first_b) in enumerate(((COL_KA, COL_KB), (COL_VA, COL_VB))):
            for h in range(c_ref.shape[1]):
                new_col = first_a + h if ci == 0 else first_b + (ci - 1) * B_HEADS_PER_PATTERN + h
                planes.append((c_ref.shape[-1], functools.partial(roll_plane, c_ref, r_ref, i, h, new_col)))

    work = []
    total = sum(w for w, _ in planes)
    n_slots = 4 * len(stages)
    done, k = 0, 0
    for slot in range(n_slots):
        work.append(functools.partial(next, stages[slot // 4]))
        while k < len(planes) and done < total * (slot + 1) // n_slots:
            done += planes[k][0]
            work.append(planes[k][1])
            k += 1
    assert k == len(planes)
    return work


def _sample_mix_kernel(*refs):
    for piece in _sample_work(*refs):
        piece()


def _run_interleaved(primary, secondary):
    k = 0
    for i, piece in enumerate(primary):
        piece()
        while k < len(secondary) and k < len(secondary) * (i + 1) // len(primary):
            secondary[k]()
            k += 1
    for piece in secondary[k:]:
        piece()


def _in_proj_mix_kernel(x_ref, g_ref, w_ref, wkvt_ref, *rest, tiles_per_seq, tail_tiles, n_sample_in):
    sample_in = rest[:n_sample_in]
    act_ref, qkv2_ref, qkv3_ref, kvt_ref = rest[n_sample_in:n_sample_in + 4]
    sample_out = rest[n_sample_in + 4:-1]
    ys_ref = rest[-1]
    step = pl.program_id(0)
    tile = step // 2
    for phase in (0, 1):
        @pl.when(step % 2 == phase)
        def _():
            _run_interleaved(_sample_work(*sample_in, *sample_out),
                             _in_proj_work(x_ref, g_ref, w_ref, act_ref, qkv2_ref, qkv3_ref, ys_ref, phase))
            if phase == 1:
                @pl.when(tile % tiles_per_seq >= tiles_per_seq - tail_tiles)
                def _():
                    _kv_tail(x_ref, g_ref, wkvt_ref, kvt_ref)


def _sample_operands(proj_s, caches, rel_bias, sinks):
    n = proj_s.shape[0]
    hd = HEAD_DIM
    vecs = jnp.concatenate([proj_s[:, :OFF_KA], proj_s[:, OFF_QB:OFF_KB], proj_s[:, OFF_KA:OFF_QB],
                            proj_s[:, OFF_KB:OFF_GA]], axis=1).reshape(n, N_COLS, hd)
    cols = jnp.pad(jnp.transpose(vecs, (0, 2, 1)), ((0, 0), (0, 0), (0, LANES - N_COLS)))
    cts = [jnp.transpose(c, (0, 2, 3, 4, 1)) for c in caches]

    ba, sa = _decode_bias(rel_bias[:, :A_HEADS], A_WINDOW, 1, 1)
    sink = sinks.astype(F32).reshape(A_HEADS, 1, 1)
    bbs, sbs = [], []
    for p, (win, dil) in enumerate(B_PATTERNS):
        lo = A_HEADS + p * B_HEADS_PER_PATTERN
        assert caches[1 + p].shape[1] == win == BLOCK * dil
        b, s = _decode_bias(rel_bias[:, lo:lo + B_HEADS_PER_PATTERN], win, dil, 0)
        bbs.append(b)
        sbs.append(s)
    sb = jnp.stack(sbs)

    seq_in = [cols] + cts
    consts = [ba, sa, sink] + bbs + [sb]
    out_shape = [jax.ShapeDtypeStruct((n, hd, LANES), F32), jax.ShapeDtypeStruct((n, SUBLANES, LANES), F32)]
    out_shape += [jax.ShapeDtypeStruct(c.shape, c.dtype) for c in cts]
    return seq_in, consts, out_shape


def _per_seq_spec(a):
    nd = len(a.shape)
    return pl.BlockSpec((None,) + tuple(a.shape[1:]), lambda i: (i,) + (0,) * (nd - 1))


def _const_spec(a):
    nd = a.ndim
    return pl.BlockSpec(a.shape, lambda i: (0,) * nd)


def _sample_results(outs):
    n = outs[0].shape[0]
    o_rows = jnp.transpose(outs[0][:, :, :COL_KA], (0, 2, 1))
    oa = o_rows[:, COL_QA:COL_QB].reshape(n, QA_W)
    ob = o_rows[:, COL_QB:COL_KA].reshape(n, N_PAT, PB_W)
    lse = outs[1][:, 0, :B_HEADS].reshape(n, N_PAT, B_HEADS_PER_PATTERN)
    rolled = [jnp.transpose(r, (0, 4, 1, 2, 3)) for r in outs[2:]]
    return oa, ob, lse, rolled


def _sample_mix(proj_s, caches, rel_bias, sinks):
    seq_in, consts, out_shape = _sample_operands(proj_s, caches, rel_bias, sinks)
    outs = pl.pallas_call(
        _sample_mix_kernel,
        grid=(proj_s.shape[0],),
        in_specs=[_per_seq_spec(a) for a in seq_in] + [_const_spec(a) for a in consts],
        out_specs=[_per_seq_spec(s) for s in out_shape],
        out_shape=out_shape,
        compiler_params=_cparams(("arbitrary",)),
        name="sample_mix",
    )(*seq_in, *consts)
    return _sample_results(outs)


def _in_proj_mix(x2d, ln_g, w_bf, w_kvt, n, t, tm, proj_s, caches, rel_bias, sinks):
    m, d = x2d.shape
    tps = t // tm
    tail = min(max(w for w, _ in B_PATTERNS), t)
    assert t % tm == 0 and tail % tm == 0 and all(tm % (16 * dl) == 0 for _, dl in B_PATTERNS)
    assert proj_s.shape[0] == 2 * (m // tm)
    tail_tiles = tail // tm
    d2, d3 = B_PATTERNS[1][1], B_PATTERNS[2][1]
    seq_in, consts, sample_shape = _sample_operands(proj_s, caches, rel_bias, sinks)
    tile = lambda i: i // 2
    outs = pl.pallas_call(
        functools.partial(_in_proj_mix_kernel, tiles_per_seq=tps, tail_tiles=tail_tiles,
                          n_sample_in=len(seq_in) + len(consts)),
        grid=(2 * (m // tm),),
        in_specs=[pl.BlockSpec((tm, d), lambda i: (tile(i), 0)),
                  pl.BlockSpec((1, d), lambda i: (0, 0)),
                  pl.BlockSpec(w_bf.shape, lambda i: (0, 0), pipeline_mode=pl.Buffered(1)),
                  pl.BlockSpec(w_kvt.shape, lambda i: (0, 0), pipeline_mode=pl.Buffered(1))]
                 + [_per_seq_spec(a) for a in seq_in] + [_const_spec(a) for a in consts],
        out_specs=[pl.BlockSpec((tm, ACT_W), lambda i: (tile(i), 0)),
                   pl.BlockSpec((None, d2, tm // d2, QKV_W), lambda i: (tile(i) // tps, 0, tile(i) % tps, 0)),
                   pl.BlockSpec((None, d3, tm // d3, QKV_W), lambda i: (tile(i) // tps, 0, tile(i) % tps, 0)),
                   pl.BlockSpec((None, KVT_ROWS, tm),
                                lambda i: (tile(i) // tps, 0, jnp.maximum(tile(i) % tps - (tps - tail_tiles), 0)))]
                  + [_per_seq_spec(s) for s in sample_shape],
        out_shape=[jax.ShapeDtypeStruct((m, ACT_W), BF16),
                   jax.ShapeDtypeStruct((n, d2, t // d2, QKV_W), BF16),
                   jax.ShapeDtypeStruct((n, d3, t // d3, QKV_W), BF16),
                   jax.ShapeDtypeStruct((n, KVT_ROWS, tail), F32)] + sample_shape,
        scratch_shapes=[pltpu.VMEM(((N_PAT - 1) * YS_CHUNKS, tm, LANES), F32)],
        compiler_params=_cparams(("arbitrary",)),
        name="in_proj_mix",
    )(x2d, ln_g.reshape(1, d), w_bf, w_kvt, *seq_in, *consts)
    return outs[:4], _sample_results(outs[4:])


def _post_attn_kernel(oa_ref, o1_ref, o2_ref, o3_ref, l1_ref, l2_ref, l3_ref, ga_ref, gb_ref, x_ref,
                      wpa_ref, wpb_ref, wout_ref, ln2_ref, wr_ref, br_ref, *rest, n_extra):
    outs = rest[n_extra:n_extra + 4]
    h_ref, hn_ref, route_ref, cnt_ref = outs
    scr_ref = rest[n_extra + 4]
    if n_extra:
        own_tile = pl.program_id(0) < pl.num_programs(0) - 1

        @pl.when(jnp.logical_not(own_tile))
        def _():
            for dst, src in zip(outs, rest[:n_extra]):
                dst[...] = src[...]

        @pl.when(own_tile)
        def _():
            _post_attn_tile(oa_ref, o1_ref, o2_ref, o3_ref, l1_ref, l2_ref, l3_ref, ga_ref, gb_ref, x_ref,
                            wpa_ref, wpb_ref, wout_ref, ln2_ref, wr_ref, br_ref, *outs, scr_ref)
    else:
        _post_attn_tile(oa_ref, o1_ref, o2_ref, o3_ref, l1_ref, l2_ref, l3_ref, ga_ref, gb_ref, x_ref,
                        wpa_ref, wpb_ref, wout_ref, ln2_ref, wr_ref, br_ref, *outs, scr_ref)


def _post_attn_tile(oa_ref, o1_ref, o2_ref, o3_ref, l1_ref, l2_ref, l3_ref, ga_ref, gb_ref, x_ref,
                    wpa_ref, wpb_ref, wout_ref, ln2_ref, wr_ref, br_ref, h_ref, hn_ref, route_ref, cnt_ref, scr_ref):
    tm = x_ref.shape[0]
    chunks = PB_W // LANES

    def token_major(ref, slot):
        dil = ref.shape[0]
        if dil == 1:
            return ref[0].astype(F32)
        for c in range(chunks):
            for r in range(dil):
                scr_ref[slot * chunks + c, pl.ds(r, tm // dil, stride=dil), :] = (
                    ref[r, :, c * LANES:(c + 1) * LANES].astype(F32))
        return jnp.concatenate([scr_ref[slot * chunks + c] for c in range(chunks)], axis=1)

    o1, o2, o3 = (token_major(r, s) for s, r in enumerate((o1_ref, o2_ref, o3_ref)))
    l1, l2, l3 = (token_major(r, 3 + s) for s, r in enumerate((l1_ref, l2_ref, l3_ref)))
    m = jnp.maximum(jnp.maximum(l1, l2), l3)
    a1, a2, a3 = jnp.exp(l1 - m), jnp.exp(l2 - m), jnp.exp(l3 - m)
    ob = (a1 * o1 + a2 * o2 + a3 * o3) / (a1 + a2 + a3)
    ya = jnp.dot(oa_ref[...], wpa_ref[...], preferred_element_type=F32)
    yb = jnp.dot(ob.astype(BF16), wpb_ref[...], preferred_element_type=F32)
    merged = ga_ref[...].astype(F32) * ya + gb_ref[...].astype(F32) * yb
    h = x_ref[...] + jnp.dot(merged.astype(BF16), wout_ref[...], preferred_element_type=F32)
    h_ref[...] = h
    hn = _rms(h, ln2_ref[...])
    hn_hi = hn.astype(BF16)
    hn_ref[...] = hn_hi

    hn_lo = (hn - hn_hi.astype(F32)).astype(BF16)
    logits = (jnp.dot(hn_hi, wr_ref[0], preferred_element_type=F32)
              + jnp.dot(hn_lo, wr_ref[0], preferred_element_type=F32)
              + jnp.dot(hn_hi, wr_ref[1], preferred_element_type=F32)) + br_ref[...]
    lane = lax.broadcasted_iota(jnp.int32, logits.shape, 1)
    is_grp = (lane >= N_EXPERTS) & (lane < N_EXPERTS + MOE_GROUPS)
    lg = jnp.where(is_grp, logits, NEG)
    gmax = jnp.max(lg, axis=-1, keepdims=True)
    g_lane = jnp.min(jnp.where(lg == gmax, lane, LANES), axis=-1, keepdims=True)
    p_g = 1.0 / jnp.sum(jnp.where(is_grp, jnp.exp(lg - gmax), 0.0), axis=-1, keepdims=True)
    e_lo = (g_lane - N_EXPERTS) * EXPERTS_PER_GROUP
    in_grp = (lane >= e_lo) & (lane < e_lo + EXPERTS_PER_GROUP)
    le = jnp.where(in_grp, logits, NEG)
    v1 = jnp.max(le, axis=-1, keepdims=True)
    i1 = jnp.min(jnp.where(le == v1, lane, LANES), axis=-1, keepdims=True)
    le2 = jnp.where(lane == i1, NEG, le)
    v2 = jnp.max(le2, axis=-1, keepdims=True)
    i2 = jnp.min(jnp.where(le2 == v2, lane, LANES), axis=-1, keepdims=True)
    e2 = jnp.exp(v2 - v1)
    w1 = p_g / (1.0 + e2)
    w2 = p_g * e2 / (1.0 + e2)
    route = jnp.where(lane == ROUTE_I1, i1.astype(F32), jnp.where(lane == ROUTE_I2, i2.astype(F32), 0.0))
    route_ref[...] = route + jnp.where(lane == ROUTE_W1, w1, 0.0) + jnp.where(lane == ROUTE_W2, w2, 0.0)
    picks = (lane == i1).astype(F32) + (lane == i2).astype(F32)
    cnt_ref[...] = jnp.broadcast_to(jnp.sum(picks, axis=0, keepdims=True), cnt_ref.shape)


def _post_attn(oa, obs, lses, gates_src, ga_blk, x2d, wpa, wpb, wout, ln2, wr, br, tm, tiles_per_seq, extra):
    m, d = x2d.shape
    tps = tiles_per_seq
    own = m // tm
    n_tiles = own + (extra is not None)
    mine = lambda i: jnp.minimum(i, own - 1)

    def tile(w, col=0):
        return pl.BlockSpec((tm, w), lambda i: (mine(i), col))

    def out_tile(rows, w):
        return pl.BlockSpec((rows, w), lambda i: (i, 0))

    def full(a):
        nd = a.ndim
        return pl.BlockSpec(a.shape, lambda i: (0,) * nd)

    def residue(a):
        dil = a.shape[1]
        return pl.BlockSpec((None, dil, tm // dil, PB_W), lambda i: (mine(i) // tps, 0, mine(i) % tps, 0))

    weights = [wpa, wpb, wout, ln2.reshape(1, d), wr, br]
    scratch = [pltpu.VMEM((6 * PB_W // LANES, tm, LANES), F32)]
    in_specs = ([tile(QA_W)] + [residue(a) for a in obs] + [residue(a) for a in lses]
                + [tile(d, ga_blk), tile(d, ga_blk + 1), tile(d)] + [full(w) for w in weights])
    args = [oa, *obs, *lses, gates_src, gates_src, x2d, *weights]
    if extra is not None:
        in_specs = in_specs + [full(a) for a in extra]
        args = args + list(extra)
    return pl.pallas_call(
        functools.partial(_post_attn_kernel, n_extra=0 if extra is None else len(extra)),
        grid=(n_tiles,),
        in_specs=in_specs,
        out_specs=[out_tile(tm, d), out_tile(tm, d), out_tile(tm, LANES), out_tile(SUBLANES, LANES)],
        out_shape=[jax.ShapeDtypeStruct((n_tiles * tm, d), F32),
                   jax.ShapeDtypeStruct((n_tiles * tm, d), BF16),
                   jax.ShapeDtypeStruct((n_tiles * tm, LANES), F32),
                   jax.ShapeDtypeStruct((n_tiles * SUBLANES, LANES), F32)],
        scratch_shapes=scratch,
        compiler_params=_cparams(("arbitrary",)),
        name="post_attn",
    )(*args)


def _route_plan(cnt_rows, n_tiles, n_xtiles):
    g = SUBLANES
    cnt = cnt_rows.reshape(n_tiles, g, LANES)[:, 0, :N_EXPERTS].astype(jnp.int32)
    cnt8 = (cnt + g - 1) // g * g
    loff = jnp.cumsum(cnt8, axis=1) - cnt8
    boff = jnp.cumsum(cnt8, axis=0) - cnt8
    tot = jnp.sum(cnt8, axis=0)
    region = (tot + MOE_TMX - 1) // MOE_TMX * MOE_TMX
    gend = jnp.cumsum(region)
    gbase = gend - region
    cum_tiles = gend // MOE_TMX
    j = jnp.arange(n_xtiles + MOE_SPILL_TILES, dtype=jnp.int32)
    loff_rows = jnp.zeros((n_tiles, g, LANES), F32).at[:, :, :N_EXPERTS].set(loff[:, None, :].astype(F32))
    k8 = g * jnp.arange(MOE_SLOTS // g, dtype=jnp.int32)
    run_end = loff + cnt8
    e_of_k = jnp.sum(k8[None, :, None] >= run_end[:, None, :], axis=2)
    shift = gbase[None, :] + boff - loff
    picked = jnp.sum(jnp.where(e_of_k[:, :, None] == jnp.arange(N_EXPERTS)[None, None, :], shift[:, None, :], 0), axis=2)
    ids = jnp.arange(N_EXPERTS, dtype=jnp.int32)
    xe = jnp.minimum(jnp.sum(j[:, None] >= cum_tiles[None, :], axis=1), N_EXPERTS - 1).astype(jnp.int32)
    has_tiles = region > 0
    first_tile = cum_tiles - region // MOE_TMX
    of_tile = lambda per_expert: jnp.sum(jnp.where(xe[:, None] == ids[None, :], per_expert[None, :], 0), axis=1)
    later = jnp.where(has_tiles[None, :] & (ids[None, :] > ids[:, None]), ids[None, :], N_EXPERTS)
    next_of = jnp.min(later, axis=1)
    next_of = jnp.where(next_of == N_EXPERTS, -1, next_of)
    xtile_first = (jnp.any(has_tiles[None, :] & (j[:, None] == first_tile[None, :]), axis=1)
                   & (j < cum_tiles[-1])).astype(jnp.int32)
    return dict(
        xtile_first=xtile_first, xtile_next=of_tile(next_of).astype(jnp.int32),
        xtile_parity=of_tile((jnp.cumsum(has_tiles.astype(jnp.int32)) - 1) % 2).astype(jnp.int32),
        chunk_dst=jnp.where(k8[None, :] < run_end[:, -1:], picked, n_xtiles * MOE_TMX).reshape(-1).astype(jnp.int32)
        + jnp.tile(k8, n_tiles),
        zst=gbase + tot, znch=(region - tot) // g,
        ztot=jnp.sum((region - tot) // g).reshape(1),
        xtile_expert=xe,
        n_used=cum_tiles[-1:].astype(jnp.int32),
        loff_rows=loff_rows.reshape(n_tiles * g, LANES))


def _local_slots(route, loff_row):
    tm = route.shape[0]
    lane = lax.broadcasted_iota(jnp.int32, (tm, LANES), 1)
    e1 = lane == route[:, ROUTE_I1:ROUTE_I1 + 1].astype(jnp.int32)
    e2 = lane == route[:, ROUTE_I2:ROUTE_I2 + 1].astype(jnp.int32)
    earlier = (lax.broadcasted_iota(jnp.int32, (tm, tm), 1) < lax.broadcasted_iota(jnp.int32, (tm, tm), 0))
    earlier = earlier.astype(BF16)
    c1 = jnp.dot(earlier, e1.astype(BF16), preferred_element_type=F32)
    c2 = jnp.dot(earlier, e2.astype(BF16), preferred_element_type=F32)
    cnt1 = jnp.sum(e1.astype(F32), axis=0, keepdims=True)
    pos1 = jnp.sum(jnp.where(e1, c1 + loff_row, 0.0), axis=1, keepdims=True)
    pos2 = jnp.sum(jnp.where(e2, c2 + cnt1 + loff_row, 0.0), axis=1, keepdims=True)
    slot = lax.broadcasted_iota(jnp.int32, (tm, MOE_SLOTS), 1)
    return slot == pos1.astype(jnp.int32), slot == pos2.astype(jnp.int32)


def _pack_bf16_pairs(x):
    c = x.shape[1] // 2
    bits = lambda v: lax.bitcast_convert_type(v.astype(BF16).astype(F32), jnp.uint32)
    return bits(x[:, :c]) | (bits(x[:, c:]) >> 16)


def _unpack_bf16_pairs(w):
    hi = lax.bitcast_convert_type(w & jnp.uint32(0xFFFF0000), F32).astype(BF16)
    lo = lax.bitcast_convert_type(w << 16, F32).astype(BF16)
    return jnp.concatenate([hi, lo], axis=1)


def _split3(w):
    hi = w.astype(BF16).astype(F32)
    mid = (w - hi).astype(BF16).astype(F32)
    return hi, mid, (w - hi - mid).astype(BF16).astype(F32)


def _moe_scatter_kernel(cdst_s, zst_s, znch_s, ztot_s, nused_s,
                        hn_ref, route_ref, loffv_ref, xs_hbm, buf_ref, zero_ref, sem, zsem, tsem):
    b = pl.program_id(0)
    nb = pl.num_programs(0)
    slot = b % 2
    g = SUBLANES
    d = hn_ref.shape[1]
    n_xtiles = xs_hbm.shape[0] // MOE_TMX - MOE_SPILL_TILES

    def run_copy(s, src_row, dst_row):
        return pltpu.make_async_copy(buf_ref.at[s, pl.ds(src_row, g)], xs_hbm.at[pl.ds(dst_row, g)], sem.at[s])

    def zero_copy(dst_row):
        return pltpu.make_async_copy(zero_ref.at[pl.ds(0, g)], xs_hbm.at[pl.ds(dst_row, g)], zsem)

    def zero_tile_copy(j):
        return pltpu.make_async_copy(zero_ref, xs_hbm.at[pl.ds(pl.multiple_of(j * MOE_TMX, MOE_TMX), MOE_TMX)], tsem)

    n_chunks = MOE_SLOTS // g
    spill = n_xtiles * MOE_TMX

    def wait_buffer(s):
        for _ in range(n_chunks):
            run_copy(s, 0, 0).wait()

    def ship_buffer(s, dst_row_of):
        for k in range(n_chunks):
            run_copy(s, k * g, pl.multiple_of(dst_row_of(k), g)).start()

    @pl.when(b == 0)
    def _():
        zero_ref[...] = jnp.zeros_like(zero_ref)
        buf_ref[1] = jnp.zeros(buf_ref.shape[1:], buf_ref.dtype)

        def per_expert(e, c):
            def per_chunk(k, c2):
                zero_copy(pl.multiple_of(zst_s[e] + k * g, g)).start()
                return c2
            return lax.fori_loop(0, znch_s[e], per_chunk, c)
        lax.fori_loop(0, N_EXPERTS, per_expert, 0)
        lax.fori_loop(nused_s[0], n_xtiles, lambda j, c: (zero_tile_copy(j).start(), c)[1], 0)

    @pl.when(b >= 1)
    def _():
        wait_buffer(slot)

    prev = jnp.maximum(b - 1, 0) * n_chunks
    ship_buffer(1 - slot, lambda k: jnp.where(b == 0, spill + k * g, cdst_s[prev + k]))

    route = route_ref[...]
    p1, p2 = _local_slots(route, loffv_ref[0:1, :])
    tn = (((0,), (0,)), ((), ()))
    picks = (p1 | p2).astype(BF16)
    buf_ref[slot, :, 0:d // 2] = _pack_bf16_pairs(lax.dot_general(picks, hn_ref[...], tn, preferred_element_type=F32))
    lane = lax.broadcasted_iota(jnp.int32, route.shape, 1)
    meta = jnp.zeros((MOE_SLOTS, LANES), F32)
    for p, col in ((p1, ROUTE_W1), (p2, ROUTE_W2)):
        w = route[:, col:col + 1]
        parts = _split3(w)
        wm = sum(jnp.where(lane == k, part, 0.0) for k, part in enumerate(parts))
        meta = meta + lax.dot_general(p.astype(BF16), wm.astype(BF16), tn, preferred_element_type=F32)
    buf_ref[slot, :, d // 2:d // 2 + LANES] = lax.bitcast_convert_type(meta, jnp.uint32)

    @pl.when(b == nb - 1)
    def _():
        wait_buffer(1 - slot)
        ship_buffer(slot, lambda k: cdst_s[b * n_chunks + k])
        wait_buffer(slot)
        lax.fori_loop(0, ztot_s[0], lambda _, c: (zero_copy(0).wait(), c)[1], 0)
        lax.fori_loop(nused_s[0], n_xtiles, lambda j, c: (zero_tile_copy(j).wait(), c)[1], 0)


def _moe_scatter(plan, hn_all, route_all, n_tiles, n_xtiles):
    d = hn_all.shape[1]
    tm = MOE_TM
    grid_spec = pltpu.PrefetchScalarGridSpec(
        num_scalar_prefetch=5,
        grid=(n_tiles,),
        in_specs=[pl.BlockSpec((tm, d), lambda i, *_: (i, 0)),
                  pl.BlockSpec((tm, LANES), lambda i, *_: (i, 0)),
                  pl.BlockSpec((SUBLANES, LANES), lambda i, *_: (i, 0))],
        out_specs=pl.BlockSpec(memory_space=pl.ANY),
        scratch_shapes=[pltpu.VMEM((2, MOE_SLOTS, d // 2 + LANES), jnp.uint32),
                        pltpu.VMEM((MOE_TMX, d // 2 + LANES), jnp.uint32),
                        pltpu.SemaphoreType.DMA((2,)),
                        pltpu.SemaphoreType.DMA(()),
                        pltpu.SemaphoreType.DMA(())])
    return pl.pallas_call(
        _moe_scatter_kernel,
        grid_spec=grid_spec,
        out_shape=jax.ShapeDtypeStruct(((n_xtiles + MOE_SPILL_TILES) * MOE_TMX, d // 2 + LANES), jnp.uint32),
        compiler_params=_cparams(("arbitrary",)),
        name="moe_scatter",
    )(plan["chunk_dst"], plan["zst"], plan["znch"], plan["ztot"], plan["n_used"],
      hn_all, route_all, plan["loff_rows"])


def _moe_experts_kernel(xe_s, first_s, next_s, par_s, nused_s, x_ref, w1_hbm, w3_hbm, w2_hbm, y_ref,
                        w1f_ref, w3f_ref, w2f_ref, w1b_ref, w3b_ref, w2b_ref, sem):
    d = w1b_ref.shape[0]

    def weight_copies(e, p):
        return [pltpu.make_async_copy(src.at[e], dst.at[p], sem.at[p, i])
                for i, (src, dst) in enumerate(((w1_hbm, w1f_ref), (w3_hbm, w3f_ref), (w2_hbm, w2f_ref)))]

    def one_tile(j, rows):
        @pl.when(j < nused_s[0])
        def _():
            @pl.when(first_s[j] == 1)
            def _():
                p = par_s[j]

                @pl.when(j == 0)
                def _():
                    for c in weight_copies(xe_s[j], p):
                        c.start()
                for c in weight_copies(xe_s[j], p):
                    c.wait()
                w1b_ref[...] = w1f_ref[p].astype(BF16)
                w3b_ref[...] = w3f_ref[p].astype(BF16)
                w2b_ref[...] = w2f_ref[p].astype(BF16)

                @pl.when(next_s[j] >= 0)
                def _():
                    for c in weight_copies(next_s[j], 1 - p):
                        c.start()

            x = _unpack_bf16_pairs(x_ref[rows, 0:d // 2])
            gate = jnp.sum(lax.bitcast_convert_type(x_ref[rows, d // 2:d // 2 + LANES], F32), axis=1, keepdims=True)
            a = jnp.dot(x, w1b_ref[...], preferred_element_type=F32)
            b = jnp.dot(x, w3b_ref[...], preferred_element_type=F32)
            hh = (a * _sigmoid(a)) * b * gate
            y_ref[rows] = _pack_bf16_pairs(jnp.dot(hh.astype(BF16), w2b_ref[...], preferred_element_type=F32))

        @pl.when(j >= nused_s[0])
        def _():
            y_ref[rows] = jnp.zeros((MOE_TMX, y_ref.shape[1]), y_ref.dtype)

    for sub in range(MOE_XSTEP_TILES):
        one_tile(MOE_XSTEP_TILES * pl.program_id(0) + sub, slice(sub * MOE_TMX, (sub + 1) * MOE_TMX))


def _moe_experts(plan, xs, w1, w3, w2, n_xtiles):
    ne, d, f = w1.shape
    per = MOE_XSTEP_TILES
    assert (n_xtiles + MOE_SPILL_TILES) % per == 0
    last = lambda s, nu: jnp.maximum(jnp.minimum(s, (nu[0] - 1) // per), 0)
    any_spec = pl.BlockSpec(memory_space=pl.ANY)
    grid_spec = pltpu.PrefetchScalarGridSpec(
        num_scalar_prefetch=5,
        grid=((n_xtiles + MOE_SPILL_TILES) // per,),
        in_specs=[pl.BlockSpec((per * MOE_TMX, d // 2 + LANES), lambda s, xe, fi, nx, pa, nu: (last(s, nu), 0)),
                  any_spec, any_spec, any_spec],
        out_specs=pl.BlockSpec((per * MOE_TMX, d // 2), lambda s, *_: (s, 0)),
        scratch_shapes=[pltpu.VMEM((2, d, f), F32), pltpu.VMEM((2, d, f), F32), pltpu.VMEM((2, f, d), F32),
                        pltpu.VMEM((d, f), BF16), pltpu.VMEM((d, f), BF16), pltpu.VMEM((f, d), BF16),
                        pltpu.SemaphoreType.DMA((2, 3))])
    return pl.pallas_call(
        _moe_experts_kernel,
        grid_spec=grid_spec,
        out_shape=jax.ShapeDtypeStruct(((n_xtiles + MOE_SPILL_TILES) * MOE_TMX, d // 2), jnp.uint32),
        compiler_params=_cparams(("arbitrary",)),
        name="moe_experts",
    )(plan["xtile_expert"], plan["xtile_first"], plan["xtile_next"], plan["xtile_parity"], plan["n_used"],
      xs, w1, w3, w2)


def _moe_combine_kernel(cdst_s, ys_hbm, route_ref, loffv_ref, h_ref, lnf_ref, y_ref,
                        buf_ref, sem, *, tile0, final_norm):
    b = pl.program_id(0)
    nb = pl.num_programs(0)
    slot = b % 2
    g = SUBLANES
    n_chunks = MOE_SLOTS // g
    tile = b + tile0

    def run_copy(s, src_row, dst_row):
        return pltpu.make_async_copy(ys_hbm.at[pl.ds(src_row, g)], buf_ref.at[s, pl.ds(dst_row, g)], sem.at[s])

    def fetch(t, s):
        for k in range(n_chunks):
            run_copy(s, pl.multiple_of(cdst_s[t * n_chunks + k], g), k * g).start()

    def wait_buffer(s):
        for _ in range(n_chunks):
            run_copy(s, 0, 0).wait()

    @pl.when(b == 0)
    def _():
        fetch(tile, slot)

    fetch(jnp.minimum(tile + 1, tile0 + nb - 1), 1 - slot)
    wait_buffer(slot)

    @pl.when(b == nb - 1)
    def _():
        wait_buffer(1 - slot)

    p1, p2 = _local_slots(route_ref[...], loffv_ref[0:1, :])
    picks = (p1 | p2).astype(BF16)
    y = h_ref[...] + jnp.dot(picks, _unpack_bf16_pairs(buf_ref[slot]), preferred_element_type=F32)
    if final_norm:
        y = _rms(y, lnf_ref[...])
    y_ref[...] = y


def _moe_combine(plan, ys, route_all, h_all, lnf, tile0, n_tiles, final_norm):
    d = h_all.shape[1]
    tm = MOE_TM
    grid_spec = pltpu.PrefetchScalarGridSpec(
        num_scalar_prefetch=1,
        grid=(n_tiles,),
        in_specs=[pl.BlockSpec(memory_space=pl.ANY),
                  pl.BlockSpec((tm, LANES), lambda i, *_: (i + tile0, 0)),
                  pl.BlockSpec((SUBLANES, LANES), lambda i, *_: (i + tile0, 0)),
                  pl.BlockSpec((tm, d), lambda i, *_: (i + tile0, 0)),
                  pl.BlockSpec((1, d), lambda i, *_: (0, 0))],
        out_specs=pl.BlockSpec((tm, d), lambda i, *_: (i, 0)),
        scratch_shapes=[pltpu.VMEM((2, MOE_SLOTS, d // 2), jnp.uint32), pltpu.SemaphoreType.DMA((2,))])
    return pl.pallas_call(
        functools.partial(_moe_combine_kernel, tile0=tile0, final_norm=final_norm),
        grid_spec=grid_spec,
        out_shape=jax.ShapeDtypeStruct((n_tiles * tm, d), F32),
        compiler_params=_cparams(("arbitrary",)),
        name="moe_combine",
    )(plan["chunk_dst"], ys, route_all, plan["loff_rows"], h_all, lnf.reshape(1, d))


def _prompt_states(kvt, n, t):
    hd = HEAD_DIM
    tail = kvt.shape[2]

    def state(k_lo, v_lo, rows, heads, win):
        w = min(win, t)
        kv = jnp.stack([kvt[:, k_lo:k_lo + rows, tail - w:], kvt[:, v_lo:v_lo + rows, tail - w:]], axis=1)
        return jnp.transpose(kv.reshape(n, 2, heads, hd, w), (0, 4, 1, 2, 3))

    out = [state(0, KA_W, KA_W, A_KV_HEADS, A_WINDOW)]
    for p, (win, _) in enumerate(B_PATTERNS):
        out.append(state(2 * KA_W + p * PB_W, 2 * KA_W + QB_W + p * PB_W, PB_W, B_HEADS_PER_PATTERN, win))
    return out


def _layer(xp, xs, caches, rel_bias, ln1, w_in, sinks, w_pa, w_pb, w_out, ln2, w_rg, b_rg, w_re, b_re,
           w1, w3, w2, lnf, final_norm):
    n, t, d = xp.shape
    ns = xs.shape[0]
    assert xs.shape[1] == 1 and OFF_GA + 2 * d == w_in.shape[1] and P_QA == 2 * d
    w_kvt = _kv_weights_transposed(w_in)
    wpa, wpb, wout = w_pa.astype(BF16), w_pb.astype(BF16), w_out.astype(BF16)
    wr = jnp.zeros((d, LANES), F32).at[:, :N_EXPERTS].set(w_re).at[:, N_EXPERTS:N_EXPERTS + MOE_GROUPS].set(w_rg)
    br = jnp.zeros((1, LANES), F32).at[0, :N_EXPERTS].set(b_re).at[0, N_EXPERTS:N_EXPERTS + MOE_GROUPS].set(b_rg)
    wr_hi = wr.astype(BF16)
    wr = jnp.stack([wr_hi, (wr - wr_hi.astype(F32)).astype(BF16)])

    tm = MOE_TM
    xp2 = xp.reshape(n * t, d)
    xs2 = xs.reshape(ns, d)
    proj_s, w_bf = _in_proj_sample(xs2, ln1, w_in)
    if ns == 2 * (n * t // MIX_TM):
        (act, qkv2, qkv3, kvt), sampled = _in_proj_mix(xp2, ln1, w_bf, w_kvt, n, t, MIX_TM, proj_s, caches,
                                                       rel_bias, sinks)
    else:
        act, qkv2, qkv3, kvt = _in_proj_prompt(xp2, ln1, w_bf, w_kvt, n, t, IN_PROJ_TM)
        sampled = _sample_mix(proj_s, caches, rel_bias, sinks)
    oa_s, ob_s, lse_s, st_s = sampled
    act4 = act.reshape(n, 1, t, ACT_W)
    bias_a = _band_bias(rel_bias[:, :A_HEADS], A_WINDOW - 1, 1)
    (oa,) = _band_attn(act4, bias_a, sinks, q_off=P_QA, k_off=P_KA, v_off=P_VA,
                       kv_heads=A_KV_HEADS, grp=A_GROUP, want_lse=False)
    obs, lses = [], []
    for p, (win, dil) in enumerate(B_PATTERNS):
        lo = A_HEADS + p * B_HEADS_PER_PATTERN
        bias_p = _band_bias(rel_bias[:, lo:lo + B_HEADS_PER_PATTERN], win // dil, dil)
        src, base = ((act4, P_B), (qkv2, 0), (qkv3, 0))[p]
        o, lse = _band_attn(src, bias_p, None, q_off=base, k_off=base + PB_W, v_off=base + 2 * PB_W,
                            kv_heads=B_HEADS_PER_PATTERN, grp=1, want_lse=True)
        obs.append(o)
        lses.append(lse)
    assert ns <= MOE_TM
    p_tiles = n * t // MOE_TM
    n_tiles = p_tiles + 1
    m_all = n_tiles * MOE_TM
    st_p = _prompt_states(kvt, n, t)

    rows = lambda a: jnp.pad(a, ((0, MOE_TM - ns), (0, 0)))
    obs_s = [rows(ob_s[:, p].astype(BF16)).reshape(1, 1, MOE_TM, PB_W) for p in range(N_PAT)]
    lses_s = [rows(jnp.repeat(lse_s[:, p], HEAD_DIM, axis=-1)).reshape(1, 1, MOE_TM, PB_W) for p in range(N_PAT)]
    gates_s = rows(proj_s[:, OFF_GA:].astype(BF16))
    routed_s = _post_attn(rows(oa_s.astype(BF16)), obs_s, lses_s, gates_s, 0, rows(xs2),
                          wpa, wpb, wout, ln2, wr, br, MOE_TM, 1, None)
    h_all, hn_all, route_all, cnt_all = _post_attn(oa.reshape(n * t, QA_W), obs, lses, act, 0, xp2,
                                                   wpa, wpb, wout, ln2, wr, br, tm, t // tm, routed_s)

    max_rows = 2 * m_all + n_tiles * N_EXPERTS * (SUBLANES - 1) + N_EXPERTS * (MOE_TMX - SUBLANES)
    n_xtiles = -(-max_rows // MOE_TMX)
    n_xtiles += -(n_xtiles + MOE_SPILL_TILES) % MOE_XSTEP_TILES
    plan = _route_plan(cnt_all, n_tiles, n_xtiles)
    xs_sorted = _moe_scatter(plan, hn_all, route_all, n_tiles, n_xtiles)
    ys_sorted = _moe_experts(plan, xs_sorted, w1, w3, w2, n_xtiles)
    yp = _moe_combine(plan, ys_sorted, route_all, h_all, lnf, 0, p_tiles, final_norm).reshape(n, t, d)
    ys = _moe_combine(plan, ys_sorted, route_all, h_all, lnf, p_tiles, 1, final_norm)[:ns].reshape(ns, 1, d)
    return yp, ys, st_p, st_s


def kernel(x_prompt, x_sample, cache_a_kv, cache_b1_kv, cache_b2_kv, cache_b3_kv, rel_bias, ln1_g, w_in, sinks,
           w_pa, w_pb, w_out, ln2_g, w_rg, b_rg, w_re, b_re, w1, w3, w2, lnf_g):
    depth = w_in.shape[0]
    assert depth >= 1
    xp, xs = x_prompt, x_sample
    new_p = [[] for _ in range(4)]
    new_s = [[] for _ in range(4)]
    for l in range(depth):
        caches = (cache_a_kv[l], cache_b1_kv[l], cache_b2_kv[l], cache_b3_kv[l])
        xp, xs, st_p, st_s = _layer(xp, xs, caches, rel_bias, ln1_g[l], w_in[l], sinks[l], w_pa[l], w_pb[l],
                                    w_out[l], ln2_g[l], w_rg[l], b_rg[l], w_re[l], b_re[l], w1[l], w3[l], w2[l],
                                    lnf_g, l == depth - 1)
        for i in range(4):
            new_p[i].append(st_p[i])
            new_s[i].append(st_s[i])
    a_p, b1_p, b2_p, b3_p = [jnp.stack(v) for v in new_p]
    a_s, b1_s, b2_s, b3_s = [jnp.stack(v) for v in new_s]
    return (xp, xs, a_p, a_s, b1_p, b1_s, b2_p, b2_s, b3_p, b3_s)
```

```python
import functools
import math

import numpy as np
import jax
import jax.numpy as jnp
from jax import lax
from jax.experimental import pallas as pl
from jax.experimental.pallas import tpu as pltpu

F32 = jnp.float32
BF16 = jnp.bfloat16

HEAD_DIM = 64
A_HEADS = 8
A_KV_HEADS = 2
A_GROUP = A_HEADS // A_KV_HEADS
A_WINDOW = 128
B_PATTERNS = ((128, 1), (512, 4), (2048, 16))
N_PAT = len(B_PATTERNS)
B_HEADS_PER_PATTERN = 4
B_HEADS = B_HEADS_PER_PATTERN * N_PAT
BLOCK = 128
ATTN_CHAINS = 16
NUM_BUCKETS = 32
MAX_DISTANCE = 2048
MOE_GROUPS = 4
EXPERTS_PER_GROUP = 8
N_EXPERTS = MOE_GROUPS * EXPERTS_PER_GROUP
EPS = 1e-6
NEG = -1e30
LANES = 128
SUBLANES = 8
ROUTE_I1, ROUTE_I2, ROUTE_W1, ROUTE_W2 = 0, 1, 2, 3
IN_PROJ_TM = 512
MIX_TM = 256
MOE_TM = 256
MOE_SLOTS = 2 * MOE_TM + N_EXPERTS * SUBLANES
MOE_TMX = 256
MOE_SPILL_TILES = -(-MOE_SLOTS // MOE_TMX)
MOE_XSTEP_TILES = 4
Q_SCALE = HEAD_DIM ** -0.5

QA_W = A_HEADS * HEAD_DIM
KA_W = A_KV_HEADS * HEAD_DIM
QB_W = B_HEADS * HEAD_DIM
PB_W = B_HEADS_PER_PATTERN * HEAD_DIM
QKV_W = 3 * PB_W
OFF_KA = QA_W
OFF_QB = OFF_KA + 2 * KA_W
OFF_KB = OFF_QB + QB_W
OFF_VB = OFF_KB + QB_W
OFF_GA = OFF_VB + QB_W
P_QA = 2048
P_KA = P_QA + QA_W
P_VA = P_KA + KA_W
P_B = P_VA + KA_W
ACT_W = P_B + QKV_W
KVT_ROWS = 2 * KA_W + 2 * QB_W

VMEM_LIMIT = 56 * 1024 * 1024


def _cparams(sem):
    return pltpu.CompilerParams(dimension_semantics=sem, vmem_limit_bytes=VMEM_LIMIT)


def _bucket_np(dist):
    dist = np.asarray(dist, np.int64)
    max_exact = NUM_BUCKETS // 2
    df = np.maximum(dist, max_exact).astype(np.float64)
    large = max_exact + (np.log(df / max_exact) / math.log(MAX_DISTANCE / max_exact)
                         * (NUM_BUCKETS - max_exact)).astype(np.int64)
    return np.where(dist < max_exact, dist, np.minimum(large, NUM_BUCKETS - 1))


def _table_rows(table_cols, dist, valid):
    onehot = (_bucket_np(dist)[:, None] == np.arange(NUM_BUCKETS)[None, :]).astype(np.float32)
    rows = jnp.einsum("ck,kh->hc", jnp.asarray(onehot), table_cols.astype(F32), precision=lax.Precision.HIGHEST)
    return jnp.where(jnp.asarray(valid)[None, :], rows, NEG)


def _band_bias(table_cols, max_dist, dilation):
    period = 3 * BLOCK
    m = np.arange(period)
    k = np.where(m < 2 * BLOCK, m, m - period)
    dist = BLOCK - k
    valid = (dist >= 0) & (dist <= max_dist) & (m != 2 * BLOCK)
    v = _table_rows(table_cols, np.clip(dist, 0, None) * dilation, valid)
    heads = v.shape[0]
    flat = jnp.tile(v, (1, BLOCK))[:, :BLOCK * (period - 1)]
    return flat.reshape(heads, BLOCK, period - 1)[:, :, :2 * BLOCK]


def _decode_bias(table_cols, width, dilation, first_valid):
    c = np.arange(width)
    valid = (c % dilation == 0) & (c >= first_valid)
    rows = _table_rows(table_cols, width - c, valid)
    self_bias = _table_rows(table_cols, np.zeros((1,), np.int64), np.ones((1,), bool))
    return rows[:, None, :], self_bias[:, None, :]


def _rms(x, g):
    return (x * lax.rsqrt(jnp.mean(x * x, axis=-1, keepdims=True) + EPS)) * g


def _sigmoid(x):
    return 1.0 / (1.0 + jnp.exp(-x))


YS_CHUNKS = QKV_W // LANES


def _in_proj_work(x_ref, g_ref, w_ref, act_ref, qkv2_ref, qkv3_ref, ys_ref, phase):
    tm = x_ref.shape[0]
    cache = {}

    def xb():
        if "xb" not in cache:
            cache["xb"] = _rms(x_ref[...], g_ref[...]).astype(BF16)
        return cache["xb"]

    def proj(lo, hi):
        return jnp.dot(xb(), w_ref[:, lo:hi], preferred_element_type=F32)

    def gates(c):
        act_ref[:, c:c + 512] = _sigmoid(proj(OFF_GA + c, OFF_GA + c + 512)).astype(BF16)

    def mixer_a_q():
        act_ref[:, P_QA:P_KA] = (proj(0, OFF_KA) * Q_SCALE).astype(BF16)

    def mixer_a_kv():
        act_ref[:, P_KA:P_B] = proj(OFF_KA, OFF_QB).astype(BF16)

    def pattern_part(p, j):
        lo = (OFF_QB, OFF_KB, OFF_VB)[j] + p * PB_W
        part = proj(lo, lo + PB_W)
        if j == 0:
            part = part * Q_SCALE
        if p == 0:
            act_ref[:, P_B + j * PB_W:P_B + (j + 1) * PB_W] = part.astype(BF16)
        else:
            for c in range(PB_W // LANES):
                ys_ref[(p - 1) * YS_CHUNKS + j * (PB_W // LANES) + c] = part[:, c * LANES:(c + 1) * LANES]

    def regroup(p, out_ref):
        dil = B_PATTERNS[p][1]
        for c in range(YS_CHUNKS):
            for r in range(dil):
                out_ref[r, :, c * LANES:(c + 1) * LANES] = (
                    ys_ref[(p - 1) * YS_CHUNKS + c, pl.ds(r, tm // dil, stride=dil), :].astype(BF16))

    work = []
    if phase in (None, 0):
        work += [functools.partial(gates, c) for c in range(0, P_QA, 512)] + [mixer_a_q, mixer_a_kv]
    if phase in (None, 1):
        for p in range(N_PAT):
            work += [functools.partial(pattern_part, p, j) for j in range(3)]
            if p > 0:
                work.append(functools.partial(regroup, p, (qkv2_ref, qkv3_ref)[p - 1]))
    return work


def _transpose_cast_kernel(w_ref, o_ref):
    o_ref[...] = w_ref[...].T.astype(o_ref.dtype)


def _kv_weights_transposed(w_in):
    d = w_in.shape[0]
    blk = PB_W
    assert 2 * KA_W == blk and OFF_KA % blk == 0 and OFF_KB % blk == 0 and OFF_VB % blk == 0
    ka, kb, vb = OFF_KA // blk, OFF_KB // blk, OFF_VB // blk

    def column_block(i):
        p = (i - 1) // 2
        return jnp.where(i == 0, ka, jnp.where(i % 2 == 1, kb + p, vb + p))

    return pl.pallas_call(
        _transpose_cast_kernel,
        grid=(KVT_ROWS // blk,),
        in_specs=[pl.BlockSpec((d, blk), lambda i: (0, column_block(i)))],
        out_specs=pl.BlockSpec((blk, d), lambda i: (i, 0)),
        out_shape=jax.ShapeDtypeStruct((KVT_ROWS, d), BF16),
        compiler_params=_cparams(("arbitrary",)),
        name="kv_weights_t",
    )(w_in)


def _kv_tail(x_ref, g_ref, wkvt_ref, kvt_ref):
    xb = _rms(x_ref[...], g_ref[...]).astype(BF16)
    kvt_ref[...] = lax.dot_general(wkvt_ref[...], xb, (((1,), (1,)), ((), ())), preferred_element_type=F32)


def _in_proj_kernel(x_ref, g_ref, w_ref, wkvt_ref, act_ref, qkv2_ref, qkv3_ref, kvt_ref, ys_ref, *,
                    tiles_per_seq, tail_tiles):
    for piece in _in_proj_work(x_ref, g_ref, w_ref, act_ref, qkv2_ref, qkv3_ref, ys_ref, None):
        piece()

    @pl.when(pl.program_id(0) % tiles_per_seq >= tiles_per_seq - tail_tiles)
    def _():
        _kv_tail(x_ref, g_ref, wkvt_ref, kvt_ref)


def _in_proj_prompt(x2d, ln_g, w_bf, w_kvt, n, t, tm):
    m, d = x2d.shape
    tps = t // tm
    tail = min(max(w for w, _ in B_PATTERNS), t)
    assert t % tm == 0 and tail % tm == 0 and all(tm % (16 * dl) == 0 for _, dl in B_PATTERNS)
    tail_tiles = tail // tm
    d2, d3 = B_PATTERNS[1][1], B_PATTERNS[2][1]
    return pl.pallas_call(
        functools.partial(_in_proj_kernel, tiles_per_seq=tps, tail_tiles=tail_tiles),
        grid=(m // tm,),
        in_specs=[pl.BlockSpec((tm, d), lambda i: (i, 0)),
                  pl.BlockSpec((1, d), lambda i: (0, 0)),
                  pl.BlockSpec(w_bf.shape, lambda i: (0, 0), pipeline_mode=pl.Buffered(1)),
                  pl.BlockSpec(w_kvt.shape, lambda i: (0, 0), pipeline_mode=pl.Buffered(1))],
        out_specs=[pl.BlockSpec((tm, ACT_W), lambda i: (i, 0)),
                   pl.BlockSpec((None, d2, tm // d2, QKV_W), lambda i: (i // tps, 0, i % tps, 0)),
                   pl.BlockSpec((None, d3, tm // d3, QKV_W), lambda i: (i // tps, 0, i % tps, 0)),
                   pl.BlockSpec((None, KVT_ROWS, tm),
                                lambda i: (i // tps, 0, jnp.maximum(i % tps - (tps - tail_tiles), 0)))],
        out_shape=[jax.ShapeDtypeStruct((m, ACT_W), BF16),
                   jax.ShapeDtypeStruct((n, d2, t // d2, QKV_W), BF16),
                   jax.ShapeDtypeStruct((n, d3, t // d3, QKV_W), BF16),
                   jax.ShapeDtypeStruct((n, KVT_ROWS, tail), F32)],
        scratch_shapes=[pltpu.VMEM(((N_PAT - 1) * YS_CHUNKS, tm, LANES), F32)],
        compiler_params=_cparams(("arbitrary",)),
        name="in_proj",
    )(x2d, ln_g.reshape(1, d), w_bf, w_kvt)


def _in_proj_sample_kernel(x_ref, g_ref, w_ref, y_ref, wb_ref):
    blk = w_ref.shape[1]
    wb = w_ref[...].astype(BF16)
    wb_ref[...] = wb
    y = jnp.dot(_rms(x_ref[...], g_ref[...]).astype(BF16), wb, preferred_element_type=F32)
    col = pl.program_id(0) * blk + lax.broadcasted_iota(jnp.int32, y.shape, 1)
    is_q = (col < OFF_KA) | ((col >= OFF_QB) & (col < OFF_KB))
    y = jnp.where(is_q, y * Q_SCALE, y)
    y_ref[...] = jnp.where(col >= OFF_GA, _sigmoid(y), y)


def _in_proj_sample(x2d, ln_g, w_in, blk=512):
    m, d = x2d.shape
    in_w = w_in.shape[1]
    assert in_w % blk == 0
    return pl.pallas_call(
        _in_proj_sample_kernel,
        grid=(in_w // blk,),
        in_specs=[pl.BlockSpec((m, d), lambda i: (0, 0)),
                  pl.BlockSpec((1, d), lambda i: (0, 0)),
                  pl.BlockSpec((d, blk), lambda i: (0, i))],
        out_specs=[pl.BlockSpec((m, blk), lambda i: (0, i)), pl.BlockSpec((d, blk), lambda i: (0, i))],
        out_shape=[jax.ShapeDtypeStruct((m, in_w), F32), jax.ShapeDtypeStruct((d, in_w), BF16)],
        compiler_params=_cparams(("arbitrary",)),
        name="in_proj_sample",
    )(x2d, ln_g.reshape(1, d), w_in)


def _band_attn_kernel(*refs, kv_heads, grp, has_sink, want_lse):
    sum_on_mxu = grp > 1
    if has_sink:
        sink_ref, refs = refs[0], refs[1:]
    q_ref, kp_ref, kc_ref, vp_ref, vc_ref, bias_ref, o_ref = refs[:7]
    lse_ref = refs[7] if want_lse else None
    step = pl.program_id(2)
    hd = HEAD_DIM
    nt = (((1,), (1,)), ((), ()))
    chains = [(sub, kv * grp + g, slice(kv * hd, (kv + 1) * hd))
              for sub in range(q_ref.shape[0] // BLOCK) for kv in range(kv_heads) for g in range(grp)]
    scores = []
    for sub, h, ks in chains:
        rows = slice(sub * BLOCK, (sub + 1) * BLOCK)
        q = q_ref[rows, h * hd:(h + 1) * hd]
        kp = kp_ref[:, ks] if sub == 0 else kc_ref[(sub - 1) * BLOCK:sub * BLOCK, ks]
        sp = lax.dot_general(q, kp, nt, preferred_element_type=F32) + bias_ref[h, :, 0:BLOCK]
        sc = lax.dot_general(q, kc_ref[rows, ks], nt, preferred_element_type=F32) + bias_ref[h, :, BLOCK:2 * BLOCK]
        if sub == 0:
            sp = jnp.where(step > 0, sp, NEG)
        scores.append((sp, sc))
    maxes = []
    for (sub, h, ks), (sp, sc) in zip(chains, scores):
        m = jnp.maximum(jnp.max(sp, axis=-1, keepdims=True), jnp.max(sc, axis=-1, keepdims=True))
        maxes.append(jnp.maximum(m, sink_ref[h]) if has_sink else m)
    probs = []
    ones = jnp.ones((BLOCK, LANES), BF16)
    for (sub, h, ks), (sp, sc), m in zip(chains, scores, maxes):
        pp, pc = jnp.exp(sp - m), jnp.exp(sc - m)
        if sum_on_mxu:
            pp, pc = pp.astype(BF16), pc.astype(BF16)
            den = (jnp.dot(pp, ones, preferred_element_type=F32)
                   + jnp.dot(pc, ones, preferred_element_type=F32))[:, 0:1]
        else:
            den = jnp.sum(pp, axis=-1, keepdims=True) + jnp.sum(pc, axis=-1, keepdims=True)
            pp, pc = pp.astype(BF16), pc.astype(BF16)
        if has_sink:
            den = den + jnp.exp(sink_ref[h] - m)
        probs.append((pp, pc, den))
    for (sub, h, ks), (pp, pc, den), m in zip(chains, probs, maxes):
        rows = slice(sub * BLOCK, (sub + 1) * BLOCK)
        vp = vp_ref[:, ks] if sub == 0 else vc_ref[(sub - 1) * BLOCK:sub * BLOCK, ks]
        o = (jnp.dot(pp, vp, preferred_element_type=F32)
             + jnp.dot(pc, vc_ref[rows, ks], preferred_element_type=F32))
        o_ref[rows, h * hd:(h + 1) * hd] = (o / den).astype(o_ref.dtype)
        if want_lse:
            lse_ref[rows, h * hd:(h + 1) * hd] = jnp.broadcast_to(m + jnp.log(den), (BLOCK, hd))


def _band_attn(src, bias, sink, *, q_off, k_off, v_off, kv_heads, grp, want_lse):
    n, dil, l, cols = src.shape
    sub = max(1, ATTN_CHAINS // (kv_heads * grp))
    while l % (sub * BLOCK):
        sub //= 2
    rows = sub * BLOCK
    assert sub >= 1 and l % rows == 0
    nb = l // rows
    qw = kv_heads * grp * HEAD_DIM
    kw = kv_heads * HEAD_DIM
    assert q_off % qw == 0 and k_off % kw == 0 and v_off % kw == 0
    qb, kb, vb = q_off // qw, k_off // kw, v_off // kw
    prev = lambda b: jnp.maximum(sub * b - 1, 0)
    in_specs = [
        pl.BlockSpec((None, None, rows, qw), lambda i, r, b: (i, r, b, qb)),
        pl.BlockSpec((None, None, BLOCK, kw), lambda i, r, b: (i, r, prev(b), kb)),
        pl.BlockSpec((None, None, rows, kw), lambda i, r, b: (i, r, b, kb)),
        pl.BlockSpec((None, None, BLOCK, kw), lambda i, r, b: (i, r, prev(b), vb)),
        pl.BlockSpec((None, None, rows, kw), lambda i, r, b: (i, r, b, vb)),
        pl.BlockSpec(bias.shape, lambda i, r, b: (0, 0, 0)),
    ]
    args = [src, src, src, src, src, bias]
    has_sink = sink is not None
    if has_sink:
        in_specs = [pl.BlockSpec(memory_space=pltpu.SMEM)] + in_specs
        args = [sink.astype(F32)] + args
    out_specs = [pl.BlockSpec((None, None, rows, qw), lambda i, r, b: (i, r, b, 0))]
    out_shape = [jax.ShapeDtypeStruct((n, dil, l, qw), BF16)]
    if want_lse:
        out_specs.append(pl.BlockSpec((None, None, rows, qw), lambda i, r, b: (i, r, b, 0)))
        out_shape.append(jax.ShapeDtypeStruct((n, dil, l, qw), F32))
    return pl.pallas_call(
        functools.partial(_band_attn_kernel, kv_heads=kv_heads, grp=grp, has_sink=has_sink, want_lse=want_lse),
        grid=(n, dil, nb),
        in_specs=in_specs,
        out_specs=out_specs,
        out_shape=out_shape,
        compiler_params=_cparams(("arbitrary", "arbitrary", "arbitrary")),
        name=f"band_attn_d{dil}",
    )(*args)


COL_QA = 0
COL_QB = COL_QA + A_HEADS
COL_KA = COL_QB + B_HEADS
COL_VA = COL_KA + A_KV_HEADS
COL_KB = COL_VA + A_KV_HEADS
COL_VB = COL_KB + B_HEADS
N_COLS = COL_VB + B_HEADS


def _attend_cached(kt, vt, q, k_new, v_new, bias, self_bias, sink, write):
    s = jnp.sum(kt * q, axis=1, keepdims=True) + bias
    s_new = jnp.sum(k_new * q, axis=1, keepdims=True) + self_bias
    yield
    m = jnp.maximum(jnp.max(s, axis=2, keepdims=True), s_new)
    if sink is not None:
        m = jnp.maximum(m, sink)
    yield
    p = jnp.exp(s - m)
    p_new = jnp.exp(s_new - m)
    den = jnp.sum(p, axis=2, keepdims=True) + p_new
    if sink is not None:
        den = den + jnp.exp(sink - m)
    yield
    o = (jnp.sum(vt * p, axis=2, keepdims=True) + v_new * p_new) / den
    write(o, m + jnp.log(den))
    yield


def _sample_work(cols_ref, ca_ref, c1_ref, c2_ref, c3_ref, ba_ref, sa_ref, sink_ref, b1_ref, b2_ref, b3_ref,
                 sb_ref, o_ref, lse_ref, ra_ref, r1_ref, r2_ref, r3_ref):
    cols = cols_ref[...]

    def stack(js):
        return jnp.stack([cols[:, j:j + 1] for j in js])

    o_ref[...] = jnp.zeros_like(o_ref)
    lse_ref[...] = jnp.zeros_like(lse_ref)

    def write_a(o, _):
        for h in range(A_HEADS):
            o_ref[:, COL_QA + h:COL_QA + h + 1] = o[h]

    def write_b(p):
        def write(o, lse):
            for h in range(B_HEADS_PER_PATTERN):
                j = p * B_HEADS_PER_PATTERN + h
                o_ref[:, COL_QB + j:COL_QB + j + 1] = o[h]
                lse_ref[0:1, j:j + 1] = lse[h]
        return write

    kv_of = [h // A_GROUP for h in range(A_HEADS)]
    stages = [_attend_cached(jnp.stack([ca_ref[0, kv] for kv in kv_of]), jnp.stack([ca_ref[1, kv] for kv in kv_of]),
                             stack(range(COL_QA, COL_QA + A_HEADS)), stack([COL_KA + kv for kv in kv_of]),
                             stack([COL_VA + kv for kv in kv_of]), ba_ref[...], sa_ref[...], sink_ref[...], write_a)]
    for p, (c_ref, b_ref) in enumerate(((c1_ref, b1_ref), (c2_ref, b2_ref), (c3_ref, b3_ref))):
        js = range(p * B_HEADS_PER_PATTERN, (p + 1) * B_HEADS_PER_PATTERN)
        stages.append(_attend_cached(c_ref[0], c_ref[1], stack([COL_QB + j for j in js]),
                                     stack([COL_KB + j for j in js]), stack([COL_VB + j for j in js]),
                                     b_ref[...], sb_ref[p], None, write_b(p)))

    def roll_plane(c_ref, r_ref, i, h, new_col):
        w = c_ref.shape[-1]
        x = c_ref[i, h]
        lane = lax.broadcasted_iota(jnp.int32, x.shape, 1)
        r_ref[i, h] = jnp.where(lane == w - 1, cols[:, new_col:new_col + 1], pltpu.roll(x, w - 1, 1))

    planes = []
    for ci, (c_ref, r_ref) in enumerate(((ca_ref, ra_ref), (c1_ref, r1_ref), (c2_ref, r2_ref), (c3_ref, r3_ref))):
        for i, (first_a, first_b) in enumerate(((COL_KA, COL_KB), (COL_VA, COL_VB))):
            for h in range(c_ref.shape[1]):
                new_col = first_a + h if ci == 0 else first_b + (ci - 1) * B_HEADS_PER_PATTERN + h
                planes.append((c_ref.shape[-1], functools.partial(roll_plane, c_ref, r_ref, i, h, new_col)))

    work = []
    total = sum(w for w, _ in planes)
    n_slots = 4 * len(stages)
    done, k = 0, 0
    for slot in range(n_slots):
        work.append(functools.partial(next, stages[slot // 4]))
        while k < len(planes) and done < total * (slot + 1) // n_slots:
            done += planes[k][0]
            work.append(planes[k][1])
            k += 1
    assert k == len(planes)
    return work


def _sample_mix_kernel(*refs):
    for piece in _sample_work(*refs):
        piece()


def _run_interleaved(primary, secondary):
    k = 0
    for i, piece in enumerate(primary):
        piece()
        while k < len(secondary) and k < len(secondary) * (i + 1) // len(primary):
            secondary[k]()
            k += 1
    for piece in secondary[k:]:
        piece()


def _in_proj_mix_kernel(x_ref, g_ref, w_ref, wkvt_ref, *rest, tiles_per_seq, tail_tiles, n_sample_in):
    sample_in = rest[:n_sample_in]
    act_ref, qkv2_ref, qkv3_ref, kvt_ref = rest[n_sample_in:n_sample_in + 4]
    sample_out = rest[n_sample_in + 4:-1]
    ys_ref = rest[-1]
    step = pl.program_id(0)
    tile = step // 2
    for phase in (0, 1):
        @pl.when(step % 2 == phase)
        def _():
            _run_interleaved(_sample_work(*sample_in, *sample_out),
                             _in_proj_work(x_ref, g_ref, w_ref, act_ref, qkv2_ref, qkv3_ref, ys_ref, phase))
            if phase == 1:
                @pl.when(tile % tiles_per_seq >= tiles_per_seq - tail_tiles)
                def _():
                    _kv_tail(x_ref, g_ref, wkvt_ref, kvt_ref)


def _sample_operands(proj_s, caches, rel_bias, sinks):
    n = proj_s.shape[0]
    hd = HEAD_DIM
    vecs = jnp.concatenate([proj_s[:, :OFF_KA], proj_s[:, OFF_QB:OFF_KB], proj_s[:, OFF_KA:OFF_QB],
                            proj_s[:, OFF_KB:OFF_GA]], axis=1).reshape(n, N_COLS, hd)
    cols = jnp.pad(jnp.transpose(vecs, (0, 2, 1)), ((0, 0), (0, 0), (0, LANES - N_COLS)))
    cts = [jnp.transpose(c, (0, 2, 3, 4, 1)) for c in caches]

    ba, sa = _decode_bias(rel_bias[:, :A_HEADS], A_WINDOW, 1, 1)
    sink = sinks.astype(F32).reshape(A_HEADS, 1, 1)
    bbs, sbs = [], []
    for p, (win, dil) in enumerate(B_PATTERNS):
        lo = A_HEADS + p * B_HEADS_PER_PATTERN
        assert caches[1 + p].shape[1] == win == BLOCK * dil
        b, s = _decode_bias(rel_bias[:, lo:lo + B_HEADS_PER_PATTERN], win, dil, 0)
        bbs.append(b)
        sbs.append(s)
    sb = jnp.stack(sbs)

    seq_in = [cols] + cts
    consts = [ba, sa, sink] + bbs + [sb]
    out_shape = [jax.ShapeDtypeStruct((n, hd, LANES), F32), jax.ShapeDtypeStruct((n, SUBLANES, LANES), F32)]
    out_shape += [jax.ShapeDtypeStruct(c.shape, c.dtype) for c in cts]
    return seq_in, consts, out_shape


def _per_seq_spec(a):
    nd = len(a.shape)
    return pl.BlockSpec((None,) + tuple(a.shape[1:]), lambda i: (i,) + (0,) * (nd - 1))


def _const_spec(a):
    nd = a.ndim
    return pl.BlockSpec(a.shape, lambda i: (0,) * nd)


def _sample_results(outs):
    n = outs[0].shape[0]
    o_rows = jnp.transpose(outs[0][:, :, :COL_KA], (0, 2, 1))
    oa = o_rows[:, COL_QA:COL_QB].reshape(n, QA_W)
    ob = o_rows[:, COL_QB:COL_KA].reshape(n, N_PAT, PB_W)
    lse = outs[1][:, 0, :B_HEADS].reshape(n, N_PAT, B_HEADS_PER_PATTERN)
    rolled = [jnp.transpose(r, (0, 4, 1, 2, 3)) for r in outs[2:]]
    return oa, ob, lse, rolled


def _sample_mix(proj_s, caches, rel_bias, sinks):
    seq_in, consts, out_shape = _sample_operands(proj_s, caches, rel_bias, sinks)
    outs = pl.pallas_call(
        _sample_mix_kernel,
        grid=(proj_s.shape[0],),
        in_specs=[_per_seq_spec(a) for a in seq_in] + [_const_spec(a) for a in consts],
        out_specs=[_per_seq_spec(s) for s in out_shape],
        out_shape=out_shape,
        compiler_params=_cparams(("arbitrary",)),
        name="sample_mix",
    )(*seq_in, *consts)
    return _sample_results(outs)


def _in_proj_mix(x2d, ln_g, w_bf, w_kvt, n, t, tm, proj_s, caches, rel_bias, sinks):
    m, d = x2d.shape
    tps = t // tm
    tail = min(max(w for w, _ in B_PATTERNS), t)
    assert t % tm == 0 and tail % tm == 0 and all(tm % (16 * dl) == 0 for _, dl in B_PATTERNS)
    assert proj_s.shape[0] == 2 * (m // tm)
    tail_tiles = tail // tm
    d2, d3 = B_PATTERNS[1][1], B_PATTERNS[2][1]
    seq_in, consts, sample_shape = _sample_operands(proj_s, caches, rel_bias, sinks)
    tile = lambda i: i // 2
    outs = pl.pallas_call(
        functools.partial(_in_proj_mix_kernel, tiles_per_seq=tps, tail_tiles=tail_tiles,
                          n_sample_in=len(seq_in) + len(consts)),
        grid=(2 * (m // tm),),
        in_specs=[pl.BlockSpec((tm, d), lambda i: (tile(i), 0)),
                  pl.BlockSpec((1, d), lambda i: (0, 0)),
                  pl.BlockSpec(w_bf.shape, lambda i: (0, 0), pipeline_mode=pl.Buffered(1)),
                  pl.BlockSpec(w_kvt.shape, lambda i: (0, 0), pipeline_mode=pl.Buffered(1))]
                 + [_per_seq_spec(a) for a in seq_in] + [_const_spec(a) for a in consts],
        out_specs=[pl.BlockSpec((tm, ACT_W), lambda i: (tile(i), 0)),
                   pl.BlockSpec((None, d2, tm // d2, QKV_W), lambda i: (tile(i) // tps, 0, tile(i) % tps, 0)),
                   pl.BlockSpec((None, d3, tm // d3, QKV_W), lambda i: (tile(i) // tps, 0, tile(i) % tps, 0)),
                   pl.BlockSpec((None, KVT_ROWS, tm),
                                lambda i: (tile(i) // tps, 0, jnp.maximum(tile(i) % tps - (tps - tail_tiles), 0)))]
                  + [_per_seq_spec(s) for s in sample_shape],
        out_shape=[jax.ShapeDtypeStruct((m, ACT_W), BF16),
                   jax.ShapeDtypeStruct((n, d2, t // d2, QKV_W), BF16),
                   jax.ShapeDtypeStruct((n, d3, t // d3, QKV_W), BF16),
                   jax.ShapeDtypeStruct((n, KVT_ROWS, tail), F32)] + sample_shape,
        scratch_shapes=[pltpu.VMEM(((N_PAT - 1) * YS_CHUNKS, tm, LANES), F32)],
        compiler_params=_cparams(("arbitrary",)),
        name="in_proj_mix",
    )(x2d, ln_g.reshape(1, d), w_bf, w_kvt, *seq_in, *consts)
    return outs[:4], _sample_results(outs[4:])


def _post_attn_kernel(oa_ref, o1_ref, o2_ref, o3_ref, l1_ref, l2_ref, l3_ref, ga_ref, gb_ref, x_ref,
                      wpa_ref, wpb_ref, wout_ref, ln2_ref, wr_ref, br_ref, *rest, n_extra):
    outs = rest[n_extra:n_extra + 4]
    h_ref, hn_ref, route_ref, cnt_ref = outs
    scr_ref = rest[n_extra + 4]
    if n_extra:
        own_tile = pl.program_id(0) < pl.num_programs(0) - 1

        @pl.when(jnp.logical_not(own_tile))
        def _():
            for dst, src in zip(outs, rest[:n_extra]):
                dst[...] = src[...]

        @pl.when(own_tile)
        def _():
            _post_attn_tile(oa_ref, o1_ref, o2_ref, o3_ref, l1_ref, l2_ref, l3_ref, ga_ref, gb_ref, x_ref,
                            wpa_ref, wpb_ref, wout_ref, ln2_ref, wr_ref, br_ref, *outs, scr_ref)
    else:
        _post_attn_tile(oa_ref, o1_ref, o2_ref, o3_ref, l1_ref, l2_ref, l3_ref, ga_ref, gb_ref, x_ref,
                        wpa_ref, wpb_ref, wout_ref, ln2_ref, wr_ref, br_ref, *outs, scr_ref)


def _post_attn_tile(oa_ref, o1_ref, o2_ref, o3_ref, l1_ref, l2_ref, l3_ref, ga_ref, gb_ref, x_ref,
                    wpa_ref, wpb_ref, wout_ref, ln2_ref, wr_ref, br_ref, h_ref, hn_ref, route_ref, cnt_ref, scr_ref):
    tm = x_ref.shape[0]
    chunks = PB_W // LANES

    def token_major(ref, slot):
        dil = ref.shape[0]
        if dil == 1:
            return ref[0].astype(F32)
        for c in range(chunks):
            for r in range(dil):
                scr_ref[slot * chunks + c, pl.ds(r, tm // dil, stride=dil), :] = (
                    ref[r, :, c * LANES:(c + 1) * LANES].astype(F32))
        return jnp.concatenate([scr_ref[slot * chunks + c] for c in range(chunks)], axis=1)

    o1, o2, o3 = (token_major(r, s) for s, r in enumerate((o1_ref, o2_ref, o3_ref)))
    l1, l2, l3 = (token_major(r, 3 + s) for s, r in enumerate((l1_ref, l2_ref, l3_ref)))
    m = jnp.maximum(jnp.maximum(l1, l2), l3)
    a1, a2, a3 = jnp.exp(l1 - m), jnp.exp(l2 - m), jnp.exp(l3 - m)
    ob = (a1 * o1 + a2 * o2 + a3 * o3) / (a1 + a2 + a3)
    ya = jnp.dot(oa_ref[...], wpa_ref[...], preferred_element_type=F32)
    yb = jnp.dot(ob.astype(BF16), wpb_ref[...], preferred_element_type=F32)
    merged = ga_ref[...].astype(F32) * ya + gb_ref[...].astype(F32) * yb
    h = x_ref[...] + jnp.dot(merged.astype(BF16), wout_ref[...], preferred_element_type=F32)
    h_ref[...] = h
    hn = _rms(h, ln2_ref[...])
    hn_hi = hn.astype(BF16)
    hn_ref[...] = hn_hi

    hn_lo = (hn - hn_hi.astype(F32)).astype(BF16)
    logits = (jnp.dot(hn_hi, wr_ref[0], preferred_element_type=F32)
              + jnp.dot(hn_lo, wr_ref[0], preferred_element_type=F32)
              + jnp.dot(hn_hi, wr_ref[1], preferred_element_type=F32)) + br_ref[...]
    lane = lax.broadcasted_iota(jnp.int32, logits.shape, 1)
    is_grp = (lane >= N_EXPERTS) & (lane < N_EXPERTS + MOE_GROUPS)
    lg = jnp.where(is_grp, logits, NEG)
    gmax = jnp.max(lg, axis=-1, keepdims=True)
    g_lane = jnp.min(jnp.where(lg == gmax, lane, LANES), axis=-1, keepdims=True)
    p_g = 1.0 / jnp.sum(jnp.where(is_grp, jnp.exp(lg - gmax), 0.0), axis=-1, keepdims=True)
    e_lo = (g_lane - N_EXPERTS) * EXPERTS_PER_GROUP
    in_grp = (lane >= e_lo) & (lane < e_lo + EXPERTS_PER_GROUP)
    le = jnp.where(in_grp, logits, NEG)
    v1 = jnp.max(le, axis=-1, keepdims=True)
    i1 = jnp.min(jnp.where(le == v1, lane, LANES), axis=-1, keepdims=True)
    le2 = jnp.where(lane == i1, NEG, le)
    v2 = jnp.max(le2, axis=-1, keepdims=True)
    i2 = jnp.min(jnp.where(le2 == v2, lane, LANES), axis=-1, keepdims=True)
    e2 = jnp.exp(v2 - v1)
    w1 = p_g / (1.0 + e2)
    w2 = p_g * e2 / (1.0 + e2)
    route = jnp.where(lane == ROUTE_I1, i1.astype(F32), jnp.where(lane == ROUTE_I2, i2.astype(F32), 0.0))
    route_ref[...] = route + jnp.where(lane == ROUTE_W1, w1, 0.0) + jnp.where(lane == ROUTE_W2, w2, 0.0)
    picks = (lane == i1).astype(F32) + (lane == i2).astype(F32)
    cnt_ref[...] = jnp.broadcast_to(jnp.sum(picks, axis=0, keepdims=True), cnt_ref.shape)


def _post_attn(oa, obs, lses, gates_src, ga_blk, x2d, wpa, wpb, wout, ln2, wr, br, tm, tiles_per_seq, extra):
    m, d = x2d.shape
    tps = tiles_per_seq
    own = m // tm
    n_tiles = own + (extra is not None)
    mine = lambda i: jnp.minimum(i, own - 1)

    def tile(w, col=0):
        return pl.BlockSpec((tm, w), lambda i: (mine(i), col))

    def out_tile(rows, w):
        return pl.BlockSpec((rows, w), lambda i: (i, 0))

    def full(a):
        nd = a.ndim
        return pl.BlockSpec(a.shape, lambda i: (0,) * nd)

    def residue(a):
        dil = a.shape[1]
        return pl.BlockSpec((None, dil, tm // dil, PB_W), lambda i: (mine(i) // tps, 0, mine(i) % tps, 0))

    weights = [wpa, wpb, wout, ln2.reshape(1, d), wr, br]
    scratch = [pltpu.VMEM((6 * PB_W // LANES, tm, LANES), F32)]
    in_specs = ([tile(QA_W)] + [residue(a) for a in obs] + [residue(a) for a in lses]
                + [tile(d, ga_blk), tile(d, ga_blk + 1), tile(d)] + [full(w) for w in weights])
    args = [oa, *obs, *lses, gates_src, gates_src, x2d, *weights]
    if extra is not None:
        in_specs = in_specs + [full(a) for a in extra]
        args = args + list(extra)
    return pl.pallas_call(
        functools.partial(_post_attn_kernel, n_extra=0 if extra is None else len(extra)),
        grid=(n_tiles,),
        in_specs=in_specs,
        out_specs=[out_tile(tm, d), out_tile(tm, d), out_tile(tm, LANES), out_tile(SUBLANES, LANES)],
        out_shape=[jax.ShapeDtypeStruct((n_tiles * tm, d), F32),
                   jax.ShapeDtypeStruct((n_tiles * tm, d), BF16),
                   jax.ShapeDtypeStruct((n_tiles * tm, LANES), F32),
                   jax.ShapeDtypeStruct((n_tiles * SUBLANES, LANES), F32)],
        scratch_shapes=scratch,
        compiler_params=_cparams(("arbitrary",)),
        name="post_attn",
    )(*args)


def _route_plan(cnt_rows, n_tiles, n_xtiles):
    g = SUBLANES
    cnt = cnt_rows.reshape(n_tiles, g, LANES)[:, 0, :N_EXPERTS].astype(jnp.int32)
    cnt8 = (cnt + g - 1) // g * g
    loff = jnp.cumsum(cnt8, axis=1) - cnt8
    boff = jnp.cumsum(cnt8, axis=0) - cnt8
    tot = jnp.sum(cnt8, axis=0)
    region = (tot + MOE_TMX - 1) // MOE_TMX * MOE_TMX
    gend = jnp.cumsum(region)
    gbase = gend - region
    cum_tiles = gend // MOE_TMX
    j = jnp.arange(n_xtiles + MOE_SPILL_TILES, dtype=jnp.int32)
    loff_rows = jnp.zeros((n_tiles, g, LANES), F32).at[:, :, :N_EXPERTS].set(loff[:, None, :].astype(F32))
    k8 = g * jnp.arange(MOE_SLOTS // g, dtype=jnp.int32)
    run_end = loff + cnt8
    e_of_k = jnp.sum(k8[None, :, None] >= run_end[:, None, :], axis=2)
    shift = gbase[None, :] + boff - loff
    picked = jnp.sum(jnp.where(e_of_k[:, :, None] == jnp.arange(N_EXPERTS)[None, None, :], shift[:, None, :], 0), axis=2)
    ids = jnp.arange(N_EXPERTS, dtype=jnp.int32)
    xe = jnp.minimum(jnp.sum(j[:, None] >= cum_tiles[None, :], axis=1), N_EXPERTS - 1).astype(jnp.int32)
    has_tiles = region > 0
    first_tile = cum_tiles - region // MOE_TMX
    of_tile = lambda per_expert: jnp.sum(jnp.where(xe[:, None] == ids[None, :], per_expert[None, :], 0), axis=1)
    later = jnp.where(has_tiles[None, :] & (ids[None, :] > ids[:, None]), ids[None, :], N_EXPERTS)
    next_of = jnp.min(later, axis=1)
    next_of = jnp.where(next_of == N_EXPERTS, -1, next_of)
    xtile_first = (jnp.any(has_tiles[None, :] & (j[:, None] == first_tile[None, :]), axis=1)
                   & (j < cum_tiles[-1])).astype(jnp.int32)
    return dict(
        xtile_first=xtile_first, xtile_next=of_tile(next_of).astype(jnp.int32),
        xtile_parity=of_tile((jnp.cumsum(has_tiles.astype(jnp.int32)) - 1) % 2).astype(jnp.int32),
        chunk_dst=jnp.where(k8[None, :] < run_end[:, -1:], picked, n_xtiles * MOE_TMX).reshape(-1).astype(jnp.int32)
        + jnp.tile(k8, n_tiles),
        zst=gbase + tot, znch=(region - tot) // g,
        ztot=jnp.sum((region - tot) // g).reshape(1),
        xtile_expert=xe,
        n_used=cum_tiles[-1:].astype(jnp.int32),
        loff_rows=loff_rows.reshape(n_tiles * g, LANES))


def _local_slots(route, loff_row):
    tm = route.shape[0]
    lane = lax.broadcasted_iota(jnp.int32, (tm, LANES), 1)
    e1 = lane == route[:, ROUTE_I1:ROUTE_I1 + 1].astype(jnp.int32)
    e2 = lane == route[:, ROUTE_I2:ROUTE_I2 + 1].astype(jnp.int32)
    earlier = (lax.broadcasted_iota(jnp.int32, (tm, tm), 1) < lax.broadcasted_iota(jnp.int32, (tm, tm), 0))
    earlier = earlier.astype(BF16)
    c1 = jnp.dot(earlier, e1.astype(BF16), preferred_element_type=F32)
    c2 = jnp.dot(earlier, e2.astype(BF16), preferred_element_type=F32)
    cnt1 = jnp.sum(e1.astype(F32), axis=0, keepdims=True)
    pos1 = jnp.sum(jnp.where(e1, c1 + loff_row, 0.0), axis=1, keepdims=True)
    pos2 = jnp.sum(jnp.where(e2, c2 + cnt1 + loff_row, 0.0), axis=1, keepdims=True)
    slot = lax.broadcasted_iota(jnp.int32, (tm, MOE_SLOTS), 1)
    return slot == pos1.astype(jnp.int32), slot == pos2.astype(jnp.int32)


def _pack_bf16_pairs(x):
    c = x.shape[1] // 2
    bits = lambda v: lax.bitcast_convert_type(v.astype(BF16).astype(F32), jnp.uint32)
    return bits(x[:, :c]) | (bits(x[:, c:]) >> 16)


def _unpack_bf16_pairs(w):
    hi = lax.bitcast_convert_type(w & jnp.uint32(0xFFFF0000), F32).astype(BF16)
    lo = lax.bitcast_convert_type(w << 16, F32).astype(BF16)
    return jnp.concatenate([hi, lo], axis=1)


def _split3(w):
    hi = w.astype(BF16).astype(F32)
    mid = (w - hi).astype(BF16).astype(F32)
    return hi, mid, (w - hi - mid).astype(BF16).astype(F32)


def _moe_scatter_kernel(cdst_s, zst_s, znch_s, ztot_s, nused_s,
                        hn_ref, route_ref, loffv_ref, xs_hbm, buf_ref, zero_ref, sem, zsem, tsem):
    b = pl.program_id(0)
    nb = pl.num_programs(0)
    slot = b % 2
    g = SUBLANES
    d = hn_ref.shape[1]
    n_xtiles = xs_hbm.shape[0] // MOE_TMX - MOE_SPILL_TILES

    def run_copy(s, src_row, dst_row):
        return pltpu.make_async_copy(buf_ref.at[s, pl.ds(src_row, g)], xs_hbm.at[pl.ds(dst_row, g)], sem.at[s])

    def zero_copy(dst_row):
        return pltpu.make_async_copy(zero_ref.at[pl.ds(0, g)], xs_hbm.at[pl.ds(dst_row, g)], zsem)

    def zero_tile_copy(j):
        return pltpu.make_async_copy(zero_ref, xs_hbm.at[pl.ds(pl.multiple_of(j * MOE_TMX, MOE_TMX), MOE_TMX)], tsem)

    n_chunks = MOE_SLOTS // g
    spill = n_xtiles * MOE_TMX

    def wait_buffer(s):
        for _ in range(n_chunks):
            run_copy(s, 0, 0).wait()

    def ship_buffer(s, dst_row_of):
        for k in range(n_chunks):
            run_copy(s, k * g, pl.multiple_of(dst_row_of(k), g)).start()

    @pl.when(b == 0)
    def _():
        zero_ref[...] = jnp.zeros_like(zero_ref)
        buf_ref[1] = jnp.zeros(buf_ref.shape[1:], buf_ref.dtype)

        def per_expert(e, c):
            def per_chunk(k, c2):
                zero_copy(pl.multiple_of(zst_s[e] + k * g, g)).start()
                return c2
            return lax.fori_loop(0, znch_s[e], per_chunk, c)
        lax.fori_loop(0, N_EXPERTS, per_expert, 0)
        lax.fori_loop(nused_s[0], n_xtiles, lambda j, c: (zero_tile_copy(j).start(), c)[1], 0)

    @pl.when(b >= 1)
    def _():
        wait_buffer(slot)

    prev = jnp.maximum(b - 1, 0) * n_chunks
    ship_buffer(1 - slot, lambda k: jnp.where(b == 0, spill + k * g, cdst_s[prev + k]))

    route = route_ref[...]
    p1, p2 = _local_slots(route, loffv_ref[0:1, :])
    tn = (((0,), (0,)), ((), ()))
    picks = (p1 | p2).astype(BF16)
    buf_ref[slot, :, 0:d // 2] = _pack_bf16_pairs(lax.dot_general(picks, hn_ref[...], tn, preferred_element_type=F32))
    lane = lax.broadcasted_iota(jnp.int32, route.shape, 1)
    meta = jnp.zeros((MOE_SLOTS, LANES), F32)
    for p, col in ((p1, ROUTE_W1), (p2, ROUTE_W2)):
        w = route[:, col:col + 1]
        parts = _split3(w)
        wm = sum(jnp.where(lane == k, part, 0.0) for k, part in enumerate(parts))
        meta = meta + lax.dot_general(p.astype(BF16), wm.astype(BF16), tn, preferred_element_type=F32)
    buf_ref[slot, :, d // 2:d // 2 + LANES] = lax.bitcast_convert_type(meta, jnp.uint32)

    @pl.when(b == nb - 1)
    def _():
        wait_buffer(1 - slot)
        ship_buffer(slot, lambda k: cdst_s[b * n_chunks + k])
        wait_buffer(slot)
        lax.fori_loop(0, ztot_s[0], lambda _, c: (zero_copy(0).wait(), c)[1], 0)
        lax.fori_loop(nused_s[0], n_xtiles, lambda j, c: (zero_tile_copy(j).wait(), c)[1], 0)


def _moe_scatter(plan, hn_all, route_all, n_tiles, n_xtiles):
    d = hn_all.shape[1]
    tm = MOE_TM
    grid_spec = pltpu.PrefetchScalarGridSpec(
        num_scalar_prefetch=5,
        grid=(n_tiles,),
        in_specs=[pl.BlockSpec((tm, d), lambda i, *_: (i, 0)),
                  pl.BlockSpec((tm, LANES), lambda i, *_: (i, 0)),
                  pl.BlockSpec((SUBLANES, LANES), lambda i, *_: (i, 0))],
        out_specs=pl.BlockSpec(memory_space=pl.ANY),
        scratch_shapes=[pltpu.VMEM((2, MOE_SLOTS, d // 2 + LANES), jnp.uint32),
                        pltpu.VMEM((MOE_TMX, d // 2 + LANES), jnp.uint32),
                        pltpu.SemaphoreType.DMA((2,)),
                        pltpu.SemaphoreType.DMA(()),
                        pltpu.SemaphoreType.DMA(())])
    return pl.pallas_call(
        _moe_scatter_kernel,
        grid_spec=grid_spec,
        out_shape=jax.ShapeDtypeStruct(((n_xtiles + MOE_SPILL_TILES) * MOE_TMX, d // 2 + LANES), jnp.uint32),
        compiler_params=_cparams(("arbitrary",)),
        name="moe_scatter",
    )(plan["chunk_dst"], plan["zst"], plan["znch"], plan["ztot"], plan["n_used"],
      hn_all, route_all, plan["loff_rows"])


def _moe_experts_kernel(xe_s, first_s, next_s, par_s, nused_s, x_ref, w1_hbm, w3_hbm, w2_hbm, y_ref,
                        w1f_ref, w3f_ref, w2f_ref, w1b_ref, w3b_ref, w2b_ref, sem):
    d = w1b_ref.shape[0]

    def weight_copies(e, p):
        return [pltpu.make_async_copy(src.at[e], dst.at[p], sem.at[p, i])
                for i, (src, dst) in enumerate(((w1_hbm, w1f_ref), (w3_hbm, w3f_ref), (w2_hbm, w2f_ref)))]

    def one_tile(j, rows):
        @pl.when(j < nused_s[0])
        def _():
            @pl.when(first_s[j] == 1)
            def _():
                p = par_s[j]

                @pl.when(j == 0)
                def _():
                    for c in weight_copies(xe_s[j], p):
                        c.start()
                for c in weight_copies(xe_s[j], p):
                    c.wait()
                w1b_ref[...] = w1f_ref[p].astype(BF16)
                w3b_ref[...] = w3f_ref[p].astype(BF16)
                w2b_ref[...] = w2f_ref[p].astype(BF16)

                @pl.when(next_s[j] >= 0)
                def _():
                    for c in weight_copies(next_s[j], 1 - p):
                        c.start()

            x = _unpack_bf16_pairs(x_ref[rows, 0:d // 2])
            gate = jnp.sum(lax.bitcast_convert_type(x_ref[rows, d // 2:d // 2 + LANES], F32), axis=1, keepdims=True)
            a = jnp.dot(x, w1b_ref[...], preferred_element_type=F32)
            b = jnp.dot(x, w3b_ref[...], preferred_element_type=F32)
            hh = (a * _sigmoid(a)) * b * gate
            y_ref[rows] = _pack_bf16_pairs(jnp.dot(hh.astype(BF16), w2b_ref[...], preferred_element_type=F32))

        @pl.when(j >= nused_s[0])
        def _():
            y_ref[rows] = jnp.zeros((MOE_TMX, y_ref.shape[1]), y_ref.dtype)

    for sub in range(MOE_XSTEP_TILES):
        one_tile(MOE_XSTEP_TILES * pl.program_id(0) + sub, slice(sub * MOE_TMX, (sub + 1) * MOE_TMX))


def _moe_experts(plan, xs, w1, w3, w2, n_xtiles):
    ne, d, f = w1.shape
    per = MOE_XSTEP_TILES
    assert (n_xtiles + MOE_SPILL_TILES) % per == 0
    last = lambda s, nu: jnp.maximum(jnp.minimum(s, (nu[0] - 1) // per), 0)
    any_spec = pl.BlockSpec(memory_space=pl.ANY)
    grid_spec = pltpu.PrefetchScalarGridSpec(
        num_scalar_prefetch=5,
        grid=((n_xtiles + MOE_SPILL_TILES) // per,),
        in_specs=[pl.BlockSpec((per * MOE_TMX, d // 2 + LANES), lambda s, xe, fi, nx, pa, nu: (last(s, nu), 0)),
                  any_spec, any_spec, any_spec],
        out_specs=pl.BlockSpec((per * MOE_TMX, d // 2), lambda s, *_: (s, 0)),
        scratch_shapes=[pltpu.VMEM((2, d, f), F32), pltpu.VMEM((2, d, f), F32), pltpu.VMEM((2, f, d), F32),
                        pltpu.VMEM((d, f), BF16), pltpu.VMEM((d, f), BF16), pltpu.VMEM((f, d), BF16),
                        pltpu.SemaphoreType.DMA((2, 3))])
    return pl.pallas_call(
        _moe_experts_kernel,
        grid_spec=grid_spec,
        out_shape=jax.ShapeDtypeStruct(((n_xtiles + MOE_SPILL_TILES) * MOE_TMX, d // 2), jnp.uint32),
        compiler_params=_cparams(("arbitrary",)),
        name="moe_experts",
    )(plan["xtile_expert"], plan["xtile_first"], plan["xtile_next"], plan["xtile_parity"], plan["n_used"],
      xs, w1, w3, w2)


def _moe_combine_kernel(cdst_s, ys_hbm, route_ref, loffv_ref, h_ref, lnf_ref, y_ref,
                        buf_ref, sem, *, tile0, final_norm):
    b = pl.program_id(0)
    nb = pl.num_programs(0)
    slot = b % 2
    g = SUBLANES
    n_chunks = MOE_SLOTS // g
    tile = b + tile0

    def run_copy(s, src_row, dst_row):
        return pltpu.make_async_copy(ys_hbm.at[pl.ds(src_row, g)], buf_ref.at[s, pl.ds(dst_row, g)], sem.at[s])

    def fetch(t, s):
        for k in range(n_chunks):
            run_copy(s, pl.multiple_of(cdst_s[t * n_chunks + k], g), k * g).start()

    def wait_buffer(s):
        for _ in range(n_chunks):
            run_copy(s, 0, 0).wait()

    @pl.when(b == 0)
    def _():
        fetch(tile, slot)

    fetch(jnp.minimum(tile + 1, tile0 + nb - 1), 1 - slot)
    wait_buffer(slot)

    @pl.when(b == nb - 1)
    def _():
        wait_buffer(1 - slot)

    p1, p2 = _local_slots(route_ref[...], loffv_ref[0:1, :])
    picks = (p1 | p2).astype(BF16)
    y = h_ref[...] + jnp.dot(picks, _unpack_bf16_pairs(buf_ref[slot]), preferred_element_type=F32)
    if final_norm:
        y = _rms(y, lnf_ref[...])
    y_ref[...] = y


def _moe_combine(plan, ys, route_all, h_all, lnf, tile0, n_tiles, final_norm):
    d = h_all.shape[1]
    tm = MOE_TM
    grid_spec = pltpu.PrefetchScalarGridSpec(
        num_scalar_prefetch=1,
        grid=(n_tiles,),
        in_specs=[pl.BlockSpec(memory_space=pl.ANY),
                  pl.BlockSpec((tm, LANES), lambda i, *_: (i + tile0, 0)),
                  pl.BlockSpec((SUBLANES, LANES), lambda i, *_: (i + tile0, 0)),
                  pl.BlockSpec((tm, d), lambda i, *_: (i + tile0, 0)),
                  pl.BlockSpec((1, d), lambda i, *_: (0, 0))],
        out_specs=pl.BlockSpec((tm, d), lambda i, *_: (i, 0)),
        scratch_shapes=[pltpu.VMEM((2, MOE_SLOTS, d // 2), jnp.uint32), pltpu.SemaphoreType.DMA((2,))])
    return pl.pallas_call(
        functools.partial(_moe_combine_kernel, tile0=tile0, final_norm=final_norm),
        grid_spec=grid_spec,
        out_shape=jax.ShapeDtypeStruct((n_tiles * tm, d), F32),
        compiler_params=_cparams(("arbitrary",)),
        name="moe_combine",
    )(plan["chunk_dst"], ys, route_all, plan["loff_rows"], h_all, lnf.reshape(1, d))


def _prompt_states(kvt, n, t):
    hd = HEAD_DIM
    tail = kvt.shape[2]

    def state(lo, heads, win):
        w = min(win, t)
        kv = kvt[:, lo:lo + 2 * heads * hd, tail - w:]
        return jnp.transpose(kv.reshape(n, 2, heads, hd, w), (0, 4, 1, 2, 3))

    out = [state(0, A_KV_HEADS, A_WINDOW)]
    for p, (win, _) in enumerate(B_PATTERNS):
        out.append(state(2 * KA_W + 2 * p * PB_W, B_HEADS_PER_PATTERN, win))
    return out


def _layer(xp, xs, caches, rel_bias, ln1, w_in, sinks, w_pa, w_pb, w_out, ln2, w_rg, b_rg, w_re, b_re,
           w1, w3, w2, lnf, final_norm):
    n, t, d = xp.shape
    ns = xs.shape[0]
    assert xs.shape[1] == 1 and OFF_GA + 2 * d == w_in.shape[1] and P_QA == 2 * d
    w_kvt = _kv_weights_transposed(w_in)
    wpa, wpb, wout = w_pa.astype(BF16), w_pb.astype(BF16), w_out.astype(BF16)
    lanes_left = LANES - N_EXPERTS - MOE_GROUPS
    wr = jnp.pad(jnp.concatenate([w_re, w_rg], axis=1), ((0, 0), (0, lanes_left)))
    br = jnp.pad(jnp.concatenate([b_re, b_rg])[None, :], ((0, 0), (0, lanes_left)))
    wr_hi = wr.astype(BF16)
    wr = jnp.stack([wr_hi, (wr - wr_hi.astype(F32)).astype(BF16)])

    tm = MOE_TM
    xp2 = xp.reshape(n * t, d)
    xs2 = xs.reshape(ns, d)
    proj_s, w_bf = _in_proj_sample(xs2, ln1, w_in)
    if ns == 2 * (n * t // MIX_TM):
        (act, qkv2, qkv3, kvt), sampled = _in_proj_mix(xp2, ln1, w_bf, w_kvt, n, t, MIX_TM, proj_s, caches,
                                                       rel_bias, sinks)
    else:
        act, qkv2, qkv3, kvt = _in_proj_prompt(xp2, ln1, w_bf, w_kvt, n, t, IN_PROJ_TM)
        sampled = _sample_mix(proj_s, caches, rel_bias, sinks)
    oa_s, ob_s, lse_s, st_s = sampled
    act4 = act.reshape(n, 1, t, ACT_W)
    bias_a = _band_bias(rel_bias[:, :A_HEADS], A_WINDOW - 1, 1)
    (oa,) = _band_attn(act4, bias_a, sinks, q_off=P_QA, k_off=P_KA, v_off=P_VA,
                       kv_heads=A_KV_HEADS, grp=A_GROUP, want_lse=False)
    obs, lses = [], []
    for p, (win, dil) in enumerate(B_PATTERNS):
        lo = A_HEADS + p * B_HEADS_PER_PATTERN
        bias_p = _band_bias(rel_bias[:, lo:lo + B_HEADS_PER_PATTERN], win // dil, dil)
        src, base = ((act4, P_B), (qkv2, 0), (qkv3, 0))[p]
        o, lse = _band_attn(src, bias_p, None, q_off=base, k_off=base + PB_W, v_off=base + 2 * PB_W,
                            kv_heads=B_HEADS_PER_PATTERN, grp=1, want_lse=True)
        obs.append(o)
        lses.append(lse)
    assert ns <= MOE_TM
    p_tiles = n * t // MOE_TM
    n_tiles = p_tiles + 1
    m_all = n_tiles * MOE_TM
    st_p = _prompt_states(kvt, n, t)

    rows = lambda a: jnp.pad(a, ((0, MOE_TM - ns), (0, 0)))
    obs_s = [rows(ob_s[:, p].astype(BF16)).reshape(1, 1, MOE_TM, PB_W) for p in range(N_PAT)]
    lses_s = [rows(jnp.repeat(lse_s[:, p], HEAD_DIM, axis=-1)).reshape(1, 1, MOE_TM, PB_W) for p in range(N_PAT)]
    gates_s = rows(proj_s[:, OFF_GA:].astype(BF16))
    routed_s = _post_attn(rows(oa_s.astype(BF16)), obs_s, lses_s, gates_s, 0, rows(xs2),
                          wpa, wpb, wout, ln2, wr, br, MOE_TM, 1, None)
    h_all, hn_all, route_all, cnt_all = _post_attn(oa.reshape(n * t, QA_W), obs, lses, act, 0, xp2,
                                                   wpa, wpb, wout, ln2, wr, br, tm, t // tm, routed_s)

    max_rows = 2 * m_all + n_tiles * N_EXPERTS * (SUBLANES - 1) + N_EXPERTS * (MOE_TMX - SUBLANES)
    n_xtiles = -(-max_rows // MOE_TMX)
    n_xtiles += -(n_xtiles + MOE_SPILL_TILES) % MOE_XSTEP_TILES
    plan = _route_plan(cnt_all, n_tiles, n_xtiles)
    xs_sorted = _moe_scatter(plan, hn_all, route_all, n_tiles, n_xtiles)
    ys_sorted = _moe_experts(plan, xs_sorted, w1, w3, w2, n_xtiles)
    yp = _moe_combine(plan, ys_sorted, route_all, h_all, lnf, 0, p_tiles, final_norm).reshape(n, t, d)
    ys = _moe_combine(plan, ys_sorted, route_all, h_all, lnf, p_tiles, 1, final_norm)[:ns].reshape(ns, 1, d)
    return yp, ys, st_p, st_s


def kernel(x_prompt, x_sample, cache_a_kv, cache_b1_kv, cache_b2_kv, cache_b3_kv, rel_bias, ln1_g, w_in, sinks,
           w_pa, w_pb, w_out, ln2_g, w_rg, b_rg, w_re, b_re, w1, w3, w2, lnf_g):
    depth = w_in.shape[0]
    assert depth >= 1
    xp, xs = x_prompt, x_sample
    new_p = [[] for _ in range(4)]
    new_s = [[] for _ in range(4)]
    for l in range(depth):
        caches = (cache_a_kv[l], cache_b1_kv[l], cache_b2_kv[l], cache_b3_kv[l])
        xp, xs, st_p, st_s = _layer(xp, xs, caches, rel_bias, ln1_g[l], w_in[l], sinks[l], w_pa[l], w_pb[l],
                                    w_out[l], ln2_g[l], w_rg[l], b_rg[l], w_re[l], b_re[l], w1[l], w3[l], w2[l],
                                    lnf_g, l == depth - 1)
        for i in range(4):
            new_p[i].append(st_p[i])
            new_s[i].append(st_s[i])
    a_p, b1_p, b2_p, b3_p = [jnp.stack(v) for v in new_p]
    a_s, b1_s, b2_s, b3_s = [jnp.stack(v) for v in new_s]
    return (xp, xs, a_p, a_s, b1_p, b1_s, b2_p, b2_s, b3_p, b3_s)
```

```python
import functools
import math

import numpy as np
import jax
import jax.numpy as jnp
from jax import lax
from jax.experimental import pallas as pl
from jax.experimental.pallas import tpu as pltpu

F32 = jnp.float32
BF16 = jnp.bfloat16

HEAD_DIM = 64
A_HEADS = 8
A_KV_HEADS = 2
A_GROUP = A_HEADS // A_KV_HEADS
A_WINDOW = 128
B_PATTERNS = ((128, 1), (512, 4), (2048, 16))
N_PAT = len(B_PATTERNS)
B_HEADS_PER_PATTERN = 4
B_HEADS = B_HEADS_PER_PATTERN * N_PAT
BLOCK = 128
ATTN_CHAINS = 32
NUM_BUCKETS = 32
MAX_DISTANCE = 2048
MOE_GROUPS = 4
EXPERTS_PER_GROUP = 8
N_EXPERTS = MOE_GROUPS * EXPERTS_PER_GROUP
EPS = 1e-6
NEG = -1e30
LANES = 128
SUBLANES = 8
ROUTE_I1, ROUTE_I2, ROUTE_W1, ROUTE_W2 = 0, 1, 2, 3
IN_PROJ_TM = 512
MIX_TM = 256
MOE_TM = 256
MOE_SLOTS = 2 * MOE_TM + N_EXPERTS * SUBLANES
MOE_TMX = 256
MOE_SPILL_TILES = -(-MOE_SLOTS // MOE_TMX)
MOE_XSTEP_TILES = 4
Q_SCALE = HEAD_DIM ** -0.5

QA_W = A_HEADS * HEAD_DIM
KA_W = A_KV_HEADS * HEAD_DIM
QB_W = B_HEADS * HEAD_DIM
PB_W = B_HEADS_PER_PATTERN * HEAD_DIM
QKV_W = 3 * PB_W
OFF_KA = QA_W
OFF_QB = OFF_KA + 2 * KA_W
OFF_KB = OFF_QB + QB_W
OFF_VB = OFF_KB + QB_W
OFF_GA = OFF_VB + QB_W
P_QA = 2048
P_KA = P_QA + QA_W
P_VA = P_KA + KA_W
P_B = P_VA + KA_W
ACT_W = P_B + QKV_W
KVT_ROWS = 2 * KA_W + 2 * QB_W

VMEM_LIMIT = 56 * 1024 * 1024


def _cparams(sem):
    return pltpu.CompilerParams(dimension_semantics=sem, vmem_limit_bytes=VMEM_LIMIT)


def _bucket_np(dist):
    dist = np.asarray(dist, np.int64)
    max_exact = NUM_BUCKETS // 2
    df = np.maximum(dist, max_exact).astype(np.float64)
    large = max_exact + (np.log(df / max_exact) / math.log(MAX_DISTANCE / max_exact)
                         * (NUM_BUCKETS - max_exact)).astype(np.int64)
    return np.where(dist < max_exact, dist, np.minimum(large, NUM_BUCKETS - 1))


def _table_rows(table_cols, dist, valid):
    onehot = (_bucket_np(dist)[:, None] == np.arange(NUM_BUCKETS)[None, :]).astype(np.float32)
    rows = jnp.einsum("ck,kh->hc", jnp.asarray(onehot), table_cols.astype(F32), precision=lax.Precision.HIGHEST)
    return jnp.where(jnp.asarray(valid)[None, :], rows, NEG)


def _band_bias(table_cols, max_dist, dilation):
    period = 3 * BLOCK
    m = np.arange(period)
    k = np.where(m < 2 * BLOCK, m, m - period)
    dist = BLOCK - k
    valid = (dist >= 0) & (dist <= max_dist) & (m != 2 * BLOCK)
    v = _table_rows(table_cols, np.clip(dist, 0, None) * dilation, valid)
    heads = v.shape[0]
    flat = jnp.tile(v, (1, BLOCK))[:, :BLOCK * (period - 1)]
    return flat.reshape(heads, BLOCK, period - 1)[:, :, :2 * BLOCK]


def _decode_bias(table_cols, width, dilation, first_valid):
    c = np.arange(width)
    valid = (c % dilation == 0) & (c >= first_valid)
    rows = _table_rows(table_cols, width - c, valid)
    self_bias = _table_rows(table_cols, np.zeros((1,), np.int64), np.ones((1,), bool))
    return rows[:, None, :], self_bias[:, None, :]


def _rms(x, g):
    return (x * lax.rsqrt(jnp.mean(x * x, axis=-1, keepdims=True) + EPS)) * g


def _sigmoid(x):
    return 1.0 / (1.0 + jnp.exp(-x))


YS_CHUNKS = QKV_W // LANES


def _in_proj_work(x_ref, g_ref, w_ref, act_ref, qkv2_ref, qkv3_ref, ys_ref, phase):
    tm = x_ref.shape[0]
    cache = {}

    def xb():
        if "xb" not in cache:
            cache["xb"] = _rms(x_ref[...], g_ref[...]).astype(BF16)
        return cache["xb"]

    def proj(lo, hi):
        return jnp.dot(xb(), w_ref[:, lo:hi], preferred_element_type=F32)

    def gates(c):
        act_ref[:, c:c + 512] = _sigmoid(proj(OFF_GA + c, OFF_GA + c + 512)).astype(BF16)

    def mixer_a_q():
        act_ref[:, P_QA:P_KA] = (proj(0, OFF_KA) * Q_SCALE).astype(BF16)

    def mixer_a_kv():
        act_ref[:, P_KA:P_B] = proj(OFF_KA, OFF_QB).astype(BF16)

    def pattern_part(p, j):
        lo = (OFF_QB, OFF_KB, OFF_VB)[j] + p * PB_W
        part = proj(lo, lo + PB_W)
        if j == 0:
            part = part * Q_SCALE
        if p == 0:
            act_ref[:, P_B + j * PB_W:P_B + (j + 1) * PB_W] = part.astype(BF16)
        else:
            for c in range(PB_W // LANES):
                ys_ref[(p - 1) * YS_CHUNKS + j * (PB_W // LANES) + c] = part[:, c * LANES:(c + 1) * LANES]

    def regroup(p, out_ref):
        dil = B_PATTERNS[p][1]
        for c in range(YS_CHUNKS):
            for r in range(dil):
                out_ref[r, :, c * LANES:(c + 1) * LANES] = (
                    ys_ref[(p - 1) * YS_CHUNKS + c, pl.ds(r, tm // dil, stride=dil), :].astype(BF16))

    work = []
    if phase in (None, 0):
        work += [functools.partial(gates, c) for c in range(0, P_QA, 512)] + [mixer_a_q, mixer_a_kv]
    if phase in (None, 1):
        for p in range(N_PAT):
            work += [functools.partial(pattern_part, p, j) for j in range(3)]
            if p > 0:
                work.append(functools.partial(regroup, p, (qkv2_ref, qkv3_ref)[p - 1]))
    return work


def _transpose_cast_kernel(w_ref, o_ref):
    o_ref[...] = w_ref[...].T.astype(o_ref.dtype)


def _kv_weights_transposed(w_in):
    d = w_in.shape[0]
    blk = PB_W
    assert 2 * KA_W == blk and OFF_KA % blk == 0 and OFF_KB % blk == 0 and OFF_VB % blk == 0
    ka, kb, vb = OFF_KA // blk, OFF_KB // blk, OFF_VB // blk

    def column_block(i):
        p = (i - 1) // 2
        return jnp.where(i == 0, ka, jnp.where(i % 2 == 1, kb + p, vb + p))

    return pl.pallas_call(
        _transpose_cast_kernel,
        grid=(KVT_ROWS // blk,),
        in_specs=[pl.BlockSpec((d, blk), lambda i: (0, column_block(i)))],
        out_specs=pl.BlockSpec((blk, d), lambda i: (i, 0)),
        out_shape=jax.ShapeDtypeStruct((KVT_ROWS, d), BF16),
        compiler_params=_cparams(("arbitrary",)),
        name="kv_weights_t",
    )(w_in)


def _kv_tail(x_ref, g_ref, wkvt_ref, kvt_ref):
    xb = _rms(x_ref[...], g_ref[...]).astype(BF16)
    kvt_ref[...] = lax.dot_general(wkvt_ref[...], xb, (((1,), (1,)), ((), ())), preferred_element_type=F32)


def _in_proj_kernel(x_ref, g_ref, w_ref, wkvt_ref, act_ref, qkv2_ref, qkv3_ref, kvt_ref, ys_ref, *,
                    tiles_per_seq, tail_tiles):
    for piece in _in_proj_work(x_ref, g_ref, w_ref, act_ref, qkv2_ref, qkv3_ref, ys_ref, None):
        piece()

    @pl.when(pl.program_id(0) % tiles_per_seq >= tiles_per_seq - tail_tiles)
    def _():
        _kv_tail(x_ref, g_ref, wkvt_ref, kvt_ref)


def _in_proj_prompt(x2d, ln_g, w_bf, w_kvt, n, t, tm):
    m, d = x2d.shape
    tps = t // tm
    tail = min(max(w for w, _ in B_PATTERNS), t)
    assert t % tm == 0 and tail % tm == 0 and all(tm % (16 * dl) == 0 for _, dl in B_PATTERNS)
    tail_tiles = tail // tm
    d2, d3 = B_PATTERNS[1][1], B_PATTERNS[2][1]
    return pl.pallas_call(
        functools.partial(_in_proj_kernel, tiles_per_seq=tps, tail_tiles=tail_tiles),
        grid=(m // tm,),
        in_specs=[pl.BlockSpec((tm, d), lambda i: (i, 0)),
                  pl.BlockSpec((1, d), lambda i: (0, 0)),
                  pl.BlockSpec(w_bf.shape, lambda i: (0, 0), pipeline_mode=pl.Buffered(1)),
                  pl.BlockSpec(w_kvt.shape, lambda i: (0, 0), pipeline_mode=pl.Buffered(1))],
        out_specs=[pl.BlockSpec((tm, ACT_W), lambda i: (i, 0)),
                   pl.BlockSpec((None, d2, tm // d2, QKV_W), lambda i: (i // tps, 0, i % tps, 0)),
                   pl.BlockSpec((None, d3, tm // d3, QKV_W), lambda i: (i // tps, 0, i % tps, 0)),
                   pl.BlockSpec((None, KVT_ROWS, tm),
                                lambda i: (i // tps, 0, jnp.maximum(i % tps - (tps - tail_tiles), 0)))],
        out_shape=[jax.ShapeDtypeStruct((m, ACT_W), BF16),
                   jax.ShapeDtypeStruct((n, d2, t // d2, QKV_W), BF16),
                   jax.ShapeDtypeStruct((n, d3, t // d3, QKV_W), BF16),
                   jax.ShapeDtypeStruct((n, KVT_ROWS, tail), F32)],
        scratch_shapes=[pltpu.VMEM(((N_PAT - 1) * YS_CHUNKS, tm, LANES), F32)],
        compiler_params=_cparams(("arbitrary",)),
        name="in_proj",
    )(x2d, ln_g.reshape(1, d), w_bf, w_kvt)


def _in_proj_sample_kernel(x_ref, g_ref, w_ref, y_ref, wb_ref):
    blk = w_ref.shape[1]
    wb = w_ref[...].astype(BF16)
    wb_ref[...] = wb
    y = jnp.dot(_rms(x_ref[...], g_ref[...]).astype(BF16), wb, preferred_element_type=F32)
    col = pl.program_id(0) * blk + lax.broadcasted_iota(jnp.int32, y.shape, 1)
    is_q = (col < OFF_KA) | ((col >= OFF_QB) & (col < OFF_KB))
    y = jnp.where(is_q, y * Q_SCALE, y)
    y_ref[...] = jnp.where(col >= OFF_GA, _sigmoid(y), y)


def _in_proj_sample(x2d, ln_g, w_in, blk=512):
    m, d = x2d.shape
    in_w = w_in.shape[1]
    assert in_w % blk == 0
    return pl.pallas_call(
        _in_proj_sample_kernel,
        grid=(in_w // blk,),
        in_specs=[pl.BlockSpec((m, d), lambda i: (0, 0)),
                  pl.BlockSpec((1, d), lambda i: (0, 0)),
                  pl.BlockSpec((d, blk), lambda i: (0, i))],
        out_specs=[pl.BlockSpec((m, blk), lambda i: (0, i)), pl.BlockSpec((d, blk), lambda i: (0, i))],
        out_shape=[jax.ShapeDtypeStruct((m, in_w), F32), jax.ShapeDtypeStruct((d, in_w), BF16)],
        compiler_params=_cparams(("arbitrary",)),
        name="in_proj_sample",
    )(x2d, ln_g.reshape(1, d), w_in)


def _band_attn_kernel(*refs, kv_heads, grp, has_sink, want_lse):
    sum_on_mxu = grp > 1
    if has_sink:
        sink_ref, refs = refs[0], refs[1:]
    q_ref, kp_ref, kc_ref, vp_ref, vc_ref, bias_ref, o_ref = refs[:7]
    lse_ref = refs[7] if want_lse else None
    step = pl.program_id(2)
    hd = HEAD_DIM
    nt = (((1,), (1,)), ((), ()))
    chains = [(sub, kv * grp + g, slice(kv * hd, (kv + 1) * hd))
              for sub in range(q_ref.shape[0] // BLOCK) for kv in range(kv_heads) for g in range(grp)]
    scores = []
    for sub, h, ks in chains:
        rows = slice(sub * BLOCK, (sub + 1) * BLOCK)
        q = q_ref[rows, h * hd:(h + 1) * hd]
        kp = kp_ref[:, ks] if sub == 0 else kc_ref[(sub - 1) * BLOCK:sub * BLOCK, ks]
        sp = lax.dot_general(q, kp, nt, preferred_element_type=F32) + bias_ref[h, :, 0:BLOCK]
        sc = lax.dot_general(q, kc_ref[rows, ks], nt, preferred_element_type=F32) + bias_ref[h, :, BLOCK:2 * BLOCK]
        if sub == 0:
            sp = jnp.where(step > 0, sp, NEG)
        scores.append((sp, sc))
    maxes = []
    for (sub, h, ks), (sp, sc) in zip(chains, scores):
        m = jnp.maximum(jnp.max(sp, axis=-1, keepdims=True), jnp.max(sc, axis=-1, keepdims=True))
        maxes.append(jnp.maximum(m, sink_ref[h]) if has_sink else m)
    probs = []
    ones = jnp.ones((BLOCK, LANES), BF16)
    for (sub, h, ks), (sp, sc), m in zip(chains, scores, maxes):
        pp, pc = jnp.exp(sp - m), jnp.exp(sc - m)
        if sum_on_mxu:
            pp, pc = pp.astype(BF16), pc.astype(BF16)
            den = (jnp.dot(pp, ones, preferred_element_type=F32)
                   + jnp.dot(pc, ones, preferred_element_type=F32))[:, 0:1]
        else:
            den = jnp.sum(pp, axis=-1, keepdims=True) + jnp.sum(pc, axis=-1, keepdims=True)
            pp, pc = pp.astype(BF16), pc.astype(BF16)
        if has_sink:
            den = den + jnp.exp(sink_ref[h] - m)
        probs.append((pp, pc, den))
    for (sub, h, ks), (pp, pc, den), m in zip(chains, probs, maxes):
        rows = slice(sub * BLOCK, (sub + 1) * BLOCK)
        vp = vp_ref[:, ks] if sub == 0 else vc_ref[(sub - 1) * BLOCK:sub * BLOCK, ks]
        o = (jnp.dot(pp, vp, preferred_element_type=F32)
             + jnp.dot(pc, vc_ref[rows, ks], preferred_element_type=F32))
        o_ref[rows, h * hd:(h + 1) * hd] = (o / den).astype(o_ref.dtype)
        if want_lse:
            lse_ref[rows, h * hd:(h + 1) * hd] = jnp.broadcast_to(m + jnp.log(den), (BLOCK, hd))


def _band_attn(src, bias, sink, *, q_off, k_off, v_off, kv_heads, grp, want_lse):
    n, dil, l, cols = src.shape
    sub = max(1, ATTN_CHAINS // (kv_heads * grp))
    while l % (sub * BLOCK):
        sub //= 2
    rows = sub * BLOCK
    assert sub >= 1 and l % rows == 0
    nb = l // rows
    qw = kv_heads * grp * HEAD_DIM
    kw = kv_heads * HEAD_DIM
    assert q_off % qw == 0 and k_off % kw == 0 and v_off % kw == 0
    qb, kb, vb = q_off // qw, k_off // kw, v_off // kw
    prev = lambda b: jnp.maximum(sub * b - 1, 0)
    in_specs = [
        pl.BlockSpec((None, None, rows, qw), lambda i, r, b: (i, r, b, qb)),
        pl.BlockSpec((None, None, BLOCK, kw), lambda i, r, b: (i, r, prev(b), kb)),
        pl.BlockSpec((None, None, rows, kw), lambda i, r, b: (i, r, b, kb)),
        pl.BlockSpec((None, None, BLOCK, kw), lambda i, r, b: (i, r, prev(b), vb)),
        pl.BlockSpec((None, None, rows, kw), lambda i, r, b: (i, r, b, vb)),
        pl.BlockSpec(bias.shape, lambda i, r, b: (0, 0, 0)),
    ]
    args = [src, src, src, src, src, bias]
    has_sink = sink is not None
    if has_sink:
        in_specs = [pl.BlockSpec(memory_space=pltpu.SMEM)] + in_specs
        args = [sink.astype(F32)] + args
    out_specs = [pl.BlockSpec((None, None, rows, qw), lambda i, r, b: (i, r, b, 0))]
    out_shape = [jax.ShapeDtypeStruct((n, dil, l, qw), BF16)]
    if want_lse:
        out_specs.append(pl.BlockSpec((None, None, rows, qw), lambda i, r, b: (i, r, b, 0)))
        out_shape.append(jax.ShapeDtypeStruct((n, dil, l, qw), F32))
    return pl.pallas_call(
        functools.partial(_band_attn_kernel, kv_heads=kv_heads, grp=grp, has_sink=has_sink, want_lse=want_lse),
        grid=(n, dil, nb),
        in_specs=in_specs,
        out_specs=out_specs,
        out_shape=out_shape,
        compiler_params=_cparams(("arbitrary", "arbitrary", "arbitrary")),
        name=f"band_attn_d{dil}",
    )(*args)


COL_QA = 0
COL_QB = COL_QA + A_HEADS
COL_KA = COL_QB + B_HEADS
COL_VA = COL_KA + A_KV_HEADS
COL_KB = COL_VA + A_KV_HEADS
COL_VB = COL_KB + B_HEADS
N_COLS = COL_VB + B_HEADS


def _attend_cached(kt, vt, q, k_new, v_new, bias, self_bias, sink, write):
    s = jnp.sum(kt * q, axis=1, keepdims=True) + bias
    s_new = jnp.sum(k_new * q, axis=1, keepdims=True) + self_bias
    yield
    m = jnp.maximum(jnp.max(s, axis=2, keepdims=True), s_new)
    if sink is not None:
        m = jnp.maximum(m, sink)
    yield
    p = jnp.exp(s - m)
    p_new = jnp.exp(s_new - m)
    den = jnp.sum(p, axis=2, keepdims=True) + p_new
    if sink is not None:
        den = den + jnp.exp(sink - m)
    yield
    o = (jnp.sum(vt * p, axis=2, keepdims=True) + v_new * p_new) / den
    write(o, m + jnp.log(den))
    yield


def _sample_work(cols_ref, ca_ref, c1_ref, c2_ref, c3_ref, ba_ref, sa_ref, sink_ref, b1_ref, b2_ref, b3_ref,
                 sb_ref, o_ref, lse_ref, ra_ref, r1_ref, r2_ref, r3_ref):
    cols = cols_ref[...]

    def stack(js):
        return jnp.stack([cols[:, j:j + 1] for j in js])

    o_ref[...] = jnp.zeros_like(o_ref)
    lse_ref[...] = jnp.zeros_like(lse_ref)

    def write_a(o, _):
        for h in range(A_HEADS):
            o_ref[:, COL_QA + h:COL_QA + h + 1] = o[h]

    def write_b(p):
        def write(o, lse):
            for h in range(B_HEADS_PER_PATTERN):
                j = p * B_HEADS_PER_PATTERN + h
                o_ref[:, COL_QB + j:COL_QB + j + 1] = o[h]
                lse_ref[0:1, j:j + 1] = lse[h]
        return write

    kv_of = [h // A_GROUP for h in range(A_HEADS)]
    stages = [_attend_cached(jnp.stack([ca_ref[0, kv] for kv in kv_of]), jnp.stack([ca_ref[1, kv] for kv in kv_of]),
                             stack(range(COL_QA, COL_QA + A_HEADS)), stack([COL_KA + kv for kv in kv_of]),
                             stack([COL_VA + kv for kv in kv_of]), ba_ref[...], sa_ref[...], sink_ref[...], write_a)]
    for p, (c_ref, b_ref) in enumerate(((c1_ref, b1_ref), (c2_ref, b2_ref), (c3_ref, b3_ref))):
        js = range(p * B_HEADS_PER_PATTERN, (p + 1) * B_HEADS_PER_PATTERN)
        stages.append(_attend_cached(c_ref[0], c_ref[1], stack([COL_QB + j for j in js]),
                                     stack([COL_KB + j for j in js]), stack([COL_VB + j for j in js]),
                                     b_ref[...], sb_ref[p], None, write_b(p)))

    def roll_plane(c_ref, r_ref, i, h, new_col):
        w = c_ref.shape[-1]
        x = c_ref[i, h]
        lane = lax.broadcasted_iota(jnp.int32, x.shape, 1)
        r_ref[i, h] = jnp.where(lane == w - 1, cols[:, new_col:new_col + 1], pltpu.roll(x, w - 1, 1))

    planes = []
    for ci, (c_ref, r_ref) in enumerate(((ca_ref, ra_ref), (c1_ref, r1_ref), (c2_ref, r2_ref), (c3_ref, r3_ref))):
        for i, (first_a, first_b) in enumerate(((COL_KA, COL_KB), (COL_VA, COL_VB))):
            for h in range(c_ref.shape[1]):
                new_col = first_a + h if ci == 0 else first_b + (ci - 1) * B_HEADS_PER_PATTERN + h
                planes.append((c_ref.shape[-1], functools.partial(roll_plane, c_ref, r_ref, i, h, new_col)))

    work = []
    total = sum(w for w, _ in planes)
    n_slots = 4 * len(stages)
    done, k = 0, 0
    for slot in range(n_slots):
        work.append(functools.partial(next, stages[slot // 4]))
        while k < len(planes) and done < total * (slot + 1) // n_slots:
            done += planes[k][0]
            work.append(planes[k][1])
            k += 1
    assert k == len(planes)
    return work


def _sample_mix_kernel(*refs):
    for piece in _sample_work(*refs):
        piece()


def _run_interleaved(primary, secondary):
    k = 0
    for i, piece in enumerate(primary):
        piece()
        while k < len(secondary) and k < len(secondary) * (i + 1) // len(primary):
            secondary[k]()
            k += 1
    for piece in secondary[k:]:
        piece()


def _in_proj_mix_kernel(x_ref, g_ref, w_ref, wkvt_ref, *rest, tiles_per_seq, tail_tiles, n_sample_in):
    sample_in = rest[:n_sample_in]
    act_ref, qkv2_ref, qkv3_ref, kvt_ref = rest[n_sample_in:n_sample_in + 4]
    sample_out = rest[n_sample_in + 4:-1]
    ys_ref = rest[-1]
    step = pl.program_id(0)
    tile = step // 2
    for phase in (0, 1):
        @pl.when(step % 2 == phase)
        def _():
            _run_interleaved(_sample_work(*sample_in, *sample_out),
                             _in_proj_work(x_ref, g_ref, w_ref, act_ref, qkv2_ref, qkv3_ref, ys_ref, phase))
            if phase == 1:
                @pl.when(tile % tiles_per_seq >= tiles_per_seq - tail_tiles)
                def _():
                    _kv_tail(x_ref, g_ref, wkvt_ref, kvt_ref)


def _sample_operands(proj_s, caches, rel_bias, sinks):
    n = proj_s.shape[0]
    hd = HEAD_DIM
    vecs = jnp.concatenate([proj_s[:, :OFF_KA], proj_s[:, OFF_QB:OFF_KB], proj_s[:, OFF_KA:OFF_QB],
                            proj_s[:, OFF_KB:OFF_GA]], axis=1).reshape(n, N_COLS, hd)
    cols = jnp.pad(jnp.transpose(vecs, (0, 2, 1)), ((0, 0), (0, 0), (0, LANES - N_COLS)))
    cts = [jnp.transpose(c, (0, 2, 3, 4, 1)) for c in caches]

    ba, sa = _decode_bias(rel_bias[:, :A_HEADS], A_WINDOW, 1, 1)
    sink = sinks.astype(F32).reshape(A_HEADS, 1, 1)
    bbs, sbs = [], []
    for p, (win, dil) in enumerate(B_PATTERNS):
        lo = A_HEADS + p * B_HEADS_PER_PATTERN
        assert caches[1 + p].shape[1] == win == BLOCK * dil
        b, s = _decode_bias(rel_bias[:, lo:lo + B_HEADS_PER_PATTERN], win, dil, 0)
        bbs.append(b)
        sbs.append(s)
    sb = jnp.stack(sbs)

    seq_in = [cols] + cts
    consts = [ba, sa, sink] + bbs + [sb]
    out_shape = [jax.ShapeDtypeStruct((n, hd, LANES), F32), jax.ShapeDtypeStruct((n, SUBLANES, LANES), F32)]
    out_shape += [jax.ShapeDtypeStruct(c.shape, c.dtype) for c in cts]
    return seq_in, consts, out_shape


def _per_seq_spec(a):
    nd = len(a.shape)
    return pl.BlockSpec((None,) + tuple(a.shape[1:]), lambda i: (i,) + (0,) * (nd - 1))


def _const_spec(a):
    nd = a.ndim
    return pl.BlockSpec(a.shape, lambda i: (0,) * nd)


def _sample_results(outs):
    n = outs[0].shape[0]
    o_rows = jnp.transpose(outs[0][:, :, :COL_KA], (0, 2, 1))
    oa = o_rows[:, COL_QA:COL_QB].reshape(n, QA_W)
    ob = o_rows[:, COL_QB:COL_KA].reshape(n, N_PAT, PB_W)
    lse = outs[1][:, 0, :B_HEADS].reshape(n, N_PAT, B_HEADS_PER_PATTERN)
    rolled = [jnp.transpose(r, (0, 4, 1, 2, 3)) for r in outs[2:]]
    return oa, ob, lse, rolled


def _sample_mix(proj_s, caches, rel_bias, sinks):
    seq_in, consts, out_shape = _sample_operands(proj_s, caches, rel_bias, sinks)
    outs = pl.pallas_call(
        _sample_mix_kernel,
        grid=(proj_s.shape[0],),
        in_specs=[_per_seq_spec(a) for a in seq_in] + [_const_spec(a) for a in consts],
        out_specs=[_per_seq_spec(s) for s in out_shape],
        out_shape=out_shape,
        compiler_params=_cparams(("arbitrary",)),
        name="sample_mix",
    )(*seq_in, *consts)
    return _sample_results(outs)


def _in_proj_mix(x2d, ln_g, w_bf, w_kvt, n, t, tm, proj_s, caches, rel_bias, sinks):
    m, d = x2d.shape
    tps = t // tm
    tail = min(max(w for w, _ in B_PATTERNS), t)
    assert t % tm == 0 and tail % tm == 0 and all(tm % (16 * dl) == 0 for _, dl in B_PATTERNS)
    assert proj_s.shape[0] == 2 * (m // tm)
    tail_tiles = tail // tm
    d2, d3 = B_PATTERNS[1][1], B_PATTERNS[2][1]
    seq_in, consts, sample_shape = _sample_operands(proj_s, caches, rel_bias, sinks)
    tile = lambda i: i // 2
    outs = pl.pallas_call(
        functools.partial(_in_proj_mix_kernel, tiles_per_seq=tps, tail_tiles=tail_tiles,
                          n_sample_in=len(seq_in) + len(consts)),
        grid=(2 * (m // tm),),
        in_specs=[pl.BlockSpec((tm, d), lambda i: (tile(i), 0)),
                  pl.BlockSpec((1, d), lambda i: (0, 0)),
                  pl.BlockSpec(w_bf.shape, lambda i: (0, 0), pipeline_mode=pl.Buffered(1)),
                  pl.BlockSpec(w_kvt.shape, lambda i: (0, 0), pipeline_mode=pl.Buffered(1))]
                 + [_per_seq_spec(a) for a in seq_in] + [_const_spec(a) for a in consts],
        out_specs=[pl.BlockSpec((tm, ACT_W), lambda i: (tile(i), 0)),
                   pl.BlockSpec((None, d2, tm // d2, QKV_W), lambda i: (tile(i) // tps, 0, tile(i) % tps, 0)),
                   pl.BlockSpec((None, d3, tm // d3, QKV_W), lambda i: (tile(i) // tps, 0, tile(i) % tps, 0)),
                   pl.BlockSpec((None, KVT_ROWS, tm),
                                lambda i: (tile(i) // tps, 0, jnp.maximum(tile(i) % tps - (tps - tail_tiles), 0)))]
                  + [_per_seq_spec(s) for s in sample_shape],
        out_shape=[jax.ShapeDtypeStruct((m, ACT_W), BF16),
                   jax.ShapeDtypeStruct((n, d2, t // d2, QKV_W), BF16),
                   jax.ShapeDtypeStruct((n, d3, t // d3, QKV_W), BF16),
                   jax.ShapeDtypeStruct((n, KVT_ROWS, tail), F32)] + sample_shape,
        scratch_shapes=[pltpu.VMEM(((N_PAT - 1) * YS_CHUNKS, tm, LANES), F32)],
        compiler_params=_cparams(("arbitrary",)),
        name="in_proj_mix",
    )(x2d, ln_g.reshape(1, d), w_bf, w_kvt, *seq_in, *consts)
    return outs[:4], _sample_results(outs[4:])


def _post_attn_kernel(oa_ref, o1_ref, o2_ref, o3_ref, l1_ref, l2_ref, l3_ref, ga_ref, gb_ref, x_ref,
                      wpa_ref, wpb_ref, wout_ref, ln2_ref, wr_ref, br_ref, *rest, n_extra):
    outs = rest[n_extra:n_extra + 4]
    h_ref, hn_ref, route_ref, cnt_ref = outs
    scr_ref = rest[n_extra + 4]
    if n_extra:
        own_tile = pl.program_id(0) < pl.num_programs(0) - 1

        @pl.when(jnp.logical_not(own_tile))
        def _():
            for dst, src in zip(outs, rest[:n_extra]):
                dst[...] = src[...]

        @pl.when(own_tile)
        def _():
            _post_attn_tile(oa_ref, o1_ref, o2_ref, o3_ref, l1_ref, l2_ref, l3_ref, ga_ref, gb_ref, x_ref,
                            wpa_ref, wpb_ref, wout_ref, ln2_ref, wr_ref, br_ref, *outs, scr_ref)
    else:
        _post_attn_tile(oa_ref, o1_ref, o2_ref, o3_ref, l1_ref, l2_ref, l3_ref, ga_ref, gb_ref, x_ref,
                        wpa_ref, wpb_ref, wout_ref, ln2_ref, wr_ref, br_ref, *outs, scr_ref)


def _post_attn_tile(oa_ref, o1_ref, o2_ref, o3_ref, l1_ref, l2_ref, l3_ref, ga_ref, gb_ref, x_ref,
                    wpa_ref, wpb_ref, wout_ref, ln2_ref, wr_ref, br_ref, h_ref, hn_ref, route_ref, cnt_ref, scr_ref):
    tm = x_ref.shape[0]
    chunks = PB_W // LANES

    def token_major(ref, slot):
        dil = ref.shape[0]
        if dil == 1:
            return ref[0].astype(F32)
        for c in range(chunks):
            for r in range(dil):
                scr_ref[slot * chunks + c, pl.ds(r, tm // dil, stride=dil), :] = (
                    ref[r, :, c * LANES:(c + 1) * LANES].astype(F32))
        return jnp.concatenate([scr_ref[slot * chunks + c] for c in range(chunks)], axis=1)

    o1, o2, o3 = (token_major(r, s) for s, r in enumerate((o1_ref, o2_ref, o3_ref)))
    l1, l2, l3 = (token_major(r, 3 + s) for s, r in enumerate((l1_ref, l2_ref, l3_ref)))
    m = jnp.maximum(jnp.maximum(l1, l2), l3)
    a1, a2, a3 = jnp.exp(l1 - m), jnp.exp(l2 - m), jnp.exp(l3 - m)
    ob = (a1 * o1 + a2 * o2 + a3 * o3) / (a1 + a2 + a3)
    ya = jnp.dot(oa_ref[...], wpa_ref[...], preferred_element_type=F32)
    yb = jnp.dot(ob.astype(BF16), wpb_ref[...], preferred_element_type=F32)
    merged = ga_ref[...].astype(F32) * ya + gb_ref[...].astype(F32) * yb
    h = x_ref[...] + jnp.dot(merged.astype(BF16), wout_ref[...], preferred_element_type=F32)
    h_ref[...] = h
    hn = _rms(h, ln2_ref[...])
    hn_hi = hn.astype(BF16)
    hn_ref[...] = hn_hi

    hn_lo = (hn - hn_hi.astype(F32)).astype(BF16)
    logits = (jnp.dot(hn_hi, wr_ref[0], preferred_element_type=F32)
              + jnp.dot(hn_lo, wr_ref[0], preferred_element_type=F32)
              + jnp.dot(hn_hi, wr_ref[1], preferred_element_type=F32)) + br_ref[...]
    lane = lax.broadcasted_iota(jnp.int32, logits.shape, 1)
    is_grp = (lane >= N_EXPERTS) & (lane < N_EXPERTS + MOE_GROUPS)
    lg = jnp.where(is_grp, logits, NEG)
    gmax = jnp.max(lg, axis=-1, keepdims=True)
    g_lane = jnp.min(jnp.where(lg == gmax, lane, LANES), axis=-1, keepdims=True)
    p_g = 1.0 / jnp.sum(jnp.where(is_grp, jnp.exp(lg - gmax), 0.0), axis=-1, keepdims=True)
    e_lo = (g_lane - N_EXPERTS) * EXPERTS_PER_GROUP
    in_grp = (lane >= e_lo) & (lane < e_lo + EXPERTS_PER_GROUP)
    le = jnp.where(in_grp, logits, NEG)
    v1 = jnp.max(le, axis=-1, keepdims=True)
    i1 = jnp.min(jnp.where(le == v1, lane, LANES), axis=-1, keepdims=True)
    le2 = jnp.where(lane == i1, NEG, le)
    v2 = jnp.max(le2, axis=-1, keepdims=True)
    i2 = jnp.min(jnp.where(le2 == v2, lane, LANES), axis=-1, keepdims=True)
    e2 = jnp.exp(v2 - v1)
    w1 = p_g / (1.0 + e2)
    w2 = p_g * e2 / (1.0 + e2)
    route = jnp.where(lane == ROUTE_I1, i1.astype(F32), jnp.where(lane == ROUTE_I2, i2.astype(F32), 0.0))
    route_ref[...] = route + jnp.where(lane == ROUTE_W1, w1, 0.0) + jnp.where(lane == ROUTE_W2, w2, 0.0)
    picks = (lane == i1).astype(F32) + (lane == i2).astype(F32)
    cnt_ref[...] = jnp.broadcast_to(jnp.sum(picks, axis=0, keepdims=True), cnt_ref.shape)


def _post_attn(oa, obs, lses, gates_src, ga_blk, x2d, wpa, wpb, wout, ln2, wr, br, tm, tiles_per_seq, extra):
    m, d = x2d.shape
    tps = tiles_per_seq
    own = m // tm
    n_tiles = own + (extra is not None)
    mine = lambda i: jnp.minimum(i, own - 1)

    def tile(w, col=0):
        return pl.BlockSpec((tm, w), lambda i: (mine(i), col))

    def out_tile(rows, w):
        return pl.BlockSpec((rows, w), lambda i: (i, 0))

    def full(a):
        nd = a.ndim
        return pl.BlockSpec(a.shape, lambda i: (0,) * nd)

    def residue(a):
        dil = a.shape[1]
        return pl.BlockSpec((None, dil, tm // dil, PB_W), lambda i: (mine(i) // tps, 0, mine(i) % tps, 0))

    weights = [wpa, wpb, wout, ln2.reshape(1, d), wr, br]
    scratch = [pltpu.VMEM((6 * PB_W // LANES, tm, LANES), F32)]
    in_specs = ([tile(QA_W)] + [residue(a) for a in obs] + [residue(a) for a in lses]
                + [tile(d, ga_blk), tile(d, ga_blk + 1), tile(d)] + [full(w) for w in weights])
    args = [oa, *obs, *lses, gates_src, gates_src, x2d, *weights]
    if extra is not None:
        in_specs = in_specs + [full(a) for a in extra]
        args = args + list(extra)
    return pl.pallas_call(
        functools.partial(_post_attn_kernel, n_extra=0 if extra is None else len(extra)),
        grid=(n_tiles,),
        in_specs=in_specs,
        out_specs=[out_tile(tm, d), out_tile(tm, d), out_tile(tm, LANES), out_tile(SUBLANES, LANES)],
        out_shape=[jax.ShapeDtypeStruct((n_tiles * tm, d), F32),
                   jax.ShapeDtypeStruct((n_tiles * tm, d), BF16),
                   jax.ShapeDtypeStruct((n_tiles * tm, LANES), F32),
                   jax.ShapeDtypeStruct((n_tiles * SUBLANES, LANES), F32)],
        scratch_shapes=scratch,
        compiler_params=_cparams(("arbitrary",)),
        name="post_attn",
    )(*args)


def _route_plan(cnt_rows, n_tiles, n_xtiles):
    g = SUBLANES
    cnt = cnt_rows.reshape(n_tiles, g, LANES)[:, 0, :N_EXPERTS].astype(jnp.int32)
    cnt8 = (cnt + g - 1) // g * g
    loff = jnp.cumsum(cnt8, axis=1) - cnt8
    boff = jnp.cumsum(cnt8, axis=0) - cnt8
    tot = jnp.sum(cnt8, axis=0)
    region = (tot + MOE_TMX - 1) // MOE_TMX * MOE_TMX
    gend = jnp.cumsum(region)
    gbase = gend - region
    cum_tiles = gend // MOE_TMX
    j = jnp.arange(n_xtiles + MOE_SPILL_TILES, dtype=jnp.int32)
    loff_rows = jnp.zeros((n_tiles, g, LANES), F32).at[:, :, :N_EXPERTS].set(loff[:, None, :].astype(F32))
    k8 = g * jnp.arange(MOE_SLOTS // g, dtype=jnp.int32)
    run_end = loff + cnt8
    e_of_k = jnp.sum(k8[None, :, None] >= run_end[:, None, :], axis=2)
    shift = gbase[None, :] + boff - loff
    picked = jnp.sum(jnp.where(e_of_k[:, :, None] == jnp.arange(N_EXPERTS)[None, None, :], shift[:, None, :], 0), axis=2)
    ids = jnp.arange(N_EXPERTS, dtype=jnp.int32)
    xe = jnp.minimum(jnp.sum(j[:, None] >= cum_tiles[None, :], axis=1), N_EXPERTS - 1).astype(jnp.int32)
    has_tiles = region > 0
    first_tile = cum_tiles - region // MOE_TMX
    of_tile = lambda per_expert: jnp.sum(jnp.where(xe[:, None] == ids[None, :], per_expert[None, :], 0), axis=1)
    later = jnp.where(has_tiles[None, :] & (ids[None, :] > ids[:, None]), ids[None, :], N_EXPERTS)
    next_of = jnp.min(later, axis=1)
    next_of = jnp.where(next_of == N_EXPERTS, -1, next_of)
    xtile_first = (jnp.any(has_tiles[None, :] & (j[:, None] == first_tile[None, :]), axis=1)
                   & (j < cum_tiles[-1])).astype(jnp.int32)
    return dict(
        xtile_first=xtile_first, xtile_next=of_tile(next_of).astype(jnp.int32),
        xtile_parity=of_tile((jnp.cumsum(has_tiles.astype(jnp.int32)) - 1) % 2).astype(jnp.int32),
        chunk_dst=jnp.where(k8[None, :] < run_end[:, -1:], picked, n_xtiles * MOE_TMX).reshape(-1).astype(jnp.int32)
        + jnp.tile(k8, n_tiles),
        zst=gbase + tot, znch=(region - tot) // g,
        ztot=jnp.sum((region - tot) // g).reshape(1),
        xtile_expert=xe,
        n_used=cum_tiles[-1:].astype(jnp.int32),
        loff_rows=loff_rows.reshape(n_tiles * g, LANES))


def _local_slots(route, loff_row):
    tm = route.shape[0]
    lane = lax.broadcasted_iota(jnp.int32, (tm, LANES), 1)
    e1 = lane == route[:, ROUTE_I1:ROUTE_I1 + 1].astype(jnp.int32)
    e2 = lane == route[:, ROUTE_I2:ROUTE_I2 + 1].astype(jnp.int32)
    earlier = (lax.broadcasted_iota(jnp.int32, (tm, tm), 1) < lax.broadcasted_iota(jnp.int32, (tm, tm), 0))
    earlier = earlier.astype(BF16)
    c1 = jnp.dot(earlier, e1.astype(BF16), preferred_element_type=F32)
    c2 = jnp.dot(earlier, e2.astype(BF16), preferred_element_type=F32)
    cnt1 = jnp.sum(e1.astype(F32), axis=0, keepdims=True)
    pos1 = jnp.sum(jnp.where(e1, c1 + loff_row, 0.0), axis=1, keepdims=True)
    pos2 = jnp.sum(jnp.where(e2, c2 + cnt1 + loff_row, 0.0), axis=1, keepdims=True)
    slot = lax.broadcasted_iota(jnp.int32, (tm, MOE_SLOTS), 1)
    return slot == pos1.astype(jnp.int32), slot == pos2.astype(jnp.int32)


def _pack_bf16_pairs(x):
    c = x.shape[1] // 2
    bits = lambda v: lax.bitcast_convert_type(v.astype(BF16).astype(F32), jnp.uint32)
    return bits(x[:, :c]) | (bits(x[:, c:]) >> 16)


def _unpack_bf16_pairs(w):
    hi = lax.bitcast_convert_type(w & jnp.uint32(0xFFFF0000), F32).astype(BF16)
    lo = lax.bitcast_convert_type(w << 16, F32).astype(BF16)
    return jnp.concatenate([hi, lo], axis=1)


def _split3(w):
    hi = w.astype(BF16).astype(F32)
    mid = (w - hi).astype(BF16).astype(F32)
    return hi, mid, (w - hi - mid).astype(BF16).astype(F32)


def _moe_scatter_kernel(cdst_s, zst_s, znch_s, ztot_s, nused_s,
                        hn_ref, route_ref, loffv_ref, xs_hbm, buf_ref, zero_ref, sem, zsem, tsem):
    b = pl.program_id(0)
    nb = pl.num_programs(0)
    slot = b % 2
    g = SUBLANES
    d = hn_ref.shape[1]
    n_xtiles = xs_hbm.shape[0] // MOE_TMX - MOE_SPILL_TILES

    def run_copy(s, src_row, dst_row):
        return pltpu.make_async_copy(buf_ref.at[s, pl.ds(src_row, g)], xs_hbm.at[pl.ds(dst_row, g)], sem.at[s])

    def zero_copy(dst_row):
        return pltpu.make_async_copy(zero_ref.at[pl.ds(0, g)], xs_hbm.at[pl.ds(dst_row, g)], zsem)

    def zero_tile_copy(j):
        return pltpu.make_async_copy(zero_ref, xs_hbm.at[pl.ds(pl.multiple_of(j * MOE_TMX, MOE_TMX), MOE_TMX)], tsem)

    n_chunks = MOE_SLOTS // g
    spill = n_xtiles * MOE_TMX

    def wait_buffer(s):
        for _ in range(n_chunks):
            run_copy(s, 0, 0).wait()

    def ship_buffer(s, dst_row_of):
        for k in range(n_chunks):
            run_copy(s, k * g, pl.multiple_of(dst_row_of(k), g)).start()

    @pl.when(b == 0)
    def _():
        zero_ref[...] = jnp.zeros_like(zero_ref)
        buf_ref[1] = jnp.zeros(buf_ref.shape[1:], buf_ref.dtype)

        def per_expert(e, c):
            def per_chunk(k, c2):
                zero_copy(pl.multiple_of(zst_s[e] + k * g, g)).start()
                return c2
            return lax.fori_loop(0, znch_s[e], per_chunk, c)
        lax.fori_loop(0, N_EXPERTS, per_expert, 0)
        lax.fori_loop(nused_s[0], n_xtiles, lambda j, c: (zero_tile_copy(j).start(), c)[1], 0)

    @pl.when(b >= 1)
    def _():
        wait_buffer(slot)

    prev = jnp.maximum(b - 1, 0) * n_chunks
    ship_buffer(1 - slot, lambda k: jnp.where(b == 0, spill + k * g, cdst_s[prev + k]))

    route = route_ref[...]
    p1, p2 = _local_slots(route, loffv_ref[0:1, :])
    tn = (((0,), (0,)), ((), ()))
    picks = (p1 | p2).astype(BF16)
    buf_ref[slot, :, 0:d // 2] = _pack_bf16_pairs(lax.dot_general(picks, hn_ref[...], tn, preferred_element_type=F32))
    lane = lax.broadcasted_iota(jnp.int32, route.shape, 1)
    meta = jnp.zeros((MOE_SLOTS, LANES), F32)
    for p, col in ((p1, ROUTE_W1), (p2, ROUTE_W2)):
        w = route[:, col:col + 1]
        parts = _split3(w)
        wm = sum(jnp.where(lane == k, part, 0.0) for k, part in enumerate(parts))
        meta = meta + lax.dot_general(p.astype(BF16), wm.astype(BF16), tn, preferred_element_type=F32)
    buf_ref[slot, :, d // 2:d // 2 + LANES] = lax.bitcast_convert_type(meta, jnp.uint32)

    @pl.when(b == nb - 1)
    def _():
        wait_buffer(1 - slot)
        ship_buffer(slot, lambda k: cdst_s[b * n_chunks + k])
        wait_buffer(slot)
        lax.fori_loop(0, ztot_s[0], lambda _, c: (zero_copy(0).wait(), c)[1], 0)
        lax.fori_loop(nused_s[0], n_xtiles, lambda j, c: (zero_tile_copy(j).wait(), c)[1], 0)


def _moe_scatter(plan, hn_all, route_all, n_tiles, n_xtiles):
    d = hn_all.shape[1]
    tm = MOE_TM
    grid_spec = pltpu.PrefetchScalarGridSpec(
        num_scalar_prefetch=5,
        grid=(n_tiles,),
        in_specs=[pl.BlockSpec((tm, d), lambda i, *_: (i, 0)),
                  pl.BlockSpec((tm, LANES), lambda i, *_: (i, 0)),
                  pl.BlockSpec((SUBLANES, LANES), lambda i, *_: (i, 0))],
        out_specs=pl.BlockSpec(memory_space=pl.ANY),
        scratch_shapes=[pltpu.VMEM((2, MOE_SLOTS, d // 2 + LANES), jnp.uint32),
                        pltpu.VMEM((MOE_TMX, d // 2 + LANES), jnp.uint32),
                        pltpu.SemaphoreType.DMA((2,)),
                        pltpu.SemaphoreType.DMA(()),
                        pltpu.SemaphoreType.DMA(())])
    return pl.pallas_call(
        _moe_scatter_kernel,
        grid_spec=grid_spec,
        out_shape=jax.ShapeDtypeStruct(((n_xtiles + MOE_SPILL_TILES) * MOE_TMX, d // 2 + LANES), jnp.uint32),
        compiler_params=_cparams(("arbitrary",)),
        name="moe_scatter",
    )(plan["chunk_dst"], plan["zst"], plan["znch"], plan["ztot"], plan["n_used"],
      hn_all, route_all, plan["loff_rows"])


def _moe_experts_kernel(xe_s, first_s, next_s, par_s, nused_s, x_ref, w1_hbm, w3_hbm, w2_hbm, y_ref,
                        w1f_ref, w3f_ref, w2f_ref, w1b_ref, w3b_ref, w2b_ref, sem):
    d = w1b_ref.shape[0]

    def weight_copies(e, p):
        return [pltpu.make_async_copy(src.at[e], dst.at[p], sem.at[p, i])
                for i, (src, dst) in enumerate(((w1_hbm, w1f_ref), (w3_hbm, w3f_ref), (w2_hbm, w2f_ref)))]

    def one_tile(j, rows):
        @pl.when(j < nused_s[0])
        def _():
            @pl.when(first_s[j] == 1)
            def _():
                p = par_s[j]

                @pl.when(j == 0)
                def _():
                    for c in weight_copies(xe_s[j], p):
                        c.start()
                for c in weight_copies(xe_s[j], p):
                    c.wait()
                w1b_ref[...] = w1f_ref[p].astype(BF16)
                w3b_ref[...] = w3f_ref[p].astype(BF16)
                w2b_ref[...] = w2f_ref[p].astype(BF16)

                @pl.when(next_s[j] >= 0)
                def _():
                    for c in weight_copies(next_s[j], 1 - p):
                        c.start()

            x = _unpack_bf16_pairs(x_ref[rows, 0:d // 2])
            gate = jnp.sum(lax.bitcast_convert_type(x_ref[rows, d // 2:d // 2 + LANES], F32), axis=1, keepdims=True)
            a = jnp.dot(x, w1b_ref[...], preferred_element_type=F32)
            b = jnp.dot(x, w3b_ref[...], preferred_element_type=F32)
            hh = (a * _sigmoid(a)) * b * gate
            y_ref[rows] = _pack_bf16_pairs(jnp.dot(hh.astype(BF16), w2b_ref[...], preferred_element_type=F32))

        @pl.when(j >= nused_s[0])
        def _():
            y_ref[rows] = jnp.zeros((MOE_TMX, y_ref.shape[1]), y_ref.dtype)

    for sub in range(MOE_XSTEP_TILES):
        one_tile(MOE_XSTEP_TILES * pl.program_id(0) + sub, slice(sub * MOE_TMX, (sub + 1) * MOE_TMX))


def _moe_experts(plan, xs, w1, w3, w2, n_xtiles):
    ne, d, f = w1.shape
    per = MOE_XSTEP_TILES
    assert (n_xtiles + MOE_SPILL_TILES) % per == 0
    last = lambda s, nu: jnp.maximum(jnp.minimum(s, (nu[0] - 1) // per), 0)
    any_spec = pl.BlockSpec(memory_space=pl.ANY)
    grid_spec = pltpu.PrefetchScalarGridSpec(
        num_scalar_prefetch=5,
        grid=((n_xtiles + MOE_SPILL_TILES) // per,),
        in_specs=[pl.BlockSpec((per * MOE_TMX, d // 2 + LANES), lambda s, xe, fi, nx, pa, nu: (last(s, nu), 0)),
                  any_spec, any_spec, any_spec],
        out_specs=pl.BlockSpec((per * MOE_TMX, d // 2), lambda s, *_: (s, 0)),
        scratch_shapes=[pltpu.VMEM((2, d, f), F32), pltpu.VMEM((2, d, f), F32), pltpu.VMEM((2, f, d), F32),
                        pltpu.VMEM((d, f), BF16), pltpu.VMEM((d, f), BF16), pltpu.VMEM((f, d), BF16),
                        pltpu.SemaphoreType.DMA((2, 3))])
    return pl.pallas_call(
        _moe_experts_kernel,
        grid_spec=grid_spec,
        out_shape=jax.ShapeDtypeStruct(((n_xtiles + MOE_SPILL_TILES) * MOE_TMX, d // 2), jnp.uint32),
        compiler_params=_cparams(("arbitrary",)),
        name="moe_experts",
    )(plan["xtile_expert"], plan["xtile_first"], plan["xtile_next"], plan["xtile_parity"], plan["n_used"],
      xs, w1, w3, w2)


def _moe_combine_kernel(cdst_s, ys_hbm, route_ref, loffv_ref, h_ref, lnf_ref, y_ref,
                        buf_ref, sem, *, tile0, final_norm):
    b = pl.program_id(0)
    nb = pl.num_programs(0)
    slot = b % 2
    g = SUBLANES
    n_chunks = MOE_SLOTS // g
    tile = b + tile0

    def run_copy(s, src_row, dst_row):
        return pltpu.make_async_copy(ys_hbm.at[pl.ds(src_row, g)], buf_ref.at[s, pl.ds(dst_row, g)], sem.at[s])

    def fetch(t, s):
        for k in range(n_chunks):
            run_copy(s, pl.multiple_of(cdst_s[t * n_chunks + k], g), k * g).start()

    def wait_buffer(s):
        for _ in range(n_chunks):
            run_copy(s, 0, 0).wait()

    @pl.when(b == 0)
    def _():
        fetch(tile, slot)

    fetch(jnp.minimum(tile + 1, tile0 + nb - 1), 1 - slot)
    wait_buffer(slot)

    @pl.when(b == nb - 1)
    def _():
        wait_buffer(1 - slot)

    p1, p2 = _local_slots(route_ref[...], loffv_ref[0:1, :])
    picks = (p1 | p2).astype(BF16)
    y = h_ref[...] + jnp.dot(picks, _unpack_bf16_pairs(buf_ref[slot]), preferred_element_type=F32)
    if final_norm:
        y = _rms(y, lnf_ref[...])
    y_ref[...] = y


def _moe_combine(plan, ys, route_all, h_all, lnf, tile0, n_tiles, final_norm):
    d = h_all.shape[1]
    tm = MOE_TM
    grid_spec = pltpu.PrefetchScalarGridSpec(
        num_scalar_prefetch=1,
        grid=(n_tiles,),
        in_specs=[pl.BlockSpec(memory_space=pl.ANY),
                  pl.BlockSpec((tm, LANES), lambda i, *_: (i + tile0, 0)),
                  pl.BlockSpec((SUBLANES, LANES), lambda i, *_: (i + tile0, 0)),
                  pl.BlockSpec((tm, d), lambda i, *_: (i + tile0, 0)),
                  pl.BlockSpec((1, d), lambda i, *_: (0, 0))],
        out_specs=pl.BlockSpec((tm, d), lambda i, *_: (i, 0)),
        scratch_shapes=[pltpu.VMEM((2, MOE_SLOTS, d // 2), jnp.uint32), pltpu.SemaphoreType.DMA((2,))])
    return pl.pallas_call(
        functools.partial(_moe_combine_kernel, tile0=tile0, final_norm=final_norm),
        grid_spec=grid_spec,
        out_shape=jax.ShapeDtypeStruct((n_tiles * tm, d), F32),
        compiler_params=_cparams(("arbitrary",)),
        name="moe_combine",
    )(plan["chunk_dst"], ys, route_all, plan["loff_rows"], h_all, lnf.reshape(1, d))


def _prompt_states(kvt, n, t):
    hd = HEAD_DIM
    tail = kvt.shape[2]

    def state(lo, heads, win):
        w = min(win, t)
        kv = kvt[:, lo:lo + 2 * heads * hd, tail - w:]
        return jnp.transpose(kv.reshape(n, 2, heads, hd, w), (0, 4, 1, 2, 3))

    out = [state(0, A_KV_HEADS, A_WINDOW)]
    for p, (win, _) in enumerate(B_PATTERNS):
        out.append(state(2 * KA_W + 2 * p * PB_W, B_HEADS_PER_PATTERN, win))
    return out


def _layer(xp, xs, caches, rel_bias, ln1, w_in, sinks, w_pa, w_pb, w_out, ln2, w_rg, b_rg, w_re, b_re,
           w1, w3, w2, lnf, final_norm):
    n, t, d = xp.shape
    ns = xs.shape[0]
    assert xs.shape[1] == 1 and OFF_GA + 2 * d == w_in.shape[1] and P_QA == 2 * d
    w_kvt = _kv_weights_transposed(w_in)
    wpa, wpb, wout = w_pa.astype(BF16), w_pb.astype(BF16), w_out.astype(BF16)
    lanes_left = LANES - N_EXPERTS - MOE_GROUPS
    wr = jnp.pad(jnp.concatenate([w_re, w_rg], axis=1), ((0, 0), (0, lanes_left)))
    br = jnp.pad(jnp.concatenate([b_re, b_rg])[None, :], ((0, 0), (0, lanes_left)))
    wr_hi = wr.astype(BF16)
    wr = jnp.stack([wr_hi, (wr - wr_hi.astype(F32)).astype(BF16)])

    tm = MOE_TM
    xp2 = xp.reshape(n * t, d)
    xs2 = xs.reshape(ns, d)
    proj_s, w_bf = _in_proj_sample(xs2, ln1, w_in)
    if ns == 2 * (n * t // MIX_TM):
        (act, qkv2, qkv3, kvt), sampled = _in_proj_mix(xp2, ln1, w_bf, w_kvt, n, t, MIX_TM, proj_s, caches,
                                                       rel_bias, sinks)
    else:
        act, qkv2, qkv3, kvt = _in_proj_prompt(xp2, ln1, w_bf, w_kvt, n, t, IN_PROJ_TM)
        sampled = _sample_mix(proj_s, caches, rel_bias, sinks)
    oa_s, ob_s, lse_s, st_s = sampled
    act4 = act.reshape(n, 1, t, ACT_W)
    bias_a = _band_bias(rel_bias[:, :A_HEADS], A_WINDOW - 1, 1)
    (oa,) = _band_attn(act4, bias_a, sinks, q_off=P_QA, k_off=P_KA, v_off=P_VA,
                       kv_heads=A_KV_HEADS, grp=A_GROUP, want_lse=False)
    obs, lses = [], []
    for p, (win, dil) in enumerate(B_PATTERNS):
        lo = A_HEADS + p * B_HEADS_PER_PATTERN
        bias_p = _band_bias(rel_bias[:, lo:lo + B_HEADS_PER_PATTERN], win // dil, dil)
        src, base = ((act4, P_B), (qkv2, 0), (qkv3, 0))[p]
        o, lse = _band_attn(src, bias_p, None, q_off=base, k_off=base + PB_W, v_off=base + 2 * PB_W,
                            kv_heads=B_HEADS_PER_PATTERN, grp=1, want_lse=True)
        obs.append(o)
        lses.append(lse)
    assert ns <= MOE_TM
    p_tiles = n * t // MOE_TM
    n_tiles = p_tiles + 1
    m_all = n_tiles * MOE_TM
    st_p = _prompt_states(kvt, n, t)

    rows = lambda a: jnp.pad(a, ((0, MOE_TM - ns), (0, 0)))
    obs_s = [rows(ob_s[:, p].astype(BF16)).reshape(1, 1, MOE_TM, PB_W) for p in range(N_PAT)]
    lses_s = [rows(jnp.repeat(lse_s[:, p], HEAD_DIM, axis=-1)).reshape(1, 1, MOE_TM, PB_W) for p in range(N_PAT)]
    gates_s = rows(proj_s[:, OFF_GA:].astype(BF16))
    routed_s = _post_attn(rows(oa_s.astype(BF16)), obs_s, lses_s, gates_s, 0, rows(xs2),
                          wpa, wpb, wout, ln2, wr, br, MOE_TM, 1, None)
    h_all, hn_all, route_all, cnt_all = _post_attn(oa.reshape(n * t, QA_W), obs, lses, act, 0, xp2,
                                                   wpa, wpb, wout, ln2, wr, br, tm, t // tm, routed_s)

    max_rows = 2 * m_all + n_tiles * N_EXPERTS * (SUBLANES - 1) + N_EXPERTS * (MOE_TMX - SUBLANES)
    n_xtiles = -(-max_rows // MOE_TMX)
    n_xtiles += -(n_xtiles + MOE_SPILL_TILES) % MOE_XSTEP_TILES
    plan = _route_plan(cnt_all, n_tiles, n_xtiles)
    xs_sorted = _moe_scatter(plan, hn_all, route_all, n_tiles, n_xtiles)
    ys_sorted = _moe_experts(plan, xs_sorted, w1, w3, w2, n_xtiles)
    yp = _moe_combine(plan, ys_sorted, route_all, h_all, lnf, 0, p_tiles, final_norm).reshape(n, t, d)
    ys = _moe_combine(plan, ys_sorted, route_all, h_all, lnf, p_tiles, 1, final_norm)[:ns].reshape(ns, 1, d)
    return yp, ys, st_p, st_s


def kernel(x_prompt, x_sample, cache_a_kv, cache_b1_kv, cache_b2_kv, cache_b3_kv, rel_bias, ln1_g, w_in, sinks,
           w_pa, w_pb, w_out, ln2_g, w_rg, b_rg, w_re, b_re, w1, w3, w2, lnf_g):
    depth = w_in.shape[0]
    assert depth >= 1
    xp, xs = x_prompt, x_sample
    new_p = [[] for _ in range(4)]
    new_s = [[] for _ in range(4)]
    for l in range(depth):
        caches = (cache_a_kv[l], cache_b1_kv[l], cache_b2_kv[l], cache_b3_kv[l])
        xp, xs, st_p, st_s = _layer(xp, xs, caches, rel_bias, ln1_g[l], w_in[l], sinks[l], w_pa[l], w_pb[l],
                                    w_out[l], ln2_g[l], w_rg[l], b_rg[l], w_re[l], b_re[l], w1[l], w3[l], w2[l],
                                    lnf_g, l == depth - 1)
        for i in range(4):
            new_p[i].append(st_p[i])
            new_s[i].append(st_s[i])
    a_p, b1_p, b2_p, b3_p = [jnp.stack(v) for v in new_p]
    a_s, b1_s, b2_s, b3_s = [jnp.stack(v) for v in new_s]
    return (xp, xs, a_p, a_s, b1_p, b1_s, b2_p, b2_s, b3_p, b3_s)
```
